```python
import math
import jax
import jax.numpy as jnp
from jax import lax
import numpy as np

D_MODEL = 1024
BATCH = 8
SEQ = 2048
DEPTH = 2
DEC_BATCH = 32
DEC_SEQ = 4
PAST_LEN = 16384
PAGE_SIZE = 128

N_EVEN = (DEPTH + 1) // 2
N_ODD = DEPTH // 2

MLA_HEADS = 8
MLA_NOPE = 64
MLA_ROPE = 32
MLA_V = 64
MLA_Q_LORA = 384
MLA_KV_LORA = 256
MLA_WIDTH = MLA_HEADS * MLA_V
MLA_SCALE = (MLA_NOPE + MLA_ROPE) ** -0.5
ROPE_THETA = 10000.0

S5_GROUPS = 32
S5_GROUP_CH = 16
S5_STATE = 64
S5_WIDTH = S5_GROUPS * S5_GROUP_CH

NSA_HEADS = 16
NSA_KV_GROUPS = 2
NSA_REP = NSA_HEADS // NSA_KV_GROUPS
NSA_HEAD_DIM = 64
NSA_WIDTH = NSA_HEADS * NSA_HEAD_DIM
NSA_KV_WIDTH = NSA_KV_GROUPS * NSA_HEAD_DIM
NSA_BLOCK = 64
NSA_TOPK = 16
NSA_WINDOW = 512
NSA_QCHUNK = 64
NSA_SCALE = NSA_HEAD_DIM ** -0.5
FORCED_BONUS = float(NSA_REP + 1)

Q_BLOCK = 128
EPS = 1e-6

EVEN_SIZES = (MLA_Q_LORA, MLA_KV_LORA, MLA_ROPE, MLA_WIDTH, S5_WIDTH, S5_WIDTH)
EVEN_IN = sum(EVEN_SIZES)
ODD_SIZES = (NSA_WIDTH,) + (NSA_KV_WIDTH,) * 6 + (3 * NSA_HEADS, NSA_WIDTH)
ODD_IN = sum(ODD_SIZES)

kernel_name = 'hybrid_mla_s5_nsa_step'


def _split_points(sizes):
    return [int(s) for s in np.cumsum(sizes)[:-1]]


def rmsnorm(x, g):
    xf = x.astype(jnp.float32)
    y = xf * lax.rsqrt(jnp.mean(xf * xf, axis=-1, keepdims=True) + EPS)
    return (y * g.astype(jnp.float32)).astype(x.dtype)


def rope(x, pos):
    half = x.shape[-1] // 2
    inv = ROPE_THETA ** (-jnp.arange(half, dtype=jnp.float32) / half)
    ang = pos.astype(jnp.float32)[:, None] * inv[None, :]
    ang = ang.reshape((pos.shape[0],) + (1,) * (x.ndim - 3) + (half,))
    cos, sin = jnp.cos(ang), jnp.sin(ang)
    xf = x.astype(jnp.float32)
    x1, x2 = xf[..., :half], xf[..., half:]
    return jnp.concatenate([x1 * cos - x2 * sin, x1 * sin + x2 * cos], axis=-1).astype(x.dtype)


def alibi_slopes(n_heads):
    return 2.0 ** (-8.0 * jnp.arange(1, n_heads + 1, dtype=jnp.float32) / n_heads)


def masked_softmax(s, mask):
    s = jnp.where(mask, s, -jnp.inf)
    m = jnp.max(s, axis=-1, keepdims=True)
    m = jnp.where(jnp.isfinite(m), m, 0.0)
    e = jnp.exp(s - m)
    den = jnp.sum(e, axis=-1, keepdims=True)
    return e / jnp.where(den > 0, den, 1.0)


def mla_core(q_lat, q_rope, ckv, krope, qpos, kpos):
    s = (jnp.einsum('bqhc,bkc->bhqk', q_lat, ckv)
         + jnp.einsum('bqhr,bkr->bhqk', q_rope, krope)).astype(jnp.float32) * MLA_SCALE
    s = jnp.where(kpos[None, None, None, :] <= qpos[None, None, :, None], s, -jnp.inf)
    p = jax.nn.softmax(s, axis=-1).astype(ckv.dtype)
    return jnp.einsum('bhqk,bkc->bqhc', p, ckv)


def mla_prompt(q_lat, q_rope, ckv, krope, pos):
    B, T = q_lat.shape[:2]
    nb = T // Q_BLOCK
    qb = q_lat.reshape(B, nb, Q_BLOCK, MLA_HEADS, MLA_KV_LORA).swapaxes(0, 1)
    rb = q_rope.reshape(B, nb, Q_BLOCK, MLA_HEADS, MLA_ROPE).swapaxes(0, 1)
    pb = pos.reshape(nb, Q_BLOCK)
    o = lax.map(lambda a: mla_core(a[0], a[1], ckv, krope, a[2], pos), (qb, rb, pb))
    return o.swapaxes(0, 1).reshape(B, T, MLA_HEADS, MLA_KV_LORA)


def _linear_combine(left, right):
    a_l, b_l = left
    a_r, b_r = right
    return a_r * a_l, a_r * b_l + b_r


def s5_scan(u, h0, lam_re, lam_im, log_dt, b_re, b_im, c_re, c_im, d_skip):
    B, T, _ = u.shape
    f32 = jnp.float32
    uf = u.astype(f32).reshape(B, T, S5_GROUPS, S5_GROUP_CH)
    lam = lax.complex(lam_re.astype(f32), lam_im.astype(f32))
    dt = jnp.exp(log_dt.astype(f32))[:, None]
    lam_dt = lam * dt
    lam_bar = jnp.exp(lam_dt)
    b_bar = ((lam_bar - 1.0) / lam)[:, :, None] * lax.complex(b_re.astype(f32), b_im.astype(f32))
    bu = jnp.einsum('gnp,btgp->btgn', b_bar, uf.astype(jnp.complex64))
    a = jnp.broadcast_to(lam_bar, bu.shape)
    _, h = lax.associative_scan(_linear_combine, (a, bu), axis=1)
    if h0 is not None:
        steps = jnp.arange(1, T + 1, dtype=f32)[:, None, None]
        h = h + jnp.exp(lam_dt[None] * steps)[None] * h0[:, None]
    c = lax.complex(c_re.astype(f32), c_im.astype(f32))
    y = jnp.einsum('gpn,btgn->btgp', c, h).real + d_skip.astype(f32) * uf
    return y.reshape(B, T, S5_WIDTH).astype(u.dtype), h[:, -1]


def even_mixer(x, pos, w, mla_past, s5_h0):
    (norm_g, w_in, g_q, g_kv, w_uq, w_uk, w_uv, lam_re, lam_im, log_dt,
     b_re, b_im, c_re, c_im, d_skip, w_glu, b_glu, w_out) = w
    B, T, _ = x.shape
    h = rmsnorm(x, norm_g)
    proj = jnp.einsum('btd,de->bte', h, w_in)
    cq_raw, ckv_raw, kr_raw, z_a, u, z_b = jnp.split(proj, _split_points(EVEN_SIZES), axis=-1)
    cq = rmsnorm(cq_raw, g_q)
    q = jnp.einsum('btc,chd->bthd', cq, w_uq)
    q_lat = jnp.einsum('bthn,chn->bthc', q[..., :MLA_NOPE], w_uk)
    q_rope = rope(q[..., MLA_NOPE:], pos)
    ckv = rmsnorm(ckv_raw, g_kv)
    krope = rope(kr_raw, pos)
    new_rows = jnp.concatenate([ckv, krope], axis=-1)
    if mla_past is None:
        o_lat = mla_prompt(q_lat, q_rope, ckv, krope, pos)
    else:
        rows = jnp.concatenate([mla_past, new_rows], axis=1)
        kpos = jnp.arange(rows.shape[1], dtype=jnp.int32)
        o_lat = mla_core(q_lat, q_rope, rows[..., :MLA_KV_LORA], rows[..., MLA_KV_LORA:], pos, kpos)
    o_a = jnp.einsum('bthc,chv->bthv', o_lat, w_uv).reshape(B, T, MLA_WIDTH)
    y5, h_last = s5_scan(u, s5_h0, lam_re, lam_im, log_dt, b_re, b_im, c_re, c_im, d_skip)
    g5 = jax.nn.gelu(y5)
    o_b = g5 * jax.nn.sigmoid(g5 @ w_glu + b_glu)
    mix = jnp.concatenate([o_a * jax.nn.silu(z_a), o_b * jax.nn.silu(z_b)], axis=-1)
    state = jnp.stack([h_last.real, h_last.imag], axis=-1).astype(x.dtype)
    return x + mix @ w_out, new_rows, state


def gqa_attend(q, k, v, qpos, kpos, mask, slopes):
    s = jnp.einsum('...qgrd,...kgd->...grqk', q, k).astype(jnp.float32) * NSA_SCALE
    dist = (qpos[..., :, None] - kpos[..., None, :]).astype(jnp.float32)
    s = s - slopes[:, :, None, None] * dist[..., None, None, :, :]
    p = masked_softmax(s, mask[..., None, None, :, :])
    o = jnp.einsum('...grqk,...kgd->...qgrd', p.astype(v.dtype), v)
    return o, p


def nsa_compress(k, pe, phi1, phi2):
    B, L, G, d = k.shape
    nc = L // NSA_BLOCK
    kb = k[:, :nc * NSA_BLOCK].reshape(B, nc, NSA_BLOCK, G, d) + pe[None, None, :, None, :]
    kb = kb.transpose(0, 1, 3, 2, 4).reshape(B, nc, G, NSA_BLOCK * d)
    return jax.nn.silu(kb @ phi1) @ phi2


def nsa_select(p_cmp, qpos, n_sel, k_eff):
    imp = jnp.sum(p_cmp, axis=2)
    imp = jnp.pad(imp, ((0, 0), (0, 0), (0, 0), (0, n_sel - imp.shape[-1])))
    blk = jnp.arange(n_sel)[None, :]
    cur = (qpos // NSA_BLOCK)[:, None]
    forced = (blk == 0) | (blk == cur) | (blk == cur - 1)
    score = jnp.where(blk <= cur, imp + jnp.where(forced, FORCED_BONUS, 0.0), -jnp.inf)
    _, idx = lax.top_k(score, k_eff)
    return idx


def sel_attend(q, kg, vg, qpos, blk_idx, slopes):
    B, G, Tq, kk, _, d = kg.shape
    kpos = (blk_idx[..., None] * NSA_BLOCK + jnp.arange(NSA_BLOCK)).reshape(B, G, Tq, kk * NSA_BLOCK)
    kg = kg.reshape(B, G, Tq, kk * NSA_BLOCK, d)
    vg = vg.reshape(B, G, Tq, kk * NSA_BLOCK, d)
    s = jnp.einsum('bqgrd,bgqmd->bgrqm', q, kg).astype(jnp.float32) * NSA_SCALE
    dist = (qpos[None, None, :, None] - kpos).astype(jnp.float32)
    s = s - slopes[None, :, :, None, None] * dist[:, :, None]
    p = masked_softmax(s, (dist >= 0)[:, :, None])
    return jnp.einsum('bgrqm,bgqmd->bqgrd', p.astype(vg.dtype), vg)


def window_prompt(q, k, v, slopes):
    B, T = q.shape[:2]
    G, R, d = NSA_KV_GROUPS, NSA_REP, NSA_HEAD_DIM
    nb = T // Q_BLOCK
    span = NSA_WINDOW + Q_BLOCK
    kp = jnp.pad(k, ((0, 0), (NSA_WINDOW, 0), (0, 0), (0, 0)))
    vp = jnp.pad(v, ((0, 0), (NSA_WINDOW, 0), (0, 0), (0, 0)))
    qb = q.reshape(B, nb, Q_BLOCK, G, R, d).swapaxes(0, 1)

    def band(a):
        qi, i = a
        start = i * Q_BLOCK
        ki = lax.dynamic_slice_in_dim(kp, start, span, axis=1)
        vi = lax.dynamic_slice_in_dim(vp, start, span, axis=1)
        qpos = start + jnp.arange(Q_BLOCK)
        kpos = start - NSA_WINDOW + jnp.arange(span)
        mask = (kpos[None] >= 0) & (kpos[None] <= qpos[:, None]) & (qpos[:, None] - kpos[None] <= NSA_WINDOW)
        return gqa_attend(qi, ki, vi, qpos, kpos, mask, slopes)[0]

    o = lax.map(band, (qb, jnp.arange(nb)))
    return o.swapaxes(0, 1).reshape(B, T, G, R, d)


def nsa_mixer(x, pos, w, past):
    norm_g, w_in, pe_k, pe_v, phi1_k, phi2_k, phi1_v, phi2_v, w_out = w
    B, T, _ = x.shape
    G, R, d = NSA_KV_GROUPS, NSA_REP, NSA_HEAD_DIM
    h = rmsnorm(x, norm_g)
    proj = jnp.einsum('btd,de->bte', h, w_in)
    q, kc, vc, ks, vs, kw, vw, gate, z = jnp.split(proj, _split_points(ODD_SIZES), axis=-1)
    q = q.reshape(B, T, G, R, d)
    kc, vc, ks, vs, kw, vw = [t.reshape(B, T, G, d) for t in (kc, vc, ks, vs, kw, vw)]
    gate = jax.nn.sigmoid(gate.astype(jnp.float32)).reshape(B, T, 3, G, R, 1)
    slopes = alibi_slopes(NSA_HEADS).reshape(G, R)
    new_rows = jnp.stack([kc, vc, ks, vs], axis=2)
    bi = jnp.arange(B)[:, None, None, None]
    gi = jnp.arange(G)[None, :, None, None]

    if past is None:
        kc_all, vc_all = kc, vc
    else:
        cache_kv, li, page_table, win = past
        old = cache_kv[li, page_table, :, 0:2].reshape(B, PAST_LEN, 2, G, d)
        kc_all = jnp.concatenate([old[:, :, 0], kc], axis=1)
        vc_all = jnp.concatenate([old[:, :, 1], vc], axis=1)
    kcb = nsa_compress(kc_all, pe_k, phi1_k, phi2_k)
    vcb = nsa_compress(vc_all, pe_v, phi1_v, phi2_v)
    cpos = jnp.arange(kcb.shape[1]) * NSA_BLOCK + (NSA_BLOCK - 1)
    o_cmp, p_cmp = gqa_attend(q, kcb, vcb, pos, cpos, cpos[None, :] <= pos[:, None], slopes)

    n_sel = -(-kc_all.shape[1] // NSA_BLOCK)
    idx = nsa_select(p_cmp, pos, n_sel, min(NSA_TOPK, n_sel))
    k_eff = idx.shape[-1]
    if past is None:
        nblk = T // NSA_BLOCK
        ksb = ks.reshape(B, nblk, NSA_BLOCK, G, d)
        vsb = vs.reshape(B, nblk, NSA_BLOCK, G, d)
        nch = T // NSA_QCHUNK
        qc = q.reshape(B, nch, NSA_QCHUNK, G, R, d).swapaxes(0, 1)
        ic = idx.reshape(B, G, nch, NSA_QCHUNK, k_eff).transpose(2, 0, 1, 3, 4)
        pc = pos.reshape(nch, NSA_QCHUNK)

        def sel_chunk(a):
            qi, ii, pi = a
            return sel_attend(qi, ksb[bi, ii, :, gi], vsb[bi, ii, :, gi], pi, ii, slopes)

        o_sel = lax.map(sel_chunk, (qc, ic, pc)).swapaxes(0, 1).reshape(B, T, G, R, d)
    else:
        nbp = PAST_LEN // NSA_BLOCK
        sub = PAGE_SIZE // NSA_BLOCK
        nbn = -(-T // NSA_BLOCK)
        pad = nbn * NSA_BLOCK - T
        ksn = jnp.pad(ks, ((0, 0), (0, pad), (0, 0), (0, 0))).reshape(B, nbn, NSA_BLOCK, G, d)
        vsn = jnp.pad(vs, ((0, 0), (0, pad), (0, 0), (0, 0))).reshape(B, nbn, NSA_BLOCK, G, d)
        is_past = (idx < nbp)[..., None, None]
        ip = jnp.minimum(idx, nbp - 1)
        page = page_table[bi, ip // sub][..., None]
        rows = (ip % sub)[..., None] * NSA_BLOCK + jnp.arange(NSA_BLOCK)
        gsel = gi[..., None]
        kp = cache_kv[li, page, rows, 2, gsel]
        vp = cache_kv[li, page, rows, 3, gsel]
        inew = jnp.clip(idx - nbp, 0, nbn - 1)
        kg = jnp.where(is_past, kp, ksn[bi, inew, :, gi])
        vg = jnp.where(is_past, vp, vsn[bi, inew, :, gi])
        o_sel = sel_attend(q, kg, vg, pos, idx, slopes)

    if past is None:
        o_win = window_prompt(q, kw, vw, slopes)
        wrows = jnp.stack([kw, vw], axis=2)[:, -NSA_WINDOW:]
        new_win = jnp.pad(wrows, ((0, 0), (NSA_WINDOW - wrows.shape[1], 0), (0, 0), (0, 0), (0, 0)))
    else:
        kw_all = jnp.concatenate([win[:, :, 0], kw], axis=1)
        vw_all = jnp.concatenate([win[:, :, 1], vw], axis=1)
        wpos = jnp.concatenate([PAST_LEN - NSA_WINDOW + jnp.arange(NSA_WINDOW, dtype=jnp.int32), pos])
        mask = (wpos[None] >= 0) & (wpos[None] <= pos[:, None]) & (pos[:, None] - wpos[None] <= NSA_WINDOW)
        o_win, _ = gqa_attend(q, kw_all, vw_all, pos, wpos, mask, slopes)
        new_win = jnp.stack([kw_all[:, -NSA_WINDOW:], vw_all[:, -NSA_WINDOW:]], axis=2)

    o = gate[:, :, 0] * o_cmp + gate[:, :, 1] * o_sel + gate[:, :, 2] * o_win
    o = o.reshape(B, T, NSA_WIDTH).astype(x.dtype) * jax.nn.silu(z)
    return x + o @ w_out, new_rows, new_win


def setup_inputs(seed: int = 0) -> dict:
    key = jax.random.key(seed)
    ks = list(jax.random.split(key, 40))
    f32 = jnp.float32

    def nrm(k, shape, scale):
        return jax.random.normal(k, shape, f32) * scale

    n_pages = PAST_LEN // PAGE_SIZE
    n_pool = (DEC_BATCH * n_pages * 5 + 3) // 4
    G, d = NSA_KV_GROUPS, NSA_HEAD_DIM
    inp = {}
    inp['x_prompt'] = nrm(ks[0], (BATCH, SEQ, D_MODEL), 1.0)
    inp['x_sample'] = nrm(ks[1], (DEC_BATCH, DEC_SEQ, D_MODEL), 1.0)
    inp['cache_mla'] = nrm(ks[2], (N_EVEN, n_pool, PAGE_SIZE, MLA_KV_LORA + MLA_ROPE), 1.0)
    inp['state_s5'] = nrm(ks[3], (N_EVEN, DEC_BATCH, S5_GROUPS, S5_STATE, 2), 0.1)
    inp['cache_nsa_kv'] = nrm(ks[4], (N_ODD, n_pool, PAGE_SIZE, 4, G, d), 1.0)
    inp['state_nsa_win'] = nrm(ks[5], (N_ODD, DEC_BATCH, NSA_WINDOW, 2, G, d), 1.0)
    perm = jax.random.permutation(ks[6], n_pool)
    inp['page_table'] = perm[:DEC_BATCH * n_pages].reshape(DEC_BATCH, n_pages).astype(jnp.int32)
    inp['norm_even'] = 1.0 + nrm(ks[7], (N_EVEN, D_MODEL), 0.05)
    inp['w_in_even'] = nrm(ks[8], (N_EVEN, D_MODEL, EVEN_IN), D_MODEL ** -0.5)
    inp['mla_g_q'] = 1.0 + nrm(ks[9], (N_EVEN, MLA_Q_LORA), 0.05)
    inp['mla_g_kv'] = 1.0 + nrm(ks[10], (N_EVEN, MLA_KV_LORA), 0.05)
    inp['mla_w_uq'] = nrm(ks[11], (N_EVEN, MLA_Q_LORA, MLA_HEADS, MLA_NOPE + MLA_ROPE), MLA_Q_LORA ** -0.5)
    inp['mla_w_uk'] = nrm(ks[12], (N_EVEN, MLA_KV_LORA, MLA_HEADS, MLA_NOPE), MLA_KV_LORA ** -0.5)
    inp['mla_w_uv'] = nrm(ks[13], (N_EVEN, MLA_KV_LORA, MLA_HEADS, MLA_V), MLA_KV_LORA ** -0.5)
    inp['s5_lambda_re'] = -0.5 + nrm(ks[14], (N_EVEN, S5_GROUPS, S5_STATE), 0.01)
    inp['s5_lambda_im'] = math.pi * jnp.arange(S5_STATE, dtype=f32)[None, None, :] + nrm(ks[15], (N_EVEN, S5_GROUPS, S5_STATE), 0.01)
    inp['s5_log_dt'] = jax.random.uniform(ks[16], (N_EVEN, S5_GROUPS), f32, math.log(1e-3), math.log(1e-1))
    inp['s5_b_re'] = nrm(ks[17], (N_EVEN, S5_GROUPS, S5_STATE, S5_GROUP_CH), (2 * S5_GROUP_CH) ** -0.5)
    inp['s5_b_im'] = nrm(ks[18], (N_EVEN, S5_GROUPS, S5_STATE, S5_GROUP_CH), (2 * S5_GROUP_CH) ** -0.5)
    inp['s5_c_re'] = nrm(ks[19], (N_EVEN, S5_GROUPS, S5_GROUP_CH, S5_STATE), (2 * S5_STATE) ** -0.5)
    inp['s5_c_im'] = nrm(ks[20], (N_EVEN, S5_GROUPS, S5_GROUP_CH, S5_STATE), (2 * S5_STATE) ** -0.5)
    inp['s5_d'] = nrm(ks[21], (N_EVEN, S5_GROUPS, S5_GROUP_CH), 1.0)
    inp['s5_w_glu'] = nrm(ks[22], (N_EVEN, S5_WIDTH, S5_WIDTH), S5_WIDTH ** -0.5)
    inp['s5_b_glu'] = nrm(ks[23], (N_EVEN, S5_WIDTH), 0.01)
    inp['w_out_even'] = nrm(ks[24], (N_EVEN, MLA_WIDTH + S5_WIDTH, D_MODEL), (MLA_WIDTH + S5_WIDTH) ** -0.5)
    inp['norm_odd'] = 1.0 + nrm(ks[25], (N_ODD, D_MODEL), 0.05)
    inp['w_in_odd'] = nrm(ks[26], (N_ODD, D_MODEL, ODD_IN), D_MODEL ** -0.5)
    inp['nsa_pe_k'] = nrm(ks[27], (N_ODD, NSA_BLOCK, d), 0.5)
    inp['nsa_pe_v'] = nrm(ks[28], (N_ODD, NSA_BLOCK, d), 0.5)
    inp['nsa_phi1_k'] = nrm(ks[29], (N_ODD, NSA_BLOCK * d, d), (NSA_BLOCK * d) ** -0.5)
    inp['nsa_phi2_k'] = nrm(ks[30], (N_ODD, d, d), d ** -0.5)
    inp['nsa_phi1_v'] = nrm(ks[31], (N_ODD, NSA_BLOCK * d, d), (NSA_BLOCK * d) ** -0.5)
    inp['nsa_phi2_v'] = nrm(ks[32], (N_ODD, d, d), d ** -0.5)
    inp['w_out_odd'] = nrm(ks[33], (N_ODD, NSA_WIDTH, D_MODEL), NSA_WIDTH ** -0.5)
    inp['norm_final'] = 1.0 + nrm(ks[34], (D_MODEL,), 0.05)
    return inp


def reference(x_prompt, x_sample, cache_mla, state_s5, cache_nsa_kv, state_nsa_win, page_table,
              norm_even, w_in_even, mla_g_q, mla_g_kv, mla_w_uq, mla_w_uk, mla_w_uv,
              s5_lambda_re, s5_lambda_im, s5_log_dt, s5_b_re, s5_b_im, s5_c_re, s5_c_im, s5_d,
              s5_w_glu, s5_b_glu, w_out_even,
              norm_odd, w_in_odd, nsa_pe_k, nsa_pe_v, nsa_phi1_k, nsa_phi2_k, nsa_phi1_v, nsa_phi2_v,
              w_out_odd, norm_final):
    pos_p = jnp.arange(x_prompt.shape[1], dtype=jnp.int32)
    pos_s = PAST_LEN + jnp.arange(x_sample.shape[1], dtype=jnp.int32)
    db = x_sample.shape[0]
    xp, xs = x_prompt, x_sample
    mla_p, mla_s, s5_p, s5_s, nsa_p, nsa_s, win_p, win_s = [], [], [], [], [], [], [], []
    for layer in range(DEPTH):
        li = layer // 2
        if layer % 2 == 0:
            w = (norm_even[li], w_in_even[li], mla_g_q[li], mla_g_kv[li], mla_w_uq[li], mla_w_uk[li],
                 mla_w_uv[li], s5_lambda_re[li], s5_lambda_im[li], s5_log_dt[li], s5_b_re[li], s5_b_im[li],
                 s5_c_re[li], s5_c_im[li], s5_d[li], s5_w_glu[li], s5_b_glu[li], w_out_even[li])
            xp, rows_p, st_p = even_mixer(xp, pos_p, w, None, None)
            mla_past = cache_mla[li, page_table].reshape(db, PAST_LEN, MLA_KV_LORA + MLA_ROPE)
            st = state_s5[li].astype(jnp.float32)
            h0 = lax.complex(st[..., 0], st[..., 1])
            xs, rows_s, st_s = even_mixer(xs, pos_s, w, mla_past, h0)
            mla_p.append(rows_p)
            mla_s.append(rows_s)
            s5_p.append(st_p)
            s5_s.append(st_s)
        else:
            w = (norm_odd[li], w_in_odd[li], nsa_pe_k[li], nsa_pe_v[li], nsa_phi1_k[li], nsa_phi2_k[li],
                 nsa_phi1_v[li], nsa_phi2_v[li], w_out_odd[li])
            xp, rows_p, wn_p = nsa_mixer(xp, pos_p, w, None)
            xs, rows_s, wn_s = nsa_mixer(xs, pos_s, w, (cache_nsa_kv, li, page_table, state_nsa_win[li]))
            nsa_p.append(rows_p)
            nsa_s.append(rows_s)
            win_p.append(wn_p)
            win_s.append(wn_s)
    y_prompt = rmsnorm(xp, norm_final)
    y_sample = rmsnorm(xs, norm_final)
    return (y_prompt, y_sample, jnp.stack(mla_p), jnp.stack(mla_s), jnp.stack(s5_p), jnp.stack(s5_s),
            jnp.stack(nsa_p), jnp.stack(nsa_s), jnp.stack(win_p), jnp.stack(win_s))
```

```python
import functools
import math

import jax
import jax.numpy as jnp
from jax import lax
from jax.experimental import pallas as pl
from jax.experimental.pallas import tpu as pltpu

F32, BF16, I32 = jnp.float32, jnp.bfloat16, jnp.int32

D_MODEL = 1024
PAGE = 128
EPS = 1e-6
ROPE_THETA = 10000.0
MLA_H, MLA_NOPE, MLA_ROPE, MLA_V = 8, 64, 32, 64
MLA_QL, MLA_KVL = 384, 256
MLA_ROW = MLA_KVL + MLA_ROPE
MLA_W = MLA_H * MLA_V
MLA_SCALE = (MLA_NOPE + MLA_ROPE) ** -0.5
S5_G, S5_P, S5_N = 32, 16, 64
S5_W = S5_G * S5_P
S5_S = S5_G * S5_N
S5_GT = 4
S5_NT = S5_G // S5_GT
NSA_H, NSA_G, NSA_D = 16, 2, 64
NSA_R = NSA_H // NSA_G
NSA_W = NSA_H * NSA_D
NSA_KVW = NSA_G * NSA_D
NSA_BLK, NSA_TOPK, NSA_WIN = 64, 16, 512
NSA_SCALE = NSA_D ** -0.5
FORCED_BONUS = float(NSA_R + 1)

NEG = -1e30
GONE = -3e38
TQ = 128
TK = 128
SUB = 8
VMEM_LIMIT = 56 * 1024 * 1024


def _dot(a, b):
    return jnp.dot(a, b, preferred_element_type=F32)


def _dot_nt(a, b):
    return lax.dot_general(a, b, (((1,), (1,)), ((), ())), preferred_element_type=F32)


def _dot_tn(a, b):
    return lax.dot_general(a, b, (((0,), (0,)), ((), ())), preferred_element_type=F32)


def _rms(x, g):
    return x * lax.rsqrt(jnp.mean(x * x, axis=-1, keepdims=True) + EPS) * g


def _silu(x):
    return x * jax.nn.sigmoid(x)


def _rope_nat(x, cosf, sinf):
    half = x.shape[1] // 2
    xs = jnp.concatenate([x[:, half:], x[:, :half]], axis=1)
    return x * cosf + xs * sinf


def _params(sem):
    return pltpu.CompilerParams(dimension_semantics=sem, vmem_limit_bytes=VMEM_LIMIT)


def _const_spec(shape):
    n = len(shape)
    return pl.BlockSpec(shape, lambda *a, _n=n: (0,) * _n)


def _even_in_kernel(x_ref, g_ref, wcq, wckv, wkr, wza, wu, wzb, gq, gkv, cos_ref, sin_ref,
                    cq_o, rows_o, za_o, u_o, zb_o):
    h = _rms(x_ref[...], g_ref[...]).astype(BF16)
    cq_o[...] = _rms(_dot(h, wcq[...]), gq[...])
    rows_o[:, :MLA_KVL] = _rms(_dot(h, wckv[...]), gkv[...])
    rows_o[:, MLA_KVL:] = _rope_nat(_dot(h, wkr[...]), cos_ref[...], sin_ref[...])
    za_o[...] = _dot(h, wza[...])
    u_o[...] = _dot(h, wu[...])
    zb_o[...] = _dot(h, wzb[...])


def _even_in(x2, wts, cosf, sinf, tm):
    m = x2.shape[0]
    tab_blocks = cosf.shape[0] // tm
    g, wcq, wckv, wkr, wza, wu, wzb, gq, gkv = wts
    row = lambda n: pl.BlockSpec((tm, n), lambda i: (i, 0))
    tab = pl.BlockSpec((tm, MLA_ROPE), lambda i: (i % tab_blocks, 0))
    consts = [g, wcq, wckv, wkr, wza, wu, wzb, gq, gkv]
    widths = (MLA_QL, MLA_ROW, MLA_W, S5_W, S5_W)
    return pl.pallas_call(
        _even_in_kernel,
        grid=(m // tm,),
        in_specs=[row(D_MODEL)] + [_const_spec(c.shape) for c in consts] + [tab, tab],
        out_specs=[row(n) for n in widths],
        out_shape=[jax.ShapeDtypeStruct((m, n), F32) for n in widths],
        compiler_params=_params(("parallel",)),
        name="even_in",
    )(x2, *consts, cosf, sinf)


def _mla_queries(cq, wuqn, wuqr, wuk, cosf, sinf, h):
    cqb = cq.astype(BF16)
    qn = _dot(cqb, wuqn[h])
    ql = _dot(qn.astype(BF16), wuk[h])
    qr = _rope_nat(_dot(cqb, wuqr[h]), cosf, sinf)
    return ql, qr


def _mla_prompt_kernel(cq_ref, rows_ref, za_ref, wuqn, wuqr, wuk, wuv, cos_ref, sin_ref,
                       out_ref, qtl, qtr, m_s, l_s, acc_s):
    qi = pl.program_id(1)
    r = MLA_H * TQ
    cq = cq_ref[...]
    cosf, sinf = cos_ref[...], sin_ref[...]
    for h in range(MLA_H):
        ql, qr = _mla_queries(cq, wuqn, wuqr, wuk, cosf, sinf, h)
        qtl[:, h * TQ:(h + 1) * TQ] = ql.T.astype(BF16)
        qtr[:, h * TQ:(h + 1) * TQ] = qr.T.astype(BF16)
    m_s[...] = jnp.full((1, r), NEG, F32)
    l_s[...] = jnp.zeros((1, r), F32)
    acc_s[...] = jnp.zeros((MLA_KVL, r), F32)
    qpos = qi * TQ + (lax.broadcasted_iota(I32, (1, r), 1) & (TQ - 1))
    krow = lax.broadcasted_iota(I32, (TK, 1), 0)

    def body(j, carry):
        kt = rows_ref[pl.ds(pl.multiple_of(j * TK, TK), TK), :]
        ckv = kt[:, :MLA_KVL].astype(BF16)
        kr = kt[:, MLA_KVL:].astype(BF16)
        s = (_dot(ckv, qtl[...]) + _dot(kr, qtr[...])) * MLA_SCALE
        s = jnp.where(j * TK + krow <= qpos, s, NEG)
        m_old = m_s[...]
        m_new = jnp.maximum(m_old, jnp.max(s, axis=0, keepdims=True))
        alpha = jnp.exp(m_old - m_new)
        p = jnp.exp(s - m_new)
        l_s[...] = alpha * l_s[...] + jnp.sum(p, axis=0, keepdims=True)
        acc_s[...] = alpha * acc_s[...] + _dot_tn(ckv, p.astype(BF16))
        m_s[...] = m_new
        return carry

    lax.fori_loop(0, qi + 1, body, 0)
    o = (acc_s[...] / l_s[...]).astype(BF16)
    heads = [_dot_tn(o[:, h * TQ:(h + 1) * TQ], wuv[h]) for h in range(MLA_H)]
    out_ref[...] = jnp.concatenate(heads, axis=1) * _silu(za_ref[...])


def _mla_prompt(cq, rows, za, wts, cosf, sinf):
    b, t, _ = cq.shape
    wuqn, wuqr, wuk, wuv = wts
    r = MLA_H * TQ
    tok = lambda n: pl.BlockSpec((None, TQ, n), lambda bi, qi: (bi, qi, 0))
    tab = pl.BlockSpec((TQ, MLA_ROPE), lambda bi, qi: (qi, 0))
    return pl.pallas_call(
        _mla_prompt_kernel,
        grid=(b, t // TQ),
        in_specs=[tok(MLA_QL), pl.BlockSpec((None, t, MLA_ROW), lambda bi, qi: (bi, 0, 0)), tok(MLA_W)]
        + [_const_spec(w.shape) for w in wts] + [tab, tab],
        out_specs=tok(MLA_W),
        out_shape=jax.ShapeDtypeStruct((b, t, MLA_W), F32),
        scratch_shapes=[pltpu.VMEM((MLA_KVL, r), BF16), pltpu.VMEM((MLA_ROPE, r), BF16),
                        pltpu.VMEM((1, r), F32), pltpu.VMEM((1, r), F32), pltpu.VMEM((MLA_KVL, r), F32)],
        compiler_params=_params(("parallel", "arbitrary")),
        name="mla_prompt",
    )(cq, rows, za, *wts, cosf, sinf)


def _mla_decode_kernel(n_pages_step, ts, pt_ref, cq_ref, rows_ref, za_ref, wuqn, wuqr, wuk, wuv,
                       cos_ref, sin_ref, *rest):
    pages = rest[:n_pages_step]
    out_ref, ql_s, qr_s, m_s, l_s, acc_s = rest[n_pages_step:]
    gi = pl.program_id(1)
    rows = MLA_H * SUB

    @pl.when(gi == 0)
    def _():
        cq = cq_ref[...]
        cosf, sinf = cos_ref[...], sin_ref[...]
        for h in range(MLA_H):
            ql, qr = _mla_queries(cq, wuqn, wuqr, wuk, cosf, sinf, h)
            ql_s[h * SUB:(h + 1) * SUB, :] = ql.astype(BF16)
            qr_s[h * SUB:(h + 1) * SUB, :] = qr.astype(BF16)
        m_s[...] = jnp.full((rows, 1), NEG, F32)
        l_s[...] = jnp.zeros((rows, 1), F32)
        acc_s[...] = jnp.zeros((rows, MLA_KVL), F32)

    def update(s, vals):
        m_old = m_s[...]
        m_new = jnp.maximum(m_old, jnp.max(s, axis=1, keepdims=True))
        alpha = jnp.exp(m_old - m_new)
        p = jnp.exp(s - m_new)
        l_s[...] = alpha * l_s[...] + jnp.sum(p, axis=1, keepdims=True)
        pv = None
        for (lo, hi), v in vals:
            term = _dot(p[:, lo:hi].astype(BF16), v)
            pv = term if pv is None else pv + term
        acc_s[...] = alpha * acc_s[...] + pv
        m_s[...] = m_new

    ql, qr = ql_s[...], qr_s[...]
    scores, vals = [], []
    for j, pg in enumerate(pages):
        k = pg[...]
        ckv = k[:, :MLA_KVL].astype(BF16)
        scores.append(_dot_nt(ql, ckv) + _dot_nt(qr, k[:, MLA_KVL:].astype(BF16)))
        vals.append(((j * PAGE, (j + 1) * PAGE), ckv))
    update(jnp.concatenate(scores, axis=1) * MLA_SCALE, vals)

    @pl.when(gi == pl.num_programs(1) - 1)
    def _():
        kn = rows_ref[...]
        ckv = kn[:, :MLA_KVL].astype(BF16)
        s = (_dot_nt(ql, ckv) + _dot_nt(qr, kn[:, MLA_KVL:].astype(BF16))) * MLA_SCALE
        tq = lax.broadcasted_iota(I32, (rows, SUB), 0) & (SUB - 1)
        jk = lax.broadcasted_iota(I32, (rows, SUB), 1)
        s = jnp.where((jk <= tq) & (jk < ts), s, NEG)
        update(s, [((0, SUB), ckv)])
        o = (acc_s[...] / l_s[...]).astype(BF16)
        heads = [_dot(o[h * SUB:(h + 1) * SUB, :], wuv[h]) for h in range(MLA_H)]
        out_ref[...] = jnp.concatenate(heads, axis=1) * _silu(za_ref[...])


def _mla_decode(page_table, cache, cq, rows, za, wts, cosf, sinf, ts, n_pages_step=16):
    db, n_pages = page_table.shape
    wuqn, wuqr, wuk, wuv = wts
    rows_n = MLA_H * SUB
    tok = lambda n: pl.BlockSpec((None, SUB, n), lambda b, g, pt: (b, 0, 0))
    cst = lambda shape: pl.BlockSpec(shape, lambda b, g, pt, _n=len(shape): (0,) * _n)
    page_specs = [
        pl.BlockSpec((None, PAGE, MLA_ROW), lambda b, g, pt, j=j: (pt[b, g * n_pages_step + j], 0, 0))
        for j in range(n_pages_step)]
    grid_spec = pltpu.PrefetchScalarGridSpec(
        num_scalar_prefetch=1,
        grid=(db, n_pages // n_pages_step),
        in_specs=[tok(MLA_QL), tok(MLA_ROW), tok(MLA_W)] + [cst(w.shape) for w in wts]
        + [cst(cosf.shape), cst(sinf.shape)] + page_specs,
        out_specs=tok(MLA_W),
        scratch_shapes=[pltpu.VMEM((rows_n, MLA_KVL), BF16), pltpu.VMEM((rows_n, MLA_ROPE), BF16),
                        pltpu.VMEM((rows_n, 1), F32), pltpu.VMEM((rows_n, 1), F32),
                        pltpu.VMEM((rows_n, MLA_KVL), F32)])
    return pl.pallas_call(
        functools.partial(_mla_decode_kernel, n_pages_step, ts),
        grid_spec=grid_spec,
        out_shape=jax.ShapeDtypeStruct((db, SUB, MLA_W), F32),
        compiler_params=_params(("parallel", "arbitrary")),
        name="mla_decode",
    )(page_table, cq, rows, za, *wts, cosf, sinf, *([cache] * n_pages_step))


def _s5_kernel(u_ref, zb_ref, lre_ref, lim_ref, ldt_ref, wbre, wbim, wcre, wcim, d_ref, wglu, bglu,
               h0re_ref, h0im_ref, mix_o, sre_o, sim_o, bure, buim, hre, him):
    step = pl.program_id(0)
    nb, chunk, _ = u_ref.shape
    rows = nb * chunk

    @pl.when(step == 0)
    def _():
        hre[...] = h0re_ref[...]
        him[...] = h0im_ref[...]

    lre, lim = lre_ref[...], lim_ref[...]
    dt = jnp.exp(ldt_ref[...])
    mag = jnp.exp(lre * dt)
    are, aim = mag * jnp.cos(lim * dt), mag * jnp.sin(lim * dt)
    den = lre * lre + lim * lim
    cre = ((are - 1.0) * lre + aim * lim) / den
    cim = (aim * lre - (are - 1.0) * lim) / den

    ut = jnp.swapaxes(u_ref[...], 0, 1).reshape(rows, S5_W)
    kw, nw = S5_GT * S5_P, S5_GT * S5_N
    for jt in range(S5_NT):
        cr, ci = cre[:, jt * nw:(jt + 1) * nw], cim[:, jt * nw:(jt + 1) * nw]
        bre = (cr * wbre[jt] - ci * wbim[jt]).astype(BF16)
        bim = (cr * wbim[jt] + ci * wbre[jt]).astype(BF16)
        uj = ut[:, jt * kw:(jt + 1) * kw].astype(BF16)
        bure[:, jt * nw:(jt + 1) * nw] = _dot(uj, bre)
        buim[:, jt * nw:(jt + 1) * nw] = _dot(uj, bim)

    def scan(t, carry):
        hr, hi = carry
        sl = pl.ds(pl.multiple_of(t * nb, SUB), nb)
        nr = are * hr - aim * hi + bure[sl, :]
        ni = are * hi + aim * hr + buim[sl, :]
        bure[sl, :] = nr
        buim[sl, :] = ni
        return nr, ni

    hr, hi = lax.fori_loop(0, chunk, scan, (hre[...], him[...]))
    hre[...] = hr
    him[...] = hi
    sre_o[...] = hr
    sim_o[...] = hi

    ys = []
    for jt in range(S5_NT):
        sr = bure[:, jt * nw:(jt + 1) * nw].astype(BF16)
        si = buim[:, jt * nw:(jt + 1) * nw].astype(BF16)
        ys.append(_dot(sr, wcre[jt].astype(BF16)) - _dot(si, wcim[jt].astype(BF16)))
    y = jnp.concatenate(ys, axis=1) + d_ref[...] * ut
    g5 = jax.nn.gelu(y)
    ob = g5 * jax.nn.sigmoid(_dot(g5.astype(BF16), wglu[...]) + bglu[...])
    mix_o[...] = jnp.swapaxes(ob.reshape(chunk, nb, S5_W), 0, 1) * _silu(zb_ref[...])


def _s5(u, zb, wts, h0re, h0im, chunk):
    nb, t, _ = u.shape
    tok = pl.BlockSpec((nb, chunk, S5_W), lambda i: (0, i, 0))
    st = _const_spec((nb, S5_S))
    return pl.pallas_call(
        _s5_kernel,
        grid=(t // chunk,),
        in_specs=[tok, tok] + [_const_spec(w.shape) for w in wts] + [st, st],
        out_specs=[tok, st, st],
        out_shape=[jax.ShapeDtypeStruct((nb, t, S5_W), F32), jax.ShapeDtypeStruct((nb, S5_S), F32),
                   jax.ShapeDtypeStruct((nb, S5_S), F32)],
        scratch_shapes=[pltpu.VMEM((nb * chunk, S5_S), F32), pltpu.VMEM((nb * chunk, S5_S), F32),
                        pltpu.VMEM((nb, S5_S), F32), pltpu.VMEM((nb, S5_S), F32)],
        compiler_params=_params(("arbitrary",)),
        name="s5",
    )(u, zb, *wts, h0re, h0im)


def _even_out_kernel(x_ref, a_ref, b_ref, wa, wb, out_ref):
    out_ref[...] = (x_ref[...] + _dot(a_ref[...].astype(BF16), wa[...])
                    + _dot(b_ref[...].astype(BF16), wb[...]))


def _even_out(x2, mixa, mixb, wa, wb, tm):
    m = x2.shape[0]
    row = lambda n: pl.BlockSpec((tm, n), lambda i: (i, 0))
    return pl.pallas_call(
        _even_out_kernel,
        grid=(m // tm,),
        in_specs=[row(D_MODEL), row(MLA_W), row(S5_W), _const_spec(wa.shape), _const_spec(wb.shape)],
        out_specs=row(D_MODEL),
        out_shape=jax.ShapeDtypeStruct((m, D_MODEL), F32),
        compiler_params=_params(("parallel",)),
        name="even_out",
    )(x2, mixa, mixb, wa, wb)


def _odd_in_kernel(x_ref, g_ref, wq, wkv4, wkw, wg, wz, q_o, kv4_o, kw_o, gate_o, z_o):
    h = _rms(x_ref[...], g_ref[...]).astype(BF16)
    q_o[...] = _dot(h, wq[...])
    kv4_o[...] = _dot(h, wkv4[...])
    kw_o[...] = _dot(h, wkw[...])
    gate_o[...] = jax.nn.sigmoid(_dot(h, wg[...]))
    z_o[...] = _dot(h, wz[...])


def _odd_in(x2, wts, tm):
    m = x2.shape[0]
    row = lambda n: pl.BlockSpec((tm, n), lambda i: (i, 0))
    widths = (NSA_W, 4 * NSA_KVW, 2 * NSA_KVW, 3 * NSA_H, NSA_W)
    return pl.pallas_call(
        _odd_in_kernel,
        grid=(m // tm,),
        in_specs=[row(D_MODEL)] + [_const_spec(w.shape) for w in wts],
        out_specs=[row(n) for n in widths],
        out_shape=[jax.ShapeDtypeStruct((m, n), F32) for n in widths],
        compiler_params=_params(("parallel",)),
        name="odd_in",
    )(x2, *wts)


def _compress_tail(acc, phi2):
    return _dot(_silu(acc).astype(BF16), phi2[...])


def _compress_prompt_kernel(x_ref, pe_ref, w1, phi2, out_ref):
    nblk = x_ref.shape[0]
    acc = jnp.zeros((nblk, 2 * NSA_KVW), F32)
    for r in range(NSA_BLK):
        xr = (x_ref[:, r, :] + pe_ref[r:r + 1, :]).astype(BF16)
        acc = acc + _dot(xr, w1[r])
    out_ref[...] = _compress_tail(acc, phi2)


def _compress_prompt(kv4_blocks, pe, w1, phi2):
    nblk = kv4_blocks.shape[0]
    cw = 2 * NSA_KVW
    return pl.pallas_call(
        _compress_prompt_kernel,
        grid=(1,),
        in_specs=[pl.BlockSpec((nblk, NSA_BLK, cw), lambda i: (0, 0, 0)),
                  _const_spec(pe.shape), _const_spec(w1.shape), _const_spec(phi2.shape)],
        out_specs=pl.BlockSpec((nblk, cw), lambda i: (0, 0)),
        out_shape=jax.ShapeDtypeStruct((nblk, cw), F32),
        compiler_params=_params(("arbitrary",)),
        name="compress_prompt",
    )(kv4_blocks, pe, w1, phi2)


def _compress_decode_kernel(n_pages_step, pt_ref, pe_ref, w1, phi2, *rest):
    pages = rest[:n_pages_step]
    out_ref, xs = rest[n_pages_step:]
    gi = pl.program_id(1)
    bpp = PAGE // NSA_BLK
    grp = 16 // bpp
    pe = pe_ref[...]
    for k in range(n_pages_step // grp):
        x = jnp.concatenate([pages[k * grp + j][...] for j in range(grp)], axis=0) + pe[None]
        base = pl.multiple_of((gi * (n_pages_step // grp) + k) * 16, 16)
        xs[:, pl.ds(base, 16), :] = jnp.swapaxes(x, 0, 1).astype(BF16)

    @pl.when(gi == pl.num_programs(1) - 1)
    def _():
        nblk = xs.shape[1]
        acc = jnp.zeros((nblk, 2 * NSA_KVW), F32)
        for r in range(NSA_BLK):
            acc = acc + _dot(xs[r], w1[r])
        out_ref[...] = _compress_tail(acc, phi2)


def _compress_decode(page_table, cache_blocks, pe, w1, phi2, n_pages_step=16):
    db, n_pages = page_table.shape
    bpp = PAGE // NSA_BLK
    cw = 2 * NSA_KVW
    nblk = n_pages * bpp
    cst = lambda shape: pl.BlockSpec(shape, lambda b, g, pt, _n=len(shape): (0,) * _n)
    page_specs = [
        pl.BlockSpec((bpp, NSA_BLK, cw), lambda b, g, pt, j=j: (pt[b, g * n_pages_step + j], 0, 0))
        for j in range(n_pages_step)]
    grid_spec = pltpu.PrefetchScalarGridSpec(
        num_scalar_prefetch=1,
        grid=(db, n_pages // n_pages_step),
        in_specs=[cst(pe.shape), cst(w1.shape), cst(phi2.shape)] + page_specs,
        out_specs=pl.BlockSpec((None, nblk, cw), lambda b, g, pt: (b, 0, 0)),
        scratch_shapes=[pltpu.VMEM((NSA_BLK, nblk, cw), BF16)])
    return pl.pallas_call(
        functools.partial(_compress_decode_kernel, n_pages_step),
        grid_spec=grid_spec,
        out_shape=jax.ShapeDtypeStruct((db, nblk, cw), F32),
        compiler_params=_params(("parallel", "arbitrary")),
        name="compress_decode",
    )(page_table, pe, w1, phi2, *([cache_blocks] * n_pages_step))


def _alibi_slope(h):
    return 2.0 ** (-8.0 * (h + 1) / NSA_H)


def _cmp_kernel(pos0, n_sel, q_ref, cb_ref, oc_o, sel_o, idx_o):
    ti = pl.program_id(1)
    tt = q_ref.shape[0]
    nc = cb_ref.shape[0]
    nsp = sel_o.shape[1]
    qpos = pos0 + ti * tt + lax.broadcasted_iota(I32, (1, tt), 1)
    cpos = lax.broadcasted_iota(I32, (nc, 1), 0) * NSA_BLK + (NSA_BLK - 1)
    visible = cpos <= qpos
    dist = (qpos - cpos).astype(F32)
    q = q_ref[...] * NSA_SCALE
    cb = cb_ref[...]
    blk = lax.broadcasted_iota(I32, (nsp, 1), 0)
    cur = qpos // NSA_BLK
    forced = (blk == 0) | (blk == cur) | (blk == cur - 1)
    allowed = (blk <= cur) & (blk < n_sel)
    for g in range(NSA_G):
        kc = cb[:, g * NSA_D:(g + 1) * NSA_D]
        vc = cb[:, NSA_KVW + g * NSA_D:NSA_KVW + (g + 1) * NSA_D].astype(BF16)
        imp = jnp.zeros((nc, tt), F32)
        for r in range(NSA_R):
            h = g * NSA_R + r
            qh = q[:, h * NSA_D:(h + 1) * NSA_D]
            s = lax.dot_general(kc, qh, (((1,), (1,)), ((), ())), precision=lax.Precision.HIGHEST,
                                preferred_element_type=F32)
            s = jnp.where(visible, s - _alibi_slope(h) * dist, NEG)
            mx = jnp.max(s, axis=0, keepdims=True)
            e = jnp.where(visible, jnp.exp(s - mx), 0.0)
            den = jnp.sum(e, axis=0, keepdims=True)
            p = e / jnp.where(den > 0, den, 1.0)
            imp = imp + p
            oc_o[h * NSA_D:(h + 1) * NSA_D, :] = _dot_tn(vc, p.astype(BF16))
        if nsp > nc:
            imp = jnp.concatenate([imp, jnp.zeros((nsp - nc, tt), F32)], axis=0)
        score = jnp.where(allowed, imp + jnp.where(forced, FORCED_BONUS, 0.0), NEG)
        chosen = jnp.zeros((nsp, tt), F32)
        picks = []
        for _ in range(NSA_TOPK):
            mx = jnp.max(score, axis=0, keepdims=True)
            first = jnp.min(jnp.where(score == mx, blk, nsp), axis=0, keepdims=True)
            hit = blk == first
            chosen = jnp.where(hit, 1.0, chosen)
            score = jnp.where(hit, GONE, score)
            picks.append(first)
        sel_o[g] = chosen
        idx_o[g] = jnp.concatenate(picks, axis=0)


def _cmp(q, cb, pos0, n_sel, tt):
    b, t, _ = q.shape
    nc = cb.shape[1]
    nsp = -(-n_sel // SUB) * SUB
    return pl.pallas_call(
        functools.partial(_cmp_kernel, pos0, n_sel),
        grid=(b, t // tt),
        in_specs=[pl.BlockSpec((None, tt, NSA_W), lambda bi, ti: (bi, ti, 0)),
                  pl.BlockSpec((None, nc, 2 * NSA_KVW), lambda bi, ti: (bi, 0, 0))],
        out_specs=[pl.BlockSpec((None, NSA_W, tt), lambda bi, ti: (bi, 0, ti)),
                   pl.BlockSpec((None, NSA_G, nsp, tt), lambda bi, ti: (bi, 0, 0, ti)),
                   pl.BlockSpec((None, NSA_G, NSA_TOPK, tt), lambda bi, ti: (bi, 0, 0, ti))],
        out_shape=[jax.ShapeDtypeStruct((b, NSA_W, t), F32),
                   jax.ShapeDtypeStruct((b, NSA_G, nsp, t), F32),
                   jax.ShapeDtypeStruct((b, NSA_G, NSA_TOPK, t), I32)],
        compiler_params=_params(("parallel", "parallel")),
        name="nsa_cmp",
    )(q, cb)


def _nsa_prompt_kernel(q_ref, kv4_ref, kw_ref, oc_ref, sel_ref, gate_ref, out_ref, m_s, l_s, acc_s):
    qi = pl.program_id(1)
    r = NSA_R * TQ
    bpt = TK // NSA_BLK
    qt = (q_ref[...] * NSA_SCALE).T.astype(BF16)
    gt = gate_ref[...].T
    lane = lax.broadcasted_iota(I32, (1, r), 1)
    qpos = qi * TQ + (lane & (TQ - 1))
    head = lane // TQ
    krow = lax.broadcasted_iota(I32, (TK, 1), 0)

    def attend(lo, hi, k_of, v_of, mask_of, slope):
        m_s[...] = jnp.full((1, r), NEG, F32)
        l_s[...] = jnp.zeros((1, r), F32)
        acc_s[...] = jnp.zeros((NSA_D, r), F32)

        def body(j, carry):
            base = pl.multiple_of(j * TK, TK)
            kpos = j * TK + krow
            s = _dot(k_of(base).astype(BF16), qg)
            s = s + slope * (kpos - qi * TQ).astype(F32)
            s = jnp.where(mask_of(j, kpos), s, NEG)
            m_old = m_s[...]
            m_new = jnp.maximum(m_old, jnp.max(s, axis=0, keepdims=True))
            alpha = jnp.exp(m_old - m_new)
            p = jnp.exp(s - m_new)
            l_s[...] = alpha * l_s[...] + jnp.sum(p, axis=0, keepdims=True)
            acc_s[...] = alpha * acc_s[...] + _dot_tn(v_of(base).astype(BF16), p.astype(BF16))
            m_s[...] = m_new
            return carry

        lax.fori_loop(lo, hi, body, 0)
        return acc_s[...] / l_s[...]

    for g in range(NSA_G):
        qg = jnp.concatenate(
            [qt[(g * NSA_R + i) * NSA_D:(g * NSA_R + i + 1) * NSA_D, :] for i in range(NSA_R)], axis=1)
        slope = jnp.zeros((1, r), F32)
        for i in range(NSA_R):
            slope = jnp.where(head == i, _alibi_slope(g * NSA_R + i), slope)
        ks_lo, vs_lo = 2 * NSA_KVW + g * NSA_D, 3 * NSA_KVW + g * NSA_D
        kw_lo, vw_lo = g * NSA_D, NSA_KVW + g * NSA_D

        def sel_mask(j, kpos, g=g):
            srow = sel_ref[g, pl.ds(j * bpt, bpt), :]
            flags = [jnp.concatenate([srow[i:i + 1, :]] * NSA_R, axis=1) for i in range(bpt)]
            picked = flags[bpt - 1]
            for i in range(bpt - 2, -1, -1):
                picked = jnp.where(krow < (i + 1) * NSA_BLK, flags[i], picked)
            return (picked > 0.5) & (kpos <= qpos)

        def win_mask(j, kpos):
            return (kpos <= qpos) & (qpos - kpos <= NSA_WIN)

        o_sel = attend(0, qi + 1,
                       lambda b0: kv4_ref[pl.ds(b0, TK), ks_lo:ks_lo + NSA_D],
                       lambda b0: kv4_ref[pl.ds(b0, TK), vs_lo:vs_lo + NSA_D], sel_mask, slope)
        o_win = attend(jnp.maximum(qi - NSA_WIN // TK, 0), qi + 1,
                       lambda b0: kw_ref[pl.ds(b0, TK), kw_lo:kw_lo + NSA_D],
                       lambda b0: kw_ref[pl.ds(b0, TK), vw_lo:vw_lo + NSA_D], win_mask, slope)
        for i in range(NSA_R):
            h = g * NSA_R + i
            rows = slice(h * NSA_D, (h + 1) * NSA_D)
            cols = slice(i * TQ, (i + 1) * TQ)
            out_ref[rows, :] = (gt[h:h + 1, :] * oc_ref[rows, :]
                                + gt[NSA_H + h:NSA_H + h + 1, :] * o_sel[:, cols]
                                + gt[2 * NSA_H + h:2 * NSA_H + h + 1, :] * o_win[:, cols])


def _nsa_prompt(q, kv4, kw, oc_t, sel, gate):
    b, t, _ = q.shape
    nsp = sel.shape[2]
    r = NSA_R * TQ
    tok = lambda n: pl.BlockSpec((None, TQ, n), lambda bi, qi: (bi, qi, 0))
    full = lambda n: pl.BlockSpec((None, t, n), lambda bi, qi: (bi, 0, 0))
    return pl.pallas_call(
        _nsa_prompt_kernel,
        grid=(b, t // TQ),
        in_specs=[tok(NSA_W), full(4 * NSA_KVW), full(2 * NSA_KVW),
                  pl.BlockSpec((None, NSA_W, TQ), lambda bi, qi: (bi, 0, qi)),
                  pl.BlockSpec((None, NSA_G, nsp, TQ), lambda bi, qi: (bi, 0, 0, qi)),
                  tok(3 * NSA_H)],
        out_specs=pl.BlockSpec((None, NSA_W, TQ), lambda bi, qi: (bi, 0, qi)),
        out_shape=jax.ShapeDtypeStruct((b, NSA_W, t), F32),
        scratch_shapes=[pltpu.VMEM((1, r), F32), pltpu.VMEM((1, r), F32), pltpu.VMEM((NSA_D, r), F32)],
        compiler_params=_params(("parallel", "arbitrary")),
        name="nsa_prompt",
    )(q, kv4, kw, oc_t, sel, gate)


def _softmax_rows(parts):
    mx = None
    for s, _ in parts:
        pm = jnp.max(s, axis=1, keepdims=True)
        mx = pm if mx is None else jnp.maximum(mx, pm)
    den, num = None, None
    for s, v in parts:
        p = jnp.exp(s - mx)
        d = jnp.sum(p, axis=1, keepdims=True)
        n = _dot(p.astype(BF16), v)
        den = d if den is None else den + d
        num = n if num is None else num + n
    return num / den


def _nsa_sel_decode_kernel(ts, pos0, nbp, idx_ref, pt_ref, q_ref, new_ref, *rest):
    nblk = NSA_G * NSA_TOPK
    blocks = rest[:nblk]
    out_ref = rest[nblk]
    b, t = pl.program_id(0), pl.program_id(1)
    q = q_ref[...] * NSA_SCALE
    new = new_ref[...]
    lane = lax.broadcasted_iota(I32, (1, NSA_BLK), 1)
    jn = lax.broadcasted_iota(I32, (1, SUB), 1)
    for g in range(NSA_G):
        qg = q[g * NSA_R:(g + 1) * NSA_R, :].astype(BF16)
        slope = jnp.concatenate(
            [jnp.full((1, 1), _alibi_slope(g * NSA_R + i), F32) for i in range(NSA_R)], axis=0)
        parts = []
        has_new = jnp.zeros((), I32)
        for k in range(NSA_TOPK):
            bid = idx_ref[((b * NSA_G + g) * ts + t) * NSA_TOPK + k]
            blkv = blocks[g * NSA_TOPK + k][...]
            kk = blkv[:, g * NSA_D:(g + 1) * NSA_D].astype(BF16)
            vv = blkv[:, NSA_KVW + g * NSA_D:NSA_KVW + (g + 1) * NSA_D].astype(BF16)
            rel = (bid * NSA_BLK - pos0 + lane).astype(F32)
            s = _dot_nt(qg, kk) + slope * rel
            parts.append((jnp.where(bid < nbp, s, NEG), vv))
            has_new = has_new + (bid >= nbp).astype(I32)
        kn = new[:, 2 * NSA_KVW + g * NSA_D:2 * NSA_KVW + (g + 1) * NSA_D].astype(BF16)
        vn = new[:, 3 * NSA_KVW + g * NSA_D:3 * NSA_KVW + (g + 1) * NSA_D].astype(BF16)
        sn = _dot_nt(qg, kn) + slope * jn.astype(F32)
        ok = (jn <= t) & (jn < ts) & (has_new > 0)
        parts.append((jnp.where(ok, sn, NEG), vn))
        out_ref[g * NSA_R:(g + 1) * NSA_R, :] = _softmax_rows(parts)


def _nsa_sel_decode(idx_flat, page_table, cache_blocks, q4, kv4_new, ts, pos0):
    db, n_pages = page_table.shape
    bpp = PAGE // NSA_BLK
    nbp = n_pages * bpp
    cw = 2 * NSA_KVW

    def blk_map(b, t, idx, pt, g, k):
        bid = jnp.minimum(idx[((b * NSA_G + g) * ts + t) * NSA_TOPK + k], nbp - 1)
        return (pt[b, bid // bpp] * bpp + bid % bpp, 0, 1)

    blk_specs = [pl.BlockSpec((None, NSA_BLK, cw), functools.partial(blk_map, g=g, k=k))
                 for g in range(NSA_G) for k in range(NSA_TOPK)]
    grid_spec = pltpu.PrefetchScalarGridSpec(
        num_scalar_prefetch=2,
        grid=(db, ts),
        in_specs=[pl.BlockSpec((None, None, NSA_H, NSA_D), lambda b, t, idx, pt: (b, t, 0, 0)),
                  pl.BlockSpec((None, SUB, 4 * NSA_KVW), lambda b, t, idx, pt: (b, 0, 0))] + blk_specs,
        out_specs=pl.BlockSpec((None, None, NSA_H, NSA_D), lambda b, t, idx, pt: (b, t, 0, 0)))
    return pl.pallas_call(
        functools.partial(_nsa_sel_decode_kernel, ts, pos0, nbp),
        grid_spec=grid_spec,
        out_shape=jax.ShapeDtypeStruct((db, ts, NSA_H, NSA_D), F32),
        compiler_params=_params(("parallel", "arbitrary")),
        name="nsa_sel_decode",
    )(idx_flat, page_table, q4, kv4_new, *([cache_blocks] * (NSA_G * NSA_TOPK)))


def _nsa_win_decode_kernel(ts, q_ref, win_ref, new_ref, out_ref):
    rows = NSA_R * SUB
    win = win_ref[...]
    new = new_ref[...]
    tq = lax.broadcasted_iota(I32, (rows, 1), 0) & (SUB - 1)
    iw = lax.broadcasted_iota(I32, (1, NSA_WIN), 1)
    jn = lax.broadcasted_iota(I32, (1, SUB), 1)
    for g in range(NSA_G):
        qg = jnp.concatenate([q_ref[:, g * NSA_R + i, :] for i in range(NSA_R)], axis=0)
        qg = (qg * NSA_SCALE).astype(BF16)
        slope = jnp.concatenate(
            [jnp.full((SUB, 1), _alibi_slope(g * NSA_R + i), F32) for i in range(NSA_R)], axis=0)
        kw = win[:, g * NSA_D:(g + 1) * NSA_D].astype(BF16)
        vw = win[:, NSA_KVW + g * NSA_D:NSA_KVW + (g + 1) * NSA_D].astype(BF16)
        kn = new[:, g * NSA_D:(g + 1) * NSA_D].astype(BF16)
        vn = new[:, NSA_KVW + g * NSA_D:NSA_KVW + (g + 1) * NSA_D].astype(BF16)
        sw = _dot_nt(qg, kw) + slope * (iw - NSA_WIN).astype(F32)
        sw = jnp.where(iw >= tq, sw, NEG)
        sn = _dot_nt(qg, kn) + slope * jn.astype(F32)
        sn = jnp.where((jn <= tq) & (jn < ts), sn, NEG)
        o = _softmax_rows([(sw, vw), (sn, vn)])
        for i in range(NSA_R):
            out_ref[:, g * NSA_R + i, :] = o[i * SUB:(i + 1) * SUB, :]


def _nsa_win_decode(q4, win, kw_new, ts):
    db = q4.shape[0]
    return pl.pallas_call(
        functools.partial(_nsa_win_decode_kernel, ts),
        grid=(db,),
        in_specs=[pl.BlockSpec((None, SUB, NSA_H, NSA_D), lambda b: (b, 0, 0, 0)),
                  pl.BlockSpec((None, NSA_WIN, 2 * NSA_KVW), lambda b: (b, 0, 0)),
                  pl.BlockSpec((None, SUB, 2 * NSA_KVW), lambda b: (b, 0, 0))],
        out_specs=pl.BlockSpec((None, SUB, NSA_H, NSA_D), lambda b: (b, 0, 0, 0)),
        out_shape=jax.ShapeDtypeStruct((db, SUB, NSA_H, NSA_D), F32),
        compiler_params=_params(("parallel",)),
        name="nsa_win_decode",
    )(q4, win, kw_new)


def _odd_tail(o, z_ref, x_ref, wout, gfin):
    y = x_ref[...] + _dot((o * _silu(z_ref[...])).astype(BF16), wout[...])
    return _rms(y, gfin[...])


def _odd_out_prompt_kernel(ot_ref, z_ref, x_ref, wout, gfin, out_ref):
    out_ref[...] = _odd_tail(ot_ref[...].T, z_ref, x_ref, wout, gfin)


def _odd_out_prompt(ot, z, x, wout, gfin, tm):
    b, _, t = ot.shape
    tok = pl.BlockSpec((None, tm, D_MODEL), lambda bi, ti: (bi, ti, 0))
    return pl.pallas_call(
        _odd_out_prompt_kernel,
        grid=(b, t // tm),
        in_specs=[pl.BlockSpec((None, NSA_W, tm), lambda bi, ti: (bi, 0, ti)), tok, tok,
                  _const_spec(wout.shape), _const_spec(gfin.shape)],
        out_specs=tok,
        out_shape=jax.ShapeDtypeStruct((b, t, D_MODEL), F32),
        compiler_params=_params(("parallel", "parallel")),
        name="odd_out_prompt",
    )(ot, z, x, wout, gfin)


def _odd_out_decode_kernel(oc_ref, os_ref, ow_ref, gate_ref, z_ref, x_ref, wout, gfin, out_ref):
    gate = gate_ref[...]
    oc, osel, ow = oc_ref[...], os_ref[...], ow_ref[...]
    heads = []
    for h in range(NSA_H):
        c = slice(h * NSA_D, (h + 1) * NSA_D)
        heads.append(gate[:, h:h + 1] * oc[:, c] + gate[:, NSA_H + h:NSA_H + h + 1] * osel[:, c]
                     + gate[:, 2 * NSA_H + h:2 * NSA_H + h + 1] * ow[:, c])
    out_ref[...] = _odd_tail(jnp.concatenate(heads, axis=1), z_ref, x_ref, wout, gfin)


def _odd_out_decode(oc, osel, ow, gate, z, x, wout, gfin):
    m = x.shape[0]
    full = lambda a: _const_spec(a.shape)
    args = (oc, osel, ow, gate, z, x, wout, gfin)
    return pl.pallas_call(
        _odd_out_decode_kernel,
        grid=(1,),
        in_specs=[full(a) for a in args],
        out_specs=_const_spec((m, D_MODEL)),
        out_shape=jax.ShapeDtypeStruct((m, D_MODEL), F32),
        compiler_params=_params(("arbitrary",)),
        name="odd_out_decode",
    )(*args)


def _rope_tables(pos):
    half = MLA_ROPE // 2
    inv = ROPE_THETA ** (-jnp.arange(half, dtype=F32) / half)
    ang = pos.astype(F32)[:, None] * inv[None, :]
    cos, sin = jnp.cos(ang), jnp.sin(ang)
    return jnp.concatenate([cos, cos], axis=1), jnp.concatenate([-sin, sin], axis=1)


def _block_diag(x):
    t, g, r, c = x.shape
    eye = jnp.eye(g, dtype=x.dtype)
    return jnp.einsum("tgrc,gh->tgrhc", x, eye).reshape(t, g * r, g * c)


def _even_weights(norm_g, w_in, g_q, g_kv, w_uq, w_uk, w_uv):
    edges = [0, MLA_QL, MLA_QL + MLA_KVL, MLA_QL + MLA_ROW]
    edges += [edges[-1] + MLA_W, edges[-1] + MLA_W + S5_W, edges[-1] + MLA_W + 2 * S5_W]
    wb = w_in.astype(BF16)
    pieces = [wb[:, edges[i]:edges[i + 1]] for i in range(6)]
    in_w = (norm_g[None, :], *pieces, g_q[None, :], g_kv[None, :])
    uq = jnp.transpose(w_uq, (1, 0, 2)).astype(BF16)
    mla_w = (uq[:, :, :MLA_NOPE], uq[:, :, MLA_NOPE:],
             jnp.transpose(w_uk, (1, 2, 0)).astype(BF16),
             jnp.transpose(w_uv, (1, 0, 2)).astype(BF16))
    return in_w, mla_w


def _s5_weights(lam_re, lam_im, log_dt, b_re, b_im, c_re, c_im, d_skip, w_glu, b_glu):
    def bmat(b):
        return _block_diag(jnp.transpose(b.reshape(S5_NT, S5_GT, S5_N, S5_P), (0, 1, 3, 2)))

    def cmat(c):
        return _block_diag(jnp.transpose(c.reshape(S5_NT, S5_GT, S5_P, S5_N), (0, 1, 3, 2)))

    return (lam_re.reshape(1, S5_S), lam_im.reshape(1, S5_S),
            jnp.repeat(log_dt, S5_N).reshape(1, S5_S),
            bmat(b_re), bmat(b_im), cmat(c_re), cmat(c_im),
            d_skip.reshape(1, S5_W), w_glu.astype(BF16), b_glu[None, :])


def _odd_weights(norm_g, w_in, pe_k, pe_v, phi1_k, phi2_k, phi1_v, phi2_v):
    wb = w_in.astype(BF16)
    e0 = NSA_W
    e1 = e0 + 4 * NSA_KVW
    e2 = e1 + 2 * NSA_KVW
    e3 = e2 + 3 * NSA_H
    in_w = (norm_g[None, :], wb[:, :e0], wb[:, e0:e1], wb[:, e1:e2], wb[:, e2:e3], wb[:, e3:])
    pe = jnp.concatenate([pe_k, pe_k, pe_v, pe_v], axis=1)
    p1k = phi1_k.reshape(NSA_BLK, NSA_D, NSA_D)
    p1v = phi1_v.reshape(NSA_BLK, NSA_D, NSA_D)
    w1 = _block_diag(jnp.stack([p1k, p1k, p1v, p1v], axis=1)).astype(BF16)
    phi2 = _block_diag(jnp.stack([phi2_k, phi2_k, phi2_v, phi2_v], axis=0)[None])[0].astype(BF16)
    return in_w, (pe, w1, phi2)


def _pad_tokens(x, n):
    return jnp.pad(x, ((0, 0), (0, n - x.shape[1])) + ((0, 0),) * (x.ndim - 2))


def kernel(x_prompt, x_sample, cache_mla, state_s5, cache_nsa_kv, state_nsa_win, page_table, norm_even, w_in_even, mla_g_q, mla_g_kv, mla_w_uq, mla_w_uk, mla_w_uv, s5_lambda_re, s5_lambda_im, s5_log_dt, s5_b_re, s5_b_im, s5_c_re, s5_c_im, s5_d, s5_w_glu, s5_b_glu, w_out_even, norm_odd, w_in_odd, nsa_pe_k, nsa_pe_v, nsa_phi1_k, nsa_phi2_k, nsa_phi1_v, nsa_phi2_v, w_out_odd, norm_final):
    b, t, _ = x_prompt.shape
    db, ts, _ = x_sample.shape
    n_pages = page_table.shape[1]
    past = n_pages * PAGE
    n_pool = cache_mla.shape[1]
    mp, ms = b * t, db * ts
    tm = min(512, t)

    pos_p = jnp.arange(t, dtype=I32)
    pos_s = past + jnp.arange(SUB, dtype=I32)
    cos_p, sin_p = _rope_tables(pos_p)
    cos_s, sin_s = _rope_tables(pos_s)
    cos_st, sin_st = jnp.tile(cos_s[:ts], (db, 1)), jnp.tile(sin_s[:ts], (db, 1))

    even_in_w, mla_w = _even_weights(norm_even[0], w_in_even[0], mla_g_q[0], mla_g_kv[0],
                                     mla_w_uq[0], mla_w_uk[0], mla_w_uv[0])
    s5_w = _s5_weights(s5_lambda_re[0], s5_lambda_im[0], s5_log_dt[0], s5_b_re[0], s5_b_im[0],
                       s5_c_re[0], s5_c_im[0], s5_d[0], s5_w_glu[0], s5_b_glu[0])
    wo_e = w_out_even[0].astype(BF16)
    wo_a, wo_b = wo_e[:MLA_W], wo_e[MLA_W:]

    xp2 = x_prompt.reshape(mp, D_MODEL)
    cq, rows_p, za, u, zb = _even_in(xp2, even_in_w, cos_p, sin_p, tm)
    mix_a = _mla_prompt(cq.reshape(b, t, MLA_QL), rows_p.reshape(b, t, MLA_ROW),
                        za.reshape(b, t, MLA_W), mla_w, cos_p, sin_p)
    zeros_p = jnp.zeros((b, S5_S), F32)
    mix_b, sre_p, sim_p = _s5(u.reshape(b, t, S5_W), zb.reshape(b, t, S5_W), s5_w, zeros_p, zeros_p,
                              min(64, t))
    xp1 = _even_out(xp2, mix_a.reshape(mp, MLA_W), mix_b.reshape(mp, S5_W), wo_a, wo_b, tm)

    xs2 = x_sample.reshape(ms, D_MODEL)
    cq_s, rows_s, za_s, u_s, zb_s = _even_in(xs2, even_in_w, cos_st, sin_st, ms)
    pad3 = lambda a, n: _pad_tokens(a.reshape(db, ts, n), SUB)
    mix_a_s = _mla_decode(page_table, cache_mla[0], pad3(cq_s, MLA_QL), pad3(rows_s, MLA_ROW),
                          pad3(za_s, MLA_W), mla_w, cos_s, sin_s, ts)[:, :ts]
    st = state_s5[0]
    mix_b_s, sre_s, sim_s = _s5(u_s.reshape(db, ts, S5_W), zb_s.reshape(db, ts, S5_W), s5_w,
                                st[..., 0].reshape(db, S5_S), st[..., 1].reshape(db, S5_S), ts)
    xs1 = _even_out(xs2, mix_a_s.reshape(ms, MLA_W), mix_b_s.reshape(ms, S5_W), wo_a, wo_b, ms)

    odd_in_w, cmp_w = _odd_weights(norm_odd[0], w_in_odd[0], nsa_pe_k[0], nsa_pe_v[0], nsa_phi1_k[0],
                                   nsa_phi2_k[0], nsa_phi1_v[0], nsa_phi2_v[0])
    wo_o = w_out_odd[0].astype(BF16)
    gfin = norm_final[None, :]

    q, kv4, kw, gate, z = _odd_in(xp1, odd_in_w, tm)
    nblk_p = t // NSA_BLK
    cb = _compress_prompt(kv4.reshape(mp // NSA_BLK, NSA_BLK, 4 * NSA_KVW), *cmp_w)
    q3 = q.reshape(b, t, NSA_W)
    oc_t, sel, _ = _cmp(q3, cb.reshape(b, nblk_p, 2 * NSA_KVW), 0, nblk_p, min(256, t))
    o_t = _nsa_prompt(q3, kv4.reshape(b, t, 4 * NSA_KVW), kw.reshape(b, t, 2 * NSA_KVW), oc_t, sel,
                      gate.reshape(b, t, 3 * NSA_H))
    y_prompt = _odd_out_prompt(o_t, z.reshape(b, t, NSA_W), xp1.reshape(b, t, D_MODEL), wo_o, gfin,
                               min(256, t))

    q_s, kv4_s, kw_s, gate_s, z_s = _odd_in(xs1, odd_in_w, ms)
    cache_blocks = cache_nsa_kv[0].reshape(n_pool * (PAGE // NSA_BLK), NSA_BLK, 4 * NSA_KVW)
    cb_s = _compress_decode(page_table, cache_blocks, *cmp_w)
    n_sel_s = -(-(past + ts) // NSA_BLK)
    q_s3 = q_s.reshape(db, ts, NSA_W)
    oc_ts, _, idx_s = _cmp(_pad_tokens(q_s3, SUB), cb_s, past, n_sel_s, SUB)
    q_s4 = q_s.reshape(db, ts, NSA_H, NSA_D)
    kv4_sp = pad3(kv4_s, 4 * NSA_KVW)
    idx_flat = jnp.transpose(idx_s[..., :ts], (0, 1, 3, 2)).reshape(-1)
    o_sel_s = _nsa_sel_decode(idx_flat, page_table, cache_blocks, q_s4, kv4_sp, ts, past)
    win = state_nsa_win[0].reshape(db, NSA_WIN, 2 * NSA_KVW)
    kw_s3 = kw_s.reshape(db, ts, 2 * NSA_KVW)
    o_win_s = _nsa_win_decode(_pad_tokens(q_s4, SUB), win, _pad_tokens(kw_s3, SUB), ts)
    oc_s = jnp.transpose(oc_ts, (0, 2, 1))[:, :ts].reshape(ms, NSA_W)
    y_sample = _odd_out_decode(oc_s, o_sel_s.reshape(ms, NSA_W), o_win_s[:, :ts].reshape(ms, NSA_W),
                               gate_s, z_s, xs1, wo_o, gfin)

    state = lambda re, im, n: jnp.stack([re, im], axis=-1).reshape(1, n, S5_G, S5_N, 2)
    win_p = kw.reshape(b, t, 2, NSA_G, NSA_D)
    if t >= NSA_WIN:
        win_p = win_p[:, t - NSA_WIN:]
    else:
        win_p = jnp.pad(win_p, ((0, 0), (NSA_WIN - t, 0), (0, 0), (0, 0), (0, 0)))
    win_s = jnp.concatenate([win, kw_s3], axis=1)[:, ts:].reshape(db, NSA_WIN, 2, NSA_G, NSA_D)
    return (y_prompt, y_sample.reshape(db, ts, D_MODEL),
            rows_p.reshape(1, b, t, MLA_ROW), rows_s.reshape(1, db, ts, MLA_ROW),
            state(sre_p, sim_p, b), state(sre_s, sim_s, db),
            kv4.reshape(1, b, t, 4, NSA_G, NSA_D), kv4_s.reshape(1, db, ts, 4, NSA_G, NSA_D),
            win_p[None], win_s[None])
```

```python
import functools

import jax
import jax.numpy as jnp
from jax import lax
from jax.experimental import pallas as pl
from jax.experimental.pallas import tpu as pltpu

F32, BF16, I32 = jnp.float32, jnp.bfloat16, jnp.int32

D_MODEL = 1024
PAGE = 128
EPS = 1e-6
ROPE_THETA = 10000.0
MLA_H, MLA_NOPE, MLA_ROPE, MLA_V = 8, 64, 32, 64
MLA_QL, MLA_KVL = 384, 256
MLA_ROW = MLA_KVL + MLA_ROPE
MLA_W = MLA_H * MLA_V
MLA_SCALE = (MLA_NOPE + MLA_ROPE) ** -0.5
S5_G, S5_P, S5_N = 32, 16, 64
S5_W = S5_G * S5_P
S5_S = S5_G * S5_N
S5_GT = 4
S5_NT = S5_G // S5_GT
NSA_H, NSA_G, NSA_D = 16, 2, 64
NSA_R = NSA_H // NSA_G
NSA_W = NSA_H * NSA_D
NSA_KVW = NSA_G * NSA_D
NSA_BLK, NSA_TOPK, NSA_WIN = 64, 16, 512
NSA_SCALE = NSA_D ** -0.5
FORCED_BONUS = float(NSA_R + 1)
BPP = PAGE // NSA_BLK

NEG = -1e30
GONE = -3e38
TQ = 128
TK = 256
SUB = 8
PACK = 16
VMEM_LIMIT = 56 * 1024 * 1024


def _dot(a, b):
    return jnp.dot(a, b, preferred_element_type=F32)


def _dot_nt(a, b):
    return lax.dot_general(a, b, (((1,), (1,)), ((), ())), preferred_element_type=F32)


def _dot_tn(a, b):
    return lax.dot_general(a, b, (((0,), (0,)), ((), ())), preferred_element_type=F32)


def _rms(x, g):
    return x * lax.rsqrt(jnp.mean(x * x, axis=-1, keepdims=True) + EPS) * g


def _silu(x):
    return x * jax.nn.sigmoid(x)


def _rope_nat(x, cosf, sinf):
    half = x.shape[1] // 2
    xs = jnp.concatenate([x[:, half:], x[:, :half]], axis=1)
    return x * cosf + xs * sinf


def _params(sem):
    return pltpu.CompilerParams(dimension_semantics=sem, vmem_limit_bytes=VMEM_LIMIT)


def _const_spec(shape):
    n = len(shape)
    return pl.BlockSpec(shape, lambda *a, _n=n: (0,) * _n)


def _seq_major_spec(width, tm, seq):
    per = seq // tm
    return pl.BlockSpec((None, width, tm), lambda i: (i // per, 0, i % per))


def _even_in_kernel(feature_major, x_ref, g_ref, wcq, wckv, wkr, wza, wu, wzb, gq, gkv, cos_ref, sin_ref,
                    cq_o, rows_o, za_o, u_o, zb_o):
    h = _rms(x_ref[...], g_ref[...]).astype(BF16)
    cq_o[...] = _rms(_dot(h, wcq[...]), gq[...])
    ckv = _rms(_dot(h, wckv[...]), gkv[...])
    krope = _rope_nat(_dot(h, wkr[...]), cos_ref[...], sin_ref[...])
    if feature_major:
        rows_o[:MLA_KVL, :] = ckv.T
        rows_o[MLA_KVL:, :] = krope.T
    else:
        rows_o[:, :MLA_KVL] = ckv
        rows_o[:, MLA_KVL:] = krope
    za_o[...] = _dot(h, wza[...])
    u_o[...] = _dot(h, wu[...])
    zb_o[...] = _dot(h, wzb[...])


def _even_in(x2, wts, cosf, sinf, tm, seq=None):
    m = x2.shape[0]
    tab_blocks = cosf.shape[0] // tm
    row = lambda n: pl.BlockSpec((tm, n), lambda i: (i, 0))
    tab = pl.BlockSpec((tm, MLA_ROPE), lambda i: (i % tab_blocks, 0))
    widths = (MLA_QL, MLA_ROW, MLA_W, S5_W, S5_W)
    out_specs = [row(n) for n in widths]
    out_shape = [jax.ShapeDtypeStruct((m, n), F32) for n in widths]
    if seq is not None:
        out_specs[1] = _seq_major_spec(MLA_ROW, tm, seq)
        out_shape[1] = jax.ShapeDtypeStruct((m // seq, MLA_ROW, seq), F32)
    return pl.pallas_call(
        functools.partial(_even_in_kernel, seq is not None),
        grid=(m // tm,),
        in_specs=[row(D_MODEL)] + [_const_spec(c.shape) for c in wts] + [tab, tab],
        out_specs=out_specs,
        out_shape=out_shape,
        compiler_params=_params(("parallel",)),
        name="even_in",
    )(x2, *wts, cosf, sinf)


def _mla_queries(cq, wuqn, wuqr, wuk, cosf, sinf, h):
    cqb = cq.astype(BF16)
    qn = _dot(cqb, wuqn[h])
    ql = _dot(qn.astype(BF16), wuk[h])
    qr = _rope_nat(_dot(cqb, wuqr[h]), cosf, sinf)
    return ql, qr


def _mla_prompt_kernel(cq_ref, rows_ref, za_ref, wuqn, wuqr, wuk, wuv, cos_ref, sin_ref,
                       out_ref, qtl, qtr, m_s, l_s, acc_s):
    qi = pl.program_id(1)
    r = MLA_H * TQ
    cq = cq_ref[...]
    cosf, sinf = cos_ref[...], sin_ref[...]
    for h in range(MLA_H):
        ql, qr = _mla_queries(cq, wuqn, wuqr, wuk, cosf, sinf, h)
        qtl[:, h * TQ:(h + 1) * TQ] = ql.T.astype(BF16)
        qtr[:, h * TQ:(h + 1) * TQ] = qr.T.astype(BF16)
    m_s[...] = jnp.full((1, r), NEG, F32)
    l_s[...] = jnp.zeros((1, r), F32)
    acc_s[...] = jnp.zeros((MLA_KVL, r), F32)
    qpos = qi * TQ + (lax.broadcasted_iota(I32, (1, r), 1) & (TQ - 1))
    krow = lax.broadcasted_iota(I32, (TK, 1), 0)

    def body(j, carry):
        kt = rows_ref[:, pl.ds(pl.multiple_of(j * TK, TK), TK)]
        ckv = kt[:MLA_KVL, :].astype(BF16)
        kr = kt[MLA_KVL:, :].astype(BF16)
        s = (_dot_tn(ckv, qtl[...]) + _dot_tn(kr, qtr[...])) * MLA_SCALE
        s = jnp.where(j * TK + krow <= qpos, s, NEG)
        m_old = m_s[...]
        m_new = jnp.maximum(m_old, jnp.max(s, axis=0, keepdims=True))
        alpha = jnp.exp(m_old - m_new)
        p = jnp.exp(s - m_new)
        l_s[...] = alpha * l_s[...] + jnp.sum(p, axis=0, keepdims=True)
        acc_s[...] = alpha * acc_s[...] + _dot(ckv, p.astype(BF16))
        m_s[...] = m_new
        return carry

    lax.fori_loop(0, (qi * TQ + TQ + TK - 1) // TK, body, 0)
    o = (acc_s[...] / l_s[...]).astype(BF16)
    heads = [_dot_tn(o[:, h * TQ:(h + 1) * TQ], wuv[h]) for h in range(MLA_H)]
    out_ref[...] = jnp.concatenate(heads, axis=1) * _silu(za_ref[...])


def _mla_prompt(cq, rows_t, za, wts, cosf, sinf):
    b, t, _ = cq.shape
    r = MLA_H * TQ
    tok = lambda n: pl.BlockSpec((None, TQ, n), lambda bi, qi: (bi, qi, 0))
    tab = pl.BlockSpec((TQ, MLA_ROPE), lambda bi, qi: (qi, 0))
    return pl.pallas_call(
        _mla_prompt_kernel,
        grid=(b, t // TQ),
        in_specs=[tok(MLA_QL), pl.BlockSpec((None, MLA_ROW, t), lambda bi, qi: (bi, 0, 0)), tok(MLA_W)]
        + [_const_spec(w.shape) for w in wts] + [tab, tab],
        out_specs=tok(MLA_W),
        out_shape=jax.ShapeDtypeStruct((b, t, MLA_W), F32),
        scratch_shapes=[pltpu.VMEM((MLA_KVL, r), BF16), pltpu.VMEM((MLA_ROPE, r), BF16),
                        pltpu.VMEM((1, r), F32), pltpu.VMEM((1, r), F32), pltpu.VMEM((MLA_KVL, r), F32)],
        compiler_params=_params(("parallel", "arbitrary")),
        name="mla_prompt",
    )(cq, rows_t, za, *wts, cosf, sinf)


def _mla_decode_kernel(n_pages_step, ts, pt_ref, cq_ref, rows_ref, za_ref, wuqn, wuqr, wuk, wuv,
                       cos_ref, sin_ref, *rest):
    pages = rest[:n_pages_step]
    out_ref, ql_s, qr_s, m_s, l_s, acc_s = rest[n_pages_step:]
    gi = pl.program_id(1)
    rows = MLA_H * SUB

    @pl.when(gi == 0)
    def _():
        cq = cq_ref[...]
        cosf, sinf = cos_ref[...], sin_ref[...]
        for h in range(MLA_H):
            ql, qr = _mla_queries(cq, wuqn, wuqr, wuk, cosf, sinf, h)
            ql_s[h * SUB:(h + 1) * SUB, :] = ql.astype(BF16)
            qr_s[h * SUB:(h + 1) * SUB, :] = qr.astype(BF16)
        m_s[...] = jnp.full((rows, 1), NEG, F32)
        l_s[...] = jnp.zeros((rows, 1), F32)
        acc_s[...] = jnp.zeros((rows, MLA_KVL), F32)

    def update(s, vals, feature_major):
        m_old = m_s[...]
        m_new = jnp.maximum(m_old, jnp.max(s, axis=1, keepdims=True))
        alpha = jnp.exp(m_old - m_new)
        p = jnp.exp(s - m_new)
        l_s[...] = alpha * l_s[...] + jnp.sum(p, axis=1, keepdims=True)
        pv = None
        for (lo, hi), v in vals:
            pj = p[:, lo:hi].astype(BF16)
            term = _dot_nt(pj, v) if feature_major else _dot(pj, v)
            pv = term if pv is None else pv + term
        acc_s[...] = alpha * acc_s[...] + pv
        m_s[...] = m_new

    ql, qr = ql_s[...], qr_s[...]
    scores, vals = [], []
    for j, pg in enumerate(pages):
        k = pg[...]
        ckv = k[:MLA_KVL, :].astype(BF16)
        scores.append(_dot(ql, ckv) + _dot(qr, k[MLA_KVL:, :].astype(BF16)))
        vals.append(((j * PAGE, (j + 1) * PAGE), ckv))
    update(jnp.concatenate(scores, axis=1) * MLA_SCALE, vals, True)

    @pl.when(gi == pl.num_programs(1) - 1)
    def _():
        kn = rows_ref[...]
        ckv = kn[:, :MLA_KVL].astype(BF16)
        s = (_dot_nt(ql, ckv) + _dot_nt(qr, kn[:, MLA_KVL:].astype(BF16))) * MLA_SCALE
        tq = lax.broadcasted_iota(I32, (rows, SUB), 0) & (SUB - 1)
        jk = lax.broadcasted_iota(I32, (rows, SUB), 1)
        s = jnp.where((jk <= tq) & (jk < ts), s, NEG)
        update(s, [((0, SUB), ckv)], False)
        o = (acc_s[...] / l_s[...]).astype(BF16)
        heads = [_dot(o[h * SUB:(h + 1) * SUB, :], wuv[h]) for h in range(MLA_H)]
        out_ref[...] = jnp.concatenate(heads, axis=1) * _silu(za_ref[...])


def _mla_decode(page_table, cache_t, cq, rows, za, wts, cosf, sinf, ts, n_pages_step=16):
    db, n_pages = page_table.shape
    rows_n = MLA_H * SUB
    tok = lambda n: pl.BlockSpec((None, SUB, n), lambda b, g, pt: (b, 0, 0))
    cst = lambda shape: pl.BlockSpec(shape, lambda b, g, pt, _n=len(shape): (0,) * _n)
    page_specs = [
        pl.BlockSpec((None, MLA_ROW, PAGE), lambda b, g, pt, j=j: (pt[b, g * n_pages_step + j], 0, 0))
        for j in range(n_pages_step)]
    grid_spec = pltpu.PrefetchScalarGridSpec(
        num_scalar_prefetch=1,
        grid=(db, n_pages // n_pages_step),
        in_specs=[tok(MLA_QL), tok(MLA_ROW), tok(MLA_W)] + [cst(w.shape) for w in wts]
        + [cst(cosf.shape), cst(sinf.shape)] + page_specs,
        out_specs=tok(MLA_W),
        scratch_shapes=[pltpu.VMEM((rows_n, MLA_KVL), BF16), pltpu.VMEM((rows_n, MLA_ROPE), BF16),
                        pltpu.VMEM((rows_n, 1), F32), pltpu.VMEM((rows_n, 1), F32),
                        pltpu.VMEM((rows_n, MLA_KVL), F32)])
    return pl.pallas_call(
        functools.partial(_mla_decode_kernel, n_pages_step, ts),
        grid_spec=grid_spec,
        out_shape=jax.ShapeDtypeStruct((db, SUB, MLA_W), F32),
        compiler_params=_params(("parallel", "arbitrary")),
        name="mla_decode",
    )(page_table, cq, rows, za, *wts, cosf, sinf, *([cache_t] * n_pages_step))


def _s5_kernel(u_ref, zb_ref, lre_ref, lim_ref, ldt_ref, wbre, wbim, wcre, wcim, d_ref, wglu, bglu,
               h0re_ref, h0im_ref, mix_o, sre_o, sim_o, bure, buim, hre, him):
    step = pl.program_id(0)
    nb, chunk, _ = u_ref.shape
    rows = nb * chunk

    @pl.when(step == 0)
    def _():
        hre[...] = h0re_ref[...]
        him[...] = h0im_ref[...]

    lre, lim = lre_ref[...], lim_ref[...]
    dt = jnp.exp(ldt_ref[...])
    mag = jnp.exp(lre * dt)
    are, aim = mag * jnp.cos(lim * dt), mag * jnp.sin(lim * dt)
    den = lre * lre + lim * lim
    cre = ((are - 1.0) * lre + aim * lim) / den
    cim = (aim * lre - (are - 1.0) * lim) / den

    ut = jnp.swapaxes(u_ref[...], 0, 1).reshape(rows, S5_W)
    kw, nw = S5_GT * S5_P, S5_GT * S5_N
    for jt in range(S5_NT):
        cr, ci = cre[:, jt * nw:(jt + 1) * nw], cim[:, jt * nw:(jt + 1) * nw]
        bre = (cr * wbre[jt] - ci * wbim[jt]).astype(BF16)
        bim = (cr * wbim[jt] + ci * wbre[jt]).astype(BF16)
        uj = ut[:, jt * kw:(jt + 1) * kw].astype(BF16)
        bure[:, jt * nw:(jt + 1) * nw] = _dot(uj, bre)
        buim[:, jt * nw:(jt + 1) * nw] = _dot(uj, bim)

    def scan(t, carry):
        hr, hi = carry
        sl = pl.ds(pl.multiple_of(t * nb, SUB), nb)
        nr = are * hr - aim * hi + bure[sl, :]
        ni = are * hi + aim * hr + buim[sl, :]
        bure[sl, :] = nr
        buim[sl, :] = ni
        return nr, ni

    hr, hi = lax.fori_loop(0, chunk, scan, (hre[...], him[...]))
    hre[...] = hr
    him[...] = hi
    sre_o[...] = hr
    sim_o[...] = hi

    ys = []
    for jt in range(S5_NT):
        sr = bure[:, jt * nw:(jt + 1) * nw].astype(BF16)
        si = buim[:, jt * nw:(jt + 1) * nw].astype(BF16)
        ys.append(_dot(sr, wcre[jt].astype(BF16)) - _dot(si, wcim[jt].astype(BF16)))
    y = jnp.concatenate(ys, axis=1) + d_ref[...] * ut
    g5 = jax.nn.gelu(y)
    ob = g5 * jax.nn.sigmoid(_dot(g5.astype(BF16), wglu[...]) + bglu[...])
    mix_o[...] = jnp.swapaxes(ob.reshape(chunk, nb, S5_W), 0, 1) * _silu(zb_ref[...])


def _s5(u, zb, wts, h0re, h0im, chunk):
    nb, t, _ = u.shape
    tok = pl.BlockSpec((nb, chunk, S5_W), lambda i: (0, i, 0))
    st = _const_spec((nb, S5_S))
    return pl.pallas_call(
        _s5_kernel,
        grid=(t // chunk,),
        in_specs=[tok, tok] + [_const_spec(w.shape) for w in wts] + [st, st],
        out_specs=[tok, st, st],
        out_shape=[jax.ShapeDtypeStruct((nb, t, S5_W), F32), jax.ShapeDtypeStruct((nb, S5_S), F32),
                   jax.ShapeDtypeStruct((nb, S5_S), F32)],
        scratch_shapes=[pltpu.VMEM((nb * chunk, S5_S), F32), pltpu.VMEM((nb * chunk, S5_S), F32),
                        pltpu.VMEM((nb, S5_S), F32), pltpu.VMEM((nb, S5_S), F32)],
        compiler_params=_params(("arbitrary",)),
        name="s5",
    )(u, zb, *wts, h0re, h0im)


def _even_out_kernel(x_ref, a_ref, b_ref, wa, wb, out_ref):
    out_ref[...] = (x_ref[...] + _dot(a_ref[...].astype(BF16), wa[...])
                    + _dot(b_ref[...].astype(BF16), wb[...]))


def _even_out(x2, mixa, mixb, wa, wb, tm):
    m = x2.shape[0]
    row = lambda n: pl.BlockSpec((tm, n), lambda i: (i, 0))
    return pl.pallas_call(
        _even_out_kernel,
        grid=(m // tm,),
        in_specs=[row(D_MODEL), row(MLA_W), row(S5_W), _const_spec(wa.shape), _const_spec(wb.shape)],
        out_specs=row(D_MODEL),
        out_shape=jax.ShapeDtypeStruct((m, D_MODEL), F32),
        compiler_params=_params(("parallel",)),
        name="even_out",
    )(x2, mixa, mixb, wa, wb)


def _odd_in_kernel(feature_major, x_ref, g_ref, wq, wkv4, wkw, wg, wz, q_o, kv4_o, kw_o, gate_o, z_o):
    h = _rms(x_ref[...], g_ref[...]).astype(BF16)
    q_o[...] = _dot(h, wq[...])
    kv4, kw = _dot(h, wkv4[...]), _dot(h, wkw[...])
    kv4_o[...] = kv4.T if feature_major else kv4
    kw_o[...] = kw.T if feature_major else kw
    gate_o[...] = jax.nn.sigmoid(_dot(h, wg[...]))
    z_o[...] = _dot(h, wz[...])


def _odd_in(x2, wts, tm, seq=None):
    m = x2.shape[0]
    row = lambda n: pl.BlockSpec((tm, n), lambda i: (i, 0))
    widths = (NSA_W, 4 * NSA_KVW, 2 * NSA_KVW, 3 * NSA_H, NSA_W)
    out_specs = [row(n) for n in widths]
    out_shape = [jax.ShapeDtypeStruct((m, n), F32) for n in widths]
    if seq is not None:
        for i in (1, 2):
            out_specs[i] = _seq_major_spec(widths[i], tm, seq)
            out_shape[i] = jax.ShapeDtypeStruct((m // seq, widths[i], seq), F32)
    return pl.pallas_call(
        functools.partial(_odd_in_kernel, seq is not None),
        grid=(m // tm,),
        in_specs=[row(D_MODEL)] + [_const_spec(w.shape) for w in wts],
        out_specs=out_specs,
        out_shape=out_shape,
        compiler_params=_params(("parallel",)),
        name="odd_in",
    )(x2, *wts)


def _compress_stage(xs, x_t, pe, base):
    n = x_t.shape[1] // NSA_BLK
    x = x_t.T.reshape(n, NSA_BLK, 2 * NSA_KVW) + pe[None]
    xs[:, pl.ds(pl.multiple_of(base, PACK), n), :] = jnp.swapaxes(x, 0, 1).astype(BF16)


def _compress_finish(xs, w1, phi2, out_ref):
    acc = jnp.zeros((xs.shape[1], 2 * NSA_KVW), F32)
    for r in range(NSA_BLK):
        acc = acc + _dot(xs[r], w1[r])
    out_ref[...] = _dot(_silu(acc).astype(BF16), phi2[...])


def _compress_prompt_kernel(x_ref, pe_ref, w1, phi2, out_ref, xs):
    bi = pl.program_id(0)
    nblk = x_ref.shape[1] // NSA_BLK
    _compress_stage(xs, x_ref[...], pe_ref[...], bi * nblk)

    @pl.when(bi == pl.num_programs(0) - 1)
    def _():
        _compress_finish(xs, w1, phi2, out_ref)


def _compress_prompt(kv4_t, pe, w1, phi2):
    b, _, t = kv4_t.shape
    cw = 2 * NSA_KVW
    nblk = b * (t // NSA_BLK)
    return pl.pallas_call(
        _compress_prompt_kernel,
        grid=(b,),
        in_specs=[pl.BlockSpec((None, cw, t), lambda i: (i, 0, 0)),
                  _const_spec(pe.shape), _const_spec(w1.shape), _const_spec(phi2.shape)],
        out_specs=pl.BlockSpec((nblk, cw), lambda i: (0, 0)),
        out_shape=jax.ShapeDtypeStruct((nblk, cw), F32),
        scratch_shapes=[pltpu.VMEM((NSA_BLK, nblk, cw), BF16)],
        compiler_params=_params(("arbitrary",)),
        name="compress_prompt",
    )(kv4_t, pe, w1, phi2)


def _compress_decode_kernel(n_pages_step, pt_ref, pe_ref, w1, phi2, *rest):
    pages = rest[:n_pages_step]
    out_ref, xs = rest[n_pages_step:]
    gi = pl.program_id(1)
    grp = PACK // BPP
    pe = pe_ref[...]
    for k in range(n_pages_step // grp):
        x_t = jnp.concatenate([pages[k * grp + j][...] for j in range(grp)], axis=1)
        _compress_stage(xs, x_t, pe, (gi * (n_pages_step // grp) + k) * PACK)

    @pl.when(gi == pl.num_programs(1) - 1)
    def _():
        _compress_finish(xs, w1, phi2, out_ref)


def _compress_decode(page_table, cache_t, pe, w1, phi2, n_pages_step=16):
    db, n_pages = page_table.shape
    cw = 2 * NSA_KVW
    nblk = n_pages * BPP
    cst = lambda shape: pl.BlockSpec(shape, lambda b, g, pt, _n=len(shape): (0,) * _n)
    page_specs = [
        pl.BlockSpec((None, cw, PAGE), lambda b, g, pt, j=j: (pt[b, g * n_pages_step + j], 0, 0))
        for j in range(n_pages_step)]
    grid_spec = pltpu.PrefetchScalarGridSpec(
        num_scalar_prefetch=1,
        grid=(db, n_pages // n_pages_step),
        in_specs=[cst(pe.shape), cst(w1.shape), cst(phi2.shape)] + page_specs,
        out_specs=pl.BlockSpec((None, nblk, cw), lambda b, g, pt: (b, 0, 0)),
        scratch_shapes=[pltpu.VMEM((NSA_BLK, nblk, cw), BF16)])
    return pl.pallas_call(
        functools.partial(_compress_decode_kernel, n_pages_step),
        grid_spec=grid_spec,
        out_shape=jax.ShapeDtypeStruct((db, nblk, cw), F32),
        compiler_params=_params(("parallel", "arbitrary")),
        name="compress_decode",
    )(page_table, pe, w1, phi2, *([cache_t] * n_pages_step))


def _alibi_slope(h):
    return 2.0 ** (-8.0 * (h + 1) / NSA_H)


def _split_bf16(x):
    hi = x.astype(BF16)
    return hi, (x - hi.astype(F32)).astype(BF16)


def _cmp_kernel(q_ref, cb_ref, pos_ref, oc_o, imp_o, s_scr):
    tt = q_ref.shape[0]
    nc = cb_ref.shape[0]
    qpos = pos_ref[...]
    cpos = lax.broadcasted_iota(I32, (nc, 1), 0) * NSA_BLK + (NSA_BLK - 1)
    visible = cpos <= qpos
    dist = (qpos - cpos).astype(F32)
    q = q_ref[...] * NSA_SCALE
    cb = cb_ref[...]
    for g in range(NSA_G):
        k_hi, k_lo = _split_bf16(cb[:, g * NSA_D:(g + 1) * NSA_D])
        vc = cb[:, NSA_KVW + g * NSA_D:NSA_KVW + (g + 1) * NSA_D].astype(BF16)
        for i in range(NSA_R):
            h = g * NSA_R + i
            q_hi, q_lo = _split_bf16(q[:, h * NSA_D:(h + 1) * NSA_D])
            s = _dot_nt(k_hi, q_hi) + _dot_nt(k_hi, q_lo) + _dot_nt(k_lo, q_hi)
            s_scr[i * nc:(i + 1) * nc, :] = s - _alibi_slope(h) * dist
        s3 = jnp.where(visible[None], s_scr[...].reshape(NSA_R, nc, tt), NEG)
        mx = jnp.max(s3, axis=1, keepdims=True)
        e = jnp.where(visible[None], jnp.exp(s3 - mx), 0.0)
        den = jnp.sum(e, axis=1, keepdims=True)
        p = e / jnp.where(den > 0, den, 1.0)
        imp_o[g] = jnp.sum(p, axis=0)
        for i in range(NSA_R):
            h = g * NSA_R + i
            oc_o[h * NSA_D:(h + 1) * NSA_D, :] = _dot_tn(vc, p[i].astype(BF16))


def _cmp(q, cb, pos, tt):
    b, t, _ = q.shape
    nc = cb.shape[1]
    return pl.pallas_call(
        _cmp_kernel,
        grid=(b, t // tt),
        in_specs=[pl.BlockSpec((None, tt, NSA_W), lambda bi, ti: (bi, ti, 0)),
                  pl.BlockSpec((None, nc, 2 * NSA_KVW), lambda bi, ti: (bi, 0, 0)),
                  pl.BlockSpec((1, tt), lambda bi, ti: (0, ti))],
        out_specs=[pl.BlockSpec((None, NSA_W, tt), lambda bi, ti: (bi, 0, ti)),
                   pl.BlockSpec((None, NSA_G, nc, tt), lambda bi, ti: (bi, 0, 0, ti))],
        out_shape=[jax.ShapeDtypeStruct((b, NSA_W, t), F32),
                   jax.ShapeDtypeStruct((b, NSA_G, nc, t), F32)],
        scratch_shapes=[pltpu.VMEM((NSA_R * nc, tt), F32)],
        compiler_params=_params(("parallel", "parallel")),
        name="nsa_cmp",
    )(q, cb, pos)


def _topk_kernel(n_sel, imp_ref, pos_ref, sel_o, idx_o):
    _, nc, tt = imp_ref.shape
    nsp = sel_o.shape[1]
    qpos = pos_ref[...]
    blk = lax.broadcasted_iota(I32, (nsp, 1), 0)
    cur = jnp.right_shift(qpos, NSA_BLK.bit_length() - 1)
    forced = (blk == 0) | (blk == cur) | (blk == cur - 1)
    allowed = (blk <= cur) & (blk < n_sel)
    for g in range(NSA_G):
        imp = imp_ref[g]
        if nsp > nc:
            imp = jnp.concatenate([imp, jnp.zeros((nsp - nc, tt), F32)], axis=0)
        score = jnp.where(allowed, imp + jnp.where(forced, FORCED_BONUS, 0.0), NEG)
        chosen = jnp.zeros((nsp, tt), F32)
        picks = []
        for _ in range(NSA_TOPK):
            mx = jnp.max(score, axis=0, keepdims=True)
            first = jnp.min(jnp.where(score == mx, blk, nsp), axis=0, keepdims=True)
            hit = blk == first
            chosen = jnp.where(hit, 1.0, chosen)
            score = jnp.where(hit, GONE, score)
            picks.append(first)
        sel_o[g] = chosen
        idx_o[g] = jnp.concatenate(picks, axis=0)


def _topk(imp, pos, n_sel, tt):
    b, _, nc, t = imp.shape
    nsp = -(-n_sel // SUB) * SUB
    return pl.pallas_call(
        functools.partial(_topk_kernel, n_sel),
        grid=(b, t // tt),
        in_specs=[pl.BlockSpec((None, NSA_G, nc, tt), lambda bi, ti: (bi, 0, 0, ti)),
                  pl.BlockSpec((1, tt), lambda bi, ti: (0, ti))],
        out_specs=[pl.BlockSpec((None, NSA_G, nsp, tt), lambda bi, ti: (bi, 0, 0, ti)),
                   pl.BlockSpec((None, NSA_G, NSA_TOPK, tt), lambda bi, ti: (bi, 0, 0, ti))],
        out_shape=[jax.ShapeDtypeStruct((b, NSA_G, nsp, t), F32),
                   jax.ShapeDtypeStruct((b, NSA_G, NSA_TOPK, t), I32)],
        compiler_params=_params(("parallel", "parallel")),
        name="nsa_topk",
    )(imp, pos)


def _nsa_prompt_kernel(q_ref, kv4_ref, kw_ref, oc_ref, sel_ref, gate_ref, out_ref, m_s, l_s, acc_s):
    qi = pl.program_id(1)
    r = NSA_R * TQ
    bpt = TK // NSA_BLK
    qt = (q_ref[...] * NSA_SCALE).T.astype(BF16)
    gt = gate_ref[...].T
    lane = lax.broadcasted_iota(I32, (1, r), 1)
    qpos = qi * TQ + (lane & (TQ - 1))
    head = lane // TQ
    krow = lax.broadcasted_iota(I32, (TK, 1), 0)
    hi_tile = (qi * TQ + TQ + TK - 1) // TK

    def attend(lo, k_of, v_of, mask_of, slope):
        m_s[...] = jnp.full((1, r), NEG, F32)
        l_s[...] = jnp.zeros((1, r), F32)
        acc_s[...] = jnp.zeros((NSA_D, r), F32)

        def body(j, carry):
            base = pl.multiple_of(j * TK, TK)
            kpos = j * TK + krow
            s = _dot_tn(k_of(base).astype(BF16), qg)
            s = s + slope * (kpos - qi * TQ).astype(F32)
            s = jnp.where(mask_of(j, kpos), s, NEG)
            m_old = m_s[...]
            m_new = jnp.maximum(m_old, jnp.max(s, axis=0, keepdims=True))
            alpha = jnp.exp(m_old - m_new)
            p = jnp.exp(s - m_new)
            l_s[...] = alpha * l_s[...] + jnp.sum(p, axis=0, keepdims=True)
            acc_s[...] = alpha * acc_s[...] + _dot(v_of(base).astype(BF16), p.astype(BF16))
            m_s[...] = m_new
            return carry

        lax.fori_loop(lo, hi_tile, body, 0)
        return acc_s[...] / l_s[...]

    for g in range(NSA_G):
        qg = jnp.concatenate(
            [qt[(g * NSA_R + i) * NSA_D:(g * NSA_R + i + 1) * NSA_D, :] for i in range(NSA_R)], axis=1)
        slope = jnp.zeros((1, r), F32)
        for i in range(NSA_R):
            slope = jnp.where(head == i, _alibi_slope(g * NSA_R + i), slope)
        ks_lo, vs_lo = 2 * NSA_KVW + g * NSA_D, 3 * NSA_KVW + g * NSA_D
        kw_lo, vw_lo = g * NSA_D, NSA_KVW + g * NSA_D

        def sel_mask(j, kpos, g=g):
            srow = sel_ref[g, pl.ds(j * bpt, bpt), :]
            flags = [jnp.concatenate([srow[i:i + 1, :]] * NSA_R, axis=1) for i in range(bpt)]
            picked = flags[bpt - 1]
            for i in range(bpt - 2, -1, -1):
                picked = jnp.where(krow < (i + 1) * NSA_BLK, flags[i], picked)
            return (picked > 0.5) & (kpos <= qpos)

        def win_mask(j, kpos):
            return (kpos <= qpos) & (qpos - kpos <= NSA_WIN)

        o_sel = attend(0,
                       lambda b0: kv4_ref[ks_lo:ks_lo + NSA_D, pl.ds(b0, TK)],
                       lambda b0: kv4_ref[vs_lo:vs_lo + NSA_D, pl.ds(b0, TK)], sel_mask, slope)
        o_win = attend(jnp.maximum(qi * TQ - NSA_WIN, 0) // TK,
                       lambda b0: kw_ref[kw_lo:kw_lo + NSA_D, pl.ds(b0, TK)],
                       lambda b0: kw_ref[vw_lo:vw_lo + NSA_D, pl.ds(b0, TK)], win_mask, slope)
        for i in range(NSA_R):
            h = g * NSA_R + i
            rows = slice(h * NSA_D, (h + 1) * NSA_D)
            cols = slice(i * TQ, (i + 1) * TQ)
            out_ref[rows, :] = (gt[h:h + 1, :] * oc_ref[rows, :]
                                + gt[NSA_H + h:NSA_H + h + 1, :] * o_sel[:, cols]
                                + gt[2 * NSA_H + h:2 * NSA_H + h + 1, :] * o_win[:, cols])


def _nsa_prompt(q, kv4_t, kw_t, oc_t, sel, gate):
    b, t, _ = q.shape
    nsp = sel.shape[2]
    r = NSA_R * TQ
    tok = lambda n: pl.BlockSpec((None, TQ, n), lambda bi, qi: (bi, qi, 0))
    full = lambda n: pl.BlockSpec((None, n, t), lambda bi, qi: (bi, 0, 0))
    return pl.pallas_call(
        _nsa_prompt_kernel,
        grid=(b, t // TQ),
        in_specs=[tok(NSA_W), full(4 * NSA_KVW), full(2 * NSA_KVW),
                  pl.BlockSpec((None, NSA_W, TQ), lambda bi, qi: (bi, 0, qi)),
                  pl.BlockSpec((None, NSA_G, nsp, TQ), lambda bi, qi: (bi, 0, 0, qi)),
                  tok(3 * NSA_H)],
        out_specs=pl.BlockSpec((None, NSA_W, TQ), lambda bi, qi: (bi, 0, qi)),
        out_shape=jax.ShapeDtypeStruct((b, NSA_W, t), F32),
        scratch_shapes=[pltpu.VMEM((1, r), F32), pltpu.VMEM((1, r), F32), pltpu.VMEM((NSA_D, r), F32)],
        compiler_params=_params(("parallel", "arbitrary")),
        name="nsa_prompt",
    )(q, kv4_t, kw_t, oc_t, sel, gate)


def _softmax_rows(parts):
    mx = None
    for s, _, _ in parts:
        pm = jnp.max(s, axis=1, keepdims=True)
        mx = pm if mx is None else jnp.maximum(mx, pm)
    den, num = None, None
    for s, v, feature_major in parts:
        p = jnp.exp(s - mx)
        d = jnp.sum(p, axis=1, keepdims=True)
        n = _dot_nt(p.astype(BF16), v) if feature_major else _dot(p.astype(BF16), v)
        den = d if den is None else den + d
        num = n if num is None else num + n
    return num / den


def _nsa_sel_decode_kernel(ts, pos0, nbp, idx_ref, pt_ref, q_ref, new_ref, *rest):
    nblk = NSA_G * NSA_TOPK
    pages = rest[:nblk]
    out_ref = rest[nblk]
    b, t = pl.program_id(0), pl.program_id(1)
    q = q_ref[...] * NSA_SCALE
    new = new_ref[...]
    lane = lax.broadcasted_iota(I32, (1, PAGE), 1)
    jn = lax.broadcasted_iota(I32, (1, SUB), 1)
    for g in range(NSA_G):
        qg = q[g * NSA_R:(g + 1) * NSA_R, :].astype(BF16)
        slope = jnp.concatenate(
            [jnp.full((1, 1), _alibi_slope(g * NSA_R + i), F32) for i in range(NSA_R)], axis=0)
        parts = []
        has_new = jnp.zeros((), I32)
        for k in range(NSA_TOPK):
            bid = idx_ref[((b * NSA_G + g) * ts + t) * NSA_TOPK + k]
            pg = pages[g * NSA_TOPK + k][...]
            kk = pg[g * NSA_D:(g + 1) * NSA_D, :].astype(BF16)
            vv = pg[NSA_KVW + g * NSA_D:NSA_KVW + (g + 1) * NSA_D, :].astype(BF16)
            sub = bid % BPP
            rel = ((bid - sub) * NSA_BLK - pos0 + lane).astype(F32)
            s = _dot(qg, kk) + slope * rel
            mine = (lane // NSA_BLK == sub) & (bid < nbp)
            parts.append((jnp.where(mine, s, NEG), vv, True))
            has_new = has_new + (bid >= nbp).astype(I32)
        kn = new[:, 2 * NSA_KVW + g * NSA_D:2 * NSA_KVW + (g + 1) * NSA_D].astype(BF16)
        vn = new[:, 3 * NSA_KVW + g * NSA_D:3 * NSA_KVW + (g + 1) * NSA_D].astype(BF16)
        sn = _dot_nt(qg, kn) + slope * jn.astype(F32)
        ok = (jn <= t) & (jn < ts) & (has_new > 0)
        parts.append((jnp.where(ok, sn, NEG), vn, False))
        out_ref[g * NSA_R:(g + 1) * NSA_R, :] = _softmax_rows(parts)


def _nsa_sel_decode(idx_flat, page_table, cache_t, q4, kv4_new, ts, pos0):
    db, n_pages = page_table.shape
    nbp = n_pages * BPP
    cw = 2 * NSA_KVW

    def page_map(b, t, idx, pt, g, k):
        bid = jnp.minimum(idx[((b * NSA_G + g) * ts + t) * NSA_TOPK + k], nbp - 1)
        return (pt[b, bid // BPP], 1, 0)

    page_specs = [pl.BlockSpec((None, cw, PAGE), functools.partial(page_map, g=g, k=k))
                  for g in range(NSA_G) for k in range(NSA_TOPK)]
    grid_spec = pltpu.PrefetchScalarGridSpec(
        num_scalar_prefetch=2,
        grid=(db, ts),
        in_specs=[pl.BlockSpec((None, None, NSA_H, NSA_D), lambda b, t, idx, pt: (b, t, 0, 0)),
                  pl.BlockSpec((None, SUB, 4 * NSA_KVW), lambda b, t, idx, pt: (b, 0, 0))] + page_specs,
        out_specs=pl.BlockSpec((None, None, NSA_H, NSA_D), lambda b, t, idx, pt: (b, t, 0, 0)))
    return pl.pallas_call(
        functools.partial(_nsa_sel_decode_kernel, ts, pos0, nbp),
        grid_spec=grid_spec,
        out_shape=jax.ShapeDtypeStruct((db, ts, NSA_H, NSA_D), F32),
        compiler_params=_params(("parallel", "arbitrary")),
        name="nsa_sel_decode",
    )(idx_flat, page_table, q4, kv4_new, *([cache_t] * (NSA_G * NSA_TOPK)))


def _nsa_win_decode_kernel(ts, q_ref, win_ref, new_ref, out_ref):
    rows = NSA_R * SUB
    win = win_ref[...]
    new = new_ref[...]
    tq = lax.broadcasted_iota(I32, (rows, 1), 0) & (SUB - 1)
    iw = lax.broadcasted_iota(I32, (1, NSA_WIN), 1)
    jn = lax.broadcasted_iota(I32, (1, SUB), 1)
    for g in range(NSA_G):
        qg = jnp.concatenate([q_ref[:, g * NSA_R + i, :] for i in range(NSA_R)], axis=0)
        qg = (qg * NSA_SCALE).astype(BF16)
        slope = jnp.concatenate(
            [jnp.full((SUB, 1), _alibi_slope(g * NSA_R + i), F32) for i in range(NSA_R)], axis=0)
        kw = win[g * NSA_D:(g + 1) * NSA_D, :].astype(BF16)
        vw = win[NSA_KVW + g * NSA_D:NSA_KVW + (g + 1) * NSA_D, :].astype(BF16)
        kn = new[:, g * NSA_D:(g + 1) * NSA_D].astype(BF16)
        vn = new[:, NSA_KVW + g * NSA_D:NSA_KVW + (g + 1) * NSA_D].astype(BF16)
        sw = _dot(qg, kw) + slope * (iw - NSA_WIN).astype(F32)
        sw = jnp.where(iw >= tq, sw, NEG)
        sn = _dot_nt(qg, kn) + slope * jn.astype(F32)
        sn = jnp.where((jn <= tq) & (jn < ts), sn, NEG)
        o = _softmax_rows([(sw, vw, True), (sn, vn, False)])
        for i in range(NSA_R):
            out_ref[:, g * NSA_R + i, :] = o[i * SUB:(i + 1) * SUB, :]


def _nsa_win_decode(q4, win_t, kw_new, ts):
    db = q4.shape[0]
    return pl.pallas_call(
        functools.partial(_nsa_win_decode_kernel, ts),
        grid=(db,),
        in_specs=[pl.BlockSpec((None, SUB, NSA_H, NSA_D), lambda b: (b, 0, 0, 0)),
                  pl.BlockSpec((None, 2 * NSA_KVW, NSA_WIN), lambda b: (b, 0, 0)),
                  pl.BlockSpec((None, SUB, 2 * NSA_KVW), lambda b: (b, 0, 0))],
        out_specs=pl.BlockSpec((None, SUB, NSA_H, NSA_D), lambda b: (b, 0, 0, 0)),
        out_shape=jax.ShapeDtypeStruct((db, SUB, NSA_H, NSA_D), F32),
        compiler_params=_params(("parallel",)),
        name="nsa_win_decode",
    )(q4, win_t, kw_new)


def _odd_tail(o, z_ref, x_ref, wout, gfin):
    y = x_ref[...] + _dot((o * _silu(z_ref[...])).astype(BF16), wout[...])
    return _rms(y, gfin[...])


def _odd_out_prompt_kernel(ot_ref, z_ref, x_ref, wout, gfin, out_ref):
    out_ref[...] = _odd_tail(ot_ref[...].T, z_ref, x_ref, wout, gfin)


def _odd_out_prompt(ot, z, x, wout, gfin, tm):
    b, _, t = ot.shape
    tok = pl.BlockSpec((None, tm, D_MODEL), lambda bi, ti: (bi, ti, 0))
    return pl.pallas_call(
        _odd_out_prompt_kernel,
        grid=(b, t // tm),
        in_specs=[pl.BlockSpec((None, NSA_W, tm), lambda bi, ti: (bi, 0, ti)), tok, tok,
                  _const_spec(wout.shape), _const_spec(gfin.shape)],
        out_specs=tok,
        out_shape=jax.ShapeDtypeStruct((b, t, D_MODEL), F32),
        compiler_params=_params(("parallel", "parallel")),
        name="odd_out_prompt",
    )(ot, z, x, wout, gfin)


def _odd_out_decode_kernel(oc_ref, os_ref, ow_ref, gate_ref, z_ref, x_ref, wout, gfin, out_ref):
    gate = gate_ref[...]
    oc, osel, ow = oc_ref[...], os_ref[...], ow_ref[...]
    heads = []
    for h in range(NSA_H):
        c = slice(h * NSA_D, (h + 1) * NSA_D)
        heads.append(gate[:, h:h + 1] * oc[:, c] + gate[:, NSA_H + h:NSA_H + h + 1] * osel[:, c]
                     + gate[:, 2 * NSA_H + h:2 * NSA_H + h + 1] * ow[:, c])
    out_ref[...] = _odd_tail(jnp.concatenate(heads, axis=1), z_ref, x_ref, wout, gfin)


def _odd_out_decode(oc, osel, ow, gate, z, x, wout, gfin):
    m = x.shape[0]
    args = (oc, osel, ow, gate, z, x, wout, gfin)
    return pl.pallas_call(
        _odd_out_decode_kernel,
        grid=(1,),
        in_specs=[_const_spec(a.shape) for a in args],
        out_specs=_const_spec((m, D_MODEL)),
        out_shape=jax.ShapeDtypeStruct((m, D_MODEL), F32),
        compiler_params=_params(("arbitrary",)),
        name="odd_out_decode",
    )(*args)


def _rope_tables(pos):
    half = MLA_ROPE // 2
    inv = ROPE_THETA ** (-jnp.arange(half, dtype=F32) / half)
    ang = pos.astype(F32)[:, None] * inv[None, :]
    cos, sin = jnp.cos(ang), jnp.sin(ang)
    return jnp.concatenate([cos, cos], axis=1), jnp.concatenate([-sin, sin], axis=1)


def _block_diag(x):
    t, g, r, c = x.shape
    eye = jnp.eye(g, dtype=x.dtype)
    return jnp.einsum("tgrc,gh->tgrhc", x, eye).reshape(t, g * r, g * c)


def _even_weights(norm_g, w_in, g_q, g_kv, w_uq, w_uk, w_uv):
    edges = [0, MLA_QL, MLA_QL + MLA_KVL, MLA_QL + MLA_ROW]
    edges += [edges[-1] + MLA_W, edges[-1] + MLA_W + S5_W, edges[-1] + MLA_W + 2 * S5_W]
    wb = w_in.astype(BF16)
    pieces = [wb[:, edges[i]:edges[i + 1]] for i in range(6)]
    in_w = (norm_g[None, :], *pieces, g_q[None, :], g_kv[None, :])
    uq = jnp.transpose(w_uq, (1, 0, 2)).astype(BF16)
    mla_w = (uq[:, :, :MLA_NOPE], uq[:, :, MLA_NOPE:],
             jnp.transpose(w_uk, (1, 2, 0)).astype(BF16),
             jnp.transpose(w_uv, (1, 0, 2)).astype(BF16))
    return in_w, mla_w


def _s5_weights(lam_re, lam_im, log_dt, b_re, b_im, c_re, c_im, d_skip, w_glu, b_glu):
    def bmat(b):
        return _block_diag(jnp.transpose(b.reshape(S5_NT, S5_GT, S5_N, S5_P), (0, 1, 3, 2)))

    def cmat(c):
        return _block_diag(jnp.transpose(c.reshape(S5_NT, S5_GT, S5_P, S5_N), (0, 1, 3, 2)))

    return (lam_re.reshape(1, S5_S), lam_im.reshape(1, S5_S),
            jnp.repeat(log_dt, S5_N).reshape(1, S5_S),
            bmat(b_re), bmat(b_im), cmat(c_re), cmat(c_im),
            d_skip.reshape(1, S5_W), w_glu.astype(BF16), b_glu[None, :])


def _odd_weights(norm_g, w_in, pe_k, pe_v, phi1_k, phi2_k, phi1_v, phi2_v):
    wb = w_in.astype(BF16)
    e0 = NSA_W
    e1 = e0 + 4 * NSA_KVW
    e2 = e1 + 2 * NSA_KVW
    e3 = e2 + 3 * NSA_H
    in_w = (norm_g[None, :], wb[:, :e0], wb[:, e0:e1], wb[:, e1:e2], wb[:, e2:e3], wb[:, e3:])
    pe = jnp.concatenate([pe_k, pe_k, pe_v, pe_v], axis=1)
    p1k = phi1_k.reshape(NSA_BLK, NSA_D, NSA_D)
    p1v = phi1_v.reshape(NSA_BLK, NSA_D, NSA_D)
    w1 = _block_diag(jnp.stack([p1k, p1k, p1v, p1v], axis=1)).astype(BF16)
    phi2 = _block_diag(jnp.stack([phi2_k, phi2_k, phi2_v, phi2_v], axis=0)[None])[0].astype(BF16)
    return in_w, (pe, w1, phi2)


def _pad_tokens(x, n):
    return jnp.pad(x, ((0, 0), (0, n - x.shape[1])) + ((0, 0),) * (x.ndim - 2))


def _rows_last(x):
    nd = x.ndim
    xt = jnp.transpose(x, (0,) + tuple(range(2, nd)) + (1,))
    return xt.reshape(x.shape[0], -1, x.shape[1])


def _rows_second(x_t, feature_shape):
    b, _, rows = x_t.shape
    nf = len(feature_shape)
    xt = x_t.reshape((b,) + tuple(feature_shape) + (rows,))
    return jnp.transpose(xt, (0, nf + 1) + tuple(range(1, nf + 1)))


def kernel(x_prompt, x_sample, cache_mla, state_s5, cache_nsa_kv, state_nsa_win, page_table, norm_even, w_in_even, mla_g_q, mla_g_kv, mla_w_uq, mla_w_uk, mla_w_uv, s5_lambda_re, s5_lambda_im, s5_log_dt, s5_b_re, s5_b_im, s5_c_re, s5_c_im, s5_d, s5_w_glu, s5_b_glu, w_out_even, norm_odd, w_in_odd, nsa_pe_k, nsa_pe_v, nsa_phi1_k, nsa_phi2_k, nsa_phi1_v, nsa_phi2_v, w_out_odd, norm_final):
    b, t, _ = x_prompt.shape
    db, ts, _ = x_sample.shape
    n_pages = page_table.shape[1]
    past = n_pages * PAGE
    mp, ms = b * t, db * ts
    tm = min(512, t)
    assert t % (PACK * NSA_BLK) == 0 and t % TK == 0 and ts <= SUB

    pos_p = jnp.arange(t, dtype=I32)
    pos_s = past + jnp.arange(SUB, dtype=I32)
    cos_p, sin_p = _rope_tables(pos_p)
    cos_s, sin_s = _rope_tables(pos_s)
    cos_st, sin_st = jnp.tile(cos_s[:ts], (db, 1)), jnp.tile(sin_s[:ts], (db, 1))

    even_in_w, mla_w = _even_weights(norm_even[0], w_in_even[0], mla_g_q[0], mla_g_kv[0],
                                     mla_w_uq[0], mla_w_uk[0], mla_w_uv[0])
    s5_w = _s5_weights(s5_lambda_re[0], s5_lambda_im[0], s5_log_dt[0], s5_b_re[0], s5_b_im[0],
                       s5_c_re[0], s5_c_im[0], s5_d[0], s5_w_glu[0], s5_b_glu[0])
    wo_e = w_out_even[0].astype(BF16)
    wo_a, wo_b = wo_e[:MLA_W], wo_e[MLA_W:]

    xp2 = x_prompt.reshape(mp, D_MODEL)
    cq, rows_pt, za, u, zb = _even_in(xp2, even_in_w, cos_p, sin_p, tm, seq=t)
    mix_a = _mla_prompt(cq.reshape(b, t, MLA_QL), rows_pt, za.reshape(b, t, MLA_W), mla_w, cos_p, sin_p)
    zeros_p = jnp.zeros((b, S5_S), F32)
    mix_b, sre_p, sim_p = _s5(u.reshape(b, t, S5_W), zb.reshape(b, t, S5_W), s5_w, zeros_p, zeros_p,
                              min(64, t))
    xp1 = _even_out(xp2, mix_a.reshape(mp, MLA_W), mix_b.reshape(mp, S5_W), wo_a, wo_b, tm)

    xs2 = x_sample.reshape(ms, D_MODEL)
    cq_s, rows_s, za_s, u_s, zb_s = _even_in(xs2, even_in_w, cos_st, sin_st, ms)
    pad3 = lambda a, n: _pad_tokens(a.reshape(db, ts, n), SUB)
    mix_a_s = _mla_decode(page_table, _rows_last(cache_mla[0]), pad3(cq_s, MLA_QL), pad3(rows_s, MLA_ROW),
                          pad3(za_s, MLA_W), mla_w, cos_s, sin_s, ts)[:, :ts]
    st = state_s5[0]
    mix_b_s, sre_s, sim_s = _s5(u_s.reshape(db, ts, S5_W), zb_s.reshape(db, ts, S5_W), s5_w,
                                st[..., 0].reshape(db, S5_S), st[..., 1].reshape(db, S5_S), ts)
    xs1 = _even_out(xs2, mix_a_s.reshape(ms, MLA_W), mix_b_s.reshape(ms, S5_W), wo_a, wo_b, ms)

    odd_in_w, cmp_w = _odd_weights(norm_odd[0], w_in_odd[0], nsa_pe_k[0], nsa_pe_v[0], nsa_phi1_k[0],
                                   nsa_phi2_k[0], nsa_phi1_v[0], nsa_phi2_v[0])
    wo_o = w_out_odd[0].astype(BF16)
    gfin = norm_final[None, :]

    q, kv4_t, kw_t, gate, z = _odd_in(xp1, odd_in_w, tm, seq=t)
    nblk_p = t // NSA_BLK
    cb = _compress_prompt(kv4_t, *cmp_w)
    q3 = q.reshape(b, t, NSA_W)
    tt = min(256, t)
    oc_t, imp = _cmp(q3, cb.reshape(b, nblk_p, 2 * NSA_KVW), pos_p[None], tt)
    sel, _ = _topk(imp, pos_p[None], nblk_p, tt)
    o_t = _nsa_prompt(q3, kv4_t, kw_t, oc_t, sel, gate.reshape(b, t, 3 * NSA_H))
    y_prompt = _odd_out_prompt(o_t, z.reshape(b, t, NSA_W), xp1.reshape(b, t, D_MODEL), wo_o, gfin, tt)

    q_s, kv4_s, kw_s, gate_s, z_s = _odd_in(xs1, odd_in_w, ms)
    cache_nsa_t = _rows_last(cache_nsa_kv[0])
    cb_s = _compress_decode(page_table, cache_nsa_t, *cmp_w)
    n_sel_s = -(-(past + ts) // NSA_BLK)
    q_s3 = q_s.reshape(db, ts, NSA_W)
    oc_ts, imp_s = _cmp(_pad_tokens(q_s3, SUB), cb_s, pos_s[None], SUB)
    imp_l = jnp.transpose(imp_s[..., :ts], (1, 2, 0, 3)).reshape(1, NSA_G, cb_s.shape[1], ms)
    pos_l = jnp.tile(pos_s[:ts], db)[None]
    _, idx_l = _topk(imp_l, pos_l, n_sel_s, ms)
    idx_flat = jnp.transpose(idx_l.reshape(NSA_G, NSA_TOPK, db, ts), (2, 0, 3, 1)).reshape(-1)
    q_s4 = q_s.reshape(db, ts, NSA_H, NSA_D)
    o_sel_s = _nsa_sel_decode(idx_flat, page_table, cache_nsa_t, q_s4, pad3(kv4_s, 4 * NSA_KVW), ts, past)
    win_t = _rows_last(state_nsa_win[0])
    kw_s3 = kw_s.reshape(db, ts, 2 * NSA_KVW)
    o_win_s = _nsa_win_decode(_pad_tokens(q_s4, SUB), win_t, _pad_tokens(kw_s3, SUB), ts)
    oc_s = jnp.transpose(oc_ts, (0, 2, 1))[:, :ts].reshape(ms, NSA_W)
    y_sample = _odd_out_decode(oc_s, o_sel_s.reshape(ms, NSA_W), o_win_s[:, :ts].reshape(ms, NSA_W),
                               gate_s, z_s, xs1, wo_o, gfin)

    state = lambda re, im, n: jnp.stack([re, im], axis=-1).reshape(1, n, S5_G, S5_N, 2)
    win_shape = (2, NSA_G, NSA_D)
    if t >= NSA_WIN:
        win_pt = kw_t[:, :, t - NSA_WIN:]
    else:
        win_pt = jnp.pad(kw_t, ((0, 0), (0, 0), (NSA_WIN - t, 0)))
    win_st = jnp.concatenate([win_t[:, :, ts:], jnp.transpose(kw_s3, (0, 2, 1))], axis=2)
    return (y_prompt, y_sample.reshape(db, ts, D_MODEL),
            _rows_second(rows_pt, (MLA_ROW,))[None], rows_s.reshape(1, db, ts, MLA_ROW),
            state(sre_p, sim_p, b), state(sre_s, sim_s, db),
            _rows_second(kv4_t, (4, NSA_G, NSA_D))[None], kv4_s.reshape(1, db, ts, 4, NSA_G, NSA_D),
            _rows_second(win_pt, win_shape)[None], _rows_second(win_st, win_shape)[None])
```

```python
import functools

import jax
import jax.numpy as jnp
import numpy as np
from jax import lax
from jax.experimental import pallas as pl
from jax.experimental.pallas import tpu as pltpu

F32, BF16, I32 = jnp.float32, jnp.bfloat16, jnp.int32

D_MODEL = 1024
PAGE = 128
EPS = 1e-6
ROPE_THETA = 10000.0
MLA_H, MLA_NOPE, MLA_ROPE, MLA_V = 8, 64, 32, 64
MLA_QL, MLA_KVL = 384, 256
MLA_ROW = MLA_KVL + MLA_ROPE
MLA_W = MLA_H * MLA_V
MLA_SCALE = (MLA_NOPE + MLA_ROPE) ** -0.5
S5_G, S5_P, S5_N = 32, 16, 64
S5_W = S5_G * S5_P
S5_S = S5_G * S5_N
S5_GT = 4
S5_NT = S5_G // S5_GT
NSA_H, NSA_G, NSA_D = 16, 2, 64
NSA_R = NSA_H // NSA_G
NSA_W = NSA_H * NSA_D
NSA_KVW = NSA_G * NSA_D
NSA_BLK, NSA_TOPK, NSA_WIN = 64, 16, 512
NSA_SCALE = NSA_D ** -0.5
FORCED_BONUS = float(NSA_R + 1)
BPP = PAGE // NSA_BLK

LOG2E = 1.4426950408889634
NEG = -1e30
GONE = -3e38
TQ = 128
TK = 256
SUB = 8
PACK = 16
VMEM_LIMIT = 56 * 1024 * 1024


def _dot(a, b):
    return jnp.dot(a, b, preferred_element_type=F32)


def _dot_nt(a, b):
    return lax.dot_general(a, b, (((1,), (1,)), ((), ())), preferred_element_type=F32)


def _dot_tn(a, b):
    return lax.dot_general(a, b, (((0,), (0,)), ((), ())), preferred_element_type=F32)


def _rms(x, g):
    return x * lax.rsqrt(jnp.mean(x * x, axis=-1, keepdims=True) + EPS) * g


def _silu(x):
    return x * jax.nn.sigmoid(x)


def _rope_nat(x, cosf, sinf):
    half = x.shape[1] // 2
    xs = jnp.concatenate([x[:, half:], x[:, :half]], axis=1)
    return x * cosf + xs * sinf


def _params(sem):
    return pltpu.CompilerParams(dimension_semantics=sem, vmem_limit_bytes=VMEM_LIMIT)


def _const_spec(shape):
    n = len(shape)
    return pl.BlockSpec(shape, lambda *a, _n=n: (0,) * _n)


def _seq_major_spec(width, tm, seq):
    per = seq // tm
    return pl.BlockSpec((None, width, tm), lambda i: (i // per, 0, i % per))


def _even_in_kernel(feature_major, x_ref, g_ref, wcq, wckv, wkr, wza, wu, wzb, gq, gkv, cos_ref, sin_ref,
                    cq_o, rows_o, za_o, u_o, zb_o):
    h = _rms(x_ref[...], g_ref[...]).astype(BF16)
    cq_o[...] = _rms(_dot(h, wcq[...]), gq[...])
    ckv = _rms(_dot(h, wckv[...]), gkv[...])
    krope = _rope_nat(_dot(h, wkr[...]), cos_ref[...], sin_ref[...])
    if feature_major:
        rows_o[:MLA_KVL, :] = ckv.T
        rows_o[MLA_KVL:, :] = krope.T
    else:
        rows_o[:, :MLA_KVL] = ckv
        rows_o[:, MLA_KVL:] = krope
    za_o[...] = _dot(h, wza[...])
    u_o[...] = _dot(h, wu[...])
    zb_o[...] = _dot(h, wzb[...])


def _even_in(x2, wts, cosf, sinf, tm, seq=None):
    m = x2.shape[0]
    tab_blocks = cosf.shape[0] // tm
    row = lambda n: pl.BlockSpec((tm, n), lambda i: (i, 0))
    tab = pl.BlockSpec((tm, MLA_ROPE), lambda i: (i % tab_blocks, 0))
    widths = (MLA_QL, MLA_ROW, MLA_W, S5_W, S5_W)
    out_specs = [row(n) for n in widths]
    out_shape = [jax.ShapeDtypeStruct((m, n), F32) for n in widths]
    if seq is not None:
        out_specs[1] = _seq_major_spec(MLA_ROW, tm, seq)
        out_shape[1] = jax.ShapeDtypeStruct((m // seq, MLA_ROW, seq), F32)
    return pl.pallas_call(
        functools.partial(_even_in_kernel, seq is not None),
        grid=(m // tm,),
        in_specs=[row(D_MODEL)] + [_const_spec(c.shape) for c in wts] + [tab, tab],
        out_specs=out_specs,
        out_shape=out_shape,
        compiler_params=_params(("parallel",)),
        name="even_in",
    )(x2, *wts, cosf, sinf)


def _mla_queries(cq, wuqn, wuqr, wuk, cosf, sinf, h):
    cqb = cq.astype(BF16)
    qn = _dot(cqb, wuqn[h])
    ql = _dot(qn.astype(BF16), wuk[h])
    qr = _rope_nat(_dot(cqb, wuqr[h]), cosf, sinf)
    return ql, qr


def _flash_tiles(lo, hi, qk, pv, first_mask, last_mask, c_exp, s_buf, m_s, l_s, acc_s):
    m_s[...] = jnp.full(m_s.shape, NEG, F32)
    l_s[...] = jnp.zeros(l_s.shape, F32)
    acc_s[...] = jnp.zeros(acc_s.shape, F32)
    s_buf[0] = first_mask(qk(lo), lo)

    def consume(j, s):
        m_old = m_s[...]
        m_new = jnp.maximum(m_old, jnp.max(s, axis=0, keepdims=True))
        alpha = jnp.exp2((m_old - m_new) * c_exp)
        p = jnp.exp2((s - m_new) * c_exp)
        l_s[...] = alpha * l_s[...] + jnp.sum(p, axis=0, keepdims=True)
        acc_s[...] = alpha * acc_s[...] + pv(j, p.astype(BF16))
        m_s[...] = m_new

    def body(j, carry):
        s_buf[1] = qk(j + 1)
        consume(j, s_buf[0])
        s_buf[0] = s_buf[1]
        return carry

    lax.fori_loop(lo, hi - 1, body, 0)
    last = hi - 1
    consume(last, last_mask(s_buf[0], last))
    return acc_s[...] / l_s[...]


def _mla_prompt_kernel(cq_ref, rows_ref, za_ref, wuqn_t, wuqr_t, wuk_t, wuv_t, cos_ref, sin_ref,
                       out_ref, qtl, qtr, s_buf, m_s, l_s, acc_s):
    qi = pl.program_id(1)
    r = MLA_H * TQ
    half = MLA_ROPE // 2
    cq_t = cq_ref[...].T.astype(BF16)
    qn_t = _dot(wuqn_t[...], cq_t).astype(BF16)
    qr_t = _dot(wuqr_t[...], cq_t)
    cos_t, sin_t = cos_ref[...], sin_ref[...]
    for h in range(MLA_H):
        qtl[:, h * TQ:(h + 1) * TQ] = _dot(wuk_t[h], qn_t[h * MLA_NOPE:(h + 1) * MLA_NOPE, :]).astype(BF16)
        x = qr_t[h * MLA_ROPE:(h + 1) * MLA_ROPE, :]
        xs = jnp.concatenate([x[half:, :], x[:half, :]], axis=0)
        qtr[:, h * TQ:(h + 1) * TQ] = (x * cos_t + xs * sin_t).astype(BF16)
    qpos = qi * TQ + (lax.broadcasted_iota(I32, (1, r), 1) & (TQ - 1))
    krow = lax.broadcasted_iota(I32, (TK, 1), 0)

    def keys(j):
        return rows_ref[:, pl.ds(pl.multiple_of(j * TK, TK), TK)]

    def qk(j):
        kt = keys(j)
        return (_dot_tn(kt[:MLA_KVL, :].astype(BF16), qtl[...])
                + _dot_tn(kt[MLA_KVL:, :].astype(BF16), qtr[...]))

    def pv(j, p):
        return _dot(keys(j)[:MLA_KVL, :].astype(BF16), p)

    def causal(s, j):
        return jnp.where(j * TK + krow <= qpos, s, NEG)

    n_tiles = (qi * TQ + TQ + TK - 1) // TK
    o = _flash_tiles(0, n_tiles, qk, pv, lambda s, j: s, causal, MLA_SCALE * LOG2E,
                     s_buf, m_s, l_s, acc_s).astype(BF16)
    heads = [_dot(wuv_t[h], o[:, h * TQ:(h + 1) * TQ]) for h in range(MLA_H)]
    out_ref[...] = jnp.concatenate(heads, axis=0).T * _silu(za_ref[...])


def _mla_prompt(cq, rows_t, za, wts, cos_t, sin_t):
    b, t, _ = cq.shape
    r = MLA_H * TQ
    tok = lambda n: pl.BlockSpec((None, TQ, n), lambda bi, qi: (bi, qi, 0))
    tab = pl.BlockSpec((MLA_ROPE, TQ), lambda bi, qi: (0, qi))
    return pl.pallas_call(
        _mla_prompt_kernel,
        grid=(b, t // TQ),
        in_specs=[tok(MLA_QL), pl.BlockSpec((None, MLA_ROW, t), lambda bi, qi: (bi, 0, 0)), tok(MLA_W)]
        + [_const_spec(w.shape) for w in wts] + [tab, tab],
        out_specs=tok(MLA_W),
        out_shape=jax.ShapeDtypeStruct((b, t, MLA_W), F32),
        scratch_shapes=[pltpu.VMEM((MLA_KVL, r), BF16), pltpu.VMEM((MLA_ROPE, r), BF16),
                        pltpu.VMEM((2, TK, r), F32),
                        pltpu.VMEM((1, r), F32), pltpu.VMEM((1, r), F32), pltpu.VMEM((MLA_KVL, r), F32)],
        compiler_params=_params(("parallel", "arbitrary")),
        name="mla_prompt",
    )(cq, rows_t, za, *wts, cos_t, sin_t)


def _mla_decode_kernel(n_pages_step, ts, pt_ref, cq_ref, rows_ref, za_ref, wuqn, wuqr, wuk, wuv,
                       cos_ref, sin_ref, *rest):
    pages = rest[:n_pages_step]
    out_ref, ql_s, qr_s, m_s, l_s, acc_s = rest[n_pages_step:]
    gi = pl.program_id(1)
    rows = MLA_H * SUB

    @pl.when(gi == 0)
    def _():
        cq = cq_ref[...]
        cosf, sinf = cos_ref[...], sin_ref[...]
        for h in range(MLA_H):
            ql, qr = _mla_queries(cq, wuqn, wuqr, wuk, cosf, sinf, h)
            ql_s[h * SUB:(h + 1) * SUB, :] = ql.astype(BF16)
            qr_s[h * SUB:(h + 1) * SUB, :] = qr.astype(BF16)
        m_s[...] = jnp.full((rows, 1), NEG, F32)
        l_s[...] = jnp.zeros((rows, 1), F32)
        acc_s[...] = jnp.zeros((rows, MLA_KVL), F32)

    def update(s, vals, feature_major):
        m_old = m_s[...]
        m_new = jnp.maximum(m_old, jnp.max(s, axis=1, keepdims=True))
        alpha = jnp.exp(m_old - m_new)
        p = jnp.exp(s - m_new)
        l_s[...] = alpha * l_s[...] + jnp.sum(p, axis=1, keepdims=True)
        pv = None
        for (lo, hi), v in vals:
            pj = p[:, lo:hi].astype(BF16)
            term = _dot_nt(pj, v) if feature_major else _dot(pj, v)
            pv = term if pv is None else pv + term
        acc_s[...] = alpha * acc_s[...] + pv
        m_s[...] = m_new

    ql, qr = ql_s[...], qr_s[...]
    scores, vals = [], []
    for j, pg in enumerate(pages):
        k = pg[...]
        ckv = k[:MLA_KVL, :].astype(BF16)
        scores.append(_dot(ql, ckv) + _dot(qr, k[MLA_KVL:, :].astype(BF16)))
        vals.append(((j * PAGE, (j + 1) * PAGE), ckv))
    update(jnp.concatenate(scores, axis=1) * MLA_SCALE, vals, True)

    @pl.when(gi == pl.num_programs(1) - 1)
    def _():
        kn = rows_ref[...]
        ckv = kn[:, :MLA_KVL].astype(BF16)
        s = (_dot_nt(ql, ckv) + _dot_nt(qr, kn[:, MLA_KVL:].astype(BF16))) * MLA_SCALE
        tq = lax.broadcasted_iota(I32, (rows, SUB), 0) & (SUB - 1)
        jk = lax.broadcasted_iota(I32, (rows, SUB), 1)
        s = jnp.where((jk <= tq) & (jk < ts), s, NEG)
        update(s, [((0, SUB), ckv)], False)
        o = (acc_s[...] / l_s[...]).astype(BF16)
        heads = [_dot(o[h * SUB:(h + 1) * SUB, :], wuv[h]) for h in range(MLA_H)]
        out_ref[...] = jnp.concatenate(heads, axis=1) * _silu(za_ref[...])


def _mla_decode(page_table, cache_t, cq, rows, za, wts, cosf, sinf, ts, n_pages_step=16):
    db, n_pages = page_table.shape
    rows_n = MLA_H * SUB
    tok = lambda n: pl.BlockSpec((None, SUB, n), lambda b, g, pt: (b, 0, 0))
    cst = lambda shape: pl.BlockSpec(shape, lambda b, g, pt, _n=len(shape): (0,) * _n)
    page_specs = [
        pl.BlockSpec((None, MLA_ROW, PAGE), lambda b, g, pt, j=j: (pt[b, g * n_pages_step + j], 0, 0))
        for j in range(n_pages_step)]
    grid_spec = pltpu.PrefetchScalarGridSpec(
        num_scalar_prefetch=1,
        grid=(db, n_pages // n_pages_step),
        in_specs=[tok(MLA_QL), tok(MLA_ROW), tok(MLA_W)] + [cst(w.shape) for w in wts]
        + [cst(cosf.shape), cst(sinf.shape)] + page_specs,
        out_specs=tok(MLA_W),
        scratch_shapes=[pltpu.VMEM((rows_n, MLA_KVL), BF16), pltpu.VMEM((rows_n, MLA_ROPE), BF16),
                        pltpu.VMEM((rows_n, 1), F32), pltpu.VMEM((rows_n, 1), F32),
                        pltpu.VMEM((rows_n, MLA_KVL), F32)])
    return pl.pallas_call(
        functools.partial(_mla_decode_kernel, n_pages_step, ts),
        grid_spec=grid_spec,
        out_shape=jax.ShapeDtypeStruct((db, SUB, MLA_W), F32),
        compiler_params=_params(("parallel", "arbitrary")),
        name="mla_decode",
    )(page_table, cq, rows, za, *wts, cosf, sinf, *([cache_t] * n_pages_step))


def _s5_kernel(u_ref, zb_ref, lre_ref, lim_ref, ldt_ref, wbre, wbim, wcre, wcim, d_ref, wglu, bglu,
               h0re_ref, h0im_ref, mix_o, sre_o, sim_o, bure, buim, hre, him):
    step = pl.program_id(0)
    nb, chunk, _ = u_ref.shape
    rows = nb * chunk

    @pl.when(step == 0)
    def _():
        hre[...] = h0re_ref[...]
        him[...] = h0im_ref[...]

    lre, lim = lre_ref[...], lim_ref[...]
    dt = jnp.exp(ldt_ref[...])
    mag = jnp.exp(lre * dt)
    are, aim = mag * jnp.cos(lim * dt), mag * jnp.sin(lim * dt)
    den = lre * lre + lim * lim
    cre = ((are - 1.0) * lre + aim * lim) / den
    cim = (aim * lre - (are - 1.0) * lim) / den

    ut = jnp.swapaxes(u_ref[...], 0, 1).reshape(rows, S5_W)
    kw, nw = S5_GT * S5_P, S5_GT * S5_N
    for jt in range(S5_NT):
        cr, ci = cre[:, jt * nw:(jt + 1) * nw], cim[:, jt * nw:(jt + 1) * nw]
        bre = (cr * wbre[jt] - ci * wbim[jt]).astype(BF16)
        bim = (cr * wbim[jt] + ci * wbre[jt]).astype(BF16)
        uj = ut[:, jt * kw:(jt + 1) * kw].astype(BF16)
        bure[:, jt * nw:(jt + 1) * nw] = _dot(uj, bre)
        buim[:, jt * nw:(jt + 1) * nw] = _dot(uj, bim)

    def scan(t, carry):
        hr, hi = carry
        sl = pl.ds(pl.multiple_of(t * nb, SUB), nb)
        nr = are * hr - aim * hi + bure[sl, :]
        ni = are * hi + aim * hr + buim[sl, :]
        bure[sl, :] = nr
        buim[sl, :] = ni
        return nr, ni

    hr, hi = lax.fori_loop(0, chunk, scan, (hre[...], him[...]))
    hre[...] = hr
    him[...] = hi
    sre_o[...] = hr
    sim_o[...] = hi

    ys = []
    for jt in range(S5_NT):
        sr = bure[:, jt * nw:(jt + 1) * nw].astype(BF16)
        si = buim[:, jt * nw:(jt + 1) * nw].astype(BF16)
        ys.append(_dot(sr, wcre[jt].astype(BF16)) - _dot(si, wcim[jt].astype(BF16)))
    y = jnp.concatenate(ys, axis=1) + d_ref[...] * ut
    g5 = jax.nn.gelu(y)
    ob = g5 * jax.nn.sigmoid(_dot(g5.astype(BF16), wglu[...]) + bglu[...])
    mix_o[...] = jnp.swapaxes(ob.reshape(chunk, nb, S5_W), 0, 1) * _silu(zb_ref[...])


def _s5(u, zb, wts, h0re, h0im, chunk):
    nb, t, _ = u.shape
    tok = pl.BlockSpec((nb, chunk, S5_W), lambda i: (0, i, 0))
    st = _const_spec((nb, S5_S))
    return pl.pallas_call(
        _s5_kernel,
        grid=(t // chunk,),
        in_specs=[tok, tok] + [_const_spec(w.shape) for w in wts] + [st, st],
        out_specs=[tok, st, st],
        out_shape=[jax.ShapeDtypeStruct((nb, t, S5_W), F32), jax.ShapeDtypeStruct((nb, S5_S), F32),
                   jax.ShapeDtypeStruct((nb, S5_S), F32)],
        scratch_shapes=[pltpu.VMEM((nb * chunk, S5_S), F32), pltpu.VMEM((nb * chunk, S5_S), F32),
                        pltpu.VMEM((nb, S5_S), F32), pltpu.VMEM((nb, S5_S), F32)],
        compiler_params=_params(("arbitrary",)),
        name="s5",
    )(u, zb, *wts, h0re, h0im)


def _even_out_kernel(x_ref, a_ref, b_ref, wa, wb, out_ref):
    out_ref[...] = (x_ref[...] + _dot(a_ref[...].astype(BF16), wa[...])
                    + _dot(b_ref[...].astype(BF16), wb[...]))


def _even_out(x2, mixa, mixb, wa, wb, tm):
    m = x2.shape[0]
    row = lambda n: pl.BlockSpec((tm, n), lambda i: (i, 0))
    return pl.pallas_call(
        _even_out_kernel,
        grid=(m // tm,),
        in_specs=[row(D_MODEL), row(MLA_W), row(S5_W), _const_spec(wa.shape), _const_spec(wb.shape)],
        out_specs=row(D_MODEL),
        out_shape=jax.ShapeDtypeStruct((m, D_MODEL), F32),
        compiler_params=_params(("parallel",)),
        name="even_out",
    )(x2, mixa, mixb, wa, wb)


def _odd_in_kernel(feature_major, x_ref, g_ref, wq, wkv4, wkw, wg, wz, q_o, kv4_o, kw_o, gate_o, z_o):
    h = _rms(x_ref[...], g_ref[...]).astype(BF16)
    q_o[...] = _dot(h, wq[...])
    kv4, kw = _dot(h, wkv4[...]), _dot(h, wkw[...])
    kv4_o[...] = kv4.T if feature_major else kv4
    kw_o[...] = kw.T if feature_major else kw
    gate_o[...] = jax.nn.sigmoid(_dot(h, wg[...]))
    z_o[...] = _dot(h, wz[...])


def _odd_in(x2, wts, tm, seq=None):
    m = x2.shape[0]
    row = lambda n: pl.BlockSpec((tm, n), lambda i: (i, 0))
    widths = (NSA_W, 4 * NSA_KVW, 2 * NSA_KVW, 3 * NSA_H, NSA_W)
    out_specs = [row(n) for n in widths]
    out_shape = [jax.ShapeDtypeStruct((m, n), F32) for n in widths]
    if seq is not None:
        for i in (1, 2):
            out_specs[i] = _seq_major_spec(widths[i], tm, seq)
            out_shape[i] = jax.ShapeDtypeStruct((m // seq, widths[i], seq), F32)
    return pl.pallas_call(
        functools.partial(_odd_in_kernel, seq is not None),
        grid=(m // tm,),
        in_specs=[row(D_MODEL)] + [_const_spec(w.shape) for w in wts],
        out_specs=out_specs,
        out_shape=out_shape,
        compiler_params=_params(("parallel",)),
        name="odd_in",
    )(x2, *wts)


def _compress_stage(xs, x_t, pe, base):
    n = x_t.shape[1] // NSA_BLK
    x = x_t.T.reshape(n, NSA_BLK, 2 * NSA_KVW) + pe[None]
    xs[:, pl.ds(pl.multiple_of(base, PACK), n), :] = jnp.swapaxes(x, 0, 1).astype(BF16)


def _compress_finish(xs, w1, phi2, out_ref):
    acc = jnp.zeros((xs.shape[1], 2 * NSA_KVW), F32)
    for r in range(NSA_BLK):
        acc = acc + _dot(xs[r], w1[r])
    out_ref[...] = _dot(_silu(acc).astype(BF16), phi2[...])


def _compress_prompt_kernel(x_ref, pe_ref, w1, phi2, out_ref, xs):
    bi = pl.program_id(0)
    nblk = x_ref.shape[1] // NSA_BLK
    _compress_stage(xs, x_ref[...], pe_ref[...], bi * nblk)

    @pl.when(bi == pl.num_programs(0) - 1)
    def _():
        _compress_finish(xs, w1, phi2, out_ref)


def _compress_prompt(kv4_t, pe, w1, phi2):
    b, _, t = kv4_t.shape
    cw = 2 * NSA_KVW
    nblk = b * (t // NSA_BLK)
    return pl.pallas_call(
        _compress_prompt_kernel,
        grid=(b,),
        in_specs=[pl.BlockSpec((None, cw, t), lambda i: (i, 0, 0)),
                  _const_spec(pe.shape), _const_spec(w1.shape), _const_spec(phi2.shape)],
        out_specs=pl.BlockSpec((nblk, cw), lambda i: (0, 0)),
        out_shape=jax.ShapeDtypeStruct((nblk, cw), F32),
        scratch_shapes=[pltpu.VMEM((NSA_BLK, nblk, cw), BF16)],
        compiler_params=_params(("arbitrary",)),
        name="compress_prompt",
    )(kv4_t, pe, w1, phi2)


def _compress_decode_kernel(n_pages_step, pt_ref, pe_ref, w1, phi2, *rest):
    pages = rest[:n_pages_step]
    out_ref, xs = rest[n_pages_step:]
    gi = pl.program_id(1)
    grp = PACK // BPP
    pe = pe_ref[...]
    for k in range(n_pages_step // grp):
        x_t = jnp.concatenate([pages[k * grp + j][...] for j in range(grp)], axis=1)
        _compress_stage(xs, x_t, pe, (gi * (n_pages_step // grp) + k) * PACK)

    @pl.when(gi == pl.num_programs(1) - 1)
    def _():
        _compress_finish(xs, w1, phi2, out_ref)


def _compress_decode(page_table, cache_t, pe, w1, phi2, n_pages_step=16):
    db, n_pages = page_table.shape
    cw = 2 * NSA_KVW
    nblk = n_pages * BPP
    cst = lambda shape: pl.BlockSpec(shape, lambda b, g, pt, _n=len(shape): (0,) * _n)
    page_specs = [
        pl.BlockSpec((None, cw, PAGE), lambda b, g, pt, j=j: (pt[b, g * n_pages_step + j], 0, 0))
        for j in range(n_pages_step)]
    grid_spec = pltpu.PrefetchScalarGridSpec(
        num_scalar_prefetch=1,
        grid=(db, n_pages // n_pages_step),
        in_specs=[cst(pe.shape), cst(w1.shape), cst(phi2.shape)] + page_specs,
        out_specs=pl.BlockSpec((None, nblk, cw), lambda b, g, pt: (b, 0, 0)),
        scratch_shapes=[pltpu.VMEM((NSA_BLK, nblk, cw), BF16)])
    return pl.pallas_call(
        functools.partial(_compress_decode_kernel, n_pages_step),
        grid_spec=grid_spec,
        out_shape=jax.ShapeDtypeStruct((db, nblk, cw), F32),
        compiler_params=_params(("parallel", "arbitrary")),
        name="compress_decode",
    )(page_table, pe, w1, phi2, *([cache_t] * n_pages_step))


def _alibi_slope(h):
    return 2.0 ** (-8.0 * (h + 1) / NSA_H)


def _split_bf16(x):
    hi = x.astype(BF16)
    return hi, (x - hi.astype(F32)).astype(BF16)


def _cmp_kernel(q_ref, cb_ref, pos_ref, oc_o, imp_o, s_scr):
    tt = q_ref.shape[0]
    nc = cb_ref.shape[0]
    qpos = pos_ref[...]
    cpos = lax.broadcasted_iota(I32, (nc, 1), 0) * NSA_BLK + (NSA_BLK - 1)
    visible = cpos <= qpos
    dist = (qpos - cpos).astype(F32)
    q = q_ref[...] * NSA_SCALE
    cb = cb_ref[...]
    for g in range(NSA_G):
        k_hi, k_lo = _split_bf16(cb[:, g * NSA_D:(g + 1) * NSA_D])
        vc = cb[:, NSA_KVW + g * NSA_D:NSA_KVW + (g + 1) * NSA_D].astype(BF16)
        for i in range(NSA_R):
            h = g * NSA_R + i
            q_hi, q_lo = _split_bf16(q[:, h * NSA_D:(h + 1) * NSA_D])
            s = _dot_nt(k_hi, q_hi) + _dot_nt(k_hi, q_lo) + _dot_nt(k_lo, q_hi)
            s_scr[i * nc:(i + 1) * nc, :] = s - _alibi_slope(h) * dist
        s3 = jnp.where(visible[None], s_scr[...].reshape(NSA_R, nc, tt), NEG)
        mx = jnp.max(s3, axis=1, keepdims=True)
        e = jnp.where(visible[None], jnp.exp(s3 - mx), 0.0)
        den = jnp.sum(e, axis=1, keepdims=True)
        p = e / jnp.where(den > 0, den, 1.0)
        imp_o[g] = jnp.sum(p, axis=0)
        for i in range(NSA_R):
            h = g * NSA_R + i
            oc_o[h * NSA_D:(h + 1) * NSA_D, :] = _dot_tn(vc, p[i].astype(BF16))


def _cmp(q, cb, pos, tt):
    b, t, _ = q.shape
    nc = cb.shape[1]
    return pl.pallas_call(
        _cmp_kernel,
        grid=(b, t // tt),
        in_specs=[pl.BlockSpec((None, tt, NSA_W), lambda bi, ti: (bi, ti, 0)),
                  pl.BlockSpec((None, nc, 2 * NSA_KVW), lambda bi, ti: (bi, 0, 0)),
                  pl.BlockSpec((1, tt), lambda bi, ti: (0, ti))],
        out_specs=[pl.BlockSpec((None, NSA_W, tt), lambda bi, ti: (bi, 0, ti)),
                   pl.BlockSpec((None, NSA_G, nc, tt), lambda bi, ti: (bi, 0, 0, ti))],
        out_shape=[jax.ShapeDtypeStruct((b, NSA_W, t), F32),
                   jax.ShapeDtypeStruct((b, NSA_G, nc, t), F32)],
        scratch_shapes=[pltpu.VMEM((NSA_R * nc, tt), F32)],
        compiler_params=_params(("parallel", "parallel")),
        name="nsa_cmp",
    )(q, cb, pos)


def _topk_kernel(n_sel, imp_ref, pos_ref, sel_o, idx_o):
    _, nc, tt = imp_ref.shape
    nsp = sel_o.shape[1]
    qpos = pos_ref[...]
    blk = lax.broadcasted_iota(I32, (nsp, 1), 0)
    cur = jnp.right_shift(qpos, NSA_BLK.bit_length() - 1)
    forced = (blk == 0) | (blk == cur) | (blk == cur - 1)
    allowed = (blk <= cur) & (blk < n_sel)
    for g in range(NSA_G):
        imp = imp_ref[g]
        if nsp > nc:
            imp = jnp.concatenate([imp, jnp.zeros((nsp - nc, tt), F32)], axis=0)
        score = jnp.where(allowed, imp + jnp.where(forced, FORCED_BONUS, 0.0), NEG)
        chosen = jnp.zeros((nsp, tt), F32)
        picks = []
        for _ in range(NSA_TOPK):
            mx = jnp.max(score, axis=0, keepdims=True)
            first = jnp.min(jnp.where(score == mx, blk, nsp), axis=0, keepdims=True)
            hit = blk == first
            chosen = jnp.where(hit, 1.0, chosen)
            score = jnp.where(hit, GONE, score)
            picks.append(first)
        sel_o[g] = chosen
        idx_o[g] = jnp.concatenate(picks, axis=0)


def _topk(imp, pos, n_sel, tt):
    b, _, nc, t = imp.shape
    nsp = -(-(n_sel + TK // NSA_BLK) // SUB) * SUB
    return pl.pallas_call(
        functools.partial(_topk_kernel, n_sel),
        grid=(b, t // tt),
        in_specs=[pl.BlockSpec((None, NSA_G, nc, tt), lambda bi, ti: (bi, 0, 0, ti)),
                  pl.BlockSpec((1, tt), lambda bi, ti: (0, ti))],
        out_specs=[pl.BlockSpec((None, NSA_G, nsp, tt), lambda bi, ti: (bi, 0, 0, ti)),
                   pl.BlockSpec((None, NSA_G, NSA_TOPK, tt), lambda bi, ti: (bi, 0, 0, ti))],
        out_shape=[jax.ShapeDtypeStruct((b, NSA_G, nsp, t), F32),
                   jax.ShapeDtypeStruct((b, NSA_G, NSA_TOPK, t), I32)],
        compiler_params=_params(("parallel", "parallel")),
        name="nsa_topk",
    )(imp, pos)


def _bf16_parts(x, n):
    parts = []
    for _ in range(n):
        bits = np.asarray(x, np.float32).view(np.uint32)
        top = ((bits + np.uint32(0x7FFF) + ((bits >> np.uint32(16)) & np.uint32(1)))
               & np.uint32(0xFFFF0000)).view(np.float32)
        parts.append(float(top))
        x = float(np.float32(x) - top)
    return parts


def _nsa_prompt_kernel(q_ref, kv4_ref, kw_ref, oc_ref, sel_ref, gate_ref, out_ref,
                       k_aug, q_aug, s_buf, m_s, l_s, acc_s):
    qi = pl.program_id(1)
    r = NSA_R * TQ
    bpt = TK // NSA_BLK
    n_parts = SUB // 2
    qt = (q_ref[...] * NSA_SCALE).T.astype(BF16)
    gt = gate_ref[...].T
    lane = lax.broadcasted_iota(I32, (1, r), 1)
    qpos = qi * TQ + (lane & (TQ - 1))
    head = lane // TQ
    krow = lax.broadcasted_iota(I32, (TK, 1), 0)
    klane = lax.broadcasted_iota(I32, (1, TK), 1)
    row8 = lax.broadcasted_iota(I32, (SUB, 1), 0)
    n_tiles = (qi * TQ + TQ + TK - 1) // TK
    onehot = jnp.where(row8 == jnp.right_shift(klane, NSA_BLK.bit_length() - 1), 1.0, 0.0)

    def key_rows(j):
        rel = j * TK - qi * TQ + klane
        coarse = (rel & -NSA_BLK).astype(F32)
        fine = (rel & (NSA_BLK - 1)).astype(F32)
        alibi = jnp.where((row8 & 1) == 0, coarse, fine)
        return jnp.concatenate([alibi, onehot], axis=0).astype(BF16)

    def causal(s, j):
        return jnp.where(j * TK + krow <= qpos, s, NEG)

    def in_window(s, j):
        return jnp.where(qpos - (j * TK + krow) <= NSA_WIN, s, NEG)

    for g in range(NSA_G):
        q_aug[:NSA_D, :] = jnp.concatenate(
            [qt[(g * NSA_R + i) * NSA_D:(g * NSA_R + i + 1) * NSA_D, :] for i in range(NSA_R)], axis=1)
        slope8 = jnp.zeros((SUB, r), F32)
        for i in range(NSA_R):
            parts = _bf16_parts(_alibi_slope(g * NSA_R + i), n_parts)
            col = jnp.zeros((SUB, 1), F32)
            for k, part in enumerate(parts):
                col = jnp.where(jnp.right_shift(row8, 1) == k, part, col)
            slope8 = jnp.where(head == i, col, slope8)
        ks_lo, vs_lo = 2 * NSA_KVW + g * NSA_D, 3 * NSA_KVW + g * NSA_D
        kw_lo, vw_lo = g * NSA_D, NSA_KVW + g * NSA_D

        def tile(ref, lo, j):
            return ref[lo:lo + NSA_D, pl.ds(pl.multiple_of(j * TK, TK), TK)].astype(BF16)

        def qk_sel(j, g=g, ks_lo=ks_lo, slope8=slope8):
            k_aug[:NSA_D, :] = tile(kv4_ref, ks_lo, j)
            k_aug[NSA_D:, :] = key_rows(j)
            flags = sel_ref[g, pl.ds(j * bpt, SUB), :]
            off = jnp.where(flags > 0.5, 0.0, NEG)
            q_aug[NSA_D:, :] = jnp.concatenate(
                [slope8, jnp.concatenate([off] * NSA_R, axis=1)], axis=0).astype(BF16)
            return _dot_tn(k_aug[...], q_aug[...])

        def qk_win(j, kw_lo=kw_lo):
            k_aug[:NSA_D, :] = tile(kw_ref, kw_lo, j)
            k_aug[NSA_D:, :] = key_rows(j)
            return _dot_tn(k_aug[...], q_aug[...])

        o_sel = _flash_tiles(0, n_tiles, qk_sel, lambda j, p, lo=vs_lo: _dot(tile(kv4_ref, lo, j), p),
                             lambda s, j: s, causal, LOG2E, s_buf, m_s, l_s, acc_s)
        q_aug[NSA_D:, :] = jnp.concatenate([slope8, jnp.zeros((SUB, r), F32)], axis=0).astype(BF16)
        o_win = _flash_tiles(jnp.maximum(qi * TQ - NSA_WIN, 0) // TK, n_tiles, qk_win,
                             lambda j, p, lo=vw_lo: _dot(tile(kw_ref, lo, j), p),
                             in_window, lambda s, j: causal(in_window(s, j), j), LOG2E,
                             s_buf, m_s, l_s, acc_s)
        for i in range(NSA_R):
            h = g * NSA_R + i
            rows = slice(h * NSA_D, (h + 1) * NSA_D)
            cols = slice(i * TQ, (i + 1) * TQ)
            out_ref[rows, :] = (gt[h:h + 1, :] * oc_ref[rows, :]
                                + gt[NSA_H + h:NSA_H + h + 1, :] * o_sel[:, cols]
                                + gt[2 * NSA_H + h:2 * NSA_H + h + 1, :] * o_win[:, cols])


def _nsa_prompt(q, kv4_t, kw_t, oc_t, sel, gate):
    b, t, _ = q.shape
    nsp = sel.shape[2]
    r = NSA_R * TQ
    tok = lambda n: pl.BlockSpec((None, TQ, n), lambda bi, qi: (bi, qi, 0))
    full = lambda n: pl.BlockSpec((None, n, t), lambda bi, qi: (bi, 0, 0))
    return pl.pallas_call(
        _nsa_prompt_kernel,
        grid=(b, t // TQ),
        in_specs=[tok(NSA_W), full(4 * NSA_KVW), full(2 * NSA_KVW),
                  pl.BlockSpec((None, NSA_W, TQ), lambda bi, qi: (bi, 0, qi)),
                  pl.BlockSpec((None, NSA_G, nsp, TQ), lambda bi, qi: (bi, 0, 0, qi)),
                  tok(3 * NSA_H)],
        out_specs=pl.BlockSpec((None, NSA_W, TQ), lambda bi, qi: (bi, 0, qi)),
        out_shape=jax.ShapeDtypeStruct((b, NSA_W, t), F32),
        scratch_shapes=[pltpu.VMEM((NSA_D + 2 * SUB, TK), BF16), pltpu.VMEM((NSA_D + 2 * SUB, r), BF16),
                        pltpu.VMEM((2, TK, r), F32),
                        pltpu.VMEM((1, r), F32), pltpu.VMEM((1, r), F32), pltpu.VMEM((NSA_D, r), F32)],
        compiler_params=_params(("parallel", "arbitrary")),
        name="nsa_prompt",
    )(q, kv4_t, kw_t, oc_t, sel, gate)


def _softmax_rows(parts):
    mx = None
    for s, _, _ in parts:
        pm = jnp.max(s, axis=1, keepdims=True)
        mx = pm if mx is None else jnp.maximum(mx, pm)
    den, num = None, None
    for s, v, feature_major in parts:
        p = jnp.exp(s - mx)
        d = jnp.sum(p, axis=1, keepdims=True)
        n = _dot_nt(p.astype(BF16), v) if feature_major else _dot(p.astype(BF16), v)
        den = d if den is None else den + d
        num = n if num is None else num + n
    return num / den


def _nsa_sel_decode_kernel(ts, pos0, nbp, idx_ref, pt_ref, q_ref, new_ref, *rest):
    nblk = NSA_G * NSA_TOPK
    pages = rest[:nblk]
    out_ref = rest[nblk]
    b, t = pl.program_id(0), pl.program_id(1)
    q = q_ref[...] * NSA_SCALE
    new = new_ref[...]
    lane = lax.broadcasted_iota(I32, (1, PAGE), 1)
    jn = lax.broadcasted_iota(I32, (1, SUB), 1)
    for g in range(NSA_G):
        qg = q[g * NSA_R:(g + 1) * NSA_R, :].astype(BF16)
        slope = jnp.concatenate(
            [jnp.full((1, 1), _alibi_slope(g * NSA_R + i), F32) for i in range(NSA_R)], axis=0)
        parts = []
        has_new = jnp.zeros((), I32)
        for k in range(NSA_TOPK):
            bid = idx_ref[((b * NSA_G + g) * ts + t) * NSA_TOPK + k]
            pg = pages[g * NSA_TOPK + k][...]
            kk = pg[g * NSA_D:(g + 1) * NSA_D, :].astype(BF16)
            vv = pg[NSA_KVW + g * NSA_D:NSA_KVW + (g + 1) * NSA_D, :].astype(BF16)
            sub = bid % BPP
            rel = ((bid - sub) * NSA_BLK - pos0 + lane).astype(F32)
            s = _dot(qg, kk) + slope * rel
            mine = (lane // NSA_BLK == sub) & (bid < nbp)
            parts.append((jnp.where(mine, s, NEG), vv, True))
            has_new = has_new + (bid >= nbp).astype(I32)
        kn = new[:, 2 * NSA_KVW + g * NSA_D:2 * NSA_KVW + (g + 1) * NSA_D].astype(BF16)
        vn = new[:, 3 * NSA_KVW + g * NSA_D:3 * NSA_KVW + (g + 1) * NSA_D].astype(BF16)
        sn = _dot_nt(qg, kn) + slope * jn.astype(F32)
        ok = (jn <= t) & (jn < ts) & (has_new > 0)
        parts.append((jnp.where(ok, sn, NEG), vn, False))
        out_ref[g * NSA_R:(g + 1) * NSA_R, :] = _softmax_rows(parts)


def _nsa_sel_decode(idx_flat, page_table, cache_t, q4, kv4_new, ts, pos0):
    db, n_pages = page_table.shape
    nbp = n_pages * BPP
    cw = 2 * NSA_KVW

    def page_map(b, t, idx, pt, g, k):
        bid = jnp.minimum(idx[((b * NSA_G + g) * ts + t) * NSA_TOPK + k], nbp - 1)
        return (pt[b, bid // BPP], 1, 0)

    page_specs = [pl.BlockSpec((None, cw, PAGE), functools.partial(page_map, g=g, k=k))
                  for g in range(NSA_G) for k in range(NSA_TOPK)]
    grid_spec = pltpu.PrefetchScalarGridSpec(
        num_scalar_prefetch=2,
        grid=(db, ts),
        in_specs=[pl.BlockSpec((None, None, NSA_H, NSA_D), lambda b, t, idx, pt: (b, t, 0, 0)),
                  pl.BlockSpec((None, SUB, 4 * NSA_KVW), lambda b, t, idx, pt: (b, 0, 0))] + page_specs,
        out_specs=pl.BlockSpec((None, None, NSA_H, NSA_D), lambda b, t, idx, pt: (b, t, 0, 0)))
    return pl.pallas_call(
        functools.partial(_nsa_sel_decode_kernel, ts, pos0, nbp),
        grid_spec=grid_spec,
        out_shape=jax.ShapeDtypeStruct((db, ts, NSA_H, NSA_D), F32),
        compiler_params=_params(("parallel", "arbitrary")),
        name="nsa_sel_decode",
    )(idx_flat, page_table, q4, kv4_new, *([cache_t] * (NSA_G * NSA_TOPK)))


def _nsa_win_decode_kernel(ts, q_ref, win_ref, new_ref, out_ref):
    rows = NSA_R * SUB
    win = win_ref[...]
    new = new_ref[...]
    tq = lax.broadcasted_iota(I32, (rows, 1), 0) & (SUB - 1)
    iw = lax.broadcasted_iota(I32, (1, NSA_WIN), 1)
    jn = lax.broadcasted_iota(I32, (1, SUB), 1)
    for g in range(NSA_G):
        qg = jnp.concatenate([q_ref[:, g * NSA_R + i, :] for i in range(NSA_R)], axis=0)
        qg = (qg * NSA_SCALE).astype(BF16)
        slope = jnp.concatenate(
            [jnp.full((SUB, 1), _alibi_slope(g * NSA_R + i), F32) for i in range(NSA_R)], axis=0)
        kw = win[g * NSA_D:(g + 1) * NSA_D, :].astype(BF16)
        vw = win[NSA_KVW + g * NSA_D:NSA_KVW + (g + 1) * NSA_D, :].astype(BF16)
        kn = new[:, g * NSA_D:(g + 1) * NSA_D].astype(BF16)
        vn = new[:, NSA_KVW + g * NSA_D:NSA_KVW + (g + 1) * NSA_D].astype(BF16)
        sw = _dot(qg, kw) + slope * (iw - NSA_WIN).astype(F32)
        sw = jnp.where(iw >= tq, sw, NEG)
        sn = _dot_nt(qg, kn) + slope * jn.astype(F32)
        sn = jnp.where((jn <= tq) & (jn < ts), sn, NEG)
        o = _softmax_rows([(sw, vw, True), (sn, vn, False)])
        for i in range(NSA_R):
            out_ref[:, g * NSA_R + i, :] = o[i * SUB:(i + 1) * SUB, :]


def _nsa_win_decode(q4, win_t, kw_new, ts):
    db = q4.shape[0]
    return pl.pallas_call(
        functools.partial(_nsa_win_decode_kernel, ts),
        grid=(db,),
        in_specs=[pl.BlockSpec((None, SUB, NSA_H, NSA_D), lambda b: (b, 0, 0, 0)),
                  pl.BlockSpec((None, 2 * NSA_KVW, NSA_WIN), lambda b: (b, 0, 0)),
                  pl.BlockSpec((None, SUB, 2 * NSA_KVW), lambda b: (b, 0, 0))],
        out_specs=pl.BlockSpec((None, SUB, NSA_H, NSA_D), lambda b: (b, 0, 0, 0)),
        out_shape=jax.ShapeDtypeStruct((db, SUB, NSA_H, NSA_D), F32),
        compiler_params=_params(("parallel",)),
        name="nsa_win_decode",
    )(q4, win_t, kw_new)


def _odd_tail(o, z_ref, x_ref, wout, gfin):
    y = x_ref[...] + _dot((o * _silu(z_ref[...])).astype(BF16), wout[...])
    return _rms(y, gfin[...])


def _odd_out_prompt_kernel(ot_ref, z_ref, x_ref, wout, gfin, out_ref):
    out_ref[...] = _odd_tail(ot_ref[...].T, z_ref, x_ref, wout, gfin)


def _odd_out_prompt(ot, z, x, wout, gfin, tm):
    b, _, t = ot.shape
    tok = pl.BlockSpec((None, tm, D_MODEL), lambda bi, ti: (bi, ti, 0))
    return pl.pallas_call(
        _odd_out_prompt_kernel,
        grid=(b, t // tm),
        in_specs=[pl.BlockSpec((None, NSA_W, tm), lambda bi, ti: (bi, 0, ti)), tok, tok,
                  _const_spec(wout.shape), _const_spec(gfin.shape)],
        out_specs=tok,
        out_shape=jax.ShapeDtypeStruct((b, t, D_MODEL), F32),
        compiler_params=_params(("parallel", "parallel")),
        name="odd_out_prompt",
    )(ot, z, x, wout, gfin)


def _odd_out_decode_kernel(oc_ref, os_ref, ow_ref, gate_ref, z_ref, x_ref, wout, gfin, out_ref):
    gate = gate_ref[...]
    oc, osel, ow = oc_ref[...], os_ref[...], ow_ref[...]
    heads = []
    for h in range(NSA_H):
        c = slice(h * NSA_D, (h + 1) * NSA_D)
        heads.append(gate[:, h:h + 1] * oc[:, c] + gate[:, NSA_H + h:NSA_H + h + 1] * osel[:, c]
                     + gate[:, 2 * NSA_H + h:2 * NSA_H + h + 1] * ow[:, c])
    out_ref[...] = _odd_tail(jnp.concatenate(heads, axis=1), z_ref, x_ref, wout, gfin)


def _odd_out_decode(oc, osel, ow, gate, z, x, wout, gfin):
    m = x.shape[0]
    args = (oc, osel, ow, gate, z, x, wout, gfin)
    return pl.pallas_call(
        _odd_out_decode_kernel,
        grid=(1,),
        in_specs=[_const_spec(a.shape) for a in args],
        out_specs=_const_spec((m, D_MODEL)),
        out_shape=jax.ShapeDtypeStruct((m, D_MODEL), F32),
        compiler_params=_params(("arbitrary",)),
        name="odd_out_decode",
    )(*args)


def _rope_tables(pos):
    half = MLA_ROPE // 2
    inv = ROPE_THETA ** (-jnp.arange(half, dtype=F32) / half)
    ang = pos.astype(F32)[:, None] * inv[None, :]
    cos, sin = jnp.cos(ang), jnp.sin(ang)
    return jnp.concatenate([cos, cos], axis=1), jnp.concatenate([-sin, sin], axis=1)


def _block_diag(x):
    t, g, r, c = x.shape
    eye = jnp.eye(g, dtype=x.dtype)
    return jnp.einsum("tgrc,gh->tgrhc", x, eye).reshape(t, g * r, g * c)


def _even_weights(norm_g, w_in, g_q, g_kv, w_uq, w_uk, w_uv):
    edges = [0, MLA_QL, MLA_QL + MLA_KVL, MLA_QL + MLA_ROW]
    edges += [edges[-1] + MLA_W, edges[-1] + MLA_W + S5_W, edges[-1] + MLA_W + 2 * S5_W]
    wb = w_in.astype(BF16)
    pieces = [wb[:, edges[i]:edges[i + 1]] for i in range(6)]
    in_w = (norm_g[None, :], *pieces, g_q[None, :], g_kv[None, :])
    uq = jnp.transpose(w_uq, (1, 0, 2)).astype(BF16)
    mla_w = (uq[:, :, :MLA_NOPE], uq[:, :, MLA_NOPE:],
             jnp.transpose(w_uk, (1, 2, 0)).astype(BF16),
             jnp.transpose(w_uv, (1, 0, 2)).astype(BF16))
    uq_t = jnp.transpose(w_uq, (1, 2, 0)).astype(BF16)
    mla_wt = (uq_t[:, :MLA_NOPE].reshape(MLA_H * MLA_NOPE, MLA_QL),
              uq_t[:, MLA_NOPE:].reshape(MLA_H * MLA_ROPE, MLA_QL),
              jnp.transpose(w_uk, (1, 0, 2)).astype(BF16),
              jnp.transpose(w_uv, (1, 2, 0)).astype(BF16))
    return in_w, mla_w, mla_wt


def _s5_weights(lam_re, lam_im, log_dt, b_re, b_im, c_re, c_im, d_skip, w_glu, b_glu):
    def bmat(b):
        return _block_diag(jnp.transpose(b.reshape(S5_NT, S5_GT, S5_N, S5_P), (0, 1, 3, 2)))

    def cmat(c):
        return _block_diag(jnp.transpose(c.reshape(S5_NT, S5_GT, S5_P, S5_N), (0, 1, 3, 2)))

    return (lam_re.reshape(1, S5_S), lam_im.reshape(1, S5_S),
            jnp.repeat(log_dt, S5_N).reshape(1, S5_S),
            bmat(b_re), bmat(b_im), cmat(c_re), cmat(c_im),
            d_skip.reshape(1, S5_W), w_glu.astype(BF16), b_glu[None, :])


def _odd_weights(norm_g, w_in, pe_k, pe_v, phi1_k, phi2_k, phi1_v, phi2_v):
    wb = w_in.astype(BF16)
    e0 = NSA_W
    e1 = e0 + 4 * NSA_KVW
    e2 = e1 + 2 * NSA_KVW
    e3 = e2 + 3 * NSA_H
    in_w = (norm_g[None, :], wb[:, :e0], wb[:, e0:e1], wb[:, e1:e2], wb[:, e2:e3], wb[:, e3:])
    pe = jnp.concatenate([pe_k, pe_k, pe_v, pe_v], axis=1)
    p1k = phi1_k.reshape(NSA_BLK, NSA_D, NSA_D)
    p1v = phi1_v.reshape(NSA_BLK, NSA_D, NSA_D)
    w1 = _block_diag(jnp.stack([p1k, p1k, p1v, p1v], axis=1)).astype(BF16)
    phi2 = _block_diag(jnp.stack([phi2_k, phi2_k, phi2_v, phi2_v], axis=0)[None])[0].astype(BF16)
    return in_w, (pe, w1, phi2)


def _pad_tokens(x, n):
    return jnp.pad(x, ((0, 0), (0, n - x.shape[1])) + ((0, 0),) * (x.ndim - 2))


def _rows_last(x):
    nd = x.ndim
    xt = jnp.transpose(x, (0,) + tuple(range(2, nd)) + (1,))
    return xt.reshape(x.shape[0], -1, x.shape[1])


def _rows_second(x_t, feature_shape):
    b, _, rows = x_t.shape
    nf = len(feature_shape)
    xt = x_t.reshape((b,) + tuple(feature_shape) + (rows,))
    return jnp.transpose(xt, (0, nf + 1) + tuple(range(1, nf + 1)))


def kernel(x_prompt, x_sample, cache_mla, state_s5, cache_nsa_kv, state_nsa_win, page_table, norm_even, w_in_even, mla_g_q, mla_g_kv, mla_w_uq, mla_w_uk, mla_w_uv, s5_lambda_re, s5_lambda_im, s5_log_dt, s5_b_re, s5_b_im, s5_c_re, s5_c_im, s5_d, s5_w_glu, s5_b_glu, w_out_even, norm_odd, w_in_odd, nsa_pe_k, nsa_pe_v, nsa_phi1_k, nsa_phi2_k, nsa_phi1_v, nsa_phi2_v, w_out_odd, norm_final):
    b, t, _ = x_prompt.shape
    db, ts, _ = x_sample.shape
    n_pages = page_table.shape[1]
    past = n_pages * PAGE
    mp, ms = b * t, db * ts
    tm = min(512, t)
    assert t % (PACK * NSA_BLK) == 0 and t % TK == 0 and ts <= SUB
    assert t <= NSA_BLK * 256

    pos_p = jnp.arange(t, dtype=I32)
    pos_s = past + jnp.arange(SUB, dtype=I32)
    cos_p, sin_p = _rope_tables(pos_p)
    cos_s, sin_s = _rope_tables(pos_s)
    cos_st, sin_st = jnp.tile(cos_s[:ts], (db, 1)), jnp.tile(sin_s[:ts], (db, 1))

    even_in_w, mla_w, mla_wt = _even_weights(norm_even[0], w_in_even[0], mla_g_q[0], mla_g_kv[0],
                                             mla_w_uq[0], mla_w_uk[0], mla_w_uv[0])
    s5_w = _s5_weights(s5_lambda_re[0], s5_lambda_im[0], s5_log_dt[0], s5_b_re[0], s5_b_im[0],
                       s5_c_re[0], s5_c_im[0], s5_d[0], s5_w_glu[0], s5_b_glu[0])
    wo_e = w_out_even[0].astype(BF16)
    wo_a, wo_b = wo_e[:MLA_W], wo_e[MLA_W:]

    xp2 = x_prompt.reshape(mp, D_MODEL)
    cq, rows_pt, za, u, zb = _even_in(xp2, even_in_w, cos_p, sin_p, tm, seq=t)
    mix_a = _mla_prompt(cq.reshape(b, t, MLA_QL), rows_pt, za.reshape(b, t, MLA_W), mla_wt, cos_p.T, sin_p.T)
    zeros_p = jnp.zeros((b, S5_S), F32)
    mix_b, sre_p, sim_p = _s5(u.reshape(b, t, S5_W), zb.reshape(b, t, S5_W), s5_w, zeros_p, zeros_p,
                              min(64, t))
    xp1 = _even_out(xp2, mix_a.reshape(mp, MLA_W), mix_b.reshape(mp, S5_W), wo_a, wo_b, tm)

    xs2 = x_sample.reshape(ms, D_MODEL)
    cq_s, rows_s, za_s, u_s, zb_s = _even_in(xs2, even_in_w, cos_st, sin_st, ms)
    pad3 = lambda a, n: _pad_tokens(a.reshape(db, ts, n), SUB)
    mix_a_s = _mla_decode(page_table, _rows_last(cache_mla[0]), pad3(cq_s, MLA_QL), pad3(rows_s, MLA_ROW),
                          pad3(za_s, MLA_W), mla_w, cos_s, sin_s, ts)[:, :ts]
    st = state_s5[0]
    mix_b_s, sre_s, sim_s = _s5(u_s.reshape(db, ts, S5_W), zb_s.reshape(db, ts, S5_W), s5_w,
                                st[..., 0].reshape(db, S5_S), st[..., 1].reshape(db, S5_S), ts)
    xs1 = _even_out(xs2, mix_a_s.reshape(ms, MLA_W), mix_b_s.reshape(ms, S5_W), wo_a, wo_b, ms)

    odd_in_w, cmp_w = _odd_weights(norm_odd[0], w_in_odd[0], nsa_pe_k[0], nsa_pe_v[0], nsa_phi1_k[0],
                                   nsa_phi2_k[0], nsa_phi1_v[0], nsa_phi2_v[0])
    wo_o = w_out_odd[0].astype(BF16)
    gfin = norm_final[None, :]

    q, kv4_t, kw_t, gate, z = _odd_in(xp1, odd_in_w, tm, seq=t)
    nblk_p = t // NSA_BLK
    cb = _compress_prompt(kv4_t, *cmp_w)
    q3 = q.reshape(b, t, NSA_W)
    tt = min(256, t)
    oc_t, imp = _cmp(q3, cb.reshape(b, nblk_p, 2 * NSA_KVW), pos_p[None], tt)
    sel, _ = _topk(imp, pos_p[None], nblk_p, tt)
    o_t = _nsa_prompt(q3, kv4_t, kw_t, oc_t, sel, gate.reshape(b, t, 3 * NSA_H))
    y_prompt = _odd_out_prompt(o_t, z.reshape(b, t, NSA_W), xp1.reshape(b, t, D_MODEL), wo_o, gfin, tt)

    q_s, kv4_s, kw_s, gate_s, z_s = _odd_in(xs1, odd_in_w, ms)
    cache_nsa_t = _rows_last(cache_nsa_kv[0])
    cb_s = _compress_decode(page_table, cache_nsa_t, *cmp_w)
    n_sel_s = -(-(past + ts) // NSA_BLK)
    q_s3 = q_s.reshape(db, ts, NSA_W)
    oc_ts, imp_s = _cmp(_pad_tokens(q_s3, SUB), cb_s, pos_s[None], SUB)
    imp_l = jnp.transpose(imp_s[..., :ts], (1, 2, 0, 3)).reshape(1, NSA_G, cb_s.shape[1], ms)
    pos_l = jnp.tile(pos_s[:ts], db)[None]
    _, idx_l = _topk(imp_l, pos_l, n_sel_s, ms)
    idx_flat = jnp.transpose(idx_l.reshape(NSA_G, NSA_TOPK, db, ts), (2, 0, 3, 1)).reshape(-1)
    q_s4 = q_s.reshape(db, ts, NSA_H, NSA_D)
    o_sel_s = _nsa_sel_decode(idx_flat, page_table, cache_nsa_t, q_s4, pad3(kv4_s, 4 * NSA_KVW), ts, past)
    win_t = _rows_last(state_nsa_win[0])
    kw_s3 = kw_s.reshape(db, ts, 2 * NSA_KVW)
    o_win_s = _nsa_win_decode(_pad_tokens(q_s4, SUB), win_t, _pad_tokens(kw_s3, SUB), ts)
    oc_s = jnp.transpose(oc_ts, (0, 2, 1))[:, :ts].reshape(ms, NSA_W)
    y_sample = _odd_out_decode(oc_s, o_sel_s.reshape(ms, NSA_W), o_win_s[:, :ts].reshape(ms, NSA_W),
                               gate_s, z_s, xs1, wo_o, gfin)

    state = lambda re, im, n: jnp.stack([re, im], axis=-1).reshape(1, n, S5_G, S5_N, 2)
    win_shape = (2, NSA_G, NSA_D)
    if t >= NSA_WIN:
        win_pt = kw_t[:, :, t - NSA_WIN:]
    else:
        win_pt = jnp.pad(kw_t, ((0, 0), (0, 0), (NSA_WIN - t, 0)))
    win_st = jnp.concatenate([win_t[:, :, ts:], jnp.transpose(kw_s3, (0, 2, 1))], axis=2)
    return (y_prompt, y_sample.reshape(db, ts, D_MODEL),
            _rows_second(rows_pt, (MLA_ROW,))[None], rows_s.reshape(1, db, ts, MLA_ROW),
            state(sre_p, sim_p, b), state(sre_s, sim_s, db),
            _rows_second(kv4_t, (4, NSA_G, NSA_D))[None], kv4_s.reshape(1, db, ts, 4, NSA_G, NSA_D),
            _rows_second(win_pt, win_shape)[None], _rows_second(win_st, win_shape)[None])
```

```python
import functools

import jax
import jax.numpy as jnp
import numpy as np
from jax import lax
from jax.experimental import pallas as pl
from jax.experimental.pallas import tpu as pltpu

F32, BF16, I32 = jnp.float32, jnp.bfloat16, jnp.int32

D_MODEL = 1024
PAGE = 128
EPS = 1e-6
ROPE_THETA = 10000.0
MLA_H, MLA_NOPE, MLA_ROPE, MLA_V = 8, 64, 32, 64
MLA_QL, MLA_KVL = 384, 256
MLA_ROW = MLA_KVL + MLA_ROPE
MLA_W = MLA_H * MLA_V
MLA_SCALE = (MLA_NOPE + MLA_ROPE) ** -0.5
S5_G, S5_P, S5_N = 32, 16, 64
S5_W = S5_G * S5_P
S5_S = S5_G * S5_N
S5_GT = 4
S5_NT = S5_G // S5_GT
NSA_H, NSA_G, NSA_D = 16, 2, 64
NSA_R = NSA_H // NSA_G
NSA_W = NSA_H * NSA_D
NSA_KVW = NSA_G * NSA_D
NSA_BLK, NSA_TOPK, NSA_WIN = 64, 16, 512
NSA_SCALE = NSA_D ** -0.5
FORCED_BONUS = float(NSA_R + 1)
BPP = PAGE // NSA_BLK

LOG2E = 1.4426950408889634
NEG = -1e30
GONE = -3e38
TQ = 128
TK = 256
SUB = 8
PACK = 16
MLA_DECODE_GROUP = 32
VMEM_LIMIT = 56 * 1024 * 1024


def _dot(a, b):
    return jnp.dot(a, b, preferred_element_type=F32)


def _dot_nt(a, b):
    return lax.dot_general(a, b, (((1,), (1,)), ((), ())), preferred_element_type=F32)


def _dot_tn(a, b):
    return lax.dot_general(a, b, (((0,), (0,)), ((), ())), preferred_element_type=F32)


def _rms(x, g):
    return x * lax.rsqrt(jnp.mean(x * x, axis=-1, keepdims=True) + EPS) * g


def _silu(x):
    return x * jax.nn.sigmoid(x)


def _rope_nat(x, cosf, sinf):
    half = x.shape[1] // 2
    xs = jnp.concatenate([x[:, half:], x[:, :half]], axis=1)
    return x * cosf + xs * sinf


def _params(sem):
    return pltpu.CompilerParams(dimension_semantics=sem, vmem_limit_bytes=VMEM_LIMIT)


def _const_spec(shape):
    n = len(shape)
    return pl.BlockSpec(shape, lambda *a, _n=n: (0,) * _n)


def _seq_major_spec(width, tm, seq):
    per = seq // tm
    return pl.BlockSpec((None, width, tm), lambda i: (i // per, 0, i % per))


def _even_in_kernel(feature_major, x_ref, g_ref, wcq, wckv, wkr, wza, wu, wzb, gq, gkv, cos_ref, sin_ref,
                    cq_o, rows_o, za_o, u_o, zb_o):
    h = _rms(x_ref[...], g_ref[...]).astype(BF16)
    cq_o[...] = _rms(_dot(h, wcq[...]), gq[...])
    ckv = _rms(_dot(h, wckv[...]), gkv[...])
    krope = _rope_nat(_dot(h, wkr[...]), cos_ref[...], sin_ref[...])
    if feature_major:
        rows_o[:MLA_KVL, :] = ckv.T
        rows_o[MLA_KVL:, :] = krope.T
    else:
        rows_o[:, :MLA_KVL] = ckv
        rows_o[:, MLA_KVL:] = krope
    za_o[...] = _dot(h, wza[...])
    u_o[...] = _dot(h, wu[...])
    zb_o[...] = _dot(h, wzb[...])


def _even_in(x2, wts, cosf, sinf, tm, seq=None):
    m = x2.shape[0]
    tab_blocks = cosf.shape[0] // tm
    row = lambda n: pl.BlockSpec((tm, n), lambda i: (i, 0))
    tab = pl.BlockSpec((tm, MLA_ROPE), lambda i: (i % tab_blocks, 0))
    widths = (MLA_QL, MLA_ROW, MLA_W, S5_W, S5_W)
    out_specs = [row(n) for n in widths]
    out_shape = [jax.ShapeDtypeStruct((m, n), F32) for n in widths]
    if seq is not None:
        out_specs[1] = _seq_major_spec(MLA_ROW, tm, seq)
        out_shape[1] = jax.ShapeDtypeStruct((m // seq, MLA_ROW, seq), F32)
    return pl.pallas_call(
        functools.partial(_even_in_kernel, seq is not None),
        grid=(m // tm,),
        in_specs=[row(D_MODEL)] + [_const_spec(c.shape) for c in wts] + [tab, tab],
        out_specs=out_specs,
        out_shape=out_shape,
        compiler_params=_params(("parallel",)),
        name="even_in",
    )(x2, *wts, cosf, sinf)


def _mla_queries(cq, wuqn, wuqr, wuk, cosf, sinf, h):
    cqb = cq.astype(BF16)
    qn = _dot(cqb, wuqn[h])
    ql = _dot(qn.astype(BF16), wuk[h])
    qr = _rope_nat(_dot(cqb, wuqr[h]), cosf, sinf)
    return ql, qr


def _flash_tiles(lo, hi, chains, c_exp):
    for qk, _, first_mask, _, s_buf, m_s, l_s, acc_s in chains:
        m_s[...] = jnp.full(m_s.shape, NEG, F32)
        l_s[...] = jnp.zeros(l_s.shape, F32)
        acc_s[...] = jnp.zeros(acc_s.shape, F32)
        s_buf[0] = first_mask(qk(lo), lo)

    def consume(chain, j, s):
        _, pv, _, _, _, m_s, l_s, acc_s = chain
        m_old = m_s[...]
        m_new = jnp.maximum(m_old, jnp.max(s, axis=0, keepdims=True))
        alpha = jnp.exp2((m_old - m_new) * c_exp)
        p = jnp.exp2((s - m_new) * c_exp)
        l_s[...] = alpha * l_s[...] + jnp.sum(p, axis=0, keepdims=True)
        acc_s[...] = alpha * acc_s[...] + pv(j, p.astype(BF16))
        m_s[...] = m_new

    def body(j, carry):
        for chain in chains:
            chain[4][1] = chain[0](j + 1)
        for chain in chains:
            consume(chain, j, chain[4][0])
        for chain in chains:
            chain[4][0] = chain[4][1]
        return carry

    lax.fori_loop(lo, hi - 1, body, 0)
    last = hi - 1
    for chain in chains:
        consume(chain, last, chain[3](chain[4][0], last))
    return [chain[7][...] / chain[6][...] for chain in chains]


def _mla_prompt_kernel(cq_ref, rows_ref, za_ref, wuqn_t, wuqr_t, wuk_t, wuv_t, cos_ref, sin_ref,
                       out_ref, qtl, qtr, s_buf, m_s, l_s, acc_s):
    qi = pl.program_id(1)
    r = MLA_H * TQ
    half = MLA_ROPE // 2
    cq_t = cq_ref[...].T.astype(BF16)
    qn_t = _dot(wuqn_t[...], cq_t).astype(BF16)
    qr_t = _dot(wuqr_t[...], cq_t)
    cos_t, sin_t = cos_ref[...], sin_ref[...]
    for h in range(MLA_H):
        qtl[:, h * TQ:(h + 1) * TQ] = _dot(wuk_t[h], qn_t[h * MLA_NOPE:(h + 1) * MLA_NOPE, :]).astype(BF16)
        x = qr_t[h * MLA_ROPE:(h + 1) * MLA_ROPE, :]
        xs = jnp.concatenate([x[half:, :], x[:half, :]], axis=0)
        qtr[:, h * TQ:(h + 1) * TQ] = (x * cos_t + xs * sin_t).astype(BF16)
    qpos = qi * TQ + (lax.broadcasted_iota(I32, (1, r), 1) & (TQ - 1))
    krow = lax.broadcasted_iota(I32, (TK, 1), 0)

    def keys(j):
        return rows_ref[:, pl.ds(pl.multiple_of(j * TK, TK), TK)]

    def qk(j):
        kt = keys(j)
        return (_dot_tn(kt[:MLA_KVL, :].astype(BF16), qtl[...])
                + _dot_tn(kt[MLA_KVL:, :].astype(BF16), qtr[...]))

    def pv(j, p):
        return _dot(keys(j)[:MLA_KVL, :].astype(BF16), p)

    def causal(s, j):
        return jnp.where(j * TK + krow <= qpos, s, NEG)

    n_tiles = (qi * TQ + TQ + TK - 1) // TK
    chain = (qk, pv, lambda s, j: s, causal, s_buf, m_s, l_s, acc_s)
    o = _flash_tiles(0, n_tiles, [chain], MLA_SCALE * LOG2E)[0].astype(BF16)
    heads = [_dot(wuv_t[h], o[:, h * TQ:(h + 1) * TQ]) for h in range(MLA_H)]
    out_ref[...] = jnp.concatenate(heads, axis=0).T * _silu(za_ref[...])


def _mla_prompt(cq, rows_t, za, wts, cos_t, sin_t):
    b, t, _ = cq.shape
    r = MLA_H * TQ
    tok = lambda n: pl.BlockSpec((None, TQ, n), lambda bi, qi: (bi, qi, 0))
    tab = pl.BlockSpec((MLA_ROPE, TQ), lambda bi, qi: (0, qi))
    return pl.pallas_call(
        _mla_prompt_kernel,
        grid=(b, t // TQ),
        in_specs=[tok(MLA_QL), pl.BlockSpec((None, MLA_ROW, t), lambda bi, qi: (bi, 0, 0)), tok(MLA_W)]
        + [_const_spec(w.shape) for w in wts] + [tab, tab],
        out_specs=tok(MLA_W),
        out_shape=jax.ShapeDtypeStruct((b, t, MLA_W), F32),
        scratch_shapes=[pltpu.VMEM((MLA_KVL, r), BF16), pltpu.VMEM((MLA_ROPE, r), BF16),
                        pltpu.VMEM((2, TK, r), F32),
                        pltpu.VMEM((1, r), F32), pltpu.VMEM((1, r), F32), pltpu.VMEM((MLA_KVL, r), F32)],
        compiler_params=_params(("parallel", "arbitrary")),
        name="mla_prompt",
    )(cq, rows_t, za, *wts, cos_t, sin_t)


def _mla_decode_kernel(n_pages_step, ts, pt_ref, cq_ref, rows_ref, za_ref, wuqn, wuqr, wuk, wuv,
                       cos_ref, sin_ref, *rest):
    pages = rest[:n_pages_step]
    out_ref, ql_s, qr_s, m_s, l_s, acc_s = rest[n_pages_step:]
    gi = pl.program_id(1)
    rows = MLA_H * SUB

    @pl.when(gi == 0)
    def _():
        cq = cq_ref[...]
        cosf, sinf = cos_ref[...], sin_ref[...]
        for h in range(MLA_H):
            ql, qr = _mla_queries(cq, wuqn, wuqr, wuk, cosf, sinf, h)
            ql_s[h * SUB:(h + 1) * SUB, :] = ql.astype(BF16)
            qr_s[h * SUB:(h + 1) * SUB, :] = qr.astype(BF16)
        m_s[...] = jnp.full((rows, 1), NEG, F32)
        l_s[...] = jnp.zeros((rows, 1), F32)
        acc_s[...] = jnp.zeros((rows, MLA_KVL), F32)

    def update(s, vals, feature_major):
        m_old = m_s[...]
        m_new = jnp.maximum(m_old, jnp.max(s, axis=1, keepdims=True))
        alpha = jnp.exp(m_old - m_new)
        p = jnp.exp(s - m_new)
        l_s[...] = alpha * l_s[...] + jnp.sum(p, axis=1, keepdims=True)
        pv = None
        for (lo, hi), v in vals:
            pj = p[:, lo:hi].astype(BF16)
            term = _dot_nt(pj, v) if feature_major else _dot(pj, v)
            pv = term if pv is None else pv + term
        acc_s[...] = alpha * acc_s[...] + pv
        m_s[...] = m_new

    ql, qr = ql_s[...], qr_s[...]
    scores, vals = [], []
    for j, pg in enumerate(pages):
        k = pg[...]
        ckv = k[:MLA_KVL, :].astype(BF16)
        scores.append(_dot(ql, ckv) + _dot(qr, k[MLA_KVL:, :].astype(BF16)))
        vals.append(((len(vals) * PAGE, (len(vals) + 1) * PAGE), ckv))
        if len(vals) == MLA_DECODE_GROUP or j == n_pages_step - 1:
            update(jnp.concatenate(scores, axis=1) * MLA_SCALE, vals, True)
            scores, vals = [], []

    @pl.when(gi == pl.num_programs(1) - 1)
    def _():
        kn = rows_ref[...]
        ckv = kn[:, :MLA_KVL].astype(BF16)
        s = (_dot_nt(ql, ckv) + _dot_nt(qr, kn[:, MLA_KVL:].astype(BF16))) * MLA_SCALE
        tq = lax.broadcasted_iota(I32, (rows, SUB), 0) & (SUB - 1)
        jk = lax.broadcasted_iota(I32, (rows, SUB), 1)
        s = jnp.where((jk <= tq) & (jk < ts), s, NEG)
        update(s, [((0, SUB), ckv)], False)
        o = (acc_s[...] / l_s[...]).astype(BF16)
        heads = [_dot(o[h * SUB:(h + 1) * SUB, :], wuv[h]) for h in range(MLA_H)]
        out_ref[...] = jnp.concatenate(heads, axis=1) * _silu(za_ref[...])


def _mla_decode(page_table, cache_t, cq, rows, za, wts, cosf, sinf, ts, n_pages_step=32):
    db, n_pages = page_table.shape
    rows_n = MLA_H * SUB
    tok = lambda n: pl.BlockSpec((None, SUB, n), lambda b, g, pt: (b, 0, 0))
    cst = lambda shape: pl.BlockSpec(shape, lambda b, g, pt, _n=len(shape): (0,) * _n)
    page_specs = [
        pl.BlockSpec((None, MLA_ROW, PAGE), lambda b, g, pt, j=j: (pt[b, g * n_pages_step + j], 0, 0))
        for j in range(n_pages_step)]
    grid_spec = pltpu.PrefetchScalarGridSpec(
        num_scalar_prefetch=1,
        grid=(db, n_pages // n_pages_step),
        in_specs=[tok(MLA_QL), tok(MLA_ROW), tok(MLA_W)] + [cst(w.shape) for w in wts]
        + [cst(cosf.shape), cst(sinf.shape)] + page_specs,
        out_specs=tok(MLA_W),
        scratch_shapes=[pltpu.VMEM((rows_n, MLA_KVL), BF16), pltpu.VMEM((rows_n, MLA_ROPE), BF16),
                        pltpu.VMEM((rows_n, 1), F32), pltpu.VMEM((rows_n, 1), F32),
                        pltpu.VMEM((rows_n, MLA_KVL), F32)])
    return pl.pallas_call(
        functools.partial(_mla_decode_kernel, n_pages_step, ts),
        grid_spec=grid_spec,
        out_shape=jax.ShapeDtypeStruct((db, SUB, MLA_W), F32),
        compiler_params=_params(("parallel", "arbitrary")),
        name="mla_decode",
    )(page_table, cq, rows, za, *wts, cosf, sinf, *([cache_t] * n_pages_step))


def _s5_kernel(u_ref, zb_ref, lre_ref, lim_ref, ldt_ref, wbre, wbim, wcre, wcim, d_ref, wglu, bglu,
               h0re_ref, h0im_ref, mix_o, sre_o, sim_o, bure, buim, hre, him):
    step = pl.program_id(0)
    nb, chunk, _ = u_ref.shape
    rows = nb * chunk

    @pl.when(step == 0)
    def _():
        hre[...] = h0re_ref[...]
        him[...] = h0im_ref[...]

    lre, lim = lre_ref[...], lim_ref[...]
    dt = jnp.exp(ldt_ref[...])
    mag = jnp.exp(lre * dt)
    are, aim = mag * jnp.cos(lim * dt), mag * jnp.sin(lim * dt)
    den = lre * lre + lim * lim
    cre = ((are - 1.0) * lre + aim * lim) / den
    cim = (aim * lre - (are - 1.0) * lim) / den

    ut = jnp.swapaxes(u_ref[...], 0, 1).reshape(rows, S5_W)
    kw, nw = S5_GT * S5_P, S5_GT * S5_N
    for jt in range(S5_NT):
        cr, ci = cre[:, jt * nw:(jt + 1) * nw], cim[:, jt * nw:(jt + 1) * nw]
        bre = (cr * wbre[jt] - ci * wbim[jt]).astype(BF16)
        bim = (cr * wbim[jt] + ci * wbre[jt]).astype(BF16)
        uj = ut[:, jt * kw:(jt + 1) * kw].astype(BF16)
        bure[:, jt * nw:(jt + 1) * nw] = _dot(uj, bre)
        buim[:, jt * nw:(jt + 1) * nw] = _dot(uj, bim)

    def scan(t, carry):
        hr, hi = carry
        sl = pl.ds(pl.multiple_of(t * nb, SUB), nb)
        nr = are * hr - aim * hi + bure[sl, :]
        ni = are * hi + aim * hr + buim[sl, :]
        bure[sl, :] = nr
        buim[sl, :] = ni
        return nr, ni

    hr, hi = lax.fori_loop(0, chunk, scan, (hre[...], him[...]))
    hre[...] = hr
    him[...] = hi
    sre_o[...] = hr
    sim_o[...] = hi

    ys = []
    for jt in range(S5_NT):
        sr = bure[:, jt * nw:(jt + 1) * nw].astype(BF16)
        si = buim[:, jt * nw:(jt + 1) * nw].astype(BF16)
        ys.append(_dot(sr, wcre[jt].astype(BF16)) - _dot(si, wcim[jt].astype(BF16)))
    y = jnp.concatenate(ys, axis=1) + d_ref[...] * ut
    g5 = jax.nn.gelu(y)
    ob = g5 * jax.nn.sigmoid(_dot(g5.astype(BF16), wglu[...]) + bglu[...])
    mix_o[...] = jnp.swapaxes(ob.reshape(chunk, nb, S5_W), 0, 1) * _silu(zb_ref[...])


def _s5(u, zb, wts, h0re, h0im, chunk):
    nb, t, _ = u.shape
    tok = pl.BlockSpec((nb, chunk, S5_W), lambda i: (0, i, 0))
    st = _const_spec((nb, S5_S))
    return pl.pallas_call(
        _s5_kernel,
        grid=(t // chunk,),
        in_specs=[tok, tok] + [_const_spec(w.shape) for w in wts] + [st, st],
        out_specs=[tok, st, st],
        out_shape=[jax.ShapeDtypeStruct((nb, t, S5_W), F32), jax.ShapeDtypeStruct((nb, S5_S), F32),
                   jax.ShapeDtypeStruct((nb, S5_S), F32)],
        scratch_shapes=[pltpu.VMEM((nb * chunk, S5_S), F32), pltpu.VMEM((nb * chunk, S5_S), F32),
                        pltpu.VMEM((nb, S5_S), F32), pltpu.VMEM((nb, S5_S), F32)],
        compiler_params=_params(("arbitrary",)),
        name="s5",
    )(u, zb, *wts, h0re, h0im)


def _even_out_kernel(x_ref, a_ref, b_ref, wa, wb, out_ref):
    out_ref[...] = (x_ref[...] + _dot(a_ref[...].astype(BF16), wa[...])
                    + _dot(b_ref[...].astype(BF16), wb[...]))


def _even_out(x2, mixa, mixb, wa, wb, tm):
    m = x2.shape[0]
    row = lambda n: pl.BlockSpec((tm, n), lambda i: (i, 0))
    return pl.pallas_call(
        _even_out_kernel,
        grid=(m // tm,),
        in_specs=[row(D_MODEL), row(MLA_W), row(S5_W), _const_spec(wa.shape), _const_spec(wb.shape)],
        out_specs=row(D_MODEL),
        out_shape=jax.ShapeDtypeStruct((m, D_MODEL), F32),
        compiler_params=_params(("parallel",)),
        name="even_out",
    )(x2, mixa, mixb, wa, wb)


def _odd_in_kernel(feature_major, x_ref, g_ref, wq, wkv4, wkw, wg, wz, q_o, kv4_o, kw_o, gate_o, z_o):
    h = _rms(x_ref[...], g_ref[...]).astype(BF16)
    q_o[...] = _dot(h, wq[...])
    kv4, kw = _dot(h, wkv4[...]), _dot(h, wkw[...])
    kv4_o[...] = kv4.T if feature_major else kv4
    kw_o[...] = kw.T if feature_major else kw
    gate_o[...] = jax.nn.sigmoid(_dot(h, wg[...]))
    z_o[...] = _dot(h, wz[...])


def _odd_in(x2, wts, tm, seq=None):
    m = x2.shape[0]
    row = lambda n: pl.BlockSpec((tm, n), lambda i: (i, 0))
    widths = (NSA_W, 4 * NSA_KVW, 2 * NSA_KVW, 3 * NSA_H, NSA_W)
    out_specs = [row(n) for n in widths]
    out_shape = [jax.ShapeDtypeStruct((m, n), F32) for n in widths]
    if seq is not None:
        for i in (1, 2):
            out_specs[i] = _seq_major_spec(widths[i], tm, seq)
            out_shape[i] = jax.ShapeDtypeStruct((m // seq, widths[i], seq), F32)
    return pl.pallas_call(
        functools.partial(_odd_in_kernel, seq is not None),
        grid=(m // tm,),
        in_specs=[row(D_MODEL)] + [_const_spec(w.shape) for w in wts],
        out_specs=out_specs,
        out_shape=out_shape,
        compiler_params=_params(("parallel",)),
        name="odd_in",
    )(x2, *wts)


def _compress_stage(xs, x_t, pe, base):
    n = x_t.shape[1] // NSA_BLK
    x = x_t.T.reshape(n, NSA_BLK, 2 * NSA_KVW) + pe[None]
    xs[:, pl.ds(pl.multiple_of(base, PACK), n), :] = jnp.swapaxes(x, 0, 1).astype(BF16)


def _compress_finish(xs, w1, phi2, out_ref):
    acc = jnp.zeros((xs.shape[1], 2 * NSA_KVW), F32)
    for r in range(NSA_BLK):
        acc = acc + _dot(xs[r], w1[r])
    out_ref[...] = _dot(_silu(acc).astype(BF16), phi2[...])


def _compress_prompt_kernel(x_ref, pe_ref, w1, phi2, out_ref, xs):
    bi = pl.program_id(0)
    nblk = x_ref.shape[1] // NSA_BLK
    _compress_stage(xs, x_ref[...], pe_ref[...], bi * nblk)

    @pl.when(bi == pl.num_programs(0) - 1)
    def _():
        _compress_finish(xs, w1, phi2, out_ref)


def _compress_prompt(kv4_t, pe, w1, phi2):
    b, _, t = kv4_t.shape
    cw = 2 * NSA_KVW
    nblk = b * (t // NSA_BLK)
    return pl.pallas_call(
        _compress_prompt_kernel,
        grid=(b,),
        in_specs=[pl.BlockSpec((None, cw, t), lambda i: (i, 0, 0)),
                  _const_spec(pe.shape), _const_spec(w1.shape), _const_spec(phi2.shape)],
        out_specs=pl.BlockSpec((nblk, cw), lambda i: (0, 0)),
        out_shape=jax.ShapeDtypeStruct((nblk, cw), F32),
        scratch_shapes=[pltpu.VMEM((NSA_BLK, nblk, cw), BF16)],
        compiler_params=_params(("arbitrary",)),
        name="compress_prompt",
    )(kv4_t, pe, w1, phi2)


def _compress_decode_kernel(n_pages_step, pt_ref, pe_ref, w1, phi2, *rest):
    pages = rest[:n_pages_step]
    out_ref, xs = rest[n_pages_step:]
    gi = pl.program_id(1)
    grp = PACK // BPP
    pe = pe_ref[...]
    for k in range(n_pages_step // grp):
        x_t = jnp.concatenate([pages[k * grp + j][...] for j in range(grp)], axis=1)
        _compress_stage(xs, x_t, pe, (gi * (n_pages_step // grp) + k) * PACK)

    @pl.when(gi == pl.num_programs(1) - 1)
    def _():
        _compress_finish(xs, w1, phi2, out_ref)


def _compress_decode(page_table, cache_t, pe, w1, phi2, n_pages_step=16):
    db, n_pages = page_table.shape
    cw = 2 * NSA_KVW
    nblk = n_pages * BPP
    cst = lambda shape: pl.BlockSpec(shape, lambda b, g, pt, _n=len(shape): (0,) * _n)
    page_specs = [
        pl.BlockSpec((None, cw, PAGE), lambda b, g, pt, j=j: (pt[b, g * n_pages_step + j], 0, 0))
        for j in range(n_pages_step)]
    grid_spec = pltpu.PrefetchScalarGridSpec(
        num_scalar_prefetch=1,
        grid=(db, n_pages // n_pages_step),
        in_specs=[cst(pe.shape), cst(w1.shape), cst(phi2.shape)] + page_specs,
        out_specs=pl.BlockSpec((None, nblk, cw), lambda b, g, pt: (b, 0, 0)),
        scratch_shapes=[pltpu.VMEM((NSA_BLK, nblk, cw), BF16)])
    return pl.pallas_call(
        functools.partial(_compress_decode_kernel, n_pages_step),
        grid_spec=grid_spec,
        out_shape=jax.ShapeDtypeStruct((db, nblk, cw), F32),
        compiler_params=_params(("parallel", "arbitrary")),
        name="compress_decode",
    )(page_table, pe, w1, phi2, *([cache_t] * n_pages_step))


def _alibi_slope(h):
    return 2.0 ** (-8.0 * (h + 1) / NSA_H)


def _split_bf16(x):
    hi = x.astype(BF16)
    return hi, (x - hi.astype(F32)).astype(BF16)


def _cmp_kernel(q_ref, cb_ref, pos_ref, oc_o, imp_o, s_scr):
    tt = q_ref.shape[0]
    nc = cb_ref.shape[0]
    qpos = pos_ref[...]
    cpos = lax.broadcasted_iota(I32, (nc, 1), 0) * NSA_BLK + (NSA_BLK - 1)
    visible = cpos <= qpos
    dist = (qpos - cpos).astype(F32)
    q = q_ref[...] * NSA_SCALE
    cb = cb_ref[...]
    for g in range(NSA_G):
        k_hi, k_lo = _split_bf16(cb[:, g * NSA_D:(g + 1) * NSA_D])
        vc = cb[:, NSA_KVW + g * NSA_D:NSA_KVW + (g + 1) * NSA_D].astype(BF16)
        for i in range(NSA_R):
            h = g * NSA_R + i
            q_hi, q_lo = _split_bf16(q[:, h * NSA_D:(h + 1) * NSA_D])
            s = _dot_nt(k_hi, q_hi) + _dot_nt(k_hi, q_lo) + _dot_nt(k_lo, q_hi)
            s_scr[i * nc:(i + 1) * nc, :] = s - _alibi_slope(h) * dist
        s3 = jnp.where(visible[None], s_scr[...].reshape(NSA_R, nc, tt), NEG)
        mx = jnp.max(s3, axis=1, keepdims=True)
        e = jnp.where(visible[None], jnp.exp(s3 - mx), 0.0)
        den = jnp.sum(e, axis=1, keepdims=True)
        p = e / jnp.where(den > 0, den, 1.0)
        imp_o[g] = jnp.sum(p, axis=0)
        for i in range(NSA_R):
            h = g * NSA_R + i
            oc_o[h * NSA_D:(h + 1) * NSA_D, :] = _dot_tn(vc, p[i].astype(BF16))


def _cmp(q, cb, pos, tt):
    b, t, _ = q.shape
    nc = cb.shape[1]
    return pl.pallas_call(
        _cmp_kernel,
        grid=(b, t // tt),
        in_specs=[pl.BlockSpec((None, tt, NSA_W), lambda bi, ti: (bi, ti, 0)),
                  pl.BlockSpec((None, nc, 2 * NSA_KVW), lambda bi, ti: (bi, 0, 0)),
                  pl.BlockSpec((1, tt), lambda bi, ti: (0, ti))],
        out_specs=[pl.BlockSpec((None, NSA_W, tt), lambda bi, ti: (bi, 0, ti)),
                   pl.BlockSpec((None, NSA_G, nc, tt), lambda bi, ti: (bi, 0, 0, ti))],
        out_shape=[jax.ShapeDtypeStruct((b, NSA_W, t), F32),
                   jax.ShapeDtypeStruct((b, NSA_G, nc, t), F32)],
        scratch_shapes=[pltpu.VMEM((NSA_R * nc, tt), F32)],
        compiler_params=_params(("parallel", "parallel")),
        name="nsa_cmp",
    )(q, cb, pos)


def _topk_kernel(n_sel, imp_ref, pos_ref, sel_o, idx_o):
    _, nc, tt = imp_ref.shape
    nsp = sel_o.shape[1]
    qpos = pos_ref[...]
    blk = lax.broadcasted_iota(I32, (nsp, 1), 0)
    cur = jnp.right_shift(qpos, NSA_BLK.bit_length() - 1)
    forced = (blk == 0) | (blk == cur) | (blk == cur - 1)
    allowed = (blk <= cur) & (blk < n_sel)
    for g in range(NSA_G):
        imp = imp_ref[g]
        if nsp > nc:
            imp = jnp.concatenate([imp, jnp.zeros((nsp - nc, tt), F32)], axis=0)
        score = jnp.where(allowed, imp + jnp.where(forced, FORCED_BONUS, 0.0), NEG)
        chosen = jnp.zeros((nsp, tt), F32)
        picks = []
        for _ in range(NSA_TOPK):
            mx = jnp.max(score, axis=0, keepdims=True)
            first = jnp.min(jnp.where(score == mx, blk, nsp), axis=0, keepdims=True)
            hit = blk == first
            chosen = jnp.where(hit, 1.0, chosen)
            score = jnp.where(hit, GONE, score)
            picks.append(first)
        sel_o[g] = chosen
        idx_o[g] = jnp.concatenate(picks, axis=0)


def _topk(imp, pos, n_sel, tt):
    b, _, nc, t = imp.shape
    nsp = -(-(n_sel + TK // NSA_BLK) // SUB) * SUB
    return pl.pallas_call(
        functools.partial(_topk_kernel, n_sel),
        grid=(b, t // tt),
        in_specs=[pl.BlockSpec((None, NSA_G, nc, tt), lambda bi, ti: (bi, 0, 0, ti)),
                  pl.BlockSpec((1, tt), lambda bi, ti: (0, ti))],
        out_specs=[pl.BlockSpec((None, NSA_G, nsp, tt), lambda bi, ti: (bi, 0, 0, ti)),
                   pl.BlockSpec((None, NSA_G, NSA_TOPK, tt), lambda bi, ti: (bi, 0, 0, ti))],
        out_shape=[jax.ShapeDtypeStruct((b, NSA_G, nsp, t), F32),
                   jax.ShapeDtypeStruct((b, NSA_G, NSA_TOPK, t), I32)],
        compiler_params=_params(("parallel", "parallel")),
        name="nsa_topk",
    )(imp, pos)


def _bf16_parts(x, n):
    parts = []
    for _ in range(n):
        bits = np.asarray(x, np.float32).view(np.uint32)
        top = ((bits + np.uint32(0x7FFF) + ((bits >> np.uint32(16)) & np.uint32(1)))
               & np.uint32(0xFFFF0000)).view(np.float32)
        parts.append(float(top))
        x = float(np.float32(x) - top)
    return parts


def _nsa_prompt_kernel(q_ref, kv4_ref, kw_ref, oc_ref, sel_ref, gate_ref, out_ref,
                       k_aug, q_aug, s_buf, m_s, l_s, acc_s):
    qi = pl.program_id(1)
    r = NSA_R * TQ
    bpt = TK // NSA_BLK
    n_parts = SUB // 2
    qt = (q_ref[...] * NSA_SCALE).T.astype(BF16)
    gt = gate_ref[...].T
    lane = lax.broadcasted_iota(I32, (1, r), 1)
    qpos = qi * TQ + (lane & (TQ - 1))
    head = lane // TQ
    krow = lax.broadcasted_iota(I32, (TK, 1), 0)
    klane = lax.broadcasted_iota(I32, (1, TK), 1)
    row8 = lax.broadcasted_iota(I32, (SUB, 1), 0)
    n_tiles = (qi * TQ + TQ + TK - 1) // TK
    onehot = jnp.where(row8 == jnp.right_shift(klane, NSA_BLK.bit_length() - 1), 1.0, 0.0)

    def key_rows(j):
        rel = j * TK - qi * TQ + klane
        coarse = (rel & -NSA_BLK).astype(F32)
        fine = (rel & (NSA_BLK - 1)).astype(F32)
        alibi = jnp.where((row8 & 1) == 0, coarse, fine)
        return jnp.concatenate([alibi, onehot], axis=0).astype(BF16)

    def causal(s, j):
        return jnp.where(j * TK + krow <= qpos, s, NEG)

    def in_window(s, j):
        return jnp.where(qpos - (j * TK + krow) <= NSA_WIN, s, NEG)

    def tile(ref, lo, j):
        return ref[lo:lo + NSA_D, pl.ds(pl.multiple_of(j * TK, TK), TK)].astype(BF16)

    slopes, sel_chains, win_chains = [], [], []
    for g in range(NSA_G):
        kg, qa = k_aug.at[g], q_aug.at[g]
        qa[:NSA_D, :] = jnp.concatenate(
            [qt[(g * NSA_R + i) * NSA_D:(g * NSA_R + i + 1) * NSA_D, :] for i in range(NSA_R)], axis=1)
        slope8 = jnp.zeros((SUB, r), F32)
        for i in range(NSA_R):
            parts = _bf16_parts(_alibi_slope(g * NSA_R + i), n_parts)
            col = jnp.zeros((SUB, 1), F32)
            for k, part in enumerate(parts):
                col = jnp.where(jnp.right_shift(row8, 1) == k, part, col)
            slope8 = jnp.where(head == i, col, slope8)
        slopes.append(slope8)
        ks_lo, vs_lo = 2 * NSA_KVW + g * NSA_D, 3 * NSA_KVW + g * NSA_D
        kw_lo, vw_lo = g * NSA_D, NSA_KVW + g * NSA_D

        def qk_sel(j, g=g, kg=kg, qa=qa, ks_lo=ks_lo, slope8=slope8):
            kg[:NSA_D, :] = tile(kv4_ref, ks_lo, j)
            kg[NSA_D:, :] = key_rows(j)
            flags = sel_ref[g, pl.ds(j * bpt, SUB), :]
            off = jnp.where(flags > 0.5, 0.0, NEG)
            qa[NSA_D:, :] = jnp.concatenate(
                [slope8, jnp.concatenate([off] * NSA_R, axis=1)], axis=0).astype(BF16)
            return _dot_tn(kg[...], qa[...])

        def qk_win(j, kg=kg, qa=qa, kw_lo=kw_lo):
            kg[:NSA_D, :] = tile(kw_ref, kw_lo, j)
            kg[NSA_D:, :] = key_rows(j)
            return _dot_tn(kg[...], qa[...])

        state = (s_buf.at[g], m_s.at[g], l_s.at[g], acc_s.at[g])
        sel_chains.append((qk_sel, lambda j, p, lo=vs_lo: _dot(tile(kv4_ref, lo, j), p),
                           lambda s, j: s, causal) + state)
        win_chains.append((qk_win, lambda j, p, lo=vw_lo: _dot(tile(kw_ref, lo, j), p),
                           in_window, lambda s, j: causal(in_window(s, j), j)) + state)

    o_sels = _flash_tiles(0, n_tiles, sel_chains, LOG2E)
    for g in range(NSA_G):
        q_aug[g, NSA_D:, :] = jnp.concatenate([slopes[g], jnp.zeros((SUB, r), F32)], axis=0).astype(BF16)
    o_wins = _flash_tiles(jnp.maximum(qi * TQ - NSA_WIN, 0) // TK, n_tiles, win_chains, LOG2E)
    for g in range(NSA_G):
        o_sel, o_win = o_sels[g], o_wins[g]
        for i in range(NSA_R):
            h = g * NSA_R + i
            rows = slice(h * NSA_D, (h + 1) * NSA_D)
            cols = slice(i * TQ, (i + 1) * TQ)
            out_ref[rows, :] = (gt[h:h + 1, :] * oc_ref[rows, :]
                                + gt[NSA_H + h:NSA_H + h + 1, :] * o_sel[:, cols]
                                + gt[2 * NSA_H + h:2 * NSA_H + h + 1, :] * o_win[:, cols])


def _nsa_prompt(q, kv4_t, kw_t, oc_t, sel, gate):
    b, t, _ = q.shape
    nsp = sel.shape[2]
    r = NSA_R * TQ
    tok = lambda n: pl.BlockSpec((None, TQ, n), lambda bi, qi: (bi, qi, 0))
    full = lambda n: pl.BlockSpec((None, n, t), lambda bi, qi: (bi, 0, 0))
    return pl.pallas_call(
        _nsa_prompt_kernel,
        grid=(b, t // TQ),
        in_specs=[tok(NSA_W), full(4 * NSA_KVW), full(2 * NSA_KVW),
                  pl.BlockSpec((None, NSA_W, TQ), lambda bi, qi: (bi, 0, qi)),
                  pl.BlockSpec((None, NSA_G, nsp, TQ), lambda bi, qi: (bi, 0, 0, qi)),
                  tok(3 * NSA_H)],
        out_specs=pl.BlockSpec((None, NSA_W, TQ), lambda bi, qi: (bi, 0, qi)),
        out_shape=jax.ShapeDtypeStruct((b, NSA_W, t), F32),
        scratch_shapes=[pltpu.VMEM((NSA_G, NSA_D + 2 * SUB, TK), BF16),
                        pltpu.VMEM((NSA_G, NSA_D + 2 * SUB, r), BF16),
                        pltpu.VMEM((NSA_G, 2, TK, r), F32),
                        pltpu.VMEM((NSA_G, 1, r), F32), pltpu.VMEM((NSA_G, 1, r), F32),
                        pltpu.VMEM((NSA_G, NSA_D, r), F32)],
        compiler_params=_params(("parallel", "arbitrary")),
        name="nsa_prompt",
    )(q, kv4_t, kw_t, oc_t, sel, gate)


def _softmax_rows(parts):
    mx = None
    for s, _, _ in parts:
        pm = jnp.max(s, axis=1, keepdims=True)
        mx = pm if mx is None else jnp.maximum(mx, pm)
    den, num = None, None
    for s, v, feature_major in parts:
        p = jnp.exp(s - mx)
        d = jnp.sum(p, axis=1, keepdims=True)
        n = _dot_nt(p.astype(BF16), v) if feature_major else _dot(p.astype(BF16), v)
        den = d if den is None else den + d
        num = n if num is None else num + n
    return num / den


def _nsa_sel_decode_kernel(ts, pos0, nbp, idx_ref, pid_ref, q_ref, new_ref, *rest):
    nblk = NSA_G * NSA_TOPK
    pages = rest[:nblk]
    out_ref = rest[nblk]
    b, t = pl.program_id(0), pl.program_id(1)
    q = q_ref[...] * NSA_SCALE
    new = new_ref[...]
    lane = lax.broadcasted_iota(I32, (1, PAGE), 1)
    jn = lax.broadcasted_iota(I32, (1, SUB), 1)
    for g in range(NSA_G):
        qg = q[g * NSA_R:(g + 1) * NSA_R, :].astype(BF16)
        slope = jnp.concatenate(
            [jnp.full((1, 1), _alibi_slope(g * NSA_R + i), F32) for i in range(NSA_R)], axis=0)
        parts = []
        has_new = jnp.zeros((), I32)
        for k in range(NSA_TOPK):
            bid = idx_ref[((b * NSA_G + g) * ts + t) * NSA_TOPK + k]
            pg = pages[g * NSA_TOPK + k][...]
            kk = pg[g * NSA_D:(g + 1) * NSA_D, :].astype(BF16)
            vv = pg[NSA_KVW + g * NSA_D:NSA_KVW + (g + 1) * NSA_D, :].astype(BF16)
            sub = bid % BPP
            rel = ((bid - sub) * NSA_BLK - pos0 + lane).astype(F32)
            s = _dot(qg, kk) + slope * rel
            mine = (lane // NSA_BLK == sub) & (bid < nbp)
            parts.append((jnp.where(mine, s, NEG), vv, True))
            has_new = has_new + (bid >= nbp).astype(I32)
        kn = new[:, 2 * NSA_KVW + g * NSA_D:2 * NSA_KVW + (g + 1) * NSA_D].astype(BF16)
        vn = new[:, 3 * NSA_KVW + g * NSA_D:3 * NSA_KVW + (g + 1) * NSA_D].astype(BF16)
        sn = _dot_nt(qg, kn) + slope * jn.astype(F32)
        ok = (jn <= t) & (jn < ts) & (has_new > 0)
        parts.append((jnp.where(ok, sn, NEG), vn, False))
        out_ref[g * NSA_R:(g + 1) * NSA_R, :] = _softmax_rows(parts)


def _nsa_sel_decode(idx, page_table, cache_t, q4, kv4_new, ts, pos0):
    db, n_pages = page_table.shape
    nbp = n_pages * BPP
    cw = 2 * NSA_KVW
    page_ids = jnp.take_along_axis(page_table, jnp.minimum(idx, nbp - 1) // BPP, axis=1).reshape(-1)
    idx_flat = idx.reshape(-1)

    def page_map(b, t, idx_r, pid, g, k):
        return (pid[((b * NSA_G + g) * ts + t) * NSA_TOPK + k], 1, 0)

    page_specs = [pl.BlockSpec((None, cw, PAGE), functools.partial(page_map, g=g, k=k))
                  for g in range(NSA_G) for k in range(NSA_TOPK)]
    grid_spec = pltpu.PrefetchScalarGridSpec(
        num_scalar_prefetch=2,
        grid=(db, ts),
        in_specs=[pl.BlockSpec((None, None, NSA_H, NSA_D), lambda b, t, idx, pt: (b, t, 0, 0)),
                  pl.BlockSpec((None, SUB, 4 * NSA_KVW), lambda b, t, idx, pt: (b, 0, 0))] + page_specs,
        out_specs=pl.BlockSpec((None, None, NSA_H, NSA_D), lambda b, t, idx, pt: (b, t, 0, 0)))
    return pl.pallas_call(
        functools.partial(_nsa_sel_decode_kernel, ts, pos0, nbp),
        grid_spec=grid_spec,
        out_shape=jax.ShapeDtypeStruct((db, ts, NSA_H, NSA_D), F32),
        compiler_params=_params(("parallel", "arbitrary")),
        name="nsa_sel_decode",
    )(idx_flat, page_ids, q4, kv4_new, *([cache_t] * (NSA_G * NSA_TOPK)))


def _nsa_win_decode_kernel(ts, q_ref, win_ref, new_ref, out_ref):
    rows = NSA_R * SUB
    win = win_ref[...]
    new = new_ref[...]
    tq = lax.broadcasted_iota(I32, (rows, 1), 0) & (SUB - 1)
    iw = lax.broadcasted_iota(I32, (1, NSA_WIN), 1)
    jn = lax.broadcasted_iota(I32, (1, SUB), 1)
    for g in range(NSA_G):
        qg = jnp.concatenate([q_ref[:, g * NSA_R + i, :] for i in range(NSA_R)], axis=0)
        qg = (qg * NSA_SCALE).astype(BF16)
        slope = jnp.concatenate(
            [jnp.full((SUB, 1), _alibi_slope(g * NSA_R + i), F32) for i in range(NSA_R)], axis=0)
        kw = win[g * NSA_D:(g + 1) * NSA_D, :].astype(BF16)
        vw = win[NSA_KVW + g * NSA_D:NSA_KVW + (g + 1) * NSA_D, :].astype(BF16)
        kn = new[:, g * NSA_D:(g + 1) * NSA_D].astype(BF16)
        vn = new[:, NSA_KVW + g * NSA_D:NSA_KVW + (g + 1) * NSA_D].astype(BF16)
        sw = _dot(qg, kw) + slope * (iw - NSA_WIN).astype(F32)
        sw = jnp.where(iw >= tq, sw, NEG)
        sn = _dot_nt(qg, kn) + slope * jn.astype(F32)
        sn = jnp.where((jn <= tq) & (jn < ts), sn, NEG)
        o = _softmax_rows([(sw, vw, True), (sn, vn, False)])
        for i in range(NSA_R):
            out_ref[:, g * NSA_R + i, :] = o[i * SUB:(i + 1) * SUB, :]


def _nsa_win_decode(q4, win_t, kw_new, ts):
    db = q4.shape[0]
    return pl.pallas_call(
        functools.partial(_nsa_win_decode_kernel, ts),
        grid=(db,),
        in_specs=[pl.BlockSpec((None, SUB, NSA_H, NSA_D), lambda b: (b, 0, 0, 0)),
                  pl.BlockSpec((None, 2 * NSA_KVW, NSA_WIN), lambda b: (b, 0, 0)),
                  pl.BlockSpec((None, SUB, 2 * NSA_KVW), lambda b: (b, 0, 0))],
        out_specs=pl.BlockSpec((None, SUB, NSA_H, NSA_D), lambda b: (b, 0, 0, 0)),
        out_shape=jax.ShapeDtypeStruct((db, SUB, NSA_H, NSA_D), F32),
        compiler_params=_params(("parallel",)),
        name="nsa_win_decode",
    )(q4, win_t, kw_new)


def _odd_tail(o, z_ref, x_ref, wout, gfin):
    y = x_ref[...] + _dot((o * _silu(z_ref[...])).astype(BF16), wout[...])
    return _rms(y, gfin[...])


def _odd_out_prompt_kernel(ot_ref, z_ref, x_ref, wout, gfin, out_ref):
    out_ref[...] = _odd_tail(ot_ref[...].T, z_ref, x_ref, wout, gfin)


def _odd_out_prompt(ot, z, x, wout, gfin, tm):
    b, _, t = ot.shape
    tok = pl.BlockSpec((None, tm, D_MODEL), lambda bi, ti: (bi, ti, 0))
    return pl.pallas_call(
        _odd_out_prompt_kernel,
        grid=(b, t // tm),
        in_specs=[pl.BlockSpec((None, NSA_W, tm), lambda bi, ti: (bi, 0, ti)), tok, tok,
                  _const_spec(wout.shape), _const_spec(gfin.shape)],
        out_specs=tok,
        out_shape=jax.ShapeDtypeStruct((b, t, D_MODEL), F32),
        compiler_params=_params(("parallel", "parallel")),
        name="odd_out_prompt",
    )(ot, z, x, wout, gfin)


def _odd_out_decode_kernel(oc_ref, os_ref, ow_ref, gate_ref, z_ref, x_ref, wout, gfin, out_ref):
    gate = gate_ref[...]
    oc, osel, ow = oc_ref[...], os_ref[...], ow_ref[...]
    heads = []
    for h in range(NSA_H):
        c = slice(h * NSA_D, (h + 1) * NSA_D)
        heads.append(gate[:, h:h + 1] * oc[:, c] + gate[:, NSA_H + h:NSA_H + h + 1] * osel[:, c]
                     + gate[:, 2 * NSA_H + h:2 * NSA_H + h + 1] * ow[:, c])
    out_ref[...] = _odd_tail(jnp.concatenate(heads, axis=1), z_ref, x_ref, wout, gfin)


def _odd_out_decode(oc, osel, ow, gate, z, x, wout, gfin):
    m = x.shape[0]
    args = (oc, osel, ow, gate, z, x, wout, gfin)
    return pl.pallas_call(
        _odd_out_decode_kernel,
        grid=(1,),
        in_specs=[_const_spec(a.shape) for a in args],
        out_specs=_const_spec((m, D_MODEL)),
        out_shape=jax.ShapeDtypeStruct((m, D_MODEL), F32),
        compiler_params=_params(("arbitrary",)),
        name="odd_out_decode",
    )(*args)


def _rope_tables(pos):
    half = MLA_ROPE // 2
    inv = ROPE_THETA ** (-jnp.arange(half, dtype=F32) / half)
    ang = pos.astype(F32)[:, None] * inv[None, :]
    cos, sin = jnp.cos(ang), jnp.sin(ang)
    return jnp.concatenate([cos, cos], axis=1), jnp.concatenate([-sin, sin], axis=1)


def _block_diag(x):
    t, g, r, c = x.shape
    eye = jnp.eye(g, dtype=x.dtype)
    return jnp.einsum("tgrc,gh->tgrhc", x, eye).reshape(t, g * r, g * c)


def _even_weights(norm_g, w_in, g_q, g_kv, w_uq, w_uk, w_uv):
    edges = [0, MLA_QL, MLA_QL + MLA_KVL, MLA_QL + MLA_ROW]
    edges += [edges[-1] + MLA_W, edges[-1] + MLA_W + S5_W, edges[-1] + MLA_W + 2 * S5_W]
    wb = w_in.astype(BF16)
    pieces = [wb[:, edges[i]:edges[i + 1]] for i in range(6)]
    in_w = (norm_g[None, :], *pieces, g_q[None, :], g_kv[None, :])
    uq = jnp.transpose(w_uq, (1, 0, 2)).astype(BF16)
    mla_w = (uq[:, :, :MLA_NOPE], uq[:, :, MLA_NOPE:],
             jnp.transpose(w_uk, (1, 2, 0)).astype(BF16),
             jnp.transpose(w_uv, (1, 0, 2)).astype(BF16))
    uq_t = jnp.transpose(w_uq, (1, 2, 0)).astype(BF16)
    mla_wt = (uq_t[:, :MLA_NOPE].reshape(MLA_H * MLA_NOPE, MLA_QL),
              uq_t[:, MLA_NOPE:].reshape(MLA_H * MLA_ROPE, MLA_QL),
              jnp.transpose(w_uk, (1, 0, 2)).astype(BF16),
              jnp.transpose(w_uv, (1, 2, 0)).astype(BF16))
    return in_w, mla_w, mla_wt


def _s5_weights(lam_re, lam_im, log_dt, b_re, b_im, c_re, c_im, d_skip, w_glu, b_glu):
    def bmat(b):
        return _block_diag(jnp.transpose(b.reshape(S5_NT, S5_GT, S5_N, S5_P), (0, 1, 3, 2)))

    def cmat(c):
        return _block_diag(jnp.transpose(c.reshape(S5_NT, S5_GT, S5_P, S5_N), (0, 1, 3, 2)))

    return (lam_re.reshape(1, S5_S), lam_im.reshape(1, S5_S),
            jnp.repeat(log_dt, S5_N).reshape(1, S5_S),
            bmat(b_re), bmat(b_im), cmat(c_re), cmat(c_im),
            d_skip.reshape(1, S5_W), w_glu.astype(BF16), b_glu[None, :])


def _odd_weights(norm_g, w_in, pe_k, pe_v, phi1_k, phi2_k, phi1_v, phi2_v):
    wb = w_in.astype(BF16)
    e0 = NSA_W
    e1 = e0 + 4 * NSA_KVW
    e2 = e1 + 2 * NSA_KVW
    e3 = e2 + 3 * NSA_H
    in_w = (norm_g[None, :], wb[:, :e0], wb[:, e0:e1], wb[:, e1:e2], wb[:, e2:e3], wb[:, e3:])
    pe = jnp.concatenate([pe_k, pe_k, pe_v, pe_v], axis=1)
    p1k = phi1_k.reshape(NSA_BLK, NSA_D, NSA_D)
    p1v = phi1_v.reshape(NSA_BLK, NSA_D, NSA_D)
    w1 = _block_diag(jnp.stack([p1k, p1k, p1v, p1v], axis=1)).astype(BF16)
    phi2 = _block_diag(jnp.stack([phi2_k, phi2_k, phi2_v, phi2_v], axis=0)[None])[0].astype(BF16)
    return in_w, (pe, w1, phi2)


def _pad_tokens(x, n):
    return jnp.pad(x, ((0, 0), (0, n - x.shape[1])) + ((0, 0),) * (x.ndim - 2))


def _rows_last(x):
    nd = x.ndim
    xt = jnp.transpose(x, (0,) + tuple(range(2, nd)) + (1,))
    return xt.reshape(x.shape[0], -1, x.shape[1])


def _rows_second(x_t, feature_shape):
    b, _, rows = x_t.shape
    nf = len(feature_shape)
    xt = x_t.reshape((b,) + tuple(feature_shape) + (rows,))
    return jnp.transpose(xt, (0, nf + 1) + tuple(range(1, nf + 1)))


def kernel(x_prompt, x_sample, cache_mla, state_s5, cache_nsa_kv, state_nsa_win, page_table, norm_even, w_in_even, mla_g_q, mla_g_kv, mla_w_uq, mla_w_uk, mla_w_uv, s5_lambda_re, s5_lambda_im, s5_log_dt, s5_b_re, s5_b_im, s5_c_re, s5_c_im, s5_d, s5_w_glu, s5_b_glu, w_out_even, norm_odd, w_in_odd, nsa_pe_k, nsa_pe_v, nsa_phi1_k, nsa_phi2_k, nsa_phi1_v, nsa_phi2_v, w_out_odd, norm_final):
    b, t, _ = x_prompt.shape
    db, ts, _ = x_sample.shape
    n_pages = page_table.shape[1]
    past = n_pages * PAGE
    mp, ms = b * t, db * ts
    tm = min(512, t)
    assert t % (PACK * NSA_BLK) == 0 and t % TK == 0 and ts <= SUB
    assert t <= NSA_BLK * 256

    pos_p = jnp.arange(t, dtype=I32)
    pos_s = past + jnp.arange(SUB, dtype=I32)
    cos_p, sin_p = _rope_tables(pos_p)
    cos_s, sin_s = _rope_tables(pos_s)
    cos_st, sin_st = jnp.tile(cos_s[:ts], (db, 1)), jnp.tile(sin_s[:ts], (db, 1))

    even_in_w, mla_w, mla_wt = _even_weights(norm_even[0], w_in_even[0], mla_g_q[0], mla_g_kv[0],
                                             mla_w_uq[0], mla_w_uk[0], mla_w_uv[0])
    s5_w = _s5_weights(s5_lambda_re[0], s5_lambda_im[0], s5_log_dt[0], s5_b_re[0], s5_b_im[0],
                       s5_c_re[0], s5_c_im[0], s5_d[0], s5_w_glu[0], s5_b_glu[0])
    wo_e = w_out_even[0].astype(BF16)
    wo_a, wo_b = wo_e[:MLA_W], wo_e[MLA_W:]

    xp2 = x_prompt.reshape(mp, D_MODEL)
    cq, rows_pt, za, u, zb = _even_in(xp2, even_in_w, cos_p, sin_p, tm, seq=t)
    mix_a = _mla_prompt(cq.reshape(b, t, MLA_QL), rows_pt, za.reshape(b, t, MLA_W), mla_wt, cos_p.T, sin_p.T)
    zeros_p = jnp.zeros((b, S5_S), F32)
    mix_b, sre_p, sim_p = _s5(u.reshape(b, t, S5_W), zb.reshape(b, t, S5_W), s5_w, zeros_p, zeros_p,
                              min(128, t))
    xp1 = _even_out(xp2, mix_a.reshape(mp, MLA_W), mix_b.reshape(mp, S5_W), wo_a, wo_b, tm)

    xs2 = x_sample.reshape(ms, D_MODEL)
    cq_s, rows_s, za_s, u_s, zb_s = _even_in(xs2, even_in_w, cos_st, sin_st, ms)
    pad3 = lambda a, n: _pad_tokens(a.reshape(db, ts, n), SUB)
    mix_a_s = _mla_decode(page_table, _rows_last(cache_mla[0]), pad3(cq_s, MLA_QL), pad3(rows_s, MLA_ROW),
                          pad3(za_s, MLA_W), mla_w, cos_s, sin_s, ts)[:, :ts]
    st = state_s5[0]
    mix_b_s, sre_s, sim_s = _s5(u_s.reshape(db, ts, S5_W), zb_s.reshape(db, ts, S5_W), s5_w,
                                st[..., 0].reshape(db, S5_S), st[..., 1].reshape(db, S5_S), ts)
    xs1 = _even_out(xs2, mix_a_s.reshape(ms, MLA_W), mix_b_s.reshape(ms, S5_W), wo_a, wo_b, ms)

    odd_in_w, cmp_w = _odd_weights(norm_odd[0], w_in_odd[0], nsa_pe_k[0], nsa_pe_v[0], nsa_phi1_k[0],
                                   nsa_phi2_k[0], nsa_phi1_v[0], nsa_phi2_v[0])
    wo_o = w_out_odd[0].astype(BF16)
    gfin = norm_final[None, :]

    q, kv4_t, kw_t, gate, z = _odd_in(xp1, odd_in_w, tm, seq=t)
    nblk_p = t // NSA_BLK
    cb = _compress_prompt(kv4_t, *cmp_w)
    q3 = q.reshape(b, t, NSA_W)
    tt = min(256, t)
    oc_t, imp = _cmp(q3, cb.reshape(b, nblk_p, 2 * NSA_KVW), pos_p[None], tt)
    sel, _ = _topk(imp, pos_p[None], nblk_p, tt)
    o_t = _nsa_prompt(q3, kv4_t, kw_t, oc_t, sel, gate.reshape(b, t, 3 * NSA_H))
    y_prompt = _odd_out_prompt(o_t, z.reshape(b, t, NSA_W), xp1.reshape(b, t, D_MODEL), wo_o, gfin, tt)

    q_s, kv4_s, kw_s, gate_s, z_s = _odd_in(xs1, odd_in_w, ms)
    cache_nsa_t = _rows_last(cache_nsa_kv[0])
    cb_s = _compress_decode(page_table, cache_nsa_t, *cmp_w)
    n_sel_s = -(-(past + ts) // NSA_BLK)
    q_s3 = q_s.reshape(db, ts, NSA_W)
    oc_ts, imp_s = _cmp(_pad_tokens(q_s3, SUB), cb_s, pos_s[None], SUB)
    imp_l = jnp.transpose(imp_s[..., :ts], (1, 2, 0, 3)).reshape(1, NSA_G, cb_s.shape[1], ms)
    pos_l = jnp.tile(pos_s[:ts], db)[None]
    _, idx_l = _topk(imp_l, pos_l, n_sel_s, ms)
    idx_s = jnp.transpose(idx_l.reshape(NSA_G, NSA_TOPK, db, ts), (2, 0, 3, 1)).reshape(db, -1)
    q_s4 = q_s.reshape(db, ts, NSA_H, NSA_D)
    o_sel_s = _nsa_sel_decode(idx_s, page_table, cache_nsa_t, q_s4, pad3(kv4_s, 4 * NSA_KVW), ts, past)
    win_t = _rows_last(state_nsa_win[0])
    kw_s3 = kw_s.reshape(db, ts, 2 * NSA_KVW)
    o_win_s = _nsa_win_decode(_pad_tokens(q_s4, SUB), win_t, _pad_tokens(kw_s3, SUB), ts)
    oc_s = jnp.transpose(oc_ts, (0, 2, 1))[:, :ts].reshape(ms, NSA_W)
    y_sample = _odd_out_decode(oc_s, o_sel_s.reshape(ms, NSA_W), o_win_s[:, :ts].reshape(ms, NSA_W),
                               gate_s, z_s, xs1, wo_o, gfin)

    state = lambda re, im, n: jnp.stack([re, im], axis=-1).reshape(1, n, S5_G, S5_N, 2)
    win_shape = (2, NSA_G, NSA_D)
    if t >= NSA_WIN:
        win_pt = kw_t[:, :, t - NSA_WIN:]
    else:
        win_pt = jnp.pad(kw_t, ((0, 0), (0, 0), (NSA_WIN - t, 0)))
    win_st = jnp.concatenate([win_t[:, :, ts:], jnp.transpose(kw_s3, (0, 2, 1))], axis=2)
    return (y_prompt, y_sample.reshape(db, ts, D_MODEL),
            _rows_second(rows_pt, (MLA_ROW,))[None], rows_s.reshape(1, db, ts, MLA_ROW),
            state(sre_p, sim_p, b), state(sre_s, sim_s, db),
            _rows_second(kv4_t, (4, NSA_G, NSA_D))[None], kv4_s.reshape(1, db, ts, 4, NSA_G, NSA_D),
            _rows_second(win_pt, win_shape)[None], _rows_second(win_st, win_shape)[None])
```

```python
import functools

import jax
import jax.numpy as jnp
import numpy as np
from jax import lax
from jax.experimental import pallas as pl
from jax.experimental.pallas import tpu as pltpu

F32, BF16, I32 = jnp.float32, jnp.bfloat16, jnp.int32

D_MODEL = 1024
PAGE = 128
EPS = 1e-6
ROPE_THETA = 10000.0
MLA_H, MLA_NOPE, MLA_ROPE, MLA_V = 8, 64, 32, 64
MLA_QL, MLA_KVL = 384, 256
MLA_ROW = MLA_KVL + MLA_ROPE
MLA_W = MLA_H * MLA_V
MLA_SCALE = (MLA_NOPE + MLA_ROPE) ** -0.5
S5_G, S5_P, S5_N = 32, 16, 64
S5_W = S5_G * S5_P
S5_S = S5_G * S5_N
S5_GT = 4
S5_NT = S5_G // S5_GT
NSA_H, NSA_G, NSA_D = 16, 2, 64
NSA_R = NSA_H // NSA_G
NSA_W = NSA_H * NSA_D
NSA_KVW = NSA_G * NSA_D
NSA_BLK, NSA_TOPK, NSA_WIN = 64, 16, 512
NSA_SCALE = NSA_D ** -0.5
FORCED_BONUS = float(NSA_R + 1)
BPP = PAGE // NSA_BLK

LOG2E = 1.4426950408889634
NEG = -1e30
GONE = -3e38
TQ = 128
TK = 256
SUB = 8
PACK = 16
MLA_DECODE_GROUP = 64
VMEM_LIMIT = 56 * 1024 * 1024


def _dot(a, b):
    return jnp.dot(a, b, preferred_element_type=F32)


def _dot_nt(a, b):
    return lax.dot_general(a, b, (((1,), (1,)), ((), ())), preferred_element_type=F32)


def _dot_tn(a, b):
    return lax.dot_general(a, b, (((0,), (0,)), ((), ())), preferred_element_type=F32)


def _rms(x, g):
    return x * lax.rsqrt(jnp.mean(x * x, axis=-1, keepdims=True) + EPS) * g


def _silu(x):
    return x * jax.nn.sigmoid(x)


def _rope_nat(x, cosf, sinf):
    half = x.shape[1] // 2
    xs = jnp.concatenate([x[:, half:], x[:, :half]], axis=1)
    return x * cosf + xs * sinf


def _params(sem):
    return pltpu.CompilerParams(dimension_semantics=sem, vmem_limit_bytes=VMEM_LIMIT)


def _const_spec(shape):
    n = len(shape)
    return pl.BlockSpec(shape, lambda *a, _n=n: (0,) * _n)


def _seq_major_spec(width, tm, seq):
    per = seq // tm
    return pl.BlockSpec((None, width, tm), lambda i: (i // per, 0, i % per))


def _even_in_kernel(feature_major, x_ref, g_ref, wcq, wckv, wkr, wza, wu, wzb, gq, gkv, cos_ref, sin_ref,
                    cq_o, rows_o, za_o, u_o, zb_o):
    h = _rms(x_ref[...], g_ref[...]).astype(BF16)
    cq_o[...] = _rms(_dot(h, wcq[...]), gq[...])
    ckv = _rms(_dot(h, wckv[...]), gkv[...])
    krope = _rope_nat(_dot(h, wkr[...]), cos_ref[...], sin_ref[...])
    if feature_major:
        rows_o[:MLA_KVL, :] = ckv.T
        rows_o[MLA_KVL:, :] = krope.T
    else:
        rows_o[:, :MLA_KVL] = ckv
        rows_o[:, MLA_KVL:] = krope
    za_o[...] = _dot(h, wza[...])
    u_o[...] = _dot(h, wu[...])
    zb_o[...] = _dot(h, wzb[...])


def _even_in(x2, wts, cosf, sinf, tm, seq=None):
    m = x2.shape[0]
    tab_blocks = cosf.shape[0] // tm
    row = lambda n: pl.BlockSpec((tm, n), lambda i: (i, 0))
    tab = pl.BlockSpec((tm, MLA_ROPE), lambda i: (i % tab_blocks, 0))
    widths = (MLA_QL, MLA_ROW, MLA_W, S5_W, S5_W)
    out_specs = [row(n) for n in widths]
    out_shape = [jax.ShapeDtypeStruct((m, n), F32) for n in widths]
    if seq is not None:
        out_specs[1] = _seq_major_spec(MLA_ROW, tm, seq)
        out_shape[1] = jax.ShapeDtypeStruct((m // seq, MLA_ROW, seq), F32)
    return pl.pallas_call(
        functools.partial(_even_in_kernel, seq is not None),
        grid=(m // tm,),
        in_specs=[row(D_MODEL)] + [_const_spec(c.shape) for c in wts] + [tab, tab],
        out_specs=out_specs,
        out_shape=out_shape,
        compiler_params=_params(("parallel",)),
        name="even_in",
    )(x2, *wts, cosf, sinf)


def _mla_queries(cq, wuqn, wuqr, wuk, cosf, sinf, h):
    cqb = cq.astype(BF16)
    qn = _dot(cqb, wuqn[h])
    ql = _dot(qn.astype(BF16), wuk[h])
    qr = _rope_nat(_dot(cqb, wuqr[h]), cosf, sinf)
    return ql, qr


def _flash_tiles(lo, hi, chains, c_exp):
    for qk, _, first_mask, _, s_buf, m_s, acc_s, l_s in chains:
        m_s[...] = jnp.full(m_s.shape, NEG, F32)
        acc_s[...] = jnp.zeros(acc_s.shape, F32)
        if l_s is not None:
            l_s[...] = jnp.zeros(l_s.shape, F32)
        s_buf[0] = first_mask(qk(lo), lo)

    def consume(chain, j, s):
        _, pv, _, _, _, m_s, acc_s, l_s = chain
        m_old = m_s[...]
        m_new = jnp.maximum(m_old, jnp.max(s, axis=0, keepdims=True))
        alpha = jnp.exp2((m_old - m_new) * c_exp)
        x = (s - m_new) * c_exp
        if l_s is None:
            p = jnp.exp2(x.astype(BF16))
        else:
            pf = jnp.exp2(x)
            l_s[...] = alpha * l_s[...] + jnp.sum(pf, axis=0, keepdims=True)
            p = pf.astype(BF16)
        acc_s[...] = alpha * acc_s[...] + pv(j, p)
        m_s[...] = m_new

    def body(j, carry):
        for chain in chains:
            chain[4][1] = chain[0](j + 1)
        for chain in chains:
            consume(chain, j, chain[4][0])
        for chain in chains:
            chain[4][0] = chain[4][1]
        return carry

    lax.fori_loop(lo, hi - 1, body, 0)
    last = hi - 1
    for chain in chains:
        consume(chain, last, chain[3](chain[4][0], last))
    outs = []
    for chain in chains:
        acc, l_s = chain[6][...], chain[7]
        if l_s is None:
            dv = acc.shape[0] - PACK
            outs.append(acc[:dv, :] / acc[dv:dv + 1, :])
        else:
            outs.append(acc / l_s[...])
    return outs


def _with_ones(v):
    return jnp.concatenate([v, jnp.ones((PACK, v.shape[1]), BF16)], axis=0)


def _mla_prompt_kernel(cq_ref, rows_ref, za_ref, wuqn_t, wuqr_t, wuk_t, wuv_t, cos_ref, sin_ref,
                       out_ref, qtl, qtr, s_buf, m_s, l_s, acc_s):
    qi = pl.program_id(1)
    r = MLA_H * TQ
    half = MLA_ROPE // 2
    cq_t = cq_ref[...].T.astype(BF16)
    qn_t = _dot(wuqn_t[...], cq_t).astype(BF16)
    qr_t = _dot(wuqr_t[...], cq_t)
    cos_t, sin_t = cos_ref[...], sin_ref[...]
    for h in range(MLA_H):
        qtl[:, h * TQ:(h + 1) * TQ] = _dot(wuk_t[h], qn_t[h * MLA_NOPE:(h + 1) * MLA_NOPE, :]).astype(BF16)
        x = qr_t[h * MLA_ROPE:(h + 1) * MLA_ROPE, :]
        xs = jnp.concatenate([x[half:, :], x[:half, :]], axis=0)
        qtr[:, h * TQ:(h + 1) * TQ] = (x * cos_t + xs * sin_t).astype(BF16)
    qpos = qi * TQ + (lax.broadcasted_iota(I32, (1, r), 1) & (TQ - 1))
    krow = lax.broadcasted_iota(I32, (TK, 1), 0)

    def keys(j):
        return rows_ref[:, pl.ds(pl.multiple_of(j * TK, TK), TK)]

    def qk(j):
        kt = keys(j)
        return (_dot_tn(kt[:MLA_KVL, :].astype(BF16), qtl[...])
                + _dot_tn(kt[MLA_KVL:, :].astype(BF16), qtr[...]))

    def pv(j, p):
        return _dot(keys(j)[:MLA_KVL, :].astype(BF16), p)

    def causal(s, j):
        return jnp.where(j * TK + krow <= qpos, s, NEG)

    n_tiles = (qi * TQ + TQ + TK - 1) // TK
    chain = (qk, pv, lambda s, j: s, causal, s_buf, m_s, acc_s, l_s)
    o = _flash_tiles(0, n_tiles, [chain], MLA_SCALE * LOG2E)[0].astype(BF16)
    heads = [_dot(wuv_t[h], o[:, h * TQ:(h + 1) * TQ]) for h in range(MLA_H)]
    out_ref[...] = jnp.concatenate(heads, axis=0).T * _silu(za_ref[...])


def _mla_prompt(cq, rows_t, za, wts, cos_t, sin_t):
    b, t, _ = cq.shape
    r = MLA_H * TQ
    tok = lambda n: pl.BlockSpec((None, TQ, n), lambda bi, qi: (bi, qi, 0))
    tab = pl.BlockSpec((MLA_ROPE, TQ), lambda bi, qi: (0, qi))
    return pl.pallas_call(
        _mla_prompt_kernel,
        grid=(b, t // TQ),
        in_specs=[tok(MLA_QL), pl.BlockSpec((None, MLA_ROW, t), lambda bi, qi: (bi, 0, 0)), tok(MLA_W)]
        + [_const_spec(w.shape) for w in wts] + [tab, tab],
        out_specs=tok(MLA_W),
        out_shape=jax.ShapeDtypeStruct((b, t, MLA_W), F32),
        scratch_shapes=[pltpu.VMEM((MLA_KVL, r), BF16), pltpu.VMEM((MLA_ROPE, r), BF16),
                        pltpu.VMEM((2, TK, r), F32),
                        pltpu.VMEM((1, r), F32), pltpu.VMEM((1, r), F32), pltpu.VMEM((MLA_KVL, r), F32)],
        compiler_params=_params(("parallel", "arbitrary")),
        name="mla_prompt",
    )(cq, rows_t, za, *wts, cos_t, sin_t)


def _mla_decode_kernel(n_pages_step, ts, pt_ref, cq_ref, rows_ref, za_ref, wuqn, wuqr, wuk, wuv,
                       cos_ref, sin_ref, *rest):
    pages = rest[:n_pages_step]
    out_ref, ql_s, qr_s, m_s, l_s, acc_s = rest[n_pages_step:]
    gi = pl.program_id(1)
    rows = MLA_H * SUB

    @pl.when(gi == 0)
    def _():
        cq = cq_ref[...]
        cosf, sinf = cos_ref[...], sin_ref[...]
        for h in range(MLA_H):
            ql, qr = _mla_queries(cq, wuqn, wuqr, wuk, cosf, sinf, h)
            ql_s[h * SUB:(h + 1) * SUB, :] = ql.astype(BF16)
            qr_s[h * SUB:(h + 1) * SUB, :] = qr.astype(BF16)
        m_s[...] = jnp.full((rows, 1), NEG, F32)
        l_s[...] = jnp.zeros((rows, 1), F32)
        acc_s[...] = jnp.zeros((rows, MLA_KVL), F32)

    def update(s, vals, feature_major):
        m_old = m_s[...]
        m_new = jnp.maximum(m_old, jnp.max(s, axis=1, keepdims=True))
        alpha = jnp.exp(m_old - m_new)
        p = jnp.exp(s - m_new)
        l_s[...] = alpha * l_s[...] + jnp.sum(p, axis=1, keepdims=True)
        pv = None
        for (lo, hi), v in vals:
            pj = p[:, lo:hi].astype(BF16)
            term = _dot_nt(pj, v) if feature_major else _dot(pj, v)
            pv = term if pv is None else pv + term
        acc_s[...] = alpha * acc_s[...] + pv
        m_s[...] = m_new

    ql, qr = ql_s[...], qr_s[...]
    scores, vals = [], []
    for j, pg in enumerate(pages):
        k = pg[...]
        ckv = k[:MLA_KVL, :].astype(BF16)
        scores.append(_dot(ql, ckv) + _dot(qr, k[MLA_KVL:, :].astype(BF16)))
        vals.append(((len(vals) * PAGE, (len(vals) + 1) * PAGE), ckv))
        if len(vals) == MLA_DECODE_GROUP or j == n_pages_step - 1:
            update(jnp.concatenate(scores, axis=1) * MLA_SCALE, vals, True)
            scores, vals = [], []

    @pl.when(gi == pl.num_programs(1) - 1)
    def _():
        kn = rows_ref[...]
        ckv = kn[:, :MLA_KVL].astype(BF16)
        s = (_dot_nt(ql, ckv) + _dot_nt(qr, kn[:, MLA_KVL:].astype(BF16))) * MLA_SCALE
        tq = lax.broadcasted_iota(I32, (rows, SUB), 0) & (SUB - 1)
        jk = lax.broadcasted_iota(I32, (rows, SUB), 1)
        s = jnp.where((jk <= tq) & (jk < ts), s, NEG)
        update(s, [((0, SUB), ckv)], False)
        o = (acc_s[...] / l_s[...]).astype(BF16)
        heads = [_dot(o[h * SUB:(h + 1) * SUB, :], wuv[h]) for h in range(MLA_H)]
        out_ref[...] = jnp.concatenate(heads, axis=1) * _silu(za_ref[...])


def _mla_decode(page_table, cache_t, cq, rows, za, wts, cosf, sinf, ts, n_pages_step=64):
    db, n_pages = page_table.shape
    rows_n = MLA_H * SUB
    tok = lambda n: pl.BlockSpec((None, SUB, n), lambda b, g, pt: (b, 0, 0))
    cst = lambda shape: pl.BlockSpec(shape, lambda b, g, pt, _n=len(shape): (0,) * _n)
    page_specs = [
        pl.BlockSpec((None, MLA_ROW, PAGE), lambda b, g, pt, j=j: (pt[b, g * n_pages_step + j], 0, 0))
        for j in range(n_pages_step)]
    grid_spec = pltpu.PrefetchScalarGridSpec(
        num_scalar_prefetch=1,
        grid=(db, n_pages // n_pages_step),
        in_specs=[tok(MLA_QL), tok(MLA_ROW), tok(MLA_W)] + [cst(w.shape) for w in wts]
        + [cst(cosf.shape), cst(sinf.shape)] + page_specs,
        out_specs=tok(MLA_W),
        scratch_shapes=[pltpu.VMEM((rows_n, MLA_KVL), BF16), pltpu.VMEM((rows_n, MLA_ROPE), BF16),
                        pltpu.VMEM((rows_n, 1), F32), pltpu.VMEM((rows_n, 1), F32),
                        pltpu.VMEM((rows_n, MLA_KVL), F32)])
    return pl.pallas_call(
        functools.partial(_mla_decode_kernel, n_pages_step, ts),
        grid_spec=grid_spec,
        out_shape=jax.ShapeDtypeStruct((db, SUB, MLA_W), F32),
        compiler_params=_params(("parallel", "arbitrary")),
        name="mla_decode",
    )(page_table, cq, rows, za, *wts, cosf, sinf, *([cache_t] * n_pages_step))


def _s5_kernel(u_ref, zb_ref, lre_ref, lim_ref, ldt_ref, wbre, wbim, wcre, wcim, d_ref, wglu, bglu,
               h0re_ref, h0im_ref, mix_o, sre_o, sim_o, bure, buim, hre, him):
    step = pl.program_id(0)
    nb, chunk, _ = u_ref.shape
    rows = nb * chunk

    @pl.when(step == 0)
    def _():
        hre[...] = h0re_ref[...]
        him[...] = h0im_ref[...]

    lre, lim = lre_ref[...], lim_ref[...]
    dt = jnp.exp(ldt_ref[...])
    mag = jnp.exp(lre * dt)
    are, aim = mag * jnp.cos(lim * dt), mag * jnp.sin(lim * dt)
    den = lre * lre + lim * lim
    cre = ((are - 1.0) * lre + aim * lim) / den
    cim = (aim * lre - (are - 1.0) * lim) / den

    ut = jnp.swapaxes(u_ref[...], 0, 1).reshape(rows, S5_W)
    kw, nw = S5_GT * S5_P, S5_GT * S5_N
    for jt in range(S5_NT):
        cr, ci = cre[:, jt * nw:(jt + 1) * nw], cim[:, jt * nw:(jt + 1) * nw]
        bre = (cr * wbre[jt] - ci * wbim[jt]).astype(BF16)
        bim = (cr * wbim[jt] + ci * wbre[jt]).astype(BF16)
        uj = ut[:, jt * kw:(jt + 1) * kw].astype(BF16)
        bure[:, jt * nw:(jt + 1) * nw] = _dot(uj, bre)
        buim[:, jt * nw:(jt + 1) * nw] = _dot(uj, bim)

    def scan(t, carry):
        hr, hi = carry
        sl = pl.ds(pl.multiple_of(t * nb, SUB), nb)
        nr = are * hr - aim * hi + bure[sl, :]
        ni = are * hi + aim * hr + buim[sl, :]
        bure[sl, :] = nr
        buim[sl, :] = ni
        return nr, ni

    hr, hi = lax.fori_loop(0, chunk, scan, (hre[...], him[...]))
    hre[...] = hr
    him[...] = hi
    sre_o[...] = hr
    sim_o[...] = hi

    ys = []
    for jt in range(S5_NT):
        sr = bure[:, jt * nw:(jt + 1) * nw].astype(BF16)
        si = buim[:, jt * nw:(jt + 1) * nw].astype(BF16)
        ys.append(_dot(sr, wcre[jt].astype(BF16)) - _dot(si, wcim[jt].astype(BF16)))
    y = jnp.concatenate(ys, axis=1) + d_ref[...] * ut
    g5 = jax.nn.gelu(y)
    ob = g5 * jax.nn.sigmoid(_dot(g5.astype(BF16), wglu[...]) + bglu[...])
    mix_o[...] = jnp.swapaxes(ob.reshape(chunk, nb, S5_W), 0, 1) * _silu(zb_ref[...])


def _s5(u, zb, wts, h0re, h0im, chunk):
    nb, t, _ = u.shape
    tok = pl.BlockSpec((nb, chunk, S5_W), lambda i: (0, i, 0))
    st = _const_spec((nb, S5_S))
    return pl.pallas_call(
        _s5_kernel,
        grid=(t // chunk,),
        in_specs=[tok, tok] + [_const_spec(w.shape) for w in wts] + [st, st],
        out_specs=[tok, st, st],
        out_shape=[jax.ShapeDtypeStruct((nb, t, S5_W), F32), jax.ShapeDtypeStruct((nb, S5_S), F32),
                   jax.ShapeDtypeStruct((nb, S5_S), F32)],
        scratch_shapes=[pltpu.VMEM((nb * chunk, S5_S), F32), pltpu.VMEM((nb * chunk, S5_S), F32),
                        pltpu.VMEM((nb, S5_S), F32), pltpu.VMEM((nb, S5_S), F32)],
        compiler_params=_params(("arbitrary",)),
        name="s5",
    )(u, zb, *wts, h0re, h0im)


def _even_out_kernel(x_ref, a_ref, b_ref, wa, wb, out_ref):
    out_ref[...] = (x_ref[...] + _dot(a_ref[...].astype(BF16), wa[...])
                    + _dot(b_ref[...].astype(BF16), wb[...]))


def _even_out(x2, mixa, mixb, wa, wb, tm):
    m = x2.shape[0]
    row = lambda n: pl.BlockSpec((tm, n), lambda i: (i, 0))
    return pl.pallas_call(
        _even_out_kernel,
        grid=(m // tm,),
        in_specs=[row(D_MODEL), row(MLA_W), row(S5_W), _const_spec(wa.shape), _const_spec(wb.shape)],
        out_specs=row(D_MODEL),
        out_shape=jax.ShapeDtypeStruct((m, D_MODEL), F32),
        compiler_params=_params(("parallel",)),
        name="even_out",
    )(x2, mixa, mixb, wa, wb)


def _odd_in_kernel(feature_major, x_ref, g_ref, wq, wkv4, wkw, wg, wz, q_o, kv4_o, kw_o, gate_o, z_o):
    h = _rms(x_ref[...], g_ref[...]).astype(BF16)
    q_o[...] = _dot(h, wq[...])
    kv4, kw = _dot(h, wkv4[...]), _dot(h, wkw[...])
    kv4_o[...] = kv4.T if feature_major else kv4
    kw_o[...] = kw.T if feature_major else kw
    gate_o[...] = jax.nn.sigmoid(_dot(h, wg[...]))
    z_o[...] = _dot(h, wz[...])


def _odd_in(x2, wts, tm, seq=None):
    m = x2.shape[0]
    row = lambda n: pl.BlockSpec((tm, n), lambda i: (i, 0))
    widths = (NSA_W, 4 * NSA_KVW, 2 * NSA_KVW, 3 * NSA_H, NSA_W)
    out_specs = [row(n) for n in widths]
    out_shape = [jax.ShapeDtypeStruct((m, n), F32) for n in widths]
    if seq is not None:
        for i in (1, 2):
            out_specs[i] = _seq_major_spec(widths[i], tm, seq)
            out_shape[i] = jax.ShapeDtypeStruct((m // seq, widths[i], seq), F32)
    return pl.pallas_call(
        functools.partial(_odd_in_kernel, seq is not None),
        grid=(m // tm,),
        in_specs=[row(D_MODEL)] + [_const_spec(w.shape) for w in wts],
        out_specs=out_specs,
        out_shape=out_shape,
        compiler_params=_params(("parallel",)),
        name="odd_in",
    )(x2, *wts)


def _compress_stage(xs, x_t, pe, base):
    n = x_t.shape[1] // NSA_BLK
    x = x_t.T.reshape(n, NSA_BLK, 2 * NSA_KVW) + pe[None]
    xs[:, pl.ds(pl.multiple_of(base, PACK), n), :] = jnp.swapaxes(x, 0, 1).astype(BF16)


def _compress_finish(xs, w1, phi2, out_ref):
    acc = jnp.zeros((xs.shape[1], 2 * NSA_KVW), F32)
    for r in range(NSA_BLK):
        acc = acc + _dot(xs[r], w1[r])
    out_ref[...] = _dot(_silu(acc).astype(BF16), phi2[...])


def _compress_prompt_kernel(x_ref, pe_ref, w1, phi2, out_ref, xs):
    bi = pl.program_id(0)
    nblk = x_ref.shape[1] // NSA_BLK
    _compress_stage(xs, x_ref[...], pe_ref[...], bi * nblk)

    @pl.when(bi == pl.num_programs(0) - 1)
    def _():
        _compress_finish(xs, w1, phi2, out_ref)


def _compress_prompt(kv4_t, pe, w1, phi2):
    b, _, t = kv4_t.shape
    cw = 2 * NSA_KVW
    nblk = b * (t // NSA_BLK)
    return pl.pallas_call(
        _compress_prompt_kernel,
        grid=(b,),
        in_specs=[pl.BlockSpec((None, cw, t), lambda i: (i, 0, 0)),
                  _const_spec(pe.shape), _const_spec(w1.shape), _const_spec(phi2.shape)],
        out_specs=pl.BlockSpec((nblk, cw), lambda i: (0, 0)),
        out_shape=jax.ShapeDtypeStruct((nblk, cw), F32),
        scratch_shapes=[pltpu.VMEM((NSA_BLK, nblk, cw), BF16)],
        compiler_params=_params(("arbitrary",)),
        name="compress_prompt",
    )(kv4_t, pe, w1, phi2)


def _compress_decode_kernel(n_pages_step, pt_ref, pe_ref, w1, phi2, *rest):
    pages = rest[:n_pages_step]
    out_ref, xs = rest[n_pages_step:]
    gi = pl.program_id(1)
    grp = PACK // BPP
    pe = pe_ref[...]
    for k in range(n_pages_step // grp):
        x_t = jnp.concatenate([pages[k * grp + j][...] for j in range(grp)], axis=1)
        _compress_stage(xs, x_t, pe, (gi * (n_pages_step // grp) + k) * PACK)

    @pl.when(gi == pl.num_programs(1) - 1)
    def _():
        _compress_finish(xs, w1, phi2, out_ref)


def _compress_decode(page_table, cache_t, pe, w1, phi2, n_pages_step=32):
    db, n_pages = page_table.shape
    cw = 2 * NSA_KVW
    nblk = n_pages * BPP
    cst = lambda shape: pl.BlockSpec(shape, lambda b, g, pt, _n=len(shape): (0,) * _n)
    page_specs = [
        pl.BlockSpec((None, cw, PAGE), lambda b, g, pt, j=j: (pt[b, g * n_pages_step + j], 0, 0))
        for j in range(n_pages_step)]
    grid_spec = pltpu.PrefetchScalarGridSpec(
        num_scalar_prefetch=1,
        grid=(db, n_pages // n_pages_step),
        in_specs=[cst(pe.shape), cst(w1.shape), cst(phi2.shape)] + page_specs,
        out_specs=pl.BlockSpec((None, nblk, cw), lambda b, g, pt: (b, 0, 0)),
        scratch_shapes=[pltpu.VMEM((NSA_BLK, nblk, cw), BF16)])
    return pl.pallas_call(
        functools.partial(_compress_decode_kernel, n_pages_step),
        grid_spec=grid_spec,
        out_shape=jax.ShapeDtypeStruct((db, nblk, cw), F32),
        compiler_params=_params(("parallel", "arbitrary")),
        name="compress_decode",
    )(page_table, pe, w1, phi2, *([cache_t] * n_pages_step))


def _alibi_slope(h):
    return 2.0 ** (-8.0 * (h + 1) / NSA_H)


def _split_bf16(x):
    hi = x.astype(BF16)
    return hi, (x - hi.astype(F32)).astype(BF16)


def _cmp_kernel(q_ref, cb_ref, pos_ref, oc_o, imp_o, s_scr):
    tt = q_ref.shape[0]
    nc = cb_ref.shape[0]
    qpos = pos_ref[...]
    cpos = lax.broadcasted_iota(I32, (nc, 1), 0) * NSA_BLK + (NSA_BLK - 1)
    visible = cpos <= qpos
    dist = (qpos - cpos).astype(F32)
    q = q_ref[...] * NSA_SCALE
    cb = cb_ref[...]
    for g in range(NSA_G):
        k_hi, k_lo = _split_bf16(cb[:, g * NSA_D:(g + 1) * NSA_D])
        vc = cb[:, NSA_KVW + g * NSA_D:NSA_KVW + (g + 1) * NSA_D].astype(BF16)
        for i in range(NSA_R):
            h = g * NSA_R + i
            q_hi, q_lo = _split_bf16(q[:, h * NSA_D:(h + 1) * NSA_D])
            s = _dot_nt(k_hi, q_hi) + _dot_nt(k_hi, q_lo) + _dot_nt(k_lo, q_hi)
            s_scr[i * nc:(i + 1) * nc, :] = s - _alibi_slope(h) * dist
        s3 = jnp.where(visible[None], s_scr[...].reshape(NSA_R, nc, tt), NEG)
        mx = jnp.max(s3, axis=1, keepdims=True)
        e = jnp.where(visible[None], jnp.exp(s3 - mx), 0.0)
        den = jnp.sum(e, axis=1, keepdims=True)
        p = e / jnp.where(den > 0, den, 1.0)
        imp_o[g] = jnp.sum(p, axis=0)
        for i in range(NSA_R):
            h = g * NSA_R + i
            oc_o[h * NSA_D:(h + 1) * NSA_D, :] = _dot_tn(vc, p[i].astype(BF16))


def _cmp(q, cb, pos, tt):
    b, t, _ = q.shape
    nc = cb.shape[1]
    return pl.pallas_call(
        _cmp_kernel,
        grid=(b, t // tt),
        in_specs=[pl.BlockSpec((None, tt, NSA_W), lambda bi, ti: (bi, ti, 0)),
                  pl.BlockSpec((None, nc, 2 * NSA_KVW), lambda bi, ti: (bi, 0, 0)),
                  pl.BlockSpec((1, tt), lambda bi, ti: (0, ti))],
        out_specs=[pl.BlockSpec((None, NSA_W, tt), lambda bi, ti: (bi, 0, ti)),
                   pl.BlockSpec((None, NSA_G, nc, tt), lambda bi, ti: (bi, 0, 0, ti))],
        out_shape=[jax.ShapeDtypeStruct((b, NSA_W, t), F32),
                   jax.ShapeDtypeStruct((b, NSA_G, nc, t), F32)],
        scratch_shapes=[pltpu.VMEM((NSA_R * nc, tt), F32)],
        compiler_params=_params(("parallel", "parallel")),
        name="nsa_cmp",
    )(q, cb, pos)


def _topk_kernel(n_sel, imp_ref, pos_ref, sel_o, idx_o):
    _, nc, tt = imp_ref.shape
    nsp = sel_o.shape[1]
    qpos = pos_ref[...]
    blk = lax.broadcasted_iota(I32, (nsp, 1), 0)
    cur = jnp.right_shift(qpos, NSA_BLK.bit_length() - 1)
    forced = (blk == 0) | (blk == cur) | (blk == cur - 1)
    allowed = (blk <= cur) & (blk < n_sel)
    for g in range(NSA_G):
        imp = imp_ref[g]
        if nsp > nc:
            imp = jnp.concatenate([imp, jnp.zeros((nsp - nc, tt), F32)], axis=0)
        score = jnp.where(allowed, imp + jnp.where(forced, FORCED_BONUS, 0.0), NEG)
        chosen = jnp.zeros((nsp, tt), F32)
        picks = []
        for _ in range(NSA_TOPK):
            mx = jnp.max(score, axis=0, keepdims=True)
            first = jnp.min(jnp.where(score == mx, blk, nsp), axis=0, keepdims=True)
            hit = blk == first
            chosen = jnp.where(hit, 1.0, chosen)
            score = jnp.where(hit, GONE, score)
            picks.append(first)
        sel_o[g] = chosen
        idx_o[g] = jnp.concatenate(picks, axis=0)


def _topk(imp, pos, n_sel, tt):
    b, _, nc, t = imp.shape
    nsp = -(-(n_sel + TK // NSA_BLK) // SUB) * SUB
    return pl.pallas_call(
        functools.partial(_topk_kernel, n_sel),
        grid=(b, t // tt),
        in_specs=[pl.BlockSpec((None, NSA_G, nc, tt), lambda bi, ti: (bi, 0, 0, ti)),
                  pl.BlockSpec((1, tt), lambda bi, ti: (0, ti))],
        out_specs=[pl.BlockSpec((None, NSA_G, nsp, tt), lambda bi, ti: (bi, 0, 0, ti)),
                   pl.BlockSpec((None, NSA_G, NSA_TOPK, tt), lambda bi, ti: (bi, 0, 0, ti))],
        out_shape=[jax.ShapeDtypeStruct((b, NSA_G, nsp, t), F32),
                   jax.ShapeDtypeStruct((b, NSA_G, NSA_TOPK, t), I32)],
        compiler_params=_params(("parallel", "parallel")),
        name="nsa_topk",
    )(imp, pos)


def _bf16_parts(x, n):
    parts = []
    for _ in range(n):
        bits = np.asarray(x, np.float32).view(np.uint32)
        top = ((bits + np.uint32(0x7FFF) + ((bits >> np.uint32(16)) & np.uint32(1)))
               & np.uint32(0xFFFF0000)).view(np.float32)
        parts.append(float(top))
        x = float(np.float32(x) - top)
    return parts


def _nsa_prompt_kernel(q_ref, kv4_ref, kw_ref, oc_ref, sel_ref, gate_ref, out_ref,
                       k_aug, q_aug, s_buf, m_s, acc_s):
    qi = pl.program_id(1)
    r = NSA_R * TQ
    bpt = TK // NSA_BLK
    n_parts = SUB // 2
    qt = (q_ref[...] * NSA_SCALE).T.astype(BF16)
    gt = gate_ref[...].T
    lane = lax.broadcasted_iota(I32, (1, r), 1)
    qpos = qi * TQ + (lane & (TQ - 1))
    head = lane // TQ
    krow = lax.broadcasted_iota(I32, (TK, 1), 0)
    klane = lax.broadcasted_iota(I32, (1, TK), 1)
    row8 = lax.broadcasted_iota(I32, (SUB, 1), 0)
    n_tiles = (qi * TQ + TQ + TK - 1) // TK
    onehot = jnp.where(row8 == jnp.right_shift(klane, NSA_BLK.bit_length() - 1), 1.0, 0.0)

    def key_rows(j):
        rel = j * TK - qi * TQ + klane
        coarse = (rel & -NSA_BLK).astype(F32)
        fine = (rel & (NSA_BLK - 1)).astype(F32)
        alibi = jnp.where((row8 & 1) == 0, coarse, fine)
        return jnp.concatenate([alibi, onehot], axis=0).astype(BF16)

    def causal(s, j):
        return jnp.where(j * TK + krow <= qpos, s, NEG)

    def in_window(s, j):
        return jnp.where(qpos - (j * TK + krow) <= NSA_WIN, s, NEG)

    def tile(ref, lo, j):
        return ref[lo:lo + NSA_D, pl.ds(pl.multiple_of(j * TK, TK), TK)].astype(BF16)

    slopes, sel_chains, win_chains = [], [], []
    for g in range(NSA_G):
        kg, qa = k_aug.at[g], q_aug.at[g]
        qa[:NSA_D, :] = jnp.concatenate(
            [qt[(g * NSA_R + i) * NSA_D:(g * NSA_R + i + 1) * NSA_D, :] for i in range(NSA_R)], axis=1)
        slope8 = jnp.zeros((SUB, r), F32)
        for i in range(NSA_R):
            parts = _bf16_parts(_alibi_slope(g * NSA_R + i), n_parts)
            col = jnp.zeros((SUB, 1), F32)
            for k, part in enumerate(parts):
                col = jnp.where(jnp.right_shift(row8, 1) == k, part, col)
            slope8 = jnp.where(head == i, col, slope8)
        slopes.append(slope8)
        ks_lo, vs_lo = 2 * NSA_KVW + g * NSA_D, 3 * NSA_KVW + g * NSA_D
        kw_lo, vw_lo = g * NSA_D, NSA_KVW + g * NSA_D

        def qk_sel(j, g=g, kg=kg, qa=qa, ks_lo=ks_lo, slope8=slope8):
            kg[:NSA_D, :] = tile(kv4_ref, ks_lo, j)
            kg[NSA_D:, :] = key_rows(j)
            flags = sel_ref[g, pl.ds(j * bpt, SUB), :]
            off = jnp.where(flags > 0.5, 0.0, NEG)
            qa[NSA_D:, :] = jnp.concatenate(
                [slope8, jnp.concatenate([off] * NSA_R, axis=1)], axis=0).astype(BF16)
            return _dot_tn(kg[...], qa[...])

        def qk_win(j, kg=kg, qa=qa, kw_lo=kw_lo):
            kg[:NSA_D, :] = tile(kw_ref, kw_lo, j)
            kg[NSA_D:, :] = key_rows(j)
            return _dot_tn(kg[...], qa[...])

        state = (s_buf.at[g], m_s.at[g], acc_s.at[g], None)
        sel_chains.append((qk_sel, lambda j, p, lo=vs_lo: _dot(_with_ones(tile(kv4_ref, lo, j)), p),
                           lambda s, j: s, causal) + state)
        win_chains.append((qk_win, lambda j, p, lo=vw_lo: _dot(_with_ones(tile(kw_ref, lo, j)), p),
                           in_window, lambda s, j: causal(in_window(s, j), j)) + state)

    o_sels = _flash_tiles(0, n_tiles, sel_chains, LOG2E)
    for g in range(NSA_G):
        q_aug[g, NSA_D:, :] = jnp.concatenate([slopes[g], jnp.zeros((SUB, r), F32)], axis=0).astype(BF16)
    o_wins = _flash_tiles(jnp.maximum(qi * TQ - NSA_WIN, 0) // TK, n_tiles, win_chains, LOG2E)
    for g in range(NSA_G):
        o_sel, o_win = o_sels[g], o_wins[g]
        for i in range(NSA_R):
            h = g * NSA_R + i
            rows = slice(h * NSA_D, (h + 1) * NSA_D)
            cols = slice(i * TQ, (i + 1) * TQ)
            out_ref[rows, :] = (gt[h:h + 1, :] * oc_ref[rows, :]
                                + gt[NSA_H + h:NSA_H + h + 1, :] * o_sel[:, cols]
                                + gt[2 * NSA_H + h:2 * NSA_H + h + 1, :] * o_win[:, cols])


def _nsa_prompt(q, kv4_t, kw_t, oc_t, sel, gate):
    b, t, _ = q.shape
    nsp = sel.shape[2]
    r = NSA_R * TQ
    tok = lambda n: pl.BlockSpec((None, TQ, n), lambda bi, qi: (bi, qi, 0))
    full = lambda n: pl.BlockSpec((None, n, t), lambda bi, qi: (bi, 0, 0))
    return pl.pallas_call(
        _nsa_prompt_kernel,
        grid=(b, t // TQ),
        in_specs=[tok(NSA_W), full(4 * NSA_KVW), full(2 * NSA_KVW),
                  pl.BlockSpec((None, NSA_W, TQ), lambda bi, qi: (bi, 0, qi)),
                  pl.BlockSpec((None, NSA_G, nsp, TQ), lambda bi, qi: (bi, 0, 0, qi)),
                  tok(3 * NSA_H)],
        out_specs=pl.BlockSpec((None, NSA_W, TQ), lambda bi, qi: (bi, 0, qi)),
        out_shape=jax.ShapeDtypeStruct((b, NSA_W, t), F32),
        scratch_shapes=[pltpu.VMEM((NSA_G, NSA_D + 2 * SUB, TK), BF16),
                        pltpu.VMEM((NSA_G, NSA_D + 2 * SUB, r), BF16),
                        pltpu.VMEM((NSA_G, 2, TK, r), F32),
                        pltpu.VMEM((NSA_G, 1, r), F32),
                        pltpu.VMEM((NSA_G, NSA_D + PACK, r), F32)],
        compiler_params=_params(("parallel", "arbitrary")),
        name="nsa_prompt",
    )(q, kv4_t, kw_t, oc_t, sel, gate)


def _softmax_rows(parts):
    mx = None
    for s, _, _ in parts:
        pm = jnp.max(s, axis=1, keepdims=True)
        mx = pm if mx is None else jnp.maximum(mx, pm)
    den, num = None, None
    for s, v, feature_major in parts:
        p = jnp.exp(s - mx)
        d = jnp.sum(p, axis=1, keepdims=True)
        n = _dot_nt(p.astype(BF16), v) if feature_major else _dot(p.astype(BF16), v)
        den = d if den is None else den + d
        num = n if num is None else num + n
    return num / den


def _nsa_sel_decode_kernel(ts, pos0, nbp, idx_ref, pid_ref, q_ref, new_ref, *rest):
    nblk = NSA_G * NSA_TOPK
    pages = rest[:nblk]
    out_ref = rest[nblk]
    b, t = pl.program_id(0), pl.program_id(1)
    q = q_ref[...] * NSA_SCALE
    new = new_ref[...]
    lane = lax.broadcasted_iota(I32, (1, PAGE), 1)
    jn = lax.broadcasted_iota(I32, (1, SUB), 1)
    for g in range(NSA_G):
        qg = q[g * NSA_R:(g + 1) * NSA_R, :].astype(BF16)
        slope = jnp.concatenate(
            [jnp.full((1, 1), _alibi_slope(g * NSA_R + i), F32) for i in range(NSA_R)], axis=0)
        parts = []
        has_new = jnp.zeros((), I32)
        for k in range(NSA_TOPK):
            bid = idx_ref[((b * NSA_G + g) * ts + t) * NSA_TOPK + k]
            pg = pages[g * NSA_TOPK + k][...]
            kk = pg[g * NSA_D:(g + 1) * NSA_D, :].astype(BF16)
            vv = pg[NSA_KVW + g * NSA_D:NSA_KVW + (g + 1) * NSA_D, :].astype(BF16)
            sub = bid % BPP
            rel = ((bid - sub) * NSA_BLK - pos0 + lane).astype(F32)
            s = _dot(qg, kk) + slope * rel
            mine = (lane // NSA_BLK == sub) & (bid < nbp)
            parts.append((jnp.where(mine, s, NEG), vv, True))
            has_new = has_new + (bid >= nbp).astype(I32)
        kn = new[:, 2 * NSA_KVW + g * NSA_D:2 * NSA_KVW + (g + 1) * NSA_D].astype(BF16)
        vn = new[:, 3 * NSA_KVW + g * NSA_D:3 * NSA_KVW + (g + 1) * NSA_D].astype(BF16)
        sn = _dot_nt(qg, kn) + slope * jn.astype(F32)
        ok = (jn <= t) & (jn < ts) & (has_new > 0)
        parts.append((jnp.where(ok, sn, NEG), vn, False))
        out_ref[g * NSA_R:(g + 1) * NSA_R, :] = _softmax_rows(parts)


def _nsa_sel_decode(idx, page_table, cache_t, q4, kv4_new, ts, pos0):
    db, n_pages = page_table.shape
    nbp = n_pages * BPP
    cw = 2 * NSA_KVW
    page_ids = jnp.take_along_axis(page_table, jnp.minimum(idx, nbp - 1) // BPP, axis=1).reshape(-1)
    idx_flat = idx.reshape(-1)

    def page_map(b, t, idx_r, pid, g, k):
        return (pid[((b * NSA_G + g) * ts + t) * NSA_TOPK + k], 1, 0)

    page_specs = [pl.BlockSpec((None, cw, PAGE), functools.partial(page_map, g=g, k=k))
                  for g in range(NSA_G) for k in range(NSA_TOPK)]
    grid_spec = pltpu.PrefetchScalarGridSpec(
        num_scalar_prefetch=2,
        grid=(db, ts),
        in_specs=[pl.BlockSpec((None, None, NSA_H, NSA_D), lambda b, t, idx, pt: (b, t, 0, 0)),
                  pl.BlockSpec((None, SUB, 4 * NSA_KVW), lambda b, t, idx, pt: (b, 0, 0))] + page_specs,
        out_specs=pl.BlockSpec((None, None, NSA_H, NSA_D), lambda b, t, idx, pt: (b, t, 0, 0)))
    return pl.pallas_call(
        functools.partial(_nsa_sel_decode_kernel, ts, pos0, nbp),
        grid_spec=grid_spec,
        out_shape=jax.ShapeDtypeStruct((db, ts, NSA_H, NSA_D), F32),
        compiler_params=_params(("parallel", "arbitrary")),
        name="nsa_sel_decode",
    )(idx_flat, page_ids, q4, kv4_new, *([cache_t] * (NSA_G * NSA_TOPK)))


def _nsa_win_decode_kernel(ts, q_ref, win_ref, new_ref, out_ref):
    rows = NSA_R * SUB
    win = win_ref[...]
    new = new_ref[...]
    tq = lax.broadcasted_iota(I32, (rows, 1), 0) & (SUB - 1)
    iw = lax.broadcasted_iota(I32, (1, NSA_WIN), 1)
    jn = lax.broadcasted_iota(I32, (1, SUB), 1)
    for g in range(NSA_G):
        qg = jnp.concatenate([q_ref[:, g * NSA_R + i, :] for i in range(NSA_R)], axis=0)
        qg = (qg * NSA_SCALE).astype(BF16)
        slope = jnp.concatenate(
            [jnp.full((SUB, 1), _alibi_slope(g * NSA_R + i), F32) for i in range(NSA_R)], axis=0)
        kw = win[g * NSA_D:(g + 1) * NSA_D, :].astype(BF16)
        vw = win[NSA_KVW + g * NSA_D:NSA_KVW + (g + 1) * NSA_D, :].astype(BF16)
        kn = new[:, g * NSA_D:(g + 1) * NSA_D].astype(BF16)
        vn = new[:, NSA_KVW + g * NSA_D:NSA_KVW + (g + 1) * NSA_D].astype(BF16)
        sw = _dot(qg, kw) + slope * (iw - NSA_WIN).astype(F32)
        sw = jnp.where(iw >= tq, sw, NEG)
        sn = _dot_nt(qg, kn) + slope * jn.astype(F32)
        sn = jnp.where((jn <= tq) & (jn < ts), sn, NEG)
        o = _softmax_rows([(sw, vw, True), (sn, vn, False)])
        for i in range(NSA_R):
            out_ref[:, g * NSA_R + i, :] = o[i * SUB:(i + 1) * SUB, :]


def _nsa_win_decode(q4, win_t, kw_new, ts):
    db = q4.shape[0]
    return pl.pallas_call(
        functools.partial(_nsa_win_decode_kernel, ts),
        grid=(db,),
        in_specs=[pl.BlockSpec((None, SUB, NSA_H, NSA_D), lambda b: (b, 0, 0, 0)),
                  pl.BlockSpec((None, 2 * NSA_KVW, NSA_WIN), lambda b: (b, 0, 0)),
                  pl.BlockSpec((None, SUB, 2 * NSA_KVW), lambda b: (b, 0, 0))],
        out_specs=pl.BlockSpec((None, SUB, NSA_H, NSA_D), lambda b: (b, 0, 0, 0)),
        out_shape=jax.ShapeDtypeStruct((db, SUB, NSA_H, NSA_D), F32),
        compiler_params=_params(("parallel",)),
        name="nsa_win_decode",
    )(q4, win_t, kw_new)


def _odd_tail(o, z_ref, x_ref, wout, gfin):
    y = x_ref[...] + _dot((o * _silu(z_ref[...])).astype(BF16), wout[...])
    return _rms(y, gfin[...])


def _odd_out_prompt_kernel(ot_ref, z_ref, x_ref, wout, gfin, out_ref):
    out_ref[...] = _odd_tail(ot_ref[...].T, z_ref, x_ref, wout, gfin)


def _odd_out_prompt(ot, z, x, wout, gfin, tm):
    b, _, t = ot.shape
    tok = pl.BlockSpec((None, tm, D_MODEL), lambda bi, ti: (bi, ti, 0))
    return pl.pallas_call(
        _odd_out_prompt_kernel,
        grid=(b, t // tm),
        in_specs=[pl.BlockSpec((None, NSA_W, tm), lambda bi, ti: (bi, 0, ti)), tok, tok,
                  _const_spec(wout.shape), _const_spec(gfin.shape)],
        out_specs=tok,
        out_shape=jax.ShapeDtypeStruct((b, t, D_MODEL), F32),
        compiler_params=_params(("parallel", "parallel")),
        name="odd_out_prompt",
    )(ot, z, x, wout, gfin)


def _odd_out_decode_kernel(oc_ref, os_ref, ow_ref, gate_ref, z_ref, x_ref, wout, gfin, out_ref):
    gate = gate_ref[...]
    oc, osel, ow = oc_ref[...], os_ref[...], ow_ref[...]
    heads = []
    for h in range(NSA_H):
        c = slice(h * NSA_D, (h + 1) * NSA_D)
        heads.append(gate[:, h:h + 1] * oc[:, c] + gate[:, NSA_H + h:NSA_H + h + 1] * osel[:, c]
                     + gate[:, 2 * NSA_H + h:2 * NSA_H + h + 1] * ow[:, c])
    out_ref[...] = _odd_tail(jnp.concatenate(heads, axis=1), z_ref, x_ref, wout, gfin)


def _odd_out_decode(oc, osel, ow, gate, z, x, wout, gfin):
    m = x.shape[0]
    args = (oc, osel, ow, gate, z, x, wout, gfin)
    return pl.pallas_call(
        _odd_out_decode_kernel,
        grid=(1,),
        in_specs=[_const_spec(a.shape) for a in args],
        out_specs=_const_spec((m, D_MODEL)),
        out_shape=jax.ShapeDtypeStruct((m, D_MODEL), F32),
        compiler_params=_params(("arbitrary",)),
        name="odd_out_decode",
    )(*args)


def _rope_tables(pos):
    half = MLA_ROPE // 2
    inv = ROPE_THETA ** (-jnp.arange(half, dtype=F32) / half)
    ang = pos.astype(F32)[:, None] * inv[None, :]
    cos, sin = jnp.cos(ang), jnp.sin(ang)
    return jnp.concatenate([cos, cos], axis=1), jnp.concatenate([-sin, sin], axis=1)


def _block_diag(x):
    t, g, r, c = x.shape
    eye = jnp.eye(g, dtype=x.dtype)
    return jnp.einsum("tgrc,gh->tgrhc", x, eye).reshape(t, g * r, g * c)


def _even_weights(norm_g, w_in, g_q, g_kv, w_uq, w_uk, w_uv):
    edges = [0, MLA_QL, MLA_QL + MLA_KVL, MLA_QL + MLA_ROW]
    edges += [edges[-1] + MLA_W, edges[-1] + MLA_W + S5_W, edges[-1] + MLA_W + 2 * S5_W]
    wb = w_in.astype(BF16)
    pieces = [wb[:, edges[i]:edges[i + 1]] for i in range(6)]
    in_w = (norm_g[None, :], *pieces, g_q[None, :], g_kv[None, :])
    uq = jnp.transpose(w_uq, (1, 0, 2)).astype(BF16)
    mla_w = (uq[:, :, :MLA_NOPE], uq[:, :, MLA_NOPE:],
             jnp.transpose(w_uk, (1, 2, 0)).astype(BF16),
             jnp.transpose(w_uv, (1, 0, 2)).astype(BF16))
    uq_t = jnp.transpose(w_uq, (1, 2, 0)).astype(BF16)
    mla_wt = (uq_t[:, :MLA_NOPE].reshape(MLA_H * MLA_NOPE, MLA_QL),
              uq_t[:, MLA_NOPE:].reshape(MLA_H * MLA_ROPE, MLA_QL),
              jnp.transpose(w_uk, (1, 0, 2)).astype(BF16),
              jnp.transpose(w_uv, (1, 2, 0)).astype(BF16))
    return in_w, mla_w, mla_wt


def _s5_weights(lam_re, lam_im, log_dt, b_re, b_im, c_re, c_im, d_skip, w_glu, b_glu):
    def bmat(b):
        return _block_diag(jnp.transpose(b.reshape(S5_NT, S5_GT, S5_N, S5_P), (0, 1, 3, 2)))

    def cmat(c):
        return _block_diag(jnp.transpose(c.reshape(S5_NT, S5_GT, S5_P, S5_N), (0, 1, 3, 2)))

    return (lam_re.reshape(1, S5_S), lam_im.reshape(1, S5_S),
            jnp.repeat(log_dt, S5_N).reshape(1, S5_S),
            bmat(b_re), bmat(b_im), cmat(c_re), cmat(c_im),
            d_skip.reshape(1, S5_W), w_glu.astype(BF16), b_glu[None, :])


def _odd_weights(norm_g, w_in, pe_k, pe_v, phi1_k, phi2_k, phi1_v, phi2_v):
    wb = w_in.astype(BF16)
    e0 = NSA_W
    e1 = e0 + 4 * NSA_KVW
    e2 = e1 + 2 * NSA_KVW
    e3 = e2 + 3 * NSA_H
    in_w = (norm_g[None, :], wb[:, :e0], wb[:, e0:e1], wb[:, e1:e2], wb[:, e2:e3], wb[:, e3:])
    pe = jnp.concatenate([pe_k, pe_k, pe_v, pe_v], axis=1)
    p1k = phi1_k.reshape(NSA_BLK, NSA_D, NSA_D)
    p1v = phi1_v.reshape(NSA_BLK, NSA_D, NSA_D)
    w1 = _block_diag(jnp.stack([p1k, p1k, p1v, p1v], axis=1)).astype(BF16)
    phi2 = _block_diag(jnp.stack([phi2_k, phi2_k, phi2_v, phi2_v], axis=0)[None])[0].astype(BF16)
    return in_w, (pe, w1, phi2)


def _pad_tokens(x, n):
    return jnp.pad(x, ((0, 0), (0, n - x.shape[1])) + ((0, 0),) * (x.ndim - 2))


def _rows_last(x):
    nd = x.ndim
    xt = jnp.transpose(x, (0,) + tuple(range(2, nd)) + (1,))
    return xt.reshape(x.shape[0], -1, x.shape[1])


def _rows_second(x_t, feature_shape):
    b, _, rows = x_t.shape
    nf = len(feature_shape)
    xt = x_t.reshape((b,) + tuple(feature_shape) + (rows,))
    return jnp.transpose(xt, (0, nf + 1) + tuple(range(1, nf + 1)))


def kernel(x_prompt, x_sample, cache_mla, state_s5, cache_nsa_kv, state_nsa_win, page_table, norm_even, w_in_even, mla_g_q, mla_g_kv, mla_w_uq, mla_w_uk, mla_w_uv, s5_lambda_re, s5_lambda_im, s5_log_dt, s5_b_re, s5_b_im, s5_c_re, s5_c_im, s5_d, s5_w_glu, s5_b_glu, w_out_even, norm_odd, w_in_odd, nsa_pe_k, nsa_pe_v, nsa_phi1_k, nsa_phi2_k, nsa_phi1_v, nsa_phi2_v, w_out_odd, norm_final):
    b, t, _ = x_prompt.shape
    db, ts, _ = x_sample.shape
    n_pages = page_table.shape[1]
    past = n_pages * PAGE
    mp, ms = b * t, db * ts
    tm = min(512, t)
    assert t % (PACK * NSA_BLK) == 0 and t % TK == 0 and ts <= SUB
    assert t <= NSA_BLK * 256

    pos_p = jnp.arange(t, dtype=I32)
    pos_s = past + jnp.arange(SUB, dtype=I32)
    cos_p, sin_p = _rope_tables(pos_p)
    cos_s, sin_s = _rope_tables(pos_s)
    cos_st, sin_st = jnp.tile(cos_s[:ts], (db, 1)), jnp.tile(sin_s[:ts], (db, 1))

    even_in_w, mla_w, mla_wt = _even_weights(norm_even[0], w_in_even[0], mla_g_q[0], mla_g_kv[0],
                                             mla_w_uq[0], mla_w_uk[0], mla_w_uv[0])
    s5_w = _s5_weights(s5_lambda_re[0], s5_lambda_im[0], s5_log_dt[0], s5_b_re[0], s5_b_im[0],
                       s5_c_re[0], s5_c_im[0], s5_d[0], s5_w_glu[0], s5_b_glu[0])
    wo_e = w_out_even[0].astype(BF16)
    wo_a, wo_b = wo_e[:MLA_W], wo_e[MLA_W:]

    xp2 = x_prompt.reshape(mp, D_MODEL)
    cq, rows_pt, za, u, zb = _even_in(xp2, even_in_w, cos_p, sin_p, tm, seq=t)
    mix_a = _mla_prompt(cq.reshape(b, t, MLA_QL), rows_pt, za.reshape(b, t, MLA_W), mla_wt, cos_p.T, sin_p.T)
    zeros_p = jnp.zeros((b, S5_S), F32)
    mix_b, sre_p, sim_p = _s5(u.reshape(b, t, S5_W), zb.reshape(b, t, S5_W), s5_w, zeros_p, zeros_p,
                              min(128, t))
    xp1 = _even_out(xp2, mix_a.reshape(mp, MLA_W), mix_b.reshape(mp, S5_W), wo_a, wo_b, tm)

    xs2 = x_sample.reshape(ms, D_MODEL)
    cq_s, rows_s, za_s, u_s, zb_s = _even_in(xs2, even_in_w, cos_st, sin_st, ms)
    pad3 = lambda a, n: _pad_tokens(a.reshape(db, ts, n), SUB)
    mix_a_s = _mla_decode(page_table, _rows_last(cache_mla[0]), pad3(cq_s, MLA_QL), pad3(rows_s, MLA_ROW),
                          pad3(za_s, MLA_W), mla_w, cos_s, sin_s, ts)[:, :ts]
    st = state_s5[0]
    mix_b_s, sre_s, sim_s = _s5(u_s.reshape(db, ts, S5_W), zb_s.reshape(db, ts, S5_W), s5_w,
                                st[..., 0].reshape(db, S5_S), st[..., 1].reshape(db, S5_S), ts)
    xs1 = _even_out(xs2, mix_a_s.reshape(ms, MLA_W), mix_b_s.reshape(ms, S5_W), wo_a, wo_b, ms)

    odd_in_w, cmp_w = _odd_weights(norm_odd[0], w_in_odd[0], nsa_pe_k[0], nsa_pe_v[0], nsa_phi1_k[0],
                                   nsa_phi2_k[0], nsa_phi1_v[0], nsa_phi2_v[0])
    wo_o = w_out_odd[0].astype(BF16)
    gfin = norm_final[None, :]

    q, kv4_t, kw_t, gate, z = _odd_in(xp1, odd_in_w, tm, seq=t)
    nblk_p = t // NSA_BLK
    cb = _compress_prompt(kv4_t, *cmp_w)
    q3 = q.reshape(b, t, NSA_W)
    tt = min(256, t)
    oc_t, imp = _cmp(q3, cb.reshape(b, nblk_p, 2 * NSA_KVW), pos_p[None], tt)
    sel, _ = _topk(imp, pos_p[None], nblk_p, tt)
    o_t = _nsa_prompt(q3, kv4_t, kw_t, oc_t, sel, gate.reshape(b, t, 3 * NSA_H))
    y_prompt = _odd_out_prompt(o_t, z.reshape(b, t, NSA_W), xp1.reshape(b, t, D_MODEL), wo_o, gfin, tt)

    q_s, kv4_s, kw_s, gate_s, z_s = _odd_in(xs1, odd_in_w, ms)
    cache_nsa_t = _rows_last(cache_nsa_kv[0])
    cb_s = _compress_decode(page_table, cache_nsa_t, *cmp_w)
    n_sel_s = -(-(past + ts) // NSA_BLK)
    q_s3 = q_s.reshape(db, ts, NSA_W)
    oc_ts, imp_s = _cmp(_pad_tokens(q_s3, SUB), cb_s, pos_s[None], SUB)
    imp_l = jnp.transpose(imp_s[..., :ts], (1, 2, 0, 3)).reshape(1, NSA_G, cb_s.shape[1], ms)
    pos_l = jnp.tile(pos_s[:ts], db)[None]
    _, idx_l = _topk(imp_l, pos_l, n_sel_s, ms)
    idx_s = jnp.transpose(idx_l.reshape(NSA_G, NSA_TOPK, db, ts), (2, 0, 3, 1)).reshape(db, -1)
    q_s4 = q_s.reshape(db, ts, NSA_H, NSA_D)
    o_sel_s = _nsa_sel_decode(idx_s, page_table, cache_nsa_t, q_s4, pad3(kv4_s, 4 * NSA_KVW), ts, past)
    win_t = _rows_last(state_nsa_win[0])
    kw_s3 = kw_s.reshape(db, ts, 2 * NSA_KVW)
    o_win_s = _nsa_win_decode(_pad_tokens(q_s4, SUB), win_t, _pad_tokens(kw_s3, SUB), ts)
    oc_s = jnp.transpose(oc_ts, (0, 2, 1))[:, :ts].reshape(ms, NSA_W)
    y_sample = _odd_out_decode(oc_s, o_sel_s.reshape(ms, NSA_W), o_win_s[:, :ts].reshape(ms, NSA_W),
                               gate_s, z_s, xs1, wo_o, gfin)

    state = lambda re, im, n: jnp.stack([re, im], axis=-1).reshape(1, n, S5_G, S5_N, 2)
    win_shape = (2, NSA_G, NSA_D)
    if t >= NSA_WIN:
        win_pt = kw_t[:, :, t - NSA_WIN:]
    else:
        win_pt = jnp.pad(kw_t, ((0, 0), (0, 0), (NSA_WIN - t, 0)))
    win_st = jnp.concatenate([win_t[:, :, ts:], jnp.transpose(kw_s3, (0, 2, 1))], axis=2)
    return (y_prompt, y_sample.reshape(db, ts, D_MODEL),
            _rows_second(rows_pt, (MLA_ROW,))[None], rows_s.reshape(1, db, ts, MLA_ROW),
            state(sre_p, sim_p, b), state(sre_s, sim_s, db),
            _rows_second(kv4_t, (4, NSA_G, NSA_D))[None], kv4_s.reshape(1, db, ts, 4, NSA_G, NSA_D),
            _rows_second(win_pt, win_shape)[None], _rows_second(win_st, win_shape)[None])
```

```python
import functools

import jax
import jax.numpy as jnp
import numpy as np
from jax import lax
from jax.experimental import pallas as pl
from jax.experimental.pallas import tpu as pltpu

F32, BF16, I32 = jnp.float32, jnp.bfloat16, jnp.int32

D_MODEL = 1024
PAGE = 128
EPS = 1e-6
ROPE_THETA = 10000.0
MLA_H, MLA_NOPE, MLA_ROPE, MLA_V = 8, 64, 32, 64
MLA_QL, MLA_KVL = 384, 256
MLA_ROW = MLA_KVL + MLA_ROPE
MLA_W = MLA_H * MLA_V
MLA_SCALE = (MLA_NOPE + MLA_ROPE) ** -0.5
S5_G, S5_P, S5_N = 32, 16, 64
S5_W = S5_G * S5_P
S5_S = S5_G * S5_N
S5_GT = 4
S5_NT = S5_G // S5_GT
NSA_H, NSA_G, NSA_D = 16, 2, 64
NSA_R = NSA_H // NSA_G
NSA_W = NSA_H * NSA_D
NSA_KVW = NSA_G * NSA_D
NSA_BLK, NSA_TOPK, NSA_WIN = 64, 16, 512
NSA_SCALE = NSA_D ** -0.5
FORCED_BONUS = float(NSA_R + 1)
BPP = PAGE // NSA_BLK

LOG2E = 1.4426950408889634
NEG = -1e30
GONE = -3e38
TQ = 128
TK = 256
SUB = 8
PACK = 16
MLA_DECODE_GROUP = 64
VMEM_LIMIT = 56 * 1024 * 1024


def _dot(a, b):
    return jnp.dot(a, b, preferred_element_type=F32)


def _dot_nt(a, b):
    return lax.dot_general(a, b, (((1,), (1,)), ((), ())), preferred_element_type=F32)


def _dot_tn(a, b):
    return lax.dot_general(a, b, (((0,), (0,)), ((), ())), preferred_element_type=F32)


def _rms(x, g):
    return x * lax.rsqrt(jnp.mean(x * x, axis=-1, keepdims=True) + EPS) * g


def _silu(x):
    return x * jax.nn.sigmoid(x)


def _rope_nat(x, cosf, sinf):
    half = x.shape[1] // 2
    xs = jnp.concatenate([x[:, half:], x[:, :half]], axis=1)
    return x * cosf + xs * sinf


def _params(sem):
    return pltpu.CompilerParams(dimension_semantics=sem, vmem_limit_bytes=VMEM_LIMIT)


def _const_spec(shape):
    n = len(shape)
    return pl.BlockSpec(shape, lambda *a, _n=n: (0,) * _n)


def _seq_major_spec(width, tm, seq):
    per = seq // tm
    return pl.BlockSpec((None, width, tm), lambda i: (i // per, 0, i % per))


def _even_in_kernel(feature_major, x_ref, g_ref, wcq, wckv, wkr, wza, wu, wzb, gq, gkv, cos_ref, sin_ref,
                    cq_o, rows_o, za_o, u_o, zb_o):
    h = _rms(x_ref[...], g_ref[...]).astype(BF16)
    cq_o[...] = _rms(_dot(h, wcq[...]), gq[...])
    ckv = _rms(_dot(h, wckv[...]), gkv[...])
    krope = _rope_nat(_dot(h, wkr[...]), cos_ref[...], sin_ref[...])
    if feature_major:
        rows_o[:MLA_KVL, :] = ckv.T
        rows_o[MLA_KVL:, :] = krope.T
    else:
        rows_o[:, :MLA_KVL] = ckv
        rows_o[:, MLA_KVL:] = krope
    za_o[...] = _dot(h, wza[...])
    u_o[...] = _dot(h, wu[...])
    zb_o[...] = _dot(h, wzb[...])


def _even_in(x2, wts, cosf, sinf, tm, seq=None):
    m = x2.shape[0]
    tab_blocks = cosf.shape[0] // tm
    row = lambda n: pl.BlockSpec((tm, n), lambda i: (i, 0))
    tab = pl.BlockSpec((tm, MLA_ROPE), lambda i: (i % tab_blocks, 0))
    widths = (MLA_QL, MLA_ROW, MLA_W, S5_W, S5_W)
    out_specs = [row(n) for n in widths]
    out_shape = [jax.ShapeDtypeStruct((m, n), F32) for n in widths]
    if seq is not None:
        out_specs[1] = _seq_major_spec(MLA_ROW, tm, seq)
        out_shape[1] = jax.ShapeDtypeStruct((m // seq, MLA_ROW, seq), F32)
    return pl.pallas_call(
        functools.partial(_even_in_kernel, seq is not None),
        grid=(m // tm,),
        in_specs=[row(D_MODEL)] + [_const_spec(c.shape) for c in wts] + [tab, tab],
        out_specs=out_specs,
        out_shape=out_shape,
        compiler_params=_params(("parallel",)),
        name="even_in",
    )(x2, *wts, cosf, sinf)


def _mla_queries(cq, wuqn, wuqr, wuk, cosf, sinf, h):
    cqb = cq.astype(BF16)
    qn = _dot(cqb, wuqn[h])
    ql = _dot(qn.astype(BF16), wuk[h])
    qr = _rope_nat(_dot(cqb, wuqr[h]), cosf, sinf)
    return ql, qr


def _flash_tiles(lo, hi, chains, c_exp):
    for qk, _, first_mask, _, s_buf, m_s, acc_s, l_s in chains:
        m_s[...] = jnp.full(m_s.shape, NEG, F32)
        acc_s[...] = jnp.zeros(acc_s.shape, F32)
        if l_s is not None:
            l_s[...] = jnp.zeros(l_s.shape, F32)
        s_buf[0] = first_mask(qk(lo), lo)

    def consume(chain, j, s):
        _, pv, _, _, _, m_s, acc_s, l_s = chain
        m_old = m_s[...]
        m_new = jnp.maximum(m_old, jnp.max(s, axis=0, keepdims=True))
        alpha = jnp.exp2((m_old - m_new) * c_exp)
        x = (s - m_new) * c_exp
        if l_s is None:
            p = jnp.exp2(x.astype(BF16))
        else:
            pf = jnp.exp2(x)
            l_s[...] = alpha * l_s[...] + jnp.sum(pf, axis=0, keepdims=True)
            p = pf.astype(BF16)
        acc_s[...] = alpha * acc_s[...] + pv(j, p)
        m_s[...] = m_new

    def body(j, carry):
        for chain in chains:
            chain[4][1] = chain[0](j + 1)
        for chain in chains:
            consume(chain, j, chain[4][0])
        for chain in chains:
            chain[4][0] = chain[4][1]
        return carry

    lax.fori_loop(lo, hi - 1, body, 0)
    last = hi - 1
    for chain in chains:
        consume(chain, last, chain[3](chain[4][0], last))
    outs = []
    for chain in chains:
        acc, l_s = chain[6][...], chain[7]
        if l_s is None:
            dv = acc.shape[0] - PACK
            outs.append(acc[:dv, :] / acc[dv:dv + 1, :])
        else:
            outs.append(acc / l_s[...])
    return outs


def _with_ones(v):
    return jnp.concatenate([v, jnp.ones((PACK, v.shape[1]), BF16)], axis=0)


def _mla_prompt_kernel(cq_ref, rows_ref, za_ref, wuqn_t, wuqr_t, wuk_t, wuv_t, cos_ref, sin_ref,
                       out_ref, qtl, qtr, s_buf, m_s, l_s, acc_s):
    qi = pl.program_id(1)
    r = MLA_H * TQ
    half = MLA_ROPE // 2
    cq_t = cq_ref[...].T.astype(BF16)
    qn_t = _dot(wuqn_t[...], cq_t).astype(BF16)
    qr_t = _dot(wuqr_t[...], cq_t)
    cos_t, sin_t = cos_ref[...], sin_ref[...]
    for h in range(MLA_H):
        qtl[:, h * TQ:(h + 1) * TQ] = _dot(wuk_t[h], qn_t[h * MLA_NOPE:(h + 1) * MLA_NOPE, :]).astype(BF16)
        x = qr_t[h * MLA_ROPE:(h + 1) * MLA_ROPE, :]
        xs = jnp.concatenate([x[half:, :], x[:half, :]], axis=0)
        qtr[:, h * TQ:(h + 1) * TQ] = (x * cos_t + xs * sin_t).astype(BF16)
    qpos = qi * TQ + (lax.broadcasted_iota(I32, (1, r), 1) & (TQ - 1))
    krow = lax.broadcasted_iota(I32, (TK, 1), 0)

    def keys(j):
        return rows_ref[:, pl.ds(pl.multiple_of(j * TK, TK), TK)]

    def qk(j):
        kt = keys(j)
        return (_dot_tn(kt[:MLA_KVL, :].astype(BF16), qtl[...])
                + _dot_tn(kt[MLA_KVL:, :].astype(BF16), qtr[...]))

    def pv(j, p):
        return _dot(keys(j)[:MLA_KVL, :].astype(BF16), p)

    def causal(s, j):
        return jnp.where(j * TK + krow <= qpos, s, NEG)

    n_tiles = (qi * TQ + TQ + TK - 1) // TK
    chain = (qk, pv, lambda s, j: s, causal, s_buf, m_s, acc_s, l_s)
    o = _flash_tiles(0, n_tiles, [chain], MLA_SCALE * LOG2E)[0].astype(BF16)
    heads = [_dot(wuv_t[h], o[:, h * TQ:(h + 1) * TQ]) for h in range(MLA_H)]
    out_ref[...] = jnp.concatenate(heads, axis=0).T * _silu(za_ref[...])


def _mla_prompt(cq, rows_t, za, wts, cos_t, sin_t):
    b, t, _ = cq.shape
    r = MLA_H * TQ
    tok = lambda n: pl.BlockSpec((None, TQ, n), lambda bi, qi: (bi, qi, 0))
    tab = pl.BlockSpec((MLA_ROPE, TQ), lambda bi, qi: (0, qi))
    return pl.pallas_call(
        _mla_prompt_kernel,
        grid=(b, t // TQ),
        in_specs=[tok(MLA_QL), pl.BlockSpec((None, MLA_ROW, t), lambda bi, qi: (bi, 0, 0)), tok(MLA_W)]
        + [_const_spec(w.shape) for w in wts] + [tab, tab],
        out_specs=tok(MLA_W),
        out_shape=jax.ShapeDtypeStruct((b, t, MLA_W), F32),
        scratch_shapes=[pltpu.VMEM((MLA_KVL, r), BF16), pltpu.VMEM((MLA_ROPE, r), BF16),
                        pltpu.VMEM((2, TK, r), F32),
                        pltpu.VMEM((1, r), F32), pltpu.VMEM((1, r), F32), pltpu.VMEM((MLA_KVL, r), F32)],
        compiler_params=_params(("parallel", "arbitrary")),
        name="mla_prompt",
    )(cq, rows_t, za, *wts, cos_t, sin_t)


def _mla_decode_kernel(n_pages_step, ts, pt_ref, cq_ref, rows_ref, za_ref, wuqn, wuqr, wuk, wuv,
                       cos_ref, sin_ref, *rest):
    pages = rest[:n_pages_step]
    out_ref, ql_s, qr_s, m_s, l_s, acc_s = rest[n_pages_step:]
    gi = pl.program_id(1)
    rows = MLA_H * SUB

    @pl.when(gi == 0)
    def _():
        cq = cq_ref[...]
        cosf, sinf = cos_ref[...], sin_ref[...]
        for h in range(MLA_H):
            ql, qr = _mla_queries(cq, wuqn, wuqr, wuk, cosf, sinf, h)
            ql_s[h * SUB:(h + 1) * SUB, :] = ql.astype(BF16)
            qr_s[h * SUB:(h + 1) * SUB, :] = qr.astype(BF16)
        m_s[...] = jnp.full((rows, 1), NEG, F32)
        l_s[...] = jnp.zeros((rows, 1), F32)
        acc_s[...] = jnp.zeros((rows, MLA_KVL), F32)

    def update(s, vals, feature_major):
        m_old = m_s[...]
        m_new = jnp.maximum(m_old, jnp.max(s, axis=1, keepdims=True))
        alpha = jnp.exp(m_old - m_new)
        p = jnp.exp(s - m_new)
        l_s[...] = alpha * l_s[...] + jnp.sum(p, axis=1, keepdims=True)
        pv = None
        for (lo, hi), v in vals:
            pj = p[:, lo:hi].astype(BF16)
            term = _dot_nt(pj, v) if feature_major else _dot(pj, v)
            pv = term if pv is None else pv + term
        acc_s[...] = alpha * acc_s[...] + pv
        m_s[...] = m_new

    ql, qr = ql_s[...], qr_s[...]
    scores, vals = [], []
    for j, pg in enumerate(pages):
        k = pg[...]
        ckv = k[:MLA_KVL, :].astype(BF16)
        scores.append(_dot(ql, ckv) + _dot(qr, k[MLA_KVL:, :].astype(BF16)))
        vals.append(((len(vals) * PAGE, (len(vals) + 1) * PAGE), ckv))
        if len(vals) == MLA_DECODE_GROUP or j == n_pages_step - 1:
            update(jnp.concatenate(scores, axis=1) * MLA_SCALE, vals, True)
            scores, vals = [], []

    @pl.when(gi == pl.num_programs(1) - 1)
    def _():
        kn = rows_ref[...]
        ckv = kn[:, :MLA_KVL].astype(BF16)
        s = (_dot_nt(ql, ckv) + _dot_nt(qr, kn[:, MLA_KVL:].astype(BF16))) * MLA_SCALE
        tq = lax.broadcasted_iota(I32, (rows, SUB), 0) & (SUB - 1)
        jk = lax.broadcasted_iota(I32, (rows, SUB), 1)
        s = jnp.where((jk <= tq) & (jk < ts), s, NEG)
        update(s, [((0, SUB), ckv)], False)
        o = (acc_s[...] / l_s[...]).astype(BF16)
        heads = [_dot(o[h * SUB:(h + 1) * SUB, :], wuv[h]) for h in range(MLA_H)]
        out_ref[...] = jnp.concatenate(heads, axis=1) * _silu(za_ref[...])


def _mla_decode(page_table, cache_t, cq, rows, za, wts, cosf, sinf, ts, n_pages_step=64):
    db, n_pages = page_table.shape
    rows_n = MLA_H * SUB
    tok = lambda n: pl.BlockSpec((None, SUB, n), lambda b, g, pt: (b, 0, 0))
    cst = lambda shape: pl.BlockSpec(shape, lambda b, g, pt, _n=len(shape): (0,) * _n)
    page_specs = [
        pl.BlockSpec((None, MLA_ROW, PAGE), lambda b, g, pt, j=j: (pt[b, g * n_pages_step + j], 0, 0))
        for j in range(n_pages_step)]
    grid_spec = pltpu.PrefetchScalarGridSpec(
        num_scalar_prefetch=1,
        grid=(db, n_pages // n_pages_step),
        in_specs=[tok(MLA_QL), tok(MLA_ROW), tok(MLA_W)] + [cst(w.shape) for w in wts]
        + [cst(cosf.shape), cst(sinf.shape)] + page_specs,
        out_specs=tok(MLA_W),
        scratch_shapes=[pltpu.VMEM((rows_n, MLA_KVL), BF16), pltpu.VMEM((rows_n, MLA_ROPE), BF16),
                        pltpu.VMEM((rows_n, 1), F32), pltpu.VMEM((rows_n, 1), F32),
                        pltpu.VMEM((rows_n, MLA_KVL), F32)])
    return pl.pallas_call(
        functools.partial(_mla_decode_kernel, n_pages_step, ts),
        grid_spec=grid_spec,
        out_shape=jax.ShapeDtypeStruct((db, SUB, MLA_W), F32),
        compiler_params=_params(("parallel", "arbitrary")),
        name="mla_decode",
    )(page_table, cq, rows, za, *wts, cosf, sinf, *([cache_t] * n_pages_step))


def _s5_kernel(u_ref, zb_ref, lre_ref, lim_ref, ldt_ref, wbre, wbim, wcre, wcim, d_ref, wglu, bglu,
               h0re_ref, h0im_ref, mix_o, sre_o, sim_o, bure, buim, hre, him):
    step = pl.program_id(0)
    nb, chunk, _ = u_ref.shape
    rows = nb * chunk

    @pl.when(step == 0)
    def _():
        hre[...] = h0re_ref[...]
        him[...] = h0im_ref[...]

    lre, lim = lre_ref[...], lim_ref[...]
    dt = jnp.exp(ldt_ref[...])
    mag = jnp.exp(lre * dt)
    are, aim = mag * jnp.cos(lim * dt), mag * jnp.sin(lim * dt)
    den = lre * lre + lim * lim
    cre = ((are - 1.0) * lre + aim * lim) / den
    cim = (aim * lre - (are - 1.0) * lim) / den

    ut = jnp.swapaxes(u_ref[...], 0, 1).reshape(rows, S5_W)
    kw, nw = S5_GT * S5_P, S5_GT * S5_N
    for jt in range(S5_NT):
        cr, ci = cre[:, jt * nw:(jt + 1) * nw], cim[:, jt * nw:(jt + 1) * nw]
        bre = (cr * wbre[jt] - ci * wbim[jt]).astype(BF16)
        bim = (cr * wbim[jt] + ci * wbre[jt]).astype(BF16)
        uj = ut[:, jt * kw:(jt + 1) * kw].astype(BF16)
        bure[:, jt * nw:(jt + 1) * nw] = _dot(uj, bre)
        buim[:, jt * nw:(jt + 1) * nw] = _dot(uj, bim)

    def scan(t, carry):
        hr, hi = carry
        sl = pl.ds(pl.multiple_of(t * nb, SUB), nb)
        nr = are * hr - aim * hi + bure[sl, :]
        ni = are * hi + aim * hr + buim[sl, :]
        bure[sl, :] = nr
        buim[sl, :] = ni
        return nr, ni

    hr, hi = lax.fori_loop(0, chunk, scan, (hre[...], him[...]))
    hre[...] = hr
    him[...] = hi
    sre_o[...] = hr
    sim_o[...] = hi

    ys = []
    for jt in range(S5_NT):
        sr = bure[:, jt * nw:(jt + 1) * nw].astype(BF16)
        si = buim[:, jt * nw:(jt + 1) * nw].astype(BF16)
        ys.append(_dot(sr, wcre[jt].astype(BF16)) - _dot(si, wcim[jt].astype(BF16)))
    y = jnp.concatenate(ys, axis=1) + d_ref[...] * ut
    g5 = jax.nn.gelu(y)
    ob = g5 * jax.nn.sigmoid(_dot(g5.astype(BF16), wglu[...]) + bglu[...])
    mix_o[...] = jnp.swapaxes(ob.reshape(chunk, nb, S5_W), 0, 1) * _silu(zb_ref[...])


def _s5(u, zb, wts, h0re, h0im, chunk):
    nb, t, _ = u.shape
    tok = pl.BlockSpec((nb, chunk, S5_W), lambda i: (0, i, 0))
    st = _const_spec((nb, S5_S))
    return pl.pallas_call(
        _s5_kernel,
        grid=(t // chunk,),
        in_specs=[tok, tok] + [_const_spec(w.shape) for w in wts] + [st, st],
        out_specs=[tok, st, st],
        out_shape=[jax.ShapeDtypeStruct((nb, t, S5_W), F32), jax.ShapeDtypeStruct((nb, S5_S), F32),
                   jax.ShapeDtypeStruct((nb, S5_S), F32)],
        scratch_shapes=[pltpu.VMEM((nb * chunk, S5_S), F32), pltpu.VMEM((nb * chunk, S5_S), F32),
                        pltpu.VMEM((nb, S5_S), F32), pltpu.VMEM((nb, S5_S), F32)],
        compiler_params=_params(("arbitrary",)),
        name="s5",
    )(u, zb, *wts, h0re, h0im)


def _even_out_kernel(x_ref, a_ref, b_ref, wa, wb, out_ref):
    out_ref[...] = (x_ref[...] + _dot(a_ref[...].astype(BF16), wa[...])
                    + _dot(b_ref[...].astype(BF16), wb[...]))


def _even_out(x2, mixa, mixb, wa, wb, tm):
    m = x2.shape[0]
    row = lambda n: pl.BlockSpec((tm, n), lambda i: (i, 0))
    return pl.pallas_call(
        _even_out_kernel,
        grid=(m // tm,),
        in_specs=[row(D_MODEL), row(MLA_W), row(S5_W), _const_spec(wa.shape), _const_spec(wb.shape)],
        out_specs=row(D_MODEL),
        out_shape=jax.ShapeDtypeStruct((m, D_MODEL), F32),
        compiler_params=_params(("parallel",)),
        name="even_out",
    )(x2, mixa, mixb, wa, wb)


def _odd_in_kernel(feature_major, x_ref, g_ref, wq, wkv4, wkw, wg, wz, q_o, kv4_o, kw_o, gate_o, z_o):
    h = _rms(x_ref[...], g_ref[...]).astype(BF16)
    q_o[...] = _dot(h, wq[...])
    kv4, kw = _dot(h, wkv4[...]), _dot(h, wkw[...])
    kv4_o[...] = kv4.T if feature_major else kv4
    kw_o[...] = kw.T if feature_major else kw
    gate_o[...] = jax.nn.sigmoid(_dot(h, wg[...]))
    z_o[...] = _dot(h, wz[...])


def _odd_in(x2, wts, tm, seq=None):
    m = x2.shape[0]
    row = lambda n: pl.BlockSpec((tm, n), lambda i: (i, 0))
    widths = (NSA_W, 4 * NSA_KVW, 2 * NSA_KVW, 3 * NSA_H, NSA_W)
    out_specs = [row(n) for n in widths]
    out_shape = [jax.ShapeDtypeStruct((m, n), F32) for n in widths]
    if seq is not None:
        for i in (1, 2):
            out_specs[i] = _seq_major_spec(widths[i], tm, seq)
            out_shape[i] = jax.ShapeDtypeStruct((m // seq, widths[i], seq), F32)
    return pl.pallas_call(
        functools.partial(_odd_in_kernel, seq is not None),
        grid=(m // tm,),
        in_specs=[row(D_MODEL)] + [_const_spec(w.shape) for w in wts],
        out_specs=out_specs,
        out_shape=out_shape,
        compiler_params=_params(("parallel",)),
        name="odd_in",
    )(x2, *wts)


def _compress_stage(xs, x_t, pe, base):
    n = x_t.shape[1] // NSA_BLK
    x = x_t.astype(BF16).T.reshape(n, NSA_BLK, 2 * NSA_KVW) + pe.astype(BF16)[None]
    xs[:, pl.ds(pl.multiple_of(base, PACK), n), :] = jnp.swapaxes(x, 0, 1)


def _compress_finish(xs, w1, phi2, out_ref):
    acc = jnp.zeros((xs.shape[1], 2 * NSA_KVW), F32)
    for r in range(NSA_BLK):
        acc = acc + _dot(xs[r], w1[r])
    out_ref[...] = _dot(_silu(acc).astype(BF16), phi2[...])


def _compress_prompt_kernel(x_ref, pe_ref, w1, phi2, out_ref, xs):
    bi = pl.program_id(0)
    nblk = x_ref.shape[1] // NSA_BLK
    _compress_stage(xs, x_ref[...], pe_ref[...], bi * nblk)

    @pl.when(bi == pl.num_programs(0) - 1)
    def _():
        _compress_finish(xs, w1, phi2, out_ref)


def _compress_prompt(kv4_t, pe, w1, phi2):
    b, _, t = kv4_t.shape
    cw = 2 * NSA_KVW
    nblk = b * (t // NSA_BLK)
    return pl.pallas_call(
        _compress_prompt_kernel,
        grid=(b,),
        in_specs=[pl.BlockSpec((None, cw, t), lambda i: (i, 0, 0)),
                  _const_spec(pe.shape), _const_spec(w1.shape), _const_spec(phi2.shape)],
        out_specs=pl.BlockSpec((nblk, cw), lambda i: (0, 0)),
        out_shape=jax.ShapeDtypeStruct((nblk, cw), F32),
        scratch_shapes=[pltpu.VMEM((NSA_BLK, nblk, cw), BF16)],
        compiler_params=_params(("arbitrary",)),
        name="compress_prompt",
    )(kv4_t, pe, w1, phi2)


def _compress_decode_kernel(n_pages_step, pt_ref, pe_ref, w1, phi2, *rest):
    pages = rest[:n_pages_step]
    out_ref, xs = rest[n_pages_step:]
    gi = pl.program_id(1)
    grp = PACK // BPP
    pe = pe_ref[...]
    for k in range(n_pages_step // grp):
        x_t = jnp.concatenate([pages[k * grp + j][...] for j in range(grp)], axis=1)
        _compress_stage(xs, x_t, pe, (gi * (n_pages_step // grp) + k) * PACK)

    @pl.when(gi == pl.num_programs(1) - 1)
    def _():
        _compress_finish(xs, w1, phi2, out_ref)


def _compress_decode(page_table, cache_t, pe, w1, phi2, n_pages_step=32):
    db, n_pages = page_table.shape
    cw = 2 * NSA_KVW
    nblk = n_pages * BPP
    cst = lambda shape: pl.BlockSpec(shape, lambda b, g, pt, _n=len(shape): (0,) * _n)
    page_specs = [
        pl.BlockSpec((None, cw, PAGE), lambda b, g, pt, j=j: (pt[b, g * n_pages_step + j], 0, 0))
        for j in range(n_pages_step)]
    grid_spec = pltpu.PrefetchScalarGridSpec(
        num_scalar_prefetch=1,
        grid=(db, n_pages // n_pages_step),
        in_specs=[cst(pe.shape), cst(w1.shape), cst(phi2.shape)] + page_specs,
        out_specs=pl.BlockSpec((None, nblk, cw), lambda b, g, pt: (b, 0, 0)),
        scratch_shapes=[pltpu.VMEM((NSA_BLK, nblk, cw), BF16)])
    return pl.pallas_call(
        functools.partial(_compress_decode_kernel, n_pages_step),
        grid_spec=grid_spec,
        out_shape=jax.ShapeDtypeStruct((db, nblk, cw), F32),
        compiler_params=_params(("parallel", "arbitrary")),
        name="compress_decode",
    )(page_table, pe, w1, phi2, *([cache_t] * n_pages_step))


def _alibi_slope(h):
    return 2.0 ** (-8.0 * (h + 1) / NSA_H)


def _split_bf16(x):
    hi = x.astype(BF16)
    return hi, (x - hi.astype(F32)).astype(BF16)


def _cmp_kernel(q_ref, cb_ref, pos_ref, oc_o, imp_o, s_scr):
    tt = q_ref.shape[0]
    nc = cb_ref.shape[0]
    qpos = pos_ref[...]
    cpos = lax.broadcasted_iota(I32, (nc, 1), 0) * NSA_BLK + (NSA_BLK - 1)
    visible = cpos <= qpos
    dist = (qpos - cpos).astype(F32)
    q = q_ref[...] * NSA_SCALE
    cb = cb_ref[...]
    for g in range(NSA_G):
        k_hi, k_lo = _split_bf16(cb[:, g * NSA_D:(g + 1) * NSA_D])
        vc = cb[:, NSA_KVW + g * NSA_D:NSA_KVW + (g + 1) * NSA_D].astype(BF16)
        for i in range(NSA_R):
            h = g * NSA_R + i
            q_hi, q_lo = _split_bf16(q[:, h * NSA_D:(h + 1) * NSA_D])
            s = _dot_nt(k_hi, q_hi) + _dot_nt(k_hi, q_lo) + _dot_nt(k_lo, q_hi)
            s_scr[i * nc:(i + 1) * nc, :] = s - _alibi_slope(h) * dist
        s3 = jnp.where(visible[None], s_scr[...].reshape(NSA_R, nc, tt), NEG)
        mx = jnp.max(s3, axis=1, keepdims=True)
        e = jnp.where(visible[None], jnp.exp(s3 - mx), 0.0)
        den = jnp.sum(e, axis=1, keepdims=True)
        p = e / jnp.where(den > 0, den, 1.0)
        imp_o[g] = jnp.sum(p, axis=0)
        for i in range(NSA_R):
            h = g * NSA_R + i
            oc_o[h * NSA_D:(h + 1) * NSA_D, :] = _dot_tn(vc, p[i].astype(BF16))


def _cmp(q, cb, pos, tt):
    b, t, _ = q.shape
    nc = cb.shape[1]
    return pl.pallas_call(
        _cmp_kernel,
        grid=(b, t // tt),
        in_specs=[pl.BlockSpec((None, tt, NSA_W), lambda bi, ti: (bi, ti, 0)),
                  pl.BlockSpec((None, nc, 2 * NSA_KVW), lambda bi, ti: (bi, 0, 0)),
                  pl.BlockSpec((1, tt), lambda bi, ti: (0, ti))],
        out_specs=[pl.BlockSpec((None, NSA_W, tt), lambda bi, ti: (bi, 0, ti)),
                   pl.BlockSpec((None, NSA_G, nc, tt), lambda bi, ti: (bi, 0, 0, ti))],
        out_shape=[jax.ShapeDtypeStruct((b, NSA_W, t), F32),
                   jax.ShapeDtypeStruct((b, NSA_G, nc, t), F32)],
        scratch_shapes=[pltpu.VMEM((NSA_R * nc, tt), F32)],
        compiler_params=_params(("parallel", "parallel")),
        name="nsa_cmp",
    )(q, cb, pos)


def _topk_kernel(n_sel, imp_ref, pos_ref, sel_o, idx_o):
    _, nc, tt = imp_ref.shape
    nsp = sel_o.shape[1]
    qpos = pos_ref[...]
    blk = lax.broadcasted_iota(I32, (nsp, 1), 0)
    cur = jnp.right_shift(qpos, NSA_BLK.bit_length() - 1)
    forced = (blk == 0) | (blk == cur) | (blk == cur - 1)
    allowed = (blk <= cur) & (blk < n_sel)
    for g in range(NSA_G):
        imp = imp_ref[g]
        if nsp > nc:
            imp = jnp.concatenate([imp, jnp.zeros((nsp - nc, tt), F32)], axis=0)
        score = jnp.where(allowed, imp + jnp.where(forced, FORCED_BONUS, 0.0), NEG)
        chosen = jnp.zeros((nsp, tt), F32)
        picks = []
        for _ in range(NSA_TOPK):
            mx = jnp.max(score, axis=0, keepdims=True)
            first = jnp.min(jnp.where(score == mx, blk, nsp), axis=0, keepdims=True)
            hit = blk == first
            chosen = jnp.where(hit, 1.0, chosen)
            score = jnp.where(hit, GONE, score)
            picks.append(first)
        sel_o[g] = chosen
        idx_o[g] = jnp.concatenate(picks, axis=0)


def _topk(imp, pos, n_sel, tt):
    b, _, nc, t = imp.shape
    nsp = -(-(n_sel + TK // NSA_BLK) // SUB) * SUB
    return pl.pallas_call(
        functools.partial(_topk_kernel, n_sel),
        grid=(b, t // tt),
        in_specs=[pl.BlockSpec((None, NSA_G, nc, tt), lambda bi, ti: (bi, 0, 0, ti)),
                  pl.BlockSpec((1, tt), lambda bi, ti: (0, ti))],
        out_specs=[pl.BlockSpec((None, NSA_G, nsp, tt), lambda bi, ti: (bi, 0, 0, ti)),
                   pl.BlockSpec((None, NSA_G, NSA_TOPK, tt), lambda bi, ti: (bi, 0, 0, ti))],
        out_shape=[jax.ShapeDtypeStruct((b, NSA_G, nsp, t), F32),
                   jax.ShapeDtypeStruct((b, NSA_G, NSA_TOPK, t), I32)],
        compiler_params=_params(("parallel", "parallel")),
        name="nsa_topk",
    )(imp, pos)


def _bf16_parts(x, n):
    parts = []
    for _ in range(n):
        bits = np.asarray(x, np.float32).view(np.uint32)
        top = ((bits + np.uint32(0x7FFF) + ((bits >> np.uint32(16)) & np.uint32(1)))
               & np.uint32(0xFFFF0000)).view(np.float32)
        parts.append(float(top))
        x = float(np.float32(x) - top)
    return parts


def _nsa_prompt_kernel(q_ref, kv4_ref, kw_ref, oc_ref, sel_ref, gate_ref, out_ref,
                       k_aug, q_aug, s_buf, m_s, acc_s):
    qi = pl.program_id(1)
    r = NSA_R * TQ
    bpt = TK // NSA_BLK
    n_parts = SUB // 2
    qt = (q_ref[...] * NSA_SCALE).T.astype(BF16)
    gt = gate_ref[...].T
    lane = lax.broadcasted_iota(I32, (1, r), 1)
    qpos = qi * TQ + (lane & (TQ - 1))
    head = lane // TQ
    krow = lax.broadcasted_iota(I32, (TK, 1), 0)
    klane = lax.broadcasted_iota(I32, (1, TK), 1)
    row8 = lax.broadcasted_iota(I32, (SUB, 1), 0)
    n_tiles = (qi * TQ + TQ + TK - 1) // TK
    onehot = jnp.where(row8 == jnp.right_shift(klane, NSA_BLK.bit_length() - 1), 1.0, 0.0)

    def key_rows(j):
        rel = j * TK - qi * TQ + klane
        coarse = (rel & -NSA_BLK).astype(F32)
        fine = (rel & (NSA_BLK - 1)).astype(F32)
        alibi = jnp.where((row8 & 1) == 0, coarse, fine)
        return jnp.concatenate([alibi, onehot], axis=0).astype(BF16)

    def causal(s, j):
        return jnp.where(j * TK + krow <= qpos, s, NEG)

    def in_window(s, j):
        return jnp.where(qpos - (j * TK + krow) <= NSA_WIN, s, NEG)

    def tile(ref, lo, j):
        return ref[lo:lo + NSA_D, pl.ds(pl.multiple_of(j * TK, TK), TK)].astype(BF16)

    slopes, sel_chains, win_chains = [], [], []
    for g in range(NSA_G):
        kg, qa = k_aug.at[g], q_aug.at[g]
        qa[:NSA_D, :] = jnp.concatenate(
            [qt[(g * NSA_R + i) * NSA_D:(g * NSA_R + i + 1) * NSA_D, :] for i in range(NSA_R)], axis=1)
        slope8 = jnp.zeros((SUB, r), F32)
        for i in range(NSA_R):
            parts = _bf16_parts(_alibi_slope(g * NSA_R + i), n_parts)
            col = jnp.zeros((SUB, 1), F32)
            for k, part in enumerate(parts):
                col = jnp.where(jnp.right_shift(row8, 1) == k, part, col)
            slope8 = jnp.where(head == i, col, slope8)
        slopes.append(slope8)
        ks_lo, vs_lo = 2 * NSA_KVW + g * NSA_D, 3 * NSA_KVW + g * NSA_D
        kw_lo, vw_lo = g * NSA_D, NSA_KVW + g * NSA_D

        def qk_sel(j, g=g, kg=kg, qa=qa, ks_lo=ks_lo, slope8=slope8):
            kg[:NSA_D, :] = tile(kv4_ref, ks_lo, j)
            kg[NSA_D:, :] = key_rows(j)
            flags = sel_ref[g, pl.ds(j * bpt, SUB), :]
            off = jnp.where(flags > 0.5, 0.0, NEG)
            qa[NSA_D:, :] = jnp.concatenate(
                [slope8, jnp.concatenate([off] * NSA_R, axis=1)], axis=0).astype(BF16)
            return _dot_tn(kg[...], qa[...])

        def qk_win(j, kg=kg, qa=qa, kw_lo=kw_lo):
            kg[:NSA_D, :] = tile(kw_ref, kw_lo, j)
            kg[NSA_D:, :] = key_rows(j)
            return _dot_tn(kg[...], qa[...])

        state = (s_buf.at[g], m_s.at[g], acc_s.at[g], None)
        sel_chains.append((qk_sel, lambda j, p, lo=vs_lo: _dot(_with_ones(tile(kv4_ref, lo, j)), p),
                           lambda s, j: s, causal) + state)
        win_chains.append((qk_win, lambda j, p, lo=vw_lo: _dot(_with_ones(tile(kw_ref, lo, j)), p),
                           in_window, lambda s, j: causal(in_window(s, j), j)) + state)

    o_sels = _flash_tiles(0, n_tiles, sel_chains, LOG2E)
    for g in range(NSA_G):
        q_aug[g, NSA_D:, :] = jnp.concatenate([slopes[g], jnp.zeros((SUB, r), F32)], axis=0).astype(BF16)
    o_wins = _flash_tiles(jnp.maximum(qi * TQ - NSA_WIN, 0) // TK, n_tiles, win_chains, LOG2E)
    for g in range(NSA_G):
        o_sel, o_win = o_sels[g], o_wins[g]
        for i in range(NSA_R):
            h = g * NSA_R + i
            rows = slice(h * NSA_D, (h + 1) * NSA_D)
            cols = slice(i * TQ, (i + 1) * TQ)
            out_ref[rows, :] = (gt[h:h + 1, :] * oc_ref[rows, :]
                                + gt[NSA_H + h:NSA_H + h + 1, :] * o_sel[:, cols]
                                + gt[2 * NSA_H + h:2 * NSA_H + h + 1, :] * o_win[:, cols])


def _nsa_prompt(q, kv4_t, kw_t, oc_t, sel, gate):
    b, t, _ = q.shape
    nsp = sel.shape[2]
    r = NSA_R * TQ
    tok = lambda n: pl.BlockSpec((None, TQ, n), lambda bi, qi: (bi, qi, 0))
    full = lambda n: pl.BlockSpec((None, n, t), lambda bi, qi: (bi, 0, 0))
    return pl.pallas_call(
        _nsa_prompt_kernel,
        grid=(b, t // TQ),
        in_specs=[tok(NSA_W), full(4 * NSA_KVW), full(2 * NSA_KVW),
                  pl.BlockSpec((None, NSA_W, TQ), lambda bi, qi: (bi, 0, qi)),
                  pl.BlockSpec((None, NSA_G, nsp, TQ), lambda bi, qi: (bi, 0, 0, qi)),
                  tok(3 * NSA_H)],
        out_specs=pl.BlockSpec((None, NSA_W, TQ), lambda bi, qi: (bi, 0, qi)),
        out_shape=jax.ShapeDtypeStruct((b, NSA_W, t), F32),
        scratch_shapes=[pltpu.VMEM((NSA_G, NSA_D + 2 * SUB, TK), BF16),
                        pltpu.VMEM((NSA_G, NSA_D + 2 * SUB, r), BF16),
                        pltpu.VMEM((NSA_G, 2, TK, r), F32),
                        pltpu.VMEM((NSA_G, 1, r), F32),
                        pltpu.VMEM((NSA_G, NSA_D + PACK, r), F32)],
        compiler_params=_params(("parallel", "arbitrary")),
        name="nsa_prompt",
    )(q, kv4_t, kw_t, oc_t, sel, gate)


def _softmax_rows(parts):
    mx = None
    for s, _, _ in parts:
        pm = jnp.max(s, axis=1, keepdims=True)
        mx = pm if mx is None else jnp.maximum(mx, pm)
    den, num = None, None
    for s, v, feature_major in parts:
        p = jnp.exp(s - mx)
        d = jnp.sum(p, axis=1, keepdims=True)
        n = _dot_nt(p.astype(BF16), v) if feature_major else _dot(p.astype(BF16), v)
        den = d if den is None else den + d
        num = n if num is None else num + n
    return num / den


def _nsa_sel_decode_kernel(ts, pos0, nbp, idx_ref, pid_ref, q_ref, new_ref, *rest):
    nblk = NSA_G * NSA_TOPK
    pages = rest[:nblk]
    out_ref = rest[nblk]
    b, t = pl.program_id(0), pl.program_id(1)
    q = q_ref[...] * NSA_SCALE
    new = new_ref[...]
    lane = lax.broadcasted_iota(I32, (1, PAGE), 1)
    jn = lax.broadcasted_iota(I32, (1, SUB), 1)
    for g in range(NSA_G):
        qg = q[g * NSA_R:(g + 1) * NSA_R, :].astype(BF16)
        slope = jnp.concatenate(
            [jnp.full((1, 1), _alibi_slope(g * NSA_R + i), F32) for i in range(NSA_R)], axis=0)
        parts = []
        has_new = jnp.zeros((), I32)
        for k in range(NSA_TOPK):
            bid = idx_ref[((b * NSA_G + g) * ts + t) * NSA_TOPK + k]
            pg = pages[g * NSA_TOPK + k][...]
            kk = pg[g * NSA_D:(g + 1) * NSA_D, :].astype(BF16)
            vv = pg[NSA_KVW + g * NSA_D:NSA_KVW + (g + 1) * NSA_D, :].astype(BF16)
            sub = bid % BPP
            rel = ((bid - sub) * NSA_BLK - pos0 + lane).astype(F32)
            s = _dot(qg, kk) + slope * rel
            mine = (lane // NSA_BLK == sub) & (bid < nbp)
            parts.append((jnp.where(mine, s, NEG), vv, True))
            has_new = has_new + (bid >= nbp).astype(I32)
        kn = new[:, 2 * NSA_KVW + g * NSA_D:2 * NSA_KVW + (g + 1) * NSA_D].astype(BF16)
        vn = new[:, 3 * NSA_KVW + g * NSA_D:3 * NSA_KVW + (g + 1) * NSA_D].astype(BF16)
        sn = _dot_nt(qg, kn) + slope * jn.astype(F32)
        ok = (jn <= t) & (jn < ts) & (has_new > 0)
        parts.append((jnp.where(ok, sn, NEG), vn, False))
        out_ref[g * NSA_R:(g + 1) * NSA_R, :] = _softmax_rows(parts)


def _nsa_sel_decode(idx, page_table, cache_t, q4, kv4_new, ts, pos0):
    db, n_pages = page_table.shape
    nbp = n_pages * BPP
    cw = 2 * NSA_KVW
    page_ids = jnp.take_along_axis(page_table, jnp.minimum(idx, nbp - 1) // BPP, axis=1).reshape(-1)
    idx_flat = idx.reshape(-1)

    def page_map(b, t, idx_r, pid, g, k):
        return (pid[((b * NSA_G + g) * ts + t) * NSA_TOPK + k], 1, 0)

    page_specs = [pl.BlockSpec((None, cw, PAGE), functools.partial(page_map, g=g, k=k))
                  for g in range(NSA_G) for k in range(NSA_TOPK)]
    grid_spec = pltpu.PrefetchScalarGridSpec(
        num_scalar_prefetch=2,
        grid=(db, ts),
        in_specs=[pl.BlockSpec((None, None, NSA_H, NSA_D), lambda b, t, idx, pt: (b, t, 0, 0)),
                  pl.BlockSpec((None, SUB, 4 * NSA_KVW), lambda b, t, idx, pt: (b, 0, 0))] + page_specs,
        out_specs=pl.BlockSpec((None, None, NSA_H, NSA_D), lambda b, t, idx, pt: (b, t, 0, 0)))
    return pl.pallas_call(
        functools.partial(_nsa_sel_decode_kernel, ts, pos0, nbp),
        grid_spec=grid_spec,
        out_shape=jax.ShapeDtypeStruct((db, ts, NSA_H, NSA_D), F32),
        compiler_params=_params(("parallel", "arbitrary")),
        name="nsa_sel_decode",
    )(idx_flat, page_ids, q4, kv4_new, *([cache_t] * (NSA_G * NSA_TOPK)))


def _nsa_win_decode_kernel(ts, q_ref, win_ref, new_ref, out_ref):
    rows = NSA_R * SUB
    win = win_ref[...]
    new = new_ref[...]
    tq = lax.broadcasted_iota(I32, (rows, 1), 0) & (SUB - 1)
    iw = lax.broadcasted_iota(I32, (1, NSA_WIN), 1)
    jn = lax.broadcasted_iota(I32, (1, SUB), 1)
    for g in range(NSA_G):
        qg = jnp.concatenate([q_ref[:, g * NSA_R + i, :] for i in range(NSA_R)], axis=0)
        qg = (qg * NSA_SCALE).astype(BF16)
        slope = jnp.concatenate(
            [jnp.full((SUB, 1), _alibi_slope(g * NSA_R + i), F32) for i in range(NSA_R)], axis=0)
        kw = win[g * NSA_D:(g + 1) * NSA_D, :].astype(BF16)
        vw = win[NSA_KVW + g * NSA_D:NSA_KVW + (g + 1) * NSA_D, :].astype(BF16)
        kn = new[:, g * NSA_D:(g + 1) * NSA_D].astype(BF16)
        vn = new[:, NSA_KVW + g * NSA_D:NSA_KVW + (g + 1) * NSA_D].astype(BF16)
        sw = _dot(qg, kw) + slope * (iw - NSA_WIN).astype(F32)
        sw = jnp.where(iw >= tq, sw, NEG)
        sn = _dot_nt(qg, kn) + slope * jn.astype(F32)
        sn = jnp.where((jn <= tq) & (jn < ts), sn, NEG)
        o = _softmax_rows([(sw, vw, True), (sn, vn, False)])
        for i in range(NSA_R):
            out_ref[:, g * NSA_R + i, :] = o[i * SUB:(i + 1) * SUB, :]


def _nsa_win_decode(q4, win_t, kw_new, ts):
    db = q4.shape[0]
    return pl.pallas_call(
        functools.partial(_nsa_win_decode_kernel, ts),
        grid=(db,),
        in_specs=[pl.BlockSpec((None, SUB, NSA_H, NSA_D), lambda b: (b, 0, 0, 0)),
                  pl.BlockSpec((None, 2 * NSA_KVW, NSA_WIN), lambda b: (b, 0, 0)),
                  pl.BlockSpec((None, SUB, 2 * NSA_KVW), lambda b: (b, 0, 0))],
        out_specs=pl.BlockSpec((None, SUB, NSA_H, NSA_D), lambda b: (b, 0, 0, 0)),
        out_shape=jax.ShapeDtypeStruct((db, SUB, NSA_H, NSA_D), F32),
        compiler_params=_params(("parallel",)),
        name="nsa_win_decode",
    )(q4, win_t, kw_new)


def _odd_tail(o, z_ref, x_ref, wout, gfin):
    y = x_ref[...] + _dot((o * _silu(z_ref[...])).astype(BF16), wout[...])
    return _rms(y, gfin[...])


def _odd_out_prompt_kernel(ot_ref, z_ref, x_ref, wout, gfin, out_ref):
    out_ref[...] = _odd_tail(ot_ref[...].T, z_ref, x_ref, wout, gfin)


def _odd_out_prompt(ot, z, x, wout, gfin, tm):
    b, _, t = ot.shape
    tok = pl.BlockSpec((None, tm, D_MODEL), lambda bi, ti: (bi, ti, 0))
    return pl.pallas_call(
        _odd_out_prompt_kernel,
        grid=(b, t // tm),
        in_specs=[pl.BlockSpec((None, NSA_W, tm), lambda bi, ti: (bi, 0, ti)), tok, tok,
                  _const_spec(wout.shape), _const_spec(gfin.shape)],
        out_specs=tok,
        out_shape=jax.ShapeDtypeStruct((b, t, D_MODEL), F32),
        compiler_params=_params(("parallel", "parallel")),
        name="odd_out_prompt",
    )(ot, z, x, wout, gfin)


def _odd_out_decode_kernel(oc_ref, os_ref, ow_ref, gate_ref, z_ref, x_ref, wout, gfin, out_ref):
    gate = gate_ref[...]
    oc, osel, ow = oc_ref[...], os_ref[...], ow_ref[...]
    heads = []
    for h in range(NSA_H):
        c = slice(h * NSA_D, (h + 1) * NSA_D)
        heads.append(gate[:, h:h + 1] * oc[:, c] + gate[:, NSA_H + h:NSA_H + h + 1] * osel[:, c]
                     + gate[:, 2 * NSA_H + h:2 * NSA_H + h + 1] * ow[:, c])
    out_ref[...] = _odd_tail(jnp.concatenate(heads, axis=1), z_ref, x_ref, wout, gfin)


def _odd_out_decode(oc, osel, ow, gate, z, x, wout, gfin):
    m = x.shape[0]
    args = (oc, osel, ow, gate, z, x, wout, gfin)
    return pl.pallas_call(
        _odd_out_decode_kernel,
        grid=(1,),
        in_specs=[_const_spec(a.shape) for a in args],
        out_specs=_const_spec((m, D_MODEL)),
        out_shape=jax.ShapeDtypeStruct((m, D_MODEL), F32),
        compiler_params=_params(("arbitrary",)),
        name="odd_out_decode",
    )(*args)


def _rope_tables(pos):
    half = MLA_ROPE // 2
    inv = ROPE_THETA ** (-jnp.arange(half, dtype=F32) / half)
    ang = pos.astype(F32)[:, None] * inv[None, :]
    cos, sin = jnp.cos(ang), jnp.sin(ang)
    return jnp.concatenate([cos, cos], axis=1), jnp.concatenate([-sin, sin], axis=1)


def _block_diag(x):
    t, g, r, c = x.shape
    eye = jnp.eye(g, dtype=x.dtype)
    return jnp.einsum("tgrc,gh->tgrhc", x, eye).reshape(t, g * r, g * c)


def _even_weights(norm_g, w_in, g_q, g_kv, w_uq, w_uk, w_uv):
    edges = [0, MLA_QL, MLA_QL + MLA_KVL, MLA_QL + MLA_ROW]
    edges += [edges[-1] + MLA_W, edges[-1] + MLA_W + S5_W, edges[-1] + MLA_W + 2 * S5_W]
    wb = w_in.astype(BF16)
    pieces = [wb[:, edges[i]:edges[i + 1]] for i in range(6)]
    in_w = (norm_g[None, :], *pieces, g_q[None, :], g_kv[None, :])
    uq = jnp.transpose(w_uq, (1, 0, 2)).astype(BF16)
    mla_w = (uq[:, :, :MLA_NOPE], uq[:, :, MLA_NOPE:],
             jnp.transpose(w_uk, (1, 2, 0)).astype(BF16),
             jnp.transpose(w_uv, (1, 0, 2)).astype(BF16))
    uq_t = jnp.transpose(w_uq, (1, 2, 0)).astype(BF16)
    mla_wt = (uq_t[:, :MLA_NOPE].reshape(MLA_H * MLA_NOPE, MLA_QL),
              uq_t[:, MLA_NOPE:].reshape(MLA_H * MLA_ROPE, MLA_QL),
              jnp.transpose(w_uk, (1, 0, 2)).astype(BF16),
              jnp.transpose(w_uv, (1, 2, 0)).astype(BF16))
    return in_w, mla_w, mla_wt


def _s5_weights(lam_re, lam_im, log_dt, b_re, b_im, c_re, c_im, d_skip, w_glu, b_glu):
    def bmat(b):
        return _block_diag(jnp.transpose(b.reshape(S5_NT, S5_GT, S5_N, S5_P), (0, 1, 3, 2)))

    def cmat(c):
        return _block_diag(jnp.transpose(c.reshape(S5_NT, S5_GT, S5_P, S5_N), (0, 1, 3, 2)))

    return (lam_re.reshape(1, S5_S), lam_im.reshape(1, S5_S),
            jnp.repeat(log_dt, S5_N).reshape(1, S5_S),
            bmat(b_re), bmat(b_im), cmat(c_re), cmat(c_im),
            d_skip.reshape(1, S5_W), w_glu.astype(BF16), b_glu[None, :])


def _odd_weights(norm_g, w_in, pe_k, pe_v, phi1_k, phi2_k, phi1_v, phi2_v):
    wb = w_in.astype(BF16)
    e0 = NSA_W
    e1 = e0 + 4 * NSA_KVW
    e2 = e1 + 2 * NSA_KVW
    e3 = e2 + 3 * NSA_H
    in_w = (norm_g[None, :], wb[:, :e0], wb[:, e0:e1], wb[:, e1:e2], wb[:, e2:e3], wb[:, e3:])
    pe = jnp.concatenate([pe_k, pe_k, pe_v, pe_v], axis=1)
    p1k = phi1_k.reshape(NSA_BLK, NSA_D, NSA_D)
    p1v = phi1_v.reshape(NSA_BLK, NSA_D, NSA_D)
    w1 = jnp.zeros((NSA_BLK, 2 * NSA_KVW, 2 * NSA_KVW), BF16)
    for s, piece in enumerate((p1k, p1k, p1v, p1v)):
        w1 = w1.at[:, s * NSA_D:(s + 1) * NSA_D, s * NSA_D:(s + 1) * NSA_D].set(piece.astype(BF16))
    phi2 = _block_diag(jnp.stack([phi2_k, phi2_k, phi2_v, phi2_v], axis=0)[None])[0].astype(BF16)
    return in_w, (pe, w1, phi2)


def _pad_tokens(x, n):
    return jnp.pad(x, ((0, 0), (0, n - x.shape[1])) + ((0, 0),) * (x.ndim - 2))


def _rows_last(x):
    nd = x.ndim
    xt = jnp.transpose(x, (0,) + tuple(range(2, nd)) + (1,))
    return xt.reshape(x.shape[0], -1, x.shape[1])


def _rows_second(x_t, feature_shape):
    b, _, rows = x_t.shape
    nf = len(feature_shape)
    xt = x_t.reshape((b,) + tuple(feature_shape) + (rows,))
    return jnp.transpose(xt, (0, nf + 1) + tuple(range(1, nf + 1)))


def kernel(x_prompt, x_sample, cache_mla, state_s5, cache_nsa_kv, state_nsa_win, page_table, norm_even, w_in_even, mla_g_q, mla_g_kv, mla_w_uq, mla_w_uk, mla_w_uv, s5_lambda_re, s5_lambda_im, s5_log_dt, s5_b_re, s5_b_im, s5_c_re, s5_c_im, s5_d, s5_w_glu, s5_b_glu, w_out_even, norm_odd, w_in_odd, nsa_pe_k, nsa_pe_v, nsa_phi1_k, nsa_phi2_k, nsa_phi1_v, nsa_phi2_v, w_out_odd, norm_final):
    b, t, _ = x_prompt.shape
    db, ts, _ = x_sample.shape
    n_pages = page_table.shape[1]
    past = n_pages * PAGE
    mp, ms = b * t, db * ts
    tm = min(1024, t)
    assert t % (PACK * NSA_BLK) == 0 and t % TK == 0 and ts <= SUB
    assert t <= NSA_BLK * 256

    pos_p = jnp.arange(t, dtype=I32)
    pos_s = past + jnp.arange(SUB, dtype=I32)
    cos_p, sin_p = _rope_tables(pos_p)
    cos_s, sin_s = _rope_tables(pos_s)
    cos_st, sin_st = jnp.tile(cos_s[:ts], (db, 1)), jnp.tile(sin_s[:ts], (db, 1))

    even_in_w, mla_w, mla_wt = _even_weights(norm_even[0], w_in_even[0], mla_g_q[0], mla_g_kv[0],
                                             mla_w_uq[0], mla_w_uk[0], mla_w_uv[0])
    s5_w = _s5_weights(s5_lambda_re[0], s5_lambda_im[0], s5_log_dt[0], s5_b_re[0], s5_b_im[0],
                       s5_c_re[0], s5_c_im[0], s5_d[0], s5_w_glu[0], s5_b_glu[0])
    wo_e = w_out_even[0].astype(BF16)
    wo_a, wo_b = wo_e[:MLA_W], wo_e[MLA_W:]

    xp2 = x_prompt.reshape(mp, D_MODEL)
    cq, rows_pt, za, u, zb = _even_in(xp2, even_in_w, cos_p, sin_p, tm, seq=t)
    mix_a = _mla_prompt(cq.reshape(b, t, MLA_QL), rows_pt, za.reshape(b, t, MLA_W), mla_wt, cos_p.T, sin_p.T)
    zeros_p = jnp.zeros((b, S5_S), F32)
    mix_b, sre_p, sim_p = _s5(u.reshape(b, t, S5_W), zb.reshape(b, t, S5_W), s5_w, zeros_p, zeros_p,
                              min(128, t))
    xp1 = _even_out(xp2, mix_a.reshape(mp, MLA_W), mix_b.reshape(mp, S5_W), wo_a, wo_b, tm)

    xs2 = x_sample.reshape(ms, D_MODEL)
    cq_s, rows_s, za_s, u_s, zb_s = _even_in(xs2, even_in_w, cos_st, sin_st, ms)
    pad3 = lambda a, n: _pad_tokens(a.reshape(db, ts, n), SUB)
    mix_a_s = _mla_decode(page_table, _rows_last(cache_mla[0]), pad3(cq_s, MLA_QL), pad3(rows_s, MLA_ROW),
                          pad3(za_s, MLA_W), mla_w, cos_s, sin_s, ts)[:, :ts]
    st = state_s5[0]
    mix_b_s, sre_s, sim_s = _s5(u_s.reshape(db, ts, S5_W), zb_s.reshape(db, ts, S5_W), s5_w,
                                st[..., 0].reshape(db, S5_S), st[..., 1].reshape(db, S5_S), ts)
    xs1 = _even_out(xs2, mix_a_s.reshape(ms, MLA_W), mix_b_s.reshape(ms, S5_W), wo_a, wo_b, ms)

    odd_in_w, cmp_w = _odd_weights(norm_odd[0], w_in_odd[0], nsa_pe_k[0], nsa_pe_v[0], nsa_phi1_k[0],
                                   nsa_phi2_k[0], nsa_phi1_v[0], nsa_phi2_v[0])
    wo_o = w_out_odd[0].astype(BF16)
    gfin = norm_final[None, :]

    q, kv4_t, kw_t, gate, z = _odd_in(xp1, odd_in_w, tm, seq=t)
    nblk_p = t // NSA_BLK
    cb = _compress_prompt(kv4_t, *cmp_w)
    q3 = q.reshape(b, t, NSA_W)
    tt = min(256, t)
    oc_t, imp = _cmp(q3, cb.reshape(b, nblk_p, 2 * NSA_KVW), pos_p[None], tt)
    sel, _ = _topk(imp, pos_p[None], nblk_p, tt)
    o_t = _nsa_prompt(q3, kv4_t, kw_t, oc_t, sel, gate.reshape(b, t, 3 * NSA_H))
    y_prompt = _odd_out_prompt(o_t, z.reshape(b, t, NSA_W), xp1.reshape(b, t, D_MODEL), wo_o, gfin, tt)

    q_s, kv4_s, kw_s, gate_s, z_s = _odd_in(xs1, odd_in_w, ms)
    cache_nsa_t = _rows_last(cache_nsa_kv[0])
    cb_s = _compress_decode(page_table, cache_nsa_t, *cmp_w)
    n_sel_s = -(-(past + ts) // NSA_BLK)
    q_s3 = q_s.reshape(db, ts, NSA_W)
    oc_ts, imp_s = _cmp(_pad_tokens(q_s3, SUB), cb_s, pos_s[None], SUB)
    imp_l = jnp.transpose(imp_s[..., :ts], (1, 2, 0, 3)).reshape(1, NSA_G, cb_s.shape[1], ms)
    pos_l = jnp.tile(pos_s[:ts], db)[None]
    _, idx_l = _topk(imp_l, pos_l, n_sel_s, ms)
    idx_s = jnp.transpose(idx_l.reshape(NSA_G, NSA_TOPK, db, ts), (2, 0, 3, 1)).reshape(db, -1)
    q_s4 = q_s.reshape(db, ts, NSA_H, NSA_D)
    o_sel_s = _nsa_sel_decode(idx_s, page_table, cache_nsa_t, q_s4, pad3(kv4_s, 4 * NSA_KVW), ts, past)
    win_t = _rows_last(state_nsa_win[0])
    kw_s3 = kw_s.reshape(db, ts, 2 * NSA_KVW)
    o_win_s = _nsa_win_decode(_pad_tokens(q_s4, SUB), win_t, _pad_tokens(kw_s3, SUB), ts)
    oc_s = jnp.transpose(oc_ts, (0, 2, 1))[:, :ts].reshape(ms, NSA_W)
    y_sample = _odd_out_decode(oc_s, o_sel_s.reshape(ms, NSA_W), o_win_s[:, :ts].reshape(ms, NSA_W),
                               gate_s, z_s, xs1, wo_o, gfin)

    state = lambda re, im, n: jnp.stack([re, im], axis=-1).reshape(1, n, S5_G, S5_N, 2)
    win_shape = (2, NSA_G, NSA_D)
    if t >= NSA_WIN:
        win_pt = kw_t[:, :, t - NSA_WIN:]
    else:
        win_pt = jnp.pad(kw_t, ((0, 0), (0, 0), (NSA_WIN - t, 0)))
    win_st = jnp.concatenate([win_t[:, :, ts:], jnp.transpose(kw_s3, (0, 2, 1))], axis=2)
    return (y_prompt, y_sample.reshape(db, ts, D_MODEL),
            _rows_second(rows_pt, (MLA_ROW,))[None], rows_s.reshape(1, db, ts, MLA_ROW),
            state(sre_p, sim_p, b), state(sre_s, sim_s, db),
            _rows_second(kv4_t, (4, NSA_G, NSA_D))[None], kv4_s.reshape(1, db, ts, 4, NSA_G, NSA_D),
            _rows_second(win_pt, win_shape)[None], _rows_second(win_st, win_shape)[None])
```

```python
import functools

import jax
import jax.numpy as jnp
import numpy as np
from jax import lax
from jax.experimental import pallas as pl
from jax.experimental.pallas import tpu as pltpu

F32, BF16, I32 = jnp.float32, jnp.bfloat16, jnp.int32

D_MODEL = 1024
PAGE = 128
EPS = 1e-6
ROPE_THETA = 10000.0
MLA_H, MLA_NOPE, MLA_ROPE, MLA_V = 8, 64, 32, 64
MLA_QL, MLA_KVL = 384, 256
MLA_ROW = MLA_KVL + MLA_ROPE
MLA_W = MLA_H * MLA_V
MLA_SCALE = (MLA_NOPE + MLA_ROPE) ** -0.5
S5_G, S5_P, S5_N = 32, 16, 64
S5_W = S5_G * S5_P
S5_S = S5_G * S5_N
S5_GT = 4
S5_NT = S5_G // S5_GT
NSA_H, NSA_G, NSA_D = 16, 2, 64
NSA_R = NSA_H // NSA_G
NSA_W = NSA_H * NSA_D
NSA_KVW = NSA_G * NSA_D
NSA_BLK, NSA_TOPK, NSA_WIN = 64, 16, 512
NSA_SCALE = NSA_D ** -0.5
FORCED_BONUS = float(NSA_R + 1)
BPP = PAGE // NSA_BLK

LOG2E = 1.4426950408889634
NEG = -1e30
GONE = -3e38
TQ = 128
TK = 256
SUB = 8
PACK = 16
MLA_DECODE_GROUP = 64
VMEM_LIMIT = 56 * 1024 * 1024


def _dot(a, b):
    return jnp.dot(a, b, preferred_element_type=F32)


def _dot_nt(a, b):
    return lax.dot_general(a, b, (((1,), (1,)), ((), ())), preferred_element_type=F32)


def _dot_tn(a, b):
    return lax.dot_general(a, b, (((0,), (0,)), ((), ())), preferred_element_type=F32)


def _rms(x, g):
    return x * lax.rsqrt(jnp.mean(x * x, axis=-1, keepdims=True) + EPS) * g


def _silu(x):
    return x * jax.nn.sigmoid(x)


def _rope_nat(x, cosf, sinf):
    half = x.shape[1] // 2
    xs = jnp.concatenate([x[:, half:], x[:, :half]], axis=1)
    return x * cosf + xs * sinf


def _params(sem):
    return pltpu.CompilerParams(dimension_semantics=sem, vmem_limit_bytes=VMEM_LIMIT)


def _const_spec(shape):
    n = len(shape)
    return pl.BlockSpec(shape, lambda *a, _n=n: (0,) * _n)


def _seq_major_spec(width, tm, seq):
    per = seq // tm
    return pl.BlockSpec((None, width, tm), lambda i: (i // per, 0, i % per))


def _even_in_kernel(feature_major, x_ref, g_ref, wcq, wckv, wkr, wza, wu, wzb, gq, gkv, cos_ref, sin_ref,
                    cq_o, rows_o, za_o, u_o, zb_o):
    h = _rms(x_ref[...], g_ref[...]).astype(BF16)
    cq_o[...] = _rms(_dot(h, wcq[...]), gq[...])
    ckv = _rms(_dot(h, wckv[...]), gkv[...])
    krope = _rope_nat(_dot(h, wkr[...]), cos_ref[...], sin_ref[...])
    if feature_major:
        rows_o[:MLA_KVL, :] = ckv.T
        rows_o[MLA_KVL:, :] = krope.T
    else:
        rows_o[:, :MLA_KVL] = ckv
        rows_o[:, MLA_KVL:] = krope
    za_o[...] = _dot(h, wza[...])
    u_o[...] = _dot(h, wu[...])
    zb_o[...] = _dot(h, wzb[...])


def _even_in(x2, wts, cosf, sinf, tm, seq=None):
    m = x2.shape[0]
    tab_blocks = cosf.shape[0] // tm
    row = lambda n: pl.BlockSpec((tm, n), lambda i: (i, 0))
    tab = pl.BlockSpec((tm, MLA_ROPE), lambda i: (i % tab_blocks, 0))
    widths = (MLA_QL, MLA_ROW, MLA_W, S5_W, S5_W)
    out_specs = [row(n) for n in widths]
    out_shape = [jax.ShapeDtypeStruct((m, n), F32) for n in widths]
    if seq is not None:
        out_specs[1] = _seq_major_spec(MLA_ROW, tm, seq)
        out_shape[1] = jax.ShapeDtypeStruct((m // seq, MLA_ROW, seq), F32)
    return pl.pallas_call(
        functools.partial(_even_in_kernel, seq is not None),
        grid=(m // tm,),
        in_specs=[row(D_MODEL)] + [_const_spec(c.shape) for c in wts] + [tab, tab],
        out_specs=out_specs,
        out_shape=out_shape,
        compiler_params=_params(("parallel",)),
        name="even_in",
    )(x2, *wts, cosf, sinf)


def _mla_queries(cq, wuqn, wuqr, wuk, cosf, sinf, h):
    cqb = cq.astype(BF16)
    qn = _dot(cqb, wuqn[h])
    ql = _dot(qn.astype(BF16), wuk[h])
    qr = _rope_nat(_dot(cqb, wuqr[h]), cosf, sinf)
    return ql, qr


def _flash_tiles(lo, hi, chains, c_exp):
    for qk, _, first_mask, _, s_buf, m_s, acc_s, l_s in chains:
        m_s[...] = jnp.full(m_s.shape, NEG, F32)
        acc_s[...] = jnp.zeros(acc_s.shape, F32)
        if l_s is not None:
            l_s[...] = jnp.zeros(l_s.shape, F32)
        s_buf[0] = first_mask(qk(lo), lo)

    def consume(chain, j, s):
        _, pv, _, _, _, m_s, acc_s, l_s = chain
        m_old = m_s[...]
        m_new = jnp.maximum(m_old, jnp.max(s, axis=0, keepdims=True))
        alpha = jnp.exp2((m_old - m_new) * c_exp)
        x = (s - m_new) * c_exp
        if l_s is None:
            p = jnp.exp2(x.astype(BF16))
        else:
            pf = jnp.exp2(x)
            l_s[...] = alpha * l_s[...] + jnp.sum(pf, axis=0, keepdims=True)
            p = pf.astype(BF16)
        acc_s[...] = alpha * acc_s[...] + pv(j, p)
        m_s[...] = m_new

    def body(j, carry):
        for chain in chains:
            chain[4][1] = chain[0](j + 1)
        for chain in chains:
            consume(chain, j, chain[4][0])
        for chain in chains:
            chain[4][0] = chain[4][1]
        return carry

    lax.fori_loop(lo, hi - 1, body, 0)
    last = hi - 1
    for chain in chains:
        consume(chain, last, chain[3](chain[4][0], last))
    outs = []
    for chain in chains:
        acc, l_s = chain[6][...], chain[7]
        if l_s is None:
            dv = acc.shape[0] - PACK
            outs.append(acc[:dv, :] / acc[dv:dv + 1, :])
        else:
            outs.append(acc / l_s[...])
    return outs


def _with_ones(v):
    return jnp.concatenate([v, jnp.ones((PACK, v.shape[1]), BF16)], axis=0)


def _mla_prompt_kernel(cq_ref, rows_ref, za_ref, wuqn_t, wuqr_t, wuk_t, wuv_t, cos_ref, sin_ref,
                       out_ref, qtl, qtr, s_buf, m_s, l_s, acc_s):
    qi = pl.program_id(1)
    r = MLA_H * TQ
    half = MLA_ROPE // 2
    cq_t = cq_ref[...].T.astype(BF16)
    qn_t = _dot(wuqn_t[...], cq_t).astype(BF16)
    qr_t = _dot(wuqr_t[...], cq_t)
    cos_t, sin_t = cos_ref[...], sin_ref[...]
    for h in range(MLA_H):
        qtl[:, h * TQ:(h + 1) * TQ] = _dot(wuk_t[h], qn_t[h * MLA_NOPE:(h + 1) * MLA_NOPE, :]).astype(BF16)
        x = qr_t[h * MLA_ROPE:(h + 1) * MLA_ROPE, :]
        xs = jnp.concatenate([x[half:, :], x[:half, :]], axis=0)
        qtr[:, h * TQ:(h + 1) * TQ] = (x * cos_t + xs * sin_t).astype(BF16)
    qpos = qi * TQ + (lax.broadcasted_iota(I32, (1, r), 1) & (TQ - 1))
    krow = lax.broadcasted_iota(I32, (TK, 1), 0)

    def keys(j):
        return rows_ref[:, pl.ds(pl.multiple_of(j * TK, TK), TK)]

    def qk(j):
        kt = keys(j)
        return (_dot_tn(kt[:MLA_KVL, :].astype(BF16), qtl[...])
                + _dot_tn(kt[MLA_KVL:, :].astype(BF16), qtr[...]))

    def pv(j, p):
        return _dot(keys(j)[:MLA_KVL, :].astype(BF16), p)

    def causal(s, j):
        return jnp.where(j * TK + krow <= qpos, s, NEG)

    n_tiles = (qi * TQ + TQ + TK - 1) // TK
    chain = (qk, pv, lambda s, j: s, causal, s_buf, m_s, acc_s, l_s)
    o = _flash_tiles(0, n_tiles, [chain], MLA_SCALE * LOG2E)[0].astype(BF16)
    heads = [_dot(wuv_t[h], o[:, h * TQ:(h + 1) * TQ]) for h in range(MLA_H)]
    out_ref[...] = jnp.concatenate(heads, axis=0).T * _silu(za_ref[...])


def _mla_prompt(cq, rows_t, za, wts, cos_t, sin_t):
    b, t, _ = cq.shape
    r = MLA_H * TQ
    tok = lambda n: pl.BlockSpec((None, TQ, n), lambda bi, qi: (bi, qi, 0))
    tab = pl.BlockSpec((MLA_ROPE, TQ), lambda bi, qi: (0, qi))
    return pl.pallas_call(
        _mla_prompt_kernel,
        grid=(b, t // TQ),
        in_specs=[tok(MLA_QL), pl.BlockSpec((None, MLA_ROW, t), lambda bi, qi: (bi, 0, 0)), tok(MLA_W)]
        + [_const_spec(w.shape) for w in wts] + [tab, tab],
        out_specs=tok(MLA_W),
        out_shape=jax.ShapeDtypeStruct((b, t, MLA_W), F32),
        scratch_shapes=[pltpu.VMEM((MLA_KVL, r), BF16), pltpu.VMEM((MLA_ROPE, r), BF16),
                        pltpu.VMEM((2, TK, r), F32),
                        pltpu.VMEM((1, r), F32), pltpu.VMEM((1, r), F32), pltpu.VMEM((MLA_KVL, r), F32)],
        compiler_params=_params(("parallel", "arbitrary")),
        name="mla_prompt",
    )(cq, rows_t, za, *wts, cos_t, sin_t)


def _mla_decode_kernel(n_pages_step, ts, pt_ref, cq_ref, rows_ref, za_ref, wuqn, wuqr, wuk, wuv,
                       cos_ref, sin_ref, *rest):
    pages = rest[:n_pages_step]
    out_ref, ql_s, qr_s, m_s, l_s, acc_s = rest[n_pages_step:]
    gi = pl.program_id(1)
    rows = MLA_H * SUB

    @pl.when(gi == 0)
    def _():
        cq = cq_ref[...]
        cosf, sinf = cos_ref[...], sin_ref[...]
        for h in range(MLA_H):
            ql, qr = _mla_queries(cq, wuqn, wuqr, wuk, cosf, sinf, h)
            ql_s[h * SUB:(h + 1) * SUB, :] = ql.astype(BF16)
            qr_s[h * SUB:(h + 1) * SUB, :] = qr.astype(BF16)
        m_s[...] = jnp.full((rows, 1), NEG, F32)
        l_s[...] = jnp.zeros((rows, 1), F32)
        acc_s[...] = jnp.zeros((rows, MLA_KVL), F32)

    def update(s, vals, feature_major):
        m_old = m_s[...]
        m_new = jnp.maximum(m_old, jnp.max(s, axis=1, keepdims=True))
        alpha = jnp.exp(m_old - m_new)
        p = jnp.exp(s - m_new)
        l_s[...] = alpha * l_s[...] + jnp.sum(p, axis=1, keepdims=True)
        pv = None
        for (lo, hi), v in vals:
            pj = p[:, lo:hi].astype(BF16)
            term = _dot_nt(pj, v) if feature_major else _dot(pj, v)
            pv = term if pv is None else pv + term
        acc_s[...] = alpha * acc_s[...] + pv
        m_s[...] = m_new

    ql, qr = ql_s[...], qr_s[...]
    scores, vals = [], []
    for j, pg in enumerate(pages):
        k = pg[...]
        ckv = k[:MLA_KVL, :].astype(BF16)
        scores.append(_dot(ql, ckv) + _dot(qr, k[MLA_KVL:, :].astype(BF16)))
        vals.append(((len(vals) * PAGE, (len(vals) + 1) * PAGE), ckv))
        if len(vals) == MLA_DECODE_GROUP or j == n_pages_step - 1:
            update(jnp.concatenate(scores, axis=1) * MLA_SCALE, vals, True)
            scores, vals = [], []

    @pl.when(gi == pl.num_programs(1) - 1)
    def _():
        kn = rows_ref[...]
        ckv = kn[:, :MLA_KVL].astype(BF16)
        s = (_dot_nt(ql, ckv) + _dot_nt(qr, kn[:, MLA_KVL:].astype(BF16))) * MLA_SCALE
        tq = lax.broadcasted_iota(I32, (rows, SUB), 0) & (SUB - 1)
        jk = lax.broadcasted_iota(I32, (rows, SUB), 1)
        s = jnp.where((jk <= tq) & (jk < ts), s, NEG)
        update(s, [((0, SUB), ckv)], False)
        o = (acc_s[...] / l_s[...]).astype(BF16)
        heads = [_dot(o[h * SUB:(h + 1) * SUB, :], wuv[h]) for h in range(MLA_H)]
        out_ref[...] = jnp.concatenate(heads, axis=1) * _silu(za_ref[...])


def _mla_decode(page_table, cache_t, cq, rows, za, wts, cosf, sinf, ts, n_pages_step=64):
    db, n_pages = page_table.shape
    rows_n = MLA_H * SUB
    tok = lambda n: pl.BlockSpec((None, SUB, n), lambda b, g, pt: (b, 0, 0))
    cst = lambda shape: pl.BlockSpec(shape, lambda b, g, pt, _n=len(shape): (0,) * _n)
    page_specs = [
        pl.BlockSpec((None, MLA_ROW, PAGE), lambda b, g, pt, j=j: (pt[b, g * n_pages_step + j], 0, 0))
        for j in range(n_pages_step)]
    grid_spec = pltpu.PrefetchScalarGridSpec(
        num_scalar_prefetch=1,
        grid=(db, n_pages // n_pages_step),
        in_specs=[tok(MLA_QL), tok(MLA_ROW), tok(MLA_W)] + [cst(w.shape) for w in wts]
        + [cst(cosf.shape), cst(sinf.shape)] + page_specs,
        out_specs=tok(MLA_W),
        scratch_shapes=[pltpu.VMEM((rows_n, MLA_KVL), BF16), pltpu.VMEM((rows_n, MLA_ROPE), BF16),
                        pltpu.VMEM((rows_n, 1), F32), pltpu.VMEM((rows_n, 1), F32),
                        pltpu.VMEM((rows_n, MLA_KVL), F32)])
    return pl.pallas_call(
        functools.partial(_mla_decode_kernel, n_pages_step, ts),
        grid_spec=grid_spec,
        out_shape=jax.ShapeDtypeStruct((db, SUB, MLA_W), F32),
        compiler_params=_params(("parallel", "arbitrary")),
        name="mla_decode",
    )(page_table, cq, rows, za, *wts, cosf, sinf, *([cache_t] * n_pages_step))


def _s5_kernel(u_ref, zb_ref, lre_ref, lim_ref, ldt_ref, wbre, wbim, wcre, wcim, d_ref, wglu, bglu,
               h0re_ref, h0im_ref, mix_o, sre_o, sim_o, bure, buim, hre, him):
    step = pl.program_id(0)
    nb, chunk, _ = u_ref.shape
    rows = nb * chunk

    @pl.when(step == 0)
    def _():
        hre[...] = h0re_ref[...]
        him[...] = h0im_ref[...]

    lre, lim = lre_ref[...], lim_ref[...]
    dt = jnp.exp(ldt_ref[...])
    mag = jnp.exp(lre * dt)
    are, aim = mag * jnp.cos(lim * dt), mag * jnp.sin(lim * dt)
    den = lre * lre + lim * lim
    cre = ((are - 1.0) * lre + aim * lim) / den
    cim = (aim * lre - (are - 1.0) * lim) / den

    ut = jnp.swapaxes(u_ref[...], 0, 1).reshape(rows, S5_W)
    kw, nw = S5_GT * S5_P, S5_GT * S5_N
    for jt in range(S5_NT):
        cr, ci = cre[:, jt * nw:(jt + 1) * nw], cim[:, jt * nw:(jt + 1) * nw]
        bre = (cr * wbre[jt] - ci * wbim[jt]).astype(BF16)
        bim = (cr * wbim[jt] + ci * wbre[jt]).astype(BF16)
        uj = ut[:, jt * kw:(jt + 1) * kw].astype(BF16)
        bure[:, jt * nw:(jt + 1) * nw] = _dot(uj, bre)
        buim[:, jt * nw:(jt + 1) * nw] = _dot(uj, bim)

    def scan(t, carry):
        hr, hi = carry
        sl = pl.ds(pl.multiple_of(t * nb, SUB), nb)
        nr = are * hr - aim * hi + bure[sl, :]
        ni = are * hi + aim * hr + buim[sl, :]
        bure[sl, :] = nr
        buim[sl, :] = ni
        return nr, ni

    hr, hi = lax.fori_loop(0, chunk, scan, (hre[...], him[...]))
    hre[...] = hr
    him[...] = hi
    sre_o[...] = hr
    sim_o[...] = hi

    ys = []
    for jt in range(S5_NT):
        sr = bure[:, jt * nw:(jt + 1) * nw].astype(BF16)
        si = buim[:, jt * nw:(jt + 1) * nw].astype(BF16)
        ys.append(_dot(sr, wcre[jt].astype(BF16)) - _dot(si, wcim[jt].astype(BF16)))
    y = jnp.concatenate(ys, axis=1) + d_ref[...] * ut
    g5 = jax.nn.gelu(y)
    ob = g5 * jax.nn.sigmoid(_dot(g5.astype(BF16), wglu[...]) + bglu[...])
    mix_o[...] = jnp.swapaxes(ob.reshape(chunk, nb, S5_W), 0, 1) * _silu(zb_ref[...])


def _s5(u, zb, wts, h0re, h0im, chunk):
    nb, t, _ = u.shape
    tok = pl.BlockSpec((nb, chunk, S5_W), lambda i: (0, i, 0))
    st = _const_spec((nb, S5_S))
    return pl.pallas_call(
        _s5_kernel,
        grid=(t // chunk,),
        in_specs=[tok, tok] + [_const_spec(w.shape) for w in wts] + [st, st],
        out_specs=[tok, st, st],
        out_shape=[jax.ShapeDtypeStruct((nb, t, S5_W), F32), jax.ShapeDtypeStruct((nb, S5_S), F32),
                   jax.ShapeDtypeStruct((nb, S5_S), F32)],
        scratch_shapes=[pltpu.VMEM((nb * chunk, S5_S), F32), pltpu.VMEM((nb * chunk, S5_S), F32),
                        pltpu.VMEM((nb, S5_S), F32), pltpu.VMEM((nb, S5_S), F32)],
        compiler_params=_params(("arbitrary",)),
        name="s5",
    )(u, zb, *wts, h0re, h0im)


def _even_out_kernel(x_ref, a_ref, b_ref, wa, wb, out_ref):
    out_ref[...] = (x_ref[...] + _dot(a_ref[...].astype(BF16), wa[...])
                    + _dot(b_ref[...].astype(BF16), wb[...]))


def _even_out(x2, mixa, mixb, wa, wb, tm):
    m = x2.shape[0]
    row = lambda n: pl.BlockSpec((tm, n), lambda i: (i, 0))
    return pl.pallas_call(
        _even_out_kernel,
        grid=(m // tm,),
        in_specs=[row(D_MODEL), row(MLA_W), row(S5_W), _const_spec(wa.shape), _const_spec(wb.shape)],
        out_specs=row(D_MODEL),
        out_shape=jax.ShapeDtypeStruct((m, D_MODEL), F32),
        compiler_params=_params(("parallel",)),
        name="even_out",
    )(x2, mixa, mixb, wa, wb)


def _odd_in_kernel(feature_major, x_ref, g_ref, wq, wkv4, wkw, wg, wz, q_o, kv4_o, kw_o, gate_o, z_o):
    h = _rms(x_ref[...], g_ref[...]).astype(BF16)
    q_o[...] = _dot(h, wq[...])
    kv4, kw = _dot(h, wkv4[...]), _dot(h, wkw[...])
    kv4_o[...] = kv4.T if feature_major else kv4
    kw_o[...] = kw.T if feature_major else kw
    gate_o[...] = jax.nn.sigmoid(_dot(h, wg[...]))
    z_o[...] = _dot(h, wz[...])


def _odd_in(x2, wts, tm, seq=None):
    m = x2.shape[0]
    row = lambda n: pl.BlockSpec((tm, n), lambda i: (i, 0))
    widths = (NSA_W, 4 * NSA_KVW, 2 * NSA_KVW, 3 * NSA_H, NSA_W)
    out_specs = [row(n) for n in widths]
    out_shape = [jax.ShapeDtypeStruct((m, n), F32) for n in widths]
    if seq is not None:
        for i in (1, 2):
            out_specs[i] = _seq_major_spec(widths[i], tm, seq)
            out_shape[i] = jax.ShapeDtypeStruct((m // seq, widths[i], seq), F32)
    return pl.pallas_call(
        functools.partial(_odd_in_kernel, seq is not None),
        grid=(m // tm,),
        in_specs=[row(D_MODEL)] + [_const_spec(w.shape) for w in wts],
        out_specs=out_specs,
        out_shape=out_shape,
        compiler_params=_params(("parallel",)),
        name="odd_in",
    )(x2, *wts)


def _compress_stage(xs, x_t, pe, base):
    n = x_t.shape[1] // NSA_BLK
    x = x_t.astype(BF16).T.reshape(n, NSA_BLK, 2 * NSA_KVW) + pe.astype(BF16)[None]
    xs[:, pl.ds(pl.multiple_of(base, PACK), n), :] = jnp.swapaxes(x, 0, 1)


def _compress_weights(w1, p1k, p1v):
    w1[...] = jnp.zeros(w1.shape, BF16)
    for s, ref in enumerate((p1k, p1k, p1v, p1v)):
        w1[:, s * NSA_D:(s + 1) * NSA_D, s * NSA_D:(s + 1) * NSA_D] = ref[...].astype(BF16)


def _compress_finish(xs, w1, phi2, out_ref):
    acc = jnp.zeros((xs.shape[1], 2 * NSA_KVW), F32)
    for r in range(NSA_BLK):
        acc = acc + _dot(xs[r], w1[r])
    out_ref[...] = _dot(_silu(acc).astype(BF16), phi2[...])


def _compress_prompt_kernel(x_ref, pe_ref, p1k, p1v, phi2, out_ref, xs, w1):
    bi = pl.program_id(0)
    nblk = x_ref.shape[1] // NSA_BLK
    _compress_stage(xs, x_ref[...], pe_ref[...], bi * nblk)

    @pl.when(bi == pl.num_programs(0) - 1)
    def _():
        _compress_weights(w1, p1k, p1v)
        _compress_finish(xs, w1, phi2, out_ref)


def _compress_prompt(kv4_t, pe, p1k, p1v, phi2):
    b, _, t = kv4_t.shape
    cw = 2 * NSA_KVW
    nblk = b * (t // NSA_BLK)
    consts = (pe, p1k, p1v, phi2)
    return pl.pallas_call(
        _compress_prompt_kernel,
        grid=(b,),
        in_specs=[pl.BlockSpec((None, cw, t), lambda i: (i, 0, 0))] + [_const_spec(c.shape) for c in consts],
        out_specs=pl.BlockSpec((nblk, cw), lambda i: (0, 0)),
        out_shape=jax.ShapeDtypeStruct((nblk, cw), F32),
        scratch_shapes=[pltpu.VMEM((NSA_BLK, nblk, cw), BF16), pltpu.VMEM((NSA_BLK, cw, cw), BF16)],
        compiler_params=_params(("arbitrary",)),
        name="compress_prompt",
    )(kv4_t, *consts)


def _compress_decode_kernel(n_pages_step, pt_ref, pe_ref, p1k, p1v, phi2, *rest):
    pages = rest[:n_pages_step]
    out_ref, xs, w1 = rest[n_pages_step:]
    bi, gi = pl.program_id(0), pl.program_id(1)
    grp = PACK // BPP
    pe = pe_ref[...]

    @pl.when((bi == 0) & (gi == 0))
    def _():
        _compress_weights(w1, p1k, p1v)

    for k in range(n_pages_step // grp):
        x_t = jnp.concatenate([pages[k * grp + j][...] for j in range(grp)], axis=1)
        _compress_stage(xs, x_t, pe, (gi * (n_pages_step // grp) + k) * PACK)

    @pl.when(gi == pl.num_programs(1) - 1)
    def _():
        _compress_finish(xs, w1, phi2, out_ref)


def _compress_decode(page_table, cache_t, pe, p1k, p1v, phi2, n_pages_step=32):
    db, n_pages = page_table.shape
    cw = 2 * NSA_KVW
    nblk = n_pages * BPP
    cst = lambda shape: pl.BlockSpec(shape, lambda b, g, pt, _n=len(shape): (0,) * _n)
    consts = (pe, p1k, p1v, phi2)
    page_specs = [
        pl.BlockSpec((None, cw, PAGE), lambda b, g, pt, j=j: (pt[b, g * n_pages_step + j], 0, 0))
        for j in range(n_pages_step)]
    grid_spec = pltpu.PrefetchScalarGridSpec(
        num_scalar_prefetch=1,
        grid=(db, n_pages // n_pages_step),
        in_specs=[cst(c.shape) for c in consts] + page_specs,
        out_specs=pl.BlockSpec((None, nblk, cw), lambda b, g, pt: (b, 0, 0)),
        scratch_shapes=[pltpu.VMEM((NSA_BLK, nblk, cw), BF16), pltpu.VMEM((NSA_BLK, cw, cw), BF16)])
    return pl.pallas_call(
        functools.partial(_compress_decode_kernel, n_pages_step),
        grid_spec=grid_spec,
        out_shape=jax.ShapeDtypeStruct((db, nblk, cw), F32),
        compiler_params=_params(("arbitrary", "arbitrary")),
        name="compress_decode",
    )(page_table, *consts, *([cache_t] * n_pages_step))


def _alibi_slope(h):
    return 2.0 ** (-8.0 * (h + 1) / NSA_H)


def _split_bf16(x):
    hi = x.astype(BF16)
    return hi, (x - hi.astype(F32)).astype(BF16)


def _cmp_kernel(q_ref, cb_ref, pos_ref, oc_o, imp_o, s_scr):
    tt = q_ref.shape[0]
    nc = cb_ref.shape[0]
    qpos = pos_ref[...]
    cpos = lax.broadcasted_iota(I32, (nc, 1), 0) * NSA_BLK + (NSA_BLK - 1)
    visible = cpos <= qpos
    dist = (qpos - cpos).astype(F32)
    q = q_ref[...] * NSA_SCALE
    cb = cb_ref[...]
    for g in range(NSA_G):
        k_hi, k_lo = _split_bf16(cb[:, g * NSA_D:(g + 1) * NSA_D])
        vc = cb[:, NSA_KVW + g * NSA_D:NSA_KVW + (g + 1) * NSA_D].astype(BF16)
        for i in range(NSA_R):
            h = g * NSA_R + i
            q_hi, q_lo = _split_bf16(q[:, h * NSA_D:(h + 1) * NSA_D])
            s = _dot_nt(k_hi, q_hi) + _dot_nt(k_hi, q_lo) + _dot_nt(k_lo, q_hi)
            s_scr[i * nc:(i + 1) * nc, :] = s - _alibi_slope(h) * dist
        s3 = jnp.where(visible[None], s_scr[...].reshape(NSA_R, nc, tt), NEG)
        mx = jnp.max(s3, axis=1, keepdims=True)
        e = jnp.where(visible[None], jnp.exp(s3 - mx), 0.0)
        den = jnp.sum(e, axis=1, keepdims=True)
        p = e / jnp.where(den > 0, den, 1.0)
        imp_o[g] = jnp.sum(p, axis=0)
        for i in range(NSA_R):
            h = g * NSA_R + i
            oc_o[h * NSA_D:(h + 1) * NSA_D, :] = _dot_tn(vc, p[i].astype(BF16))


def _cmp(q, cb, pos, tt):
    b, t, _ = q.shape
    nc = cb.shape[1]
    return pl.pallas_call(
        _cmp_kernel,
        grid=(b, t // tt),
        in_specs=[pl.BlockSpec((None, tt, NSA_W), lambda bi, ti: (bi, ti, 0)),
                  pl.BlockSpec((None, nc, 2 * NSA_KVW), lambda bi, ti: (bi, 0, 0)),
                  pl.BlockSpec((1, tt), lambda bi, ti: (0, ti))],
        out_specs=[pl.BlockSpec((None, NSA_W, tt), lambda bi, ti: (bi, 0, ti)),
                   pl.BlockSpec((None, NSA_G, nc, tt), lambda bi, ti: (bi, 0, 0, ti))],
        out_shape=[jax.ShapeDtypeStruct((b, NSA_W, t), F32),
                   jax.ShapeDtypeStruct((b, NSA_G, nc, t), F32)],
        scratch_shapes=[pltpu.VMEM((NSA_R * nc, tt), F32)],
        compiler_params=_params(("parallel", "parallel")),
        name="nsa_cmp",
    )(q, cb, pos)


def _topk_kernel(n_sel, imp_ref, pos_ref, sel_o, idx_o):
    _, nc, tt = imp_ref.shape
    nsp = sel_o.shape[1]
    qpos = pos_ref[...]
    blk = lax.broadcasted_iota(I32, (nsp, 1), 0)
    cur = jnp.right_shift(qpos, NSA_BLK.bit_length() - 1)
    forced = (blk == 0) | (blk == cur) | (blk == cur - 1)
    allowed = (blk <= cur) & (blk < n_sel)
    for g in range(NSA_G):
        imp = imp_ref[g]
        if nsp > nc:
            imp = jnp.concatenate([imp, jnp.zeros((nsp - nc, tt), F32)], axis=0)
        score = jnp.where(allowed, imp + jnp.where(forced, FORCED_BONUS, 0.0), NEG)
        chosen = jnp.zeros((nsp, tt), F32)
        picks = []
        for _ in range(NSA_TOPK):
            mx = jnp.max(score, axis=0, keepdims=True)
            first = jnp.min(jnp.where(score == mx, blk, nsp), axis=0, keepdims=True)
            hit = blk == first
            chosen = jnp.where(hit, 1.0, chosen)
            score = jnp.where(hit, GONE, score)
            picks.append(first)
        sel_o[g] = chosen
        idx_o[g] = jnp.concatenate(picks, axis=0)


def _topk(imp, pos, n_sel, tt):
    b, _, nc, t = imp.shape
    nsp = -(-(n_sel + TK // NSA_BLK) // SUB) * SUB
    return pl.pallas_call(
        functools.partial(_topk_kernel, n_sel),
        grid=(b, t // tt),
        in_specs=[pl.BlockSpec((None, NSA_G, nc, tt), lambda bi, ti: (bi, 0, 0, ti)),
                  pl.BlockSpec((1, tt), lambda bi, ti: (0, ti))],
        out_specs=[pl.BlockSpec((None, NSA_G, nsp, tt), lambda bi, ti: (bi, 0, 0, ti)),
                   pl.BlockSpec((None, NSA_G, NSA_TOPK, tt), lambda bi, ti: (bi, 0, 0, ti))],
        out_shape=[jax.ShapeDtypeStruct((b, NSA_G, nsp, t), F32),
                   jax.ShapeDtypeStruct((b, NSA_G, NSA_TOPK, t), I32)],
        compiler_params=_params(("parallel", "parallel")),
        name="nsa_topk",
    )(imp, pos)


def _bf16_parts(x, n):
    parts = []
    for _ in range(n):
        bits = np.asarray(x, np.float32).view(np.uint32)
        top = ((bits + np.uint32(0x7FFF) + ((bits >> np.uint32(16)) & np.uint32(1)))
               & np.uint32(0xFFFF0000)).view(np.float32)
        parts.append(float(top))
        x = float(np.float32(x) - top)
    return parts


def _nsa_prompt_kernel(q_ref, kv4_ref, kw_ref, oc_ref, sel_ref, gate_ref, out_ref,
                       k_aug, q_aug, s_buf, m_s, acc_s):
    qi = pl.program_id(1)
    r = NSA_R * TQ
    bpt = TK // NSA_BLK
    n_parts = SUB // 2
    qt = (q_ref[...] * NSA_SCALE).T.astype(BF16)
    gt = gate_ref[...].T
    lane = lax.broadcasted_iota(I32, (1, r), 1)
    qpos = qi * TQ + (lane & (TQ - 1))
    head = lane // TQ
    krow = lax.broadcasted_iota(I32, (TK, 1), 0)
    klane = lax.broadcasted_iota(I32, (1, TK), 1)
    row8 = lax.broadcasted_iota(I32, (SUB, 1), 0)
    n_tiles = (qi * TQ + TQ + TK - 1) // TK
    onehot = jnp.where(row8 == jnp.right_shift(klane, NSA_BLK.bit_length() - 1), 1.0, 0.0)

    def key_rows(j):
        rel = j * TK - qi * TQ + klane
        coarse = (rel & -NSA_BLK).astype(F32)
        fine = (rel & (NSA_BLK - 1)).astype(F32)
        alibi = jnp.where((row8 & 1) == 0, coarse, fine)
        return jnp.concatenate([alibi, onehot], axis=0).astype(BF16)

    def causal(s, j):
        return jnp.where(j * TK + krow <= qpos, s, NEG)

    def in_window(s, j):
        return jnp.where(qpos - (j * TK + krow) <= NSA_WIN, s, NEG)

    def tile(ref, lo, j):
        return ref[lo:lo + NSA_D, pl.ds(pl.multiple_of(j * TK, TK), TK)].astype(BF16)

    slopes, sel_chains, win_chains = [], [], []
    for g in range(NSA_G):
        kg, qa = k_aug.at[g], q_aug.at[g]
        qa[:NSA_D, :] = jnp.concatenate(
            [qt[(g * NSA_R + i) * NSA_D:(g * NSA_R + i + 1) * NSA_D, :] for i in range(NSA_R)], axis=1)
        slope8 = jnp.zeros((SUB, r), F32)
        for i in range(NSA_R):
            parts = _bf16_parts(_alibi_slope(g * NSA_R + i), n_parts)
            col = jnp.zeros((SUB, 1), F32)
            for k, part in enumerate(parts):
                col = jnp.where(jnp.right_shift(row8, 1) == k, part, col)
            slope8 = jnp.where(head == i, col, slope8)
        slopes.append(slope8)
        ks_lo, vs_lo = 2 * NSA_KVW + g * NSA_D, 3 * NSA_KVW + g * NSA_D
        kw_lo, vw_lo = g * NSA_D, NSA_KVW + g * NSA_D

        def qk_sel(j, g=g, kg=kg, qa=qa, ks_lo=ks_lo, slope8=slope8):
            kg[:NSA_D, :] = tile(kv4_ref, ks_lo, j)
            kg[NSA_D:, :] = key_rows(j)
            flags = sel_ref[g, pl.ds(j * bpt, SUB), :]
            off = jnp.where(flags > 0.5, 0.0, NEG)
            qa[NSA_D:, :] = jnp.concatenate(
                [slope8, jnp.concatenate([off] * NSA_R, axis=1)], axis=0).astype(BF16)
            return _dot_tn(kg[...], qa[...])

        def qk_win(j, kg=kg, qa=qa, kw_lo=kw_lo):
            kg[:NSA_D, :] = tile(kw_ref, kw_lo, j)
            kg[NSA_D:, :] = key_rows(j)
            return _dot_tn(kg[...], qa[...])

        state = (s_buf.at[g], m_s.at[g], acc_s.at[g], None)
        sel_chains.append((qk_sel, lambda j, p, lo=vs_lo: _dot(_with_ones(tile(kv4_ref, lo, j)), p),
                           lambda s, j: s, causal) + state)
        win_chains.append((qk_win, lambda j, p, lo=vw_lo: _dot(_with_ones(tile(kw_ref, lo, j)), p),
                           in_window, lambda s, j: causal(in_window(s, j), j)) + state)

    o_sels = _flash_tiles(0, n_tiles, sel_chains, LOG2E)
    for g in range(NSA_G):
        q_aug[g, NSA_D:, :] = jnp.concatenate([slopes[g], jnp.zeros((SUB, r), F32)], axis=0).astype(BF16)
    o_wins = _flash_tiles(jnp.maximum(qi * TQ - NSA_WIN, 0) // TK, n_tiles, win_chains, LOG2E)
    for g in range(NSA_G):
        o_sel, o_win = o_sels[g], o_wins[g]
        for i in range(NSA_R):
            h = g * NSA_R + i
            rows = slice(h * NSA_D, (h + 1) * NSA_D)
            cols = slice(i * TQ, (i + 1) * TQ)
            out_ref[rows, :] = (gt[h:h + 1, :] * oc_ref[rows, :]
                                + gt[NSA_H + h:NSA_H + h + 1, :] * o_sel[:, cols]
                                + gt[2 * NSA_H + h:2 * NSA_H + h + 1, :] * o_win[:, cols])


def _nsa_prompt(q, kv4_t, kw_t, oc_t, sel, gate):
    b, t, _ = q.shape
    nsp = sel.shape[2]
    r = NSA_R * TQ
    tok = lambda n: pl.BlockSpec((None, TQ, n), lambda bi, qi: (bi, qi, 0))
    full = lambda n: pl.BlockSpec((None, n, t), lambda bi, qi: (bi, 0, 0))
    return pl.pallas_call(
        _nsa_prompt_kernel,
        grid=(b, t // TQ),
        in_specs=[tok(NSA_W), full(4 * NSA_KVW), full(2 * NSA_KVW),
                  pl.BlockSpec((None, NSA_W, TQ), lambda bi, qi: (bi, 0, qi)),
                  pl.BlockSpec((None, NSA_G, nsp, TQ), lambda bi, qi: (bi, 0, 0, qi)),
                  tok(3 * NSA_H)],
        out_specs=pl.BlockSpec((None, NSA_W, TQ), lambda bi, qi: (bi, 0, qi)),
        out_shape=jax.ShapeDtypeStruct((b, NSA_W, t), F32),
        scratch_shapes=[pltpu.VMEM((NSA_G, NSA_D + 2 * SUB, TK), BF16),
                        pltpu.VMEM((NSA_G, NSA_D + 2 * SUB, r), BF16),
                        pltpu.VMEM((NSA_G, 2, TK, r), F32),
                        pltpu.VMEM((NSA_G, 1, r), F32),
                        pltpu.VMEM((NSA_G, NSA_D + PACK, r), F32)],
        compiler_params=_params(("parallel", "arbitrary")),
        name="nsa_prompt",
    )(q, kv4_t, kw_t, oc_t, sel, gate)


def _softmax_rows(parts):
    mx = None
    for s, _, _ in parts:
        pm = jnp.max(s, axis=1, keepdims=True)
        mx = pm if mx is None else jnp.maximum(mx, pm)
    den, num = None, None
    for s, v, feature_major in parts:
        p = jnp.exp(s - mx)
        d = jnp.sum(p, axis=1, keepdims=True)
        n = _dot_nt(p.astype(BF16), v) if feature_major else _dot(p.astype(BF16), v)
        den = d if den is None else den + d
        num = n if num is None else num + n
    return num / den


def _nsa_sel_decode_kernel(ts, pos0, nbp, idx_ref, pid_ref, q_ref, new_ref, *rest):
    nblk = NSA_G * NSA_TOPK
    pages = rest[:nblk]
    out_ref = rest[nblk]
    b, t = pl.program_id(0), pl.program_id(1)
    q = q_ref[...] * NSA_SCALE
    new = new_ref[...]
    lane = lax.broadcasted_iota(I32, (1, PAGE), 1)
    jn = lax.broadcasted_iota(I32, (1, SUB), 1)
    for g in range(NSA_G):
        qg = q[g * NSA_R:(g + 1) * NSA_R, :].astype(BF16)
        slope = jnp.concatenate(
            [jnp.full((1, 1), _alibi_slope(g * NSA_R + i), F32) for i in range(NSA_R)], axis=0)
        parts = []
        has_new = jnp.zeros((), I32)
        for k in range(NSA_TOPK):
            bid = idx_ref[((b * NSA_G + g) * ts + t) * NSA_TOPK + k]
            pg = pages[g * NSA_TOPK + k][...]
            kk = pg[g * NSA_D:(g + 1) * NSA_D, :].astype(BF16)
            vv = pg[NSA_KVW + g * NSA_D:NSA_KVW + (g + 1) * NSA_D, :].astype(BF16)
            sub = bid % BPP
            rel = ((bid - sub) * NSA_BLK - pos0 + lane).astype(F32)
            s = _dot(qg, kk) + slope * rel
            mine = (lane // NSA_BLK == sub) & (bid < nbp)
            parts.append((jnp.where(mine, s, NEG), vv, True))
            has_new = has_new + (bid >= nbp).astype(I32)
        kn = new[:, 2 * NSA_KVW + g * NSA_D:2 * NSA_KVW + (g + 1) * NSA_D].astype(BF16)
        vn = new[:, 3 * NSA_KVW + g * NSA_D:3 * NSA_KVW + (g + 1) * NSA_D].astype(BF16)
        sn = _dot_nt(qg, kn) + slope * jn.astype(F32)
        ok = (jn <= t) & (jn < ts) & (has_new > 0)
        parts.append((jnp.where(ok, sn, NEG), vn, False))
        out_ref[g * NSA_R:(g + 1) * NSA_R, :] = _softmax_rows(parts)


def _nsa_sel_decode(idx, page_table, cache_t, q4, kv4_new, ts, pos0):
    db, n_pages = page_table.shape
    nbp = n_pages * BPP
    cw = 2 * NSA_KVW
    page_ids = jnp.take_along_axis(page_table, jnp.minimum(idx, nbp - 1) // BPP, axis=1).reshape(-1)
    idx_flat = idx.reshape(-1)

    def page_map(b, t, idx_r, pid, g, k):
        return (pid[((b * NSA_G + g) * ts + t) * NSA_TOPK + k], 1, 0)

    page_specs = [pl.BlockSpec((None, cw, PAGE), functools.partial(page_map, g=g, k=k))
                  for g in range(NSA_G) for k in range(NSA_TOPK)]
    grid_spec = pltpu.PrefetchScalarGridSpec(
        num_scalar_prefetch=2,
        grid=(db, ts),
        in_specs=[pl.BlockSpec((None, None, NSA_H, NSA_D), lambda b, t, idx, pt: (b, t, 0, 0)),
                  pl.BlockSpec((None, SUB, 4 * NSA_KVW), lambda b, t, idx, pt: (b, 0, 0))] + page_specs,
        out_specs=pl.BlockSpec((None, None, NSA_H, NSA_D), lambda b, t, idx, pt: (b, t, 0, 0)))
    return pl.pallas_call(
        functools.partial(_nsa_sel_decode_kernel, ts, pos0, nbp),
        grid_spec=grid_spec,
        out_shape=jax.ShapeDtypeStruct((db, ts, NSA_H, NSA_D), F32),
        compiler_params=_params(("parallel", "arbitrary")),
        name="nsa_sel_decode",
    )(idx_flat, page_ids, q4, kv4_new, *([cache_t] * (NSA_G * NSA_TOPK)))


def _nsa_win_decode_kernel(ts, q_ref, win_ref, new_ref, out_ref, win_o):
    rows = NSA_R * SUB
    win = win_ref[...]
    new = new_ref[...]
    win_o[...] = jnp.concatenate([win[:, ts:], new.T[:, :ts]], axis=1)
    tq = lax.broadcasted_iota(I32, (rows, 1), 0) & (SUB - 1)
    iw = lax.broadcasted_iota(I32, (1, NSA_WIN), 1)
    jn = lax.broadcasted_iota(I32, (1, SUB), 1)
    for g in range(NSA_G):
        qg = jnp.concatenate([q_ref[:, g * NSA_R + i, :] for i in range(NSA_R)], axis=0)
        qg = (qg * NSA_SCALE).astype(BF16)
        slope = jnp.concatenate(
            [jnp.full((SUB, 1), _alibi_slope(g * NSA_R + i), F32) for i in range(NSA_R)], axis=0)
        kw = win[g * NSA_D:(g + 1) * NSA_D, :].astype(BF16)
        vw = win[NSA_KVW + g * NSA_D:NSA_KVW + (g + 1) * NSA_D, :].astype(BF16)
        kn = new[:, g * NSA_D:(g + 1) * NSA_D].astype(BF16)
        vn = new[:, NSA_KVW + g * NSA_D:NSA_KVW + (g + 1) * NSA_D].astype(BF16)
        sw = _dot(qg, kw) + slope * (iw - NSA_WIN).astype(F32)
        sw = jnp.where(iw >= tq, sw, NEG)
        sn = _dot_nt(qg, kn) + slope * jn.astype(F32)
        sn = jnp.where((jn <= tq) & (jn < ts), sn, NEG)
        o = _softmax_rows([(sw, vw, True), (sn, vn, False)])
        for i in range(NSA_R):
            out_ref[:, g * NSA_R + i, :] = o[i * SUB:(i + 1) * SUB, :]


def _nsa_win_decode(q4, win_t, kw_new, ts):
    db = q4.shape[0]
    return pl.pallas_call(
        functools.partial(_nsa_win_decode_kernel, ts),
        grid=(db,),
        in_specs=[pl.BlockSpec((None, SUB, NSA_H, NSA_D), lambda b: (b, 0, 0, 0)),
                  pl.BlockSpec((None, 2 * NSA_KVW, NSA_WIN), lambda b: (b, 0, 0)),
                  pl.BlockSpec((None, SUB, 2 * NSA_KVW), lambda b: (b, 0, 0))],
        out_specs=[pl.BlockSpec((None, SUB, NSA_H, NSA_D), lambda b: (b, 0, 0, 0)),
                   pl.BlockSpec((None, 2 * NSA_KVW, NSA_WIN), lambda b: (b, 0, 0))],
        out_shape=[jax.ShapeDtypeStruct((db, SUB, NSA_H, NSA_D), F32),
                   jax.ShapeDtypeStruct((db, 2 * NSA_KVW, NSA_WIN), F32)],
        compiler_params=_params(("parallel",)),
        name="nsa_win_decode",
    )(q4, win_t, kw_new)


def _odd_tail(o, z_ref, x_ref, wout, gfin):
    y = x_ref[...] + _dot((o * _silu(z_ref[...])).astype(BF16), wout[...])
    return _rms(y, gfin[...])


def _odd_out_prompt_kernel(ot_ref, z_ref, x_ref, wout, gfin, out_ref):
    out_ref[...] = _odd_tail(ot_ref[...].T, z_ref, x_ref, wout, gfin)


def _odd_out_prompt(ot, z, x, wout, gfin, tm):
    b, _, t = ot.shape
    tok = pl.BlockSpec((None, tm, D_MODEL), lambda bi, ti: (bi, ti, 0))
    return pl.pallas_call(
        _odd_out_prompt_kernel,
        grid=(b, t // tm),
        in_specs=[pl.BlockSpec((None, NSA_W, tm), lambda bi, ti: (bi, 0, ti)), tok, tok,
                  _const_spec(wout.shape), _const_spec(gfin.shape)],
        out_specs=tok,
        out_shape=jax.ShapeDtypeStruct((b, t, D_MODEL), F32),
        compiler_params=_params(("parallel", "parallel")),
        name="odd_out_prompt",
    )(ot, z, x, wout, gfin)


def _odd_out_decode_kernel(oc_ref, os_ref, ow_ref, gate_ref, z_ref, x_ref, wout, gfin, out_ref):
    gate = gate_ref[...]
    oc, osel, ow = oc_ref[...], os_ref[...], ow_ref[...]
    heads = []
    for h in range(NSA_H):
        c = slice(h * NSA_D, (h + 1) * NSA_D)
        heads.append(gate[:, h:h + 1] * oc[:, c] + gate[:, NSA_H + h:NSA_H + h + 1] * osel[:, c]
                     + gate[:, 2 * NSA_H + h:2 * NSA_H + h + 1] * ow[:, c])
    out_ref[...] = _odd_tail(jnp.concatenate(heads, axis=1), z_ref, x_ref, wout, gfin)


def _odd_out_decode(oc, osel, ow, gate, z, x, wout, gfin):
    m = x.shape[0]
    args = (oc, osel, ow, gate, z, x, wout, gfin)
    return pl.pallas_call(
        _odd_out_decode_kernel,
        grid=(1,),
        in_specs=[_const_spec(a.shape) for a in args],
        out_specs=_const_spec((m, D_MODEL)),
        out_shape=jax.ShapeDtypeStruct((m, D_MODEL), F32),
        compiler_params=_params(("arbitrary",)),
        name="odd_out_decode",
    )(*args)


def _rope_tables(pos):
    half = MLA_ROPE // 2
    inv = ROPE_THETA ** (-jnp.arange(half, dtype=F32) / half)
    ang = pos.astype(F32)[:, None] * inv[None, :]
    cos, sin = jnp.cos(ang), jnp.sin(ang)
    return jnp.concatenate([cos, cos], axis=1), jnp.concatenate([-sin, sin], axis=1)


def _block_diag(x):
    t, g, r, c = x.shape
    eye = jnp.eye(g, dtype=x.dtype)
    return jnp.einsum("tgrc,gh->tgrhc", x, eye).reshape(t, g * r, g * c)


def _even_weights(norm_g, w_in, g_q, g_kv, w_uq, w_uk, w_uv):
    edges = [0, MLA_QL, MLA_QL + MLA_KVL, MLA_QL + MLA_ROW]
    edges += [edges[-1] + MLA_W, edges[-1] + MLA_W + S5_W, edges[-1] + MLA_W + 2 * S5_W]
    wb = w_in.astype(BF16)
    pieces = [wb[:, edges[i]:edges[i + 1]] for i in range(6)]
    in_w = (norm_g[None, :], *pieces, g_q[None, :], g_kv[None, :])
    uq = jnp.transpose(w_uq, (1, 0, 2)).astype(BF16)
    mla_w = (uq[:, :, :MLA_NOPE], uq[:, :, MLA_NOPE:],
             jnp.transpose(w_uk, (1, 2, 0)).astype(BF16),
             jnp.transpose(w_uv, (1, 0, 2)).astype(BF16))
    uq_t = jnp.transpose(w_uq, (1, 2, 0)).astype(BF16)
    mla_wt = (uq_t[:, :MLA_NOPE].reshape(MLA_H * MLA_NOPE, MLA_QL),
              uq_t[:, MLA_NOPE:].reshape(MLA_H * MLA_ROPE, MLA_QL),
              jnp.transpose(w_uk, (1, 0, 2)).astype(BF16),
              jnp.transpose(w_uv, (1, 2, 0)).astype(BF16))
    return in_w, mla_w, mla_wt


def _s5_weights(lam_re, lam_im, log_dt, b_re, b_im, c_re, c_im, d_skip, w_glu, b_glu):
    def bmat(b):
        return _block_diag(jnp.transpose(b.reshape(S5_NT, S5_GT, S5_N, S5_P), (0, 1, 3, 2)))

    def cmat(c):
        return _block_diag(jnp.transpose(c.reshape(S5_NT, S5_GT, S5_P, S5_N), (0, 1, 3, 2)))

    return (lam_re.reshape(1, S5_S), lam_im.reshape(1, S5_S),
            jnp.repeat(log_dt, S5_N).reshape(1, S5_S),
            bmat(b_re), bmat(b_im), cmat(c_re), cmat(c_im),
            d_skip.reshape(1, S5_W), w_glu.astype(BF16), b_glu[None, :])


def _odd_weights(norm_g, w_in, pe_k, pe_v, phi1_k, phi2_k, phi1_v, phi2_v):
    wb = w_in.astype(BF16)
    e0 = NSA_W
    e1 = e0 + 4 * NSA_KVW
    e2 = e1 + 2 * NSA_KVW
    e3 = e2 + 3 * NSA_H
    in_w = (norm_g[None, :], wb[:, :e0], wb[:, e0:e1], wb[:, e1:e2], wb[:, e2:e3], wb[:, e3:])
    pe = jnp.concatenate([pe_k, pe_k, pe_v, pe_v], axis=1)
    p1k = phi1_k.reshape(NSA_BLK, NSA_D, NSA_D)
    p1v = phi1_v.reshape(NSA_BLK, NSA_D, NSA_D)
    phi2 = _block_diag(jnp.stack([phi2_k, phi2_k, phi2_v, phi2_v], axis=0)[None])[0].astype(BF16)
    return in_w, (pe, p1k, p1v, phi2)


def _pad_tokens(x, n):
    return jnp.pad(x, ((0, 0), (0, n - x.shape[1])) + ((0, 0),) * (x.ndim - 2))


def _rows_last(x):
    nd = x.ndim
    xt = jnp.transpose(x, (0,) + tuple(range(2, nd)) + (1,))
    return xt.reshape(x.shape[0], -1, x.shape[1])


def _rows_second(x_t, feature_shape):
    b, _, rows = x_t.shape
    nf = len(feature_shape)
    xt = x_t.reshape((b,) + tuple(feature_shape) + (rows,))
    return jnp.transpose(xt, (0, nf + 1) + tuple(range(1, nf + 1)))


def kernel(x_prompt, x_sample, cache_mla, state_s5, cache_nsa_kv, state_nsa_win, page_table, norm_even, w_in_even, mla_g_q, mla_g_kv, mla_w_uq, mla_w_uk, mla_w_uv, s5_lambda_re, s5_lambda_im, s5_log_dt, s5_b_re, s5_b_im, s5_c_re, s5_c_im, s5_d, s5_w_glu, s5_b_glu, w_out_even, norm_odd, w_in_odd, nsa_pe_k, nsa_pe_v, nsa_phi1_k, nsa_phi2_k, nsa_phi1_v, nsa_phi2_v, w_out_odd, norm_final):
    b, t, _ = x_prompt.shape
    db, ts, _ = x_sample.shape
    n_pages = page_table.shape[1]
    past = n_pages * PAGE
    mp, ms = b * t, db * ts
    tm = min(1024, t)
    assert t % (PACK * NSA_BLK) == 0 and t % TK == 0 and ts <= SUB
    assert t <= NSA_BLK * 256

    pos_p = jnp.arange(t, dtype=I32)
    pos_s = past + jnp.arange(SUB, dtype=I32)
    cos_p, sin_p = _rope_tables(pos_p)
    cos_s, sin_s = _rope_tables(pos_s)
    cos_st, sin_st = jnp.tile(cos_s[:ts], (db, 1)), jnp.tile(sin_s[:ts], (db, 1))

    even_in_w, mla_w, mla_wt = _even_weights(norm_even[0], w_in_even[0], mla_g_q[0], mla_g_kv[0],
                                             mla_w_uq[0], mla_w_uk[0], mla_w_uv[0])
    s5_w = _s5_weights(s5_lambda_re[0], s5_lambda_im[0], s5_log_dt[0], s5_b_re[0], s5_b_im[0],
                       s5_c_re[0], s5_c_im[0], s5_d[0], s5_w_glu[0], s5_b_glu[0])
    wo_e = w_out_even[0].astype(BF16)
    wo_a, wo_b = wo_e[:MLA_W], wo_e[MLA_W:]

    xp2 = x_prompt.reshape(mp, D_MODEL)
    cq, rows_pt, za, u, zb = _even_in(xp2, even_in_w, cos_p, sin_p, tm, seq=t)
    mix_a = _mla_prompt(cq.reshape(b, t, MLA_QL), rows_pt, za.reshape(b, t, MLA_W), mla_wt, cos_p.T, sin_p.T)
    zeros_p = jnp.zeros((b, S5_S), F32)
    mix_b, sre_p, sim_p = _s5(u.reshape(b, t, S5_W), zb.reshape(b, t, S5_W), s5_w, zeros_p, zeros_p,
                              min(128, t))
    xp1 = _even_out(xp2, mix_a.reshape(mp, MLA_W), mix_b.reshape(mp, S5_W), wo_a, wo_b, tm)

    xs2 = x_sample.reshape(ms, D_MODEL)
    cq_s, rows_s, za_s, u_s, zb_s = _even_in(xs2, even_in_w, cos_st, sin_st, ms)
    pad3 = lambda a, n: _pad_tokens(a.reshape(db, ts, n), SUB)
    mix_a_s = _mla_decode(page_table, _rows_last(cache_mla[0]), pad3(cq_s, MLA_QL), pad3(rows_s, MLA_ROW),
                          pad3(za_s, MLA_W), mla_w, cos_s, sin_s, ts)[:, :ts]
    st = state_s5[0]
    mix_b_s, sre_s, sim_s = _s5(u_s.reshape(db, ts, S5_W), zb_s.reshape(db, ts, S5_W), s5_w,
                                st[..., 0].reshape(db, S5_S), st[..., 1].reshape(db, S5_S), ts)
    xs1 = _even_out(xs2, mix_a_s.reshape(ms, MLA_W), mix_b_s.reshape(ms, S5_W), wo_a, wo_b, ms)

    odd_in_w, cmp_w = _odd_weights(norm_odd[0], w_in_odd[0], nsa_pe_k[0], nsa_pe_v[0], nsa_phi1_k[0],
                                   nsa_phi2_k[0], nsa_phi1_v[0], nsa_phi2_v[0])
    wo_o = w_out_odd[0].astype(BF16)
    gfin = norm_final[None, :]

    q, kv4_t, kw_t, gate, z = _odd_in(xp1, odd_in_w, tm, seq=t)
    nblk_p = t // NSA_BLK
    cb = _compress_prompt(kv4_t, *cmp_w)
    q3 = q.reshape(b, t, NSA_W)
    tt = min(256, t)
    oc_t, imp = _cmp(q3, cb.reshape(b, nblk_p, 2 * NSA_KVW), pos_p[None], tt)
    sel, _ = _topk(imp, pos_p[None], nblk_p, tt)
    o_t = _nsa_prompt(q3, kv4_t, kw_t, oc_t, sel, gate.reshape(b, t, 3 * NSA_H))
    y_prompt = _odd_out_prompt(o_t, z.reshape(b, t, NSA_W), xp1.reshape(b, t, D_MODEL), wo_o, gfin, tt)

    q_s, kv4_s, kw_s, gate_s, z_s = _odd_in(xs1, odd_in_w, ms)
    cache_nsa_t = _rows_last(cache_nsa_kv[0])
    cb_s = _compress_decode(page_table, cache_nsa_t, *cmp_w)
    n_sel_s = -(-(past + ts) // NSA_BLK)
    q_s3 = q_s.reshape(db, ts, NSA_W)
    oc_ts, imp_s = _cmp(_pad_tokens(q_s3, SUB), cb_s, pos_s[None], SUB)
    imp_l = jnp.transpose(imp_s[..., :ts], (1, 2, 0, 3)).reshape(1, NSA_G, cb_s.shape[1], ms)
    pos_l = jnp.tile(pos_s[:ts], db)[None]
    _, idx_l = _topk(imp_l, pos_l, n_sel_s, ms)
    idx_s = jnp.transpose(idx_l.reshape(NSA_G, NSA_TOPK, db, ts), (2, 0, 3, 1)).reshape(db, -1)
    q_s4 = q_s.reshape(db, ts, NSA_H, NSA_D)
    o_sel_s = _nsa_sel_decode(idx_s, page_table, cache_nsa_t, q_s4, pad3(kv4_s, 4 * NSA_KVW), ts, past)
    win_t = _rows_last(state_nsa_win[0])
    kw_s3 = kw_s.reshape(db, ts, 2 * NSA_KVW)
    o_win_s, win_st = _nsa_win_decode(_pad_tokens(q_s4, SUB), win_t, _pad_tokens(kw_s3, SUB), ts)
    oc_s = jnp.transpose(oc_ts, (0, 2, 1))[:, :ts].reshape(ms, NSA_W)
    y_sample = _odd_out_decode(oc_s, o_sel_s.reshape(ms, NSA_W), o_win_s[:, :ts].reshape(ms, NSA_W),
                               gate_s, z_s, xs1, wo_o, gfin)

    state = lambda re, im, n: jnp.stack([re, im], axis=-1).reshape(1, n, S5_G, S5_N, 2)
    win_shape = (2, NSA_G, NSA_D)
    if t >= NSA_WIN:
        win_pt = kw_t[:, :, t - NSA_WIN:]
    else:
        win_pt = jnp.pad(kw_t, ((0, 0), (0, 0), (NSA_WIN - t, 0)))
    return (y_prompt, y_sample.reshape(db, ts, D_MODEL),
            _rows_second(rows_pt, (MLA_ROW,))[None], rows_s.reshape(1, db, ts, MLA_ROW),
            state(sre_p, sim_p, b), state(sre_s, sim_s, db),
            _rows_second(kv4_t, (4, NSA_G, NSA_D))[None], kv4_s.reshape(1, db, ts, 4, NSA_G, NSA_D),
            _rows_second(win_pt, win_shape)[None], _rows_second(win_st, win_shape)[None])
```

```python
import functools

import jax
import jax.numpy as jnp
import numpy as np
from jax import lax
from jax.experimental import pallas as pl
from jax.experimental.pallas import tpu as pltpu

F32, BF16, I32 = jnp.float32, jnp.bfloat16, jnp.int32

D_MODEL = 1024
PAGE = 128
EPS = 1e-6
ROPE_THETA = 10000.0
MLA_H, MLA_NOPE, MLA_ROPE, MLA_V = 8, 64, 32, 64
MLA_QL, MLA_KVL = 384, 256
MLA_ROW = MLA_KVL + MLA_ROPE
MLA_W = MLA_H * MLA_V
MLA_SCALE = (MLA_NOPE + MLA_ROPE) ** -0.5
S5_G, S5_P, S5_N = 32, 16, 64
S5_W = S5_G * S5_P
S5_S = S5_G * S5_N
S5_GT = 4
S5_NT = S5_G // S5_GT
NSA_H, NSA_G, NSA_D = 16, 2, 64
NSA_R = NSA_H // NSA_G
NSA_W = NSA_H * NSA_D
NSA_KVW = NSA_G * NSA_D
NSA_BLK, NSA_TOPK, NSA_WIN = 64, 16, 512
NSA_SCALE = NSA_D ** -0.5
FORCED_BONUS = float(NSA_R + 1)
BPP = PAGE // NSA_BLK

LOG2E = 1.4426950408889634
NEG = -1e30
GONE = -3e38
TQ = 128
TK = 256
SUB = 8
PACK = 16
MLA_DECODE_GROUP = 64
VMEM_LIMIT = 56 * 1024 * 1024


def _dot(a, b):
    return jnp.dot(a, b, preferred_element_type=F32)


def _dot_nt(a, b):
    return lax.dot_general(a, b, (((1,), (1,)), ((), ())), preferred_element_type=F32)


def _dot_tn(a, b):
    return lax.dot_general(a, b, (((0,), (0,)), ((), ())), preferred_element_type=F32)


def _rms(x, g):
    return x * lax.rsqrt(jnp.mean(x * x, axis=-1, keepdims=True) + EPS) * g


def _silu(x):
    return x * jax.nn.sigmoid(x)


def _rope_nat(x, cosf, sinf):
    half = x.shape[1] // 2
    xs = jnp.concatenate([x[:, half:], x[:, :half]], axis=1)
    return x * cosf + xs * sinf


def _params(sem):
    return pltpu.CompilerParams(dimension_semantics=sem, vmem_limit_bytes=VMEM_LIMIT)


def _const_spec(shape):
    n = len(shape)
    return pl.BlockSpec(shape, lambda *a, _n=n: (0,) * _n)


def _seq_major_spec(width, tm, seq):
    per = seq // tm
    return pl.BlockSpec((None, width, tm), lambda i: (i // per, 0, i % per))


def _even_in_kernel(feature_major, x_ref, g_ref, wcq, wckv, wkr, wza, wu, wzb, gq, gkv, cos_ref, sin_ref,
                    cq_o, rows_o, za_o, u_o, zb_o):
    h = _rms(x_ref[...], g_ref[...]).astype(BF16)
    cq_o[...] = _rms(_dot(h, wcq[...]), gq[...])
    ckv = _rms(_dot(h, wckv[...]), gkv[...])
    krope = _rope_nat(_dot(h, wkr[...]), cos_ref[...], sin_ref[...])
    if feature_major:
        rows_o[:MLA_KVL, :] = ckv.T
        rows_o[MLA_KVL:, :] = krope.T
    else:
        rows_o[:, :MLA_KVL] = ckv
        rows_o[:, MLA_KVL:] = krope
    za_o[...] = _dot(h, wza[...])
    u_o[...] = _dot(h, wu[...])
    zb_o[...] = _dot(h, wzb[...])


def _even_in(x2, wts, cosf, sinf, tm, seq=None):
    m = x2.shape[0]
    tab_blocks = cosf.shape[0] // tm
    row = lambda n: pl.BlockSpec((tm, n), lambda i: (i, 0))
    tab = pl.BlockSpec((tm, MLA_ROPE), lambda i: (i % tab_blocks, 0))
    widths = (MLA_QL, MLA_ROW, MLA_W, S5_W, S5_W)
    out_specs = [row(n) for n in widths]
    out_shape = [jax.ShapeDtypeStruct((m, n), F32) for n in widths]
    if seq is not None:
        out_specs[1] = _seq_major_spec(MLA_ROW, tm, seq)
        out_shape[1] = jax.ShapeDtypeStruct((m // seq, MLA_ROW, seq), F32)
    return pl.pallas_call(
        functools.partial(_even_in_kernel, seq is not None),
        grid=(m // tm,),
        in_specs=[row(D_MODEL)] + [_const_spec(c.shape) for c in wts] + [tab, tab],
        out_specs=out_specs,
        out_shape=out_shape,
        compiler_params=_params(("parallel",)),
        name="even_in",
    )(x2, *wts, cosf, sinf)


def _mla_queries(cq, wuqn, wuqr, wuk, cosf, sinf, h):
    cqb = cq.astype(BF16)
    qn = _dot(cqb, wuqn[h])
    ql = _dot(qn.astype(BF16), wuk[h])
    qr = _rope_nat(_dot(cqb, wuqr[h]), cosf, sinf)
    return ql, qr


def _flash_tiles(lo, hi, chains, c_exp):
    for qk, _, first_mask, _, s_buf, m_s, acc_s, l_s in chains:
        m_s[...] = jnp.full(m_s.shape, NEG, F32)
        acc_s[...] = jnp.zeros(acc_s.shape, F32)
        if l_s is not None:
            l_s[...] = jnp.zeros(l_s.shape, F32)
        s_buf[0] = first_mask(qk(lo), lo)

    def consume(chain, j, s):
        _, pv, _, _, _, m_s, acc_s, l_s = chain
        m_old = m_s[...]
        m_new = jnp.maximum(m_old, jnp.max(s, axis=0, keepdims=True))
        alpha = jnp.exp2((m_old - m_new) * c_exp)
        x = (s - m_new) * c_exp
        if l_s is None:
            p = jnp.exp2(x.astype(BF16))
        else:
            pf = jnp.exp2(x)
            l_s[...] = alpha * l_s[...] + jnp.sum(pf, axis=0, keepdims=True)
            p = pf.astype(BF16)
        acc_s[...] = alpha * acc_s[...] + pv(j, p)
        m_s[...] = m_new

    def body(j, carry):
        for chain in chains:
            chain[4][1] = chain[0](j + 1)
        for chain in chains:
            consume(chain, j, chain[4][0])
        for chain in chains:
            chain[4][0] = chain[4][1]
        return carry

    lax.fori_loop(lo, hi - 1, body, 0)
    last = hi - 1
    for chain in chains:
        consume(chain, last, chain[3](chain[4][0], last))
    outs = []
    for chain in chains:
        acc, l_s = chain[6][...], chain[7]
        if l_s is None:
            dv = acc.shape[0] - PACK
            outs.append(acc[:dv, :] / acc[dv:dv + 1, :])
        else:
            outs.append(acc / l_s[...])
    return outs


def _with_ones(v):
    return jnp.concatenate([v, jnp.ones((PACK, v.shape[1]), BF16)], axis=0)


def _mla_prompt_kernel(cq_ref, rows_ref, za_ref, wuqn_t, wuqr_t, wuk_t, wuv_t, cos_ref, sin_ref,
                       out_ref, qtl, qtr, s_buf, m_s, l_s, acc_s):
    qi = pl.program_id(1)
    r = MLA_H * TQ
    half = MLA_ROPE // 2
    cq_t = cq_ref[...].T.astype(BF16)
    qn_t = _dot(wuqn_t[...], cq_t).astype(BF16)
    qr_t = _dot(wuqr_t[...], cq_t)
    cos_t, sin_t = cos_ref[...], sin_ref[...]
    for h in range(MLA_H):
        qtl[:, h * TQ:(h + 1) * TQ] = _dot(wuk_t[h], qn_t[h * MLA_NOPE:(h + 1) * MLA_NOPE, :]).astype(BF16)
        x = qr_t[h * MLA_ROPE:(h + 1) * MLA_ROPE, :]
        xs = jnp.concatenate([x[half:, :], x[:half, :]], axis=0)
        qtr[:, h * TQ:(h + 1) * TQ] = (x * cos_t + xs * sin_t).astype(BF16)
    qpos = qi * TQ + (lax.broadcasted_iota(I32, (1, r), 1) & (TQ - 1))
    krow = lax.broadcasted_iota(I32, (TK, 1), 0)

    def keys(j):
        return rows_ref[:, pl.ds(pl.multiple_of(j * TK, TK), TK)]

    def qk(j):
        kt = keys(j)
        return (_dot_tn(kt[:MLA_KVL, :].astype(BF16), qtl[...])
                + _dot_tn(kt[MLA_KVL:, :].astype(BF16), qtr[...]))

    def pv(j, p):
        return _dot(keys(j)[:MLA_KVL, :].astype(BF16), p)

    def causal(s, j):
        return jnp.where(j * TK + krow <= qpos, s, NEG)

    n_tiles = (qi * TQ + TQ + TK - 1) // TK
    chain = (qk, pv, lambda s, j: s, causal, s_buf, m_s, acc_s, l_s)
    o = _flash_tiles(0, n_tiles, [chain], MLA_SCALE * LOG2E)[0].astype(BF16)
    heads = [_dot(wuv_t[h], o[:, h * TQ:(h + 1) * TQ]) for h in range(MLA_H)]
    out_ref[...] = jnp.concatenate(heads, axis=0).T * _silu(za_ref[...])


def _mla_prompt(cq, rows_t, za, wts, cos_t, sin_t):
    b, t, _ = cq.shape
    r = MLA_H * TQ
    tok = lambda n: pl.BlockSpec((None, TQ, n), lambda bi, qi: (bi, qi, 0))
    tab = pl.BlockSpec((MLA_ROPE, TQ), lambda bi, qi: (0, qi))
    return pl.pallas_call(
        _mla_prompt_kernel,
        grid=(b, t // TQ),
        in_specs=[tok(MLA_QL), pl.BlockSpec((None, MLA_ROW, t), lambda bi, qi: (bi, 0, 0)), tok(MLA_W)]
        + [_const_spec(w.shape) for w in wts] + [tab, tab],
        out_specs=tok(MLA_W),
        out_shape=jax.ShapeDtypeStruct((b, t, MLA_W), F32),
        scratch_shapes=[pltpu.VMEM((MLA_KVL, r), BF16), pltpu.VMEM((MLA_ROPE, r), BF16),
                        pltpu.VMEM((2, TK, r), F32),
                        pltpu.VMEM((1, r), F32), pltpu.VMEM((1, r), F32), pltpu.VMEM((MLA_KVL, r), F32)],
        compiler_params=_params(("parallel", "arbitrary")),
        name="mla_prompt",
    )(cq, rows_t, za, *wts, cos_t, sin_t)


def _mla_decode_kernel(n_pages_step, ts, pt_ref, cq_ref, rows_ref, za_ref, wuqn, wuqr, wuk, wuv,
                       cos_ref, sin_ref, *rest):
    pages = rest[:n_pages_step]
    out_ref, ql_s, qr_s, m_s, l_s, acc_s = rest[n_pages_step:]
    gi = pl.program_id(1)
    rows = MLA_H * SUB

    @pl.when(gi == 0)
    def _():
        cq = cq_ref[...]
        cosf, sinf = cos_ref[...], sin_ref[...]
        for h in range(MLA_H):
            ql, qr = _mla_queries(cq, wuqn, wuqr, wuk, cosf, sinf, h)
            ql_s[h * SUB:(h + 1) * SUB, :] = ql.astype(BF16)
            qr_s[h * SUB:(h + 1) * SUB, :] = qr.astype(BF16)
        m_s[...] = jnp.full((rows, 1), NEG, F32)
        l_s[...] = jnp.zeros((rows, 1), F32)
        acc_s[...] = jnp.zeros((rows, MLA_KVL), F32)

    def update(s, vals, feature_major):
        m_old = m_s[...]
        m_new = jnp.maximum(m_old, jnp.max(s, axis=1, keepdims=True))
        alpha = jnp.exp(m_old - m_new)
        p = jnp.exp(s - m_new)
        l_s[...] = alpha * l_s[...] + jnp.sum(p, axis=1, keepdims=True)
        pv = None
        for (lo, hi), v in vals:
            pj = p[:, lo:hi].astype(BF16)
            term = _dot_nt(pj, v) if feature_major else _dot(pj, v)
            pv = term if pv is None else pv + term
        acc_s[...] = alpha * acc_s[...] + pv
        m_s[...] = m_new

    ql, qr = ql_s[...], qr_s[...]
    scores, vals = [], []
    for j, pg in enumerate(pages):
        k = pg[...]
        ckv = k[:MLA_KVL, :].astype(BF16)
        scores.append(_dot(ql, ckv) + _dot(qr, k[MLA_KVL:, :].astype(BF16)))
        vals.append(((len(vals) * PAGE, (len(vals) + 1) * PAGE), ckv))
        if len(vals) == MLA_DECODE_GROUP or j == n_pages_step - 1:
            update(jnp.concatenate(scores, axis=1) * MLA_SCALE, vals, True)
            scores, vals = [], []

    @pl.when(gi == pl.num_programs(1) - 1)
    def _():
        kn = rows_ref[...]
        ckv = kn[:, :MLA_KVL].astype(BF16)
        s = (_dot_nt(ql, ckv) + _dot_nt(qr, kn[:, MLA_KVL:].astype(BF16))) * MLA_SCALE
        tq = lax.broadcasted_iota(I32, (rows, SUB), 0) & (SUB - 1)
        jk = lax.broadcasted_iota(I32, (rows, SUB), 1)
        s = jnp.where((jk <= tq) & (jk < ts), s, NEG)
        update(s, [((0, SUB), ckv)], False)
        o = (acc_s[...] / l_s[...]).astype(BF16)
        heads = [_dot(o[h * SUB:(h + 1) * SUB, :], wuv[h]) for h in range(MLA_H)]
        out_ref[...] = (jnp.concatenate(heads, axis=1) * _silu(za_ref[...]))[:ts]


def _mla_decode(page_table, cache_t, cq, rows, za, wts, cosf, sinf, ts, n_pages_step=64):
    db, n_pages = page_table.shape
    rows_n = MLA_H * SUB
    tok = lambda n: pl.BlockSpec((None, SUB, n), lambda b, g, pt: (b, 0, 0))
    cst = lambda shape: pl.BlockSpec(shape, lambda b, g, pt, _n=len(shape): (0,) * _n)
    page_specs = [
        pl.BlockSpec((None, MLA_ROW, PAGE), lambda b, g, pt, j=j: (pt[b, g * n_pages_step + j], 0, 0))
        for j in range(n_pages_step)]
    grid_spec = pltpu.PrefetchScalarGridSpec(
        num_scalar_prefetch=1,
        grid=(db, n_pages // n_pages_step),
        in_specs=[tok(MLA_QL), tok(MLA_ROW), tok(MLA_W)] + [cst(w.shape) for w in wts]
        + [cst(cosf.shape), cst(sinf.shape)] + page_specs,
        out_specs=pl.BlockSpec((None, ts, MLA_W), lambda b, g, pt: (b, 0, 0)),
        scratch_shapes=[pltpu.VMEM((rows_n, MLA_KVL), BF16), pltpu.VMEM((rows_n, MLA_ROPE), BF16),
                        pltpu.VMEM((rows_n, 1), F32), pltpu.VMEM((rows_n, 1), F32),
                        pltpu.VMEM((rows_n, MLA_KVL), F32)])
    return pl.pallas_call(
        functools.partial(_mla_decode_kernel, n_pages_step, ts),
        grid_spec=grid_spec,
        out_shape=jax.ShapeDtypeStruct((db, ts, MLA_W), F32),
        compiler_params=_params(("parallel", "arbitrary")),
        name="mla_decode",
    )(page_table, cq, rows, za, *wts, cosf, sinf, *([cache_t] * n_pages_step))


def _s5_kernel(u_ref, zb_ref, lre_ref, lim_ref, ldt_ref, wbre, wbim, wcre, wcim, d_ref, wglu, bglu,
               h0re_ref, h0im_ref, mix_o, sre_o, sim_o, bure, buim, hre, him):
    step = pl.program_id(0)
    nb, chunk, _ = u_ref.shape
    rows = nb * chunk

    @pl.when(step == 0)
    def _():
        hre[...] = h0re_ref[...]
        him[...] = h0im_ref[...]

    lre, lim = lre_ref[...], lim_ref[...]
    dt = jnp.exp(ldt_ref[...])
    mag = jnp.exp(lre * dt)
    are, aim = mag * jnp.cos(lim * dt), mag * jnp.sin(lim * dt)
    den = lre * lre + lim * lim
    cre = ((are - 1.0) * lre + aim * lim) / den
    cim = (aim * lre - (are - 1.0) * lim) / den

    ut = jnp.swapaxes(u_ref[...], 0, 1).reshape(rows, S5_W)
    kw, nw = S5_GT * S5_P, S5_GT * S5_N
    for jt in range(S5_NT):
        cr, ci = cre[:, jt * nw:(jt + 1) * nw], cim[:, jt * nw:(jt + 1) * nw]
        bre = (cr * wbre[jt] - ci * wbim[jt]).astype(BF16)
        bim = (cr * wbim[jt] + ci * wbre[jt]).astype(BF16)
        uj = ut[:, jt * kw:(jt + 1) * kw].astype(BF16)
        bure[:, jt * nw:(jt + 1) * nw] = _dot(uj, bre)
        buim[:, jt * nw:(jt + 1) * nw] = _dot(uj, bim)

    def scan(t, carry):
        hr, hi = carry
        sl = pl.ds(pl.multiple_of(t * nb, SUB), nb)
        nr = are * hr - aim * hi + bure[sl, :]
        ni = are * hi + aim * hr + buim[sl, :]
        bure[sl, :] = nr
        buim[sl, :] = ni
        return nr, ni

    hr, hi = lax.fori_loop(0, chunk, scan, (hre[...], him[...]))
    hre[...] = hr
    him[...] = hi
    sre_o[...] = hr
    sim_o[...] = hi

    ys = []
    for jt in range(S5_NT):
        sr = bure[:, jt * nw:(jt + 1) * nw].astype(BF16)
        si = buim[:, jt * nw:(jt + 1) * nw].astype(BF16)
        ys.append(_dot(sr, wcre[jt].astype(BF16)) - _dot(si, wcim[jt].astype(BF16)))
    y = jnp.concatenate(ys, axis=1) + d_ref[...] * ut
    g5 = jax.nn.gelu(y)
    ob = g5 * jax.nn.sigmoid(_dot(g5.astype(BF16), wglu[...]) + bglu[...])
    mix_o[...] = jnp.swapaxes(ob.reshape(chunk, nb, S5_W), 0, 1) * _silu(zb_ref[...])


def _s5(u, zb, wts, h0re, h0im, chunk):
    nb, t, _ = u.shape
    tok = pl.BlockSpec((nb, chunk, S5_W), lambda i: (0, i, 0))
    st = _const_spec((nb, S5_S))
    return pl.pallas_call(
        _s5_kernel,
        grid=(t // chunk,),
        in_specs=[tok, tok] + [_const_spec(w.shape) for w in wts] + [st, st],
        out_specs=[tok, st, st],
        out_shape=[jax.ShapeDtypeStruct((nb, t, S5_W), F32), jax.ShapeDtypeStruct((nb, S5_S), F32),
                   jax.ShapeDtypeStruct((nb, S5_S), F32)],
        scratch_shapes=[pltpu.VMEM((nb * chunk, S5_S), F32), pltpu.VMEM((nb * chunk, S5_S), F32),
                        pltpu.VMEM((nb, S5_S), F32), pltpu.VMEM((nb, S5_S), F32)],
        compiler_params=_params(("arbitrary",)),
        name="s5",
    )(u, zb, *wts, h0re, h0im)


def _even_out_kernel(x_ref, a_ref, b_ref, wa, wb, out_ref):
    out_ref[...] = (x_ref[...] + _dot(a_ref[...].astype(BF16), wa[...])
                    + _dot(b_ref[...].astype(BF16), wb[...]))


def _even_out(x2, mixa, mixb, wa, wb, tm):
    m = x2.shape[0]
    row = lambda n: pl.BlockSpec((tm, n), lambda i: (i, 0))
    return pl.pallas_call(
        _even_out_kernel,
        grid=(m // tm,),
        in_specs=[row(D_MODEL), row(MLA_W), row(S5_W), _const_spec(wa.shape), _const_spec(wb.shape)],
        out_specs=row(D_MODEL),
        out_shape=jax.ShapeDtypeStruct((m, D_MODEL), F32),
        compiler_params=_params(("parallel",)),
        name="even_out",
    )(x2, mixa, mixb, wa, wb)


def _odd_in_kernel(feature_major, x_ref, g_ref, wq, wkv4, wkw, wg, wz, q_o, kv4_o, kw_o, gate_o, z_o):
    h = _rms(x_ref[...], g_ref[...]).astype(BF16)
    q_o[...] = _dot(h, wq[...])
    kv4, kw = _dot(h, wkv4[...]), _dot(h, wkw[...])
    kv4_o[...] = kv4.T if feature_major else kv4
    kw_o[...] = kw.T if feature_major else kw
    gate_o[...] = jax.nn.sigmoid(_dot(h, wg[...]))
    z_o[...] = _dot(h, wz[...])


def _odd_in(x2, wts, tm, seq=None):
    m = x2.shape[0]
    row = lambda n: pl.BlockSpec((tm, n), lambda i: (i, 0))
    widths = (NSA_W, 4 * NSA_KVW, 2 * NSA_KVW, 3 * NSA_H, NSA_W)
    out_specs = [row(n) for n in widths]
    out_shape = [jax.ShapeDtypeStruct((m, n), F32) for n in widths]
    if seq is not None:
        for i in (1, 2):
            out_specs[i] = _seq_major_spec(widths[i], tm, seq)
            out_shape[i] = jax.ShapeDtypeStruct((m // seq, widths[i], seq), F32)
    return pl.pallas_call(
        functools.partial(_odd_in_kernel, seq is not None),
        grid=(m // tm,),
        in_specs=[row(D_MODEL)] + [_const_spec(w.shape) for w in wts],
        out_specs=out_specs,
        out_shape=out_shape,
        compiler_params=_params(("parallel",)),
        name="odd_in",
    )(x2, *wts)


def _compress_stage(xs, x_t, pe, base):
    n = x_t.shape[1] // NSA_BLK
    x = x_t.astype(BF16).T.reshape(n, NSA_BLK, 2 * NSA_KVW) + pe.astype(BF16)[None]
    xs[:, pl.ds(pl.multiple_of(base, PACK), n), :] = jnp.swapaxes(x, 0, 1)


def _compress_weights(w1, p1k, p1v):
    w1[...] = jnp.zeros(w1.shape, BF16)
    for s, ref in enumerate((p1k, p1k, p1v, p1v)):
        w1[:, s * NSA_D:(s + 1) * NSA_D, s * NSA_D:(s + 1) * NSA_D] = ref[...].astype(BF16)


def _compress_finish(xs, w1, phi2, out_ref):
    acc = jnp.zeros((xs.shape[1], 2 * NSA_KVW), F32)
    for r in range(NSA_BLK):
        acc = acc + _dot(xs[r], w1[r])
    out_ref[...] = _dot(_silu(acc).astype(BF16), phi2[...])


def _compress_prompt_kernel(x_ref, pe_ref, p1k, p1v, phi2, out_ref, xs, w1):
    bi = pl.program_id(0)
    nblk = x_ref.shape[1] // NSA_BLK
    _compress_stage(xs, x_ref[...], pe_ref[...], bi * nblk)

    @pl.when(bi == pl.num_programs(0) - 1)
    def _():
        _compress_weights(w1, p1k, p1v)
        _compress_finish(xs, w1, phi2, out_ref)


def _compress_prompt(kv4_t, pe, p1k, p1v, phi2):
    b, _, t = kv4_t.shape
    cw = 2 * NSA_KVW
    nblk = b * (t // NSA_BLK)
    consts = (pe, p1k, p1v, phi2)
    return pl.pallas_call(
        _compress_prompt_kernel,
        grid=(b,),
        in_specs=[pl.BlockSpec((None, cw, t), lambda i: (i, 0, 0))] + [_const_spec(c.shape) for c in consts],
        out_specs=pl.BlockSpec((nblk, cw), lambda i: (0, 0)),
        out_shape=jax.ShapeDtypeStruct((nblk, cw), F32),
        scratch_shapes=[pltpu.VMEM((NSA_BLK, nblk, cw), BF16), pltpu.VMEM((NSA_BLK, cw, cw), BF16)],
        compiler_params=_params(("arbitrary",)),
        name="compress_prompt",
    )(kv4_t, *consts)


def _compress_decode_kernel(n_pages_step, pt_ref, pe_ref, p1k, p1v, phi2, *rest):
    pages = rest[:n_pages_step]
    out_ref, xs, w1 = rest[n_pages_step:]
    bi, gi = pl.program_id(0), pl.program_id(1)
    grp = PACK // BPP
    pe = pe_ref[...]

    @pl.when((bi == 0) & (gi == 0))
    def _():
        _compress_weights(w1, p1k, p1v)

    for k in range(n_pages_step // grp):
        x_t = jnp.concatenate([pages[k * grp + j][...] for j in range(grp)], axis=1)
        _compress_stage(xs, x_t, pe, (gi * (n_pages_step // grp) + k) * PACK)

    @pl.when(gi == pl.num_programs(1) - 1)
    def _():
        _compress_finish(xs, w1, phi2, out_ref)


def _compress_decode(page_table, cache_t, pe, p1k, p1v, phi2, n_pages_step=32):
    db, n_pages = page_table.shape
    cw = 2 * NSA_KVW
    nblk = n_pages * BPP
    cst = lambda shape: pl.BlockSpec(shape, lambda b, g, pt, _n=len(shape): (0,) * _n)
    consts = (pe, p1k, p1v, phi2)
    page_specs = [
        pl.BlockSpec((None, cw, PAGE), lambda b, g, pt, j=j: (pt[b, g * n_pages_step + j], 0, 0))
        for j in range(n_pages_step)]
    grid_spec = pltpu.PrefetchScalarGridSpec(
        num_scalar_prefetch=1,
        grid=(db, n_pages // n_pages_step),
        in_specs=[cst(c.shape) for c in consts] + page_specs,
        out_specs=pl.BlockSpec((None, nblk, cw), lambda b, g, pt: (b, 0, 0)),
        scratch_shapes=[pltpu.VMEM((NSA_BLK, nblk, cw), BF16), pltpu.VMEM((NSA_BLK, cw, cw), BF16)])
    return pl.pallas_call(
        functools.partial(_compress_decode_kernel, n_pages_step),
        grid_spec=grid_spec,
        out_shape=jax.ShapeDtypeStruct((db, nblk, cw), F32),
        compiler_params=_params(("arbitrary", "arbitrary")),
        name="compress_decode",
    )(page_table, *consts, *([cache_t] * n_pages_step))


def _alibi_slope(h):
    return 2.0 ** (-8.0 * (h + 1) / NSA_H)


def _split_bf16(x):
    hi = x.astype(BF16)
    return hi, (x - hi.astype(F32)).astype(BF16)


def _cmp_kernel(q_ref, cb_ref, pos_ref, oc_o, imp_o, s_scr):
    tt = q_ref.shape[0]
    nc = cb_ref.shape[0]
    qpos = pos_ref[...]
    cpos = lax.broadcasted_iota(I32, (nc, 1), 0) * NSA_BLK + (NSA_BLK - 1)
    visible = cpos <= qpos
    dist = (qpos - cpos).astype(F32)
    q = q_ref[...] * NSA_SCALE
    cb = cb_ref[...]
    for g in range(NSA_G):
        k_hi, k_lo = _split_bf16(cb[:, g * NSA_D:(g + 1) * NSA_D])
        vc = cb[:, NSA_KVW + g * NSA_D:NSA_KVW + (g + 1) * NSA_D].astype(BF16)
        for i in range(NSA_R):
            h = g * NSA_R + i
            q_hi, q_lo = _split_bf16(q[:, h * NSA_D:(h + 1) * NSA_D])
            s = _dot_nt(k_hi, q_hi) + _dot_nt(k_hi, q_lo) + _dot_nt(k_lo, q_hi)
            s_scr[i * nc:(i + 1) * nc, :] = s - _alibi_slope(h) * dist
        s3 = jnp.where(visible[None], s_scr[...].reshape(NSA_R, nc, tt), NEG)
        mx = jnp.max(s3, axis=1, keepdims=True)
        e = jnp.where(visible[None], jnp.exp(s3 - mx), 0.0)
        den = jnp.sum(e, axis=1, keepdims=True)
        p = e / jnp.where(den > 0, den, 1.0)
        imp_o[g] = jnp.sum(p, axis=0)
        for i in range(NSA_R):
            h = g * NSA_R + i
            oc_o[h * NSA_D:(h + 1) * NSA_D, :] = _dot_tn(vc, p[i].astype(BF16))


def _cmp(q, cb, pos, tt):
    b, t, _ = q.shape
    nc = cb.shape[1]
    return pl.pallas_call(
        _cmp_kernel,
        grid=(b, t // tt),
        in_specs=[pl.BlockSpec((None, tt, NSA_W), lambda bi, ti: (bi, ti, 0)),
                  pl.BlockSpec((None, nc, 2 * NSA_KVW), lambda bi, ti: (bi, 0, 0)),
                  pl.BlockSpec((1, tt), lambda bi, ti: (0, ti))],
        out_specs=[pl.BlockSpec((None, NSA_W, tt), lambda bi, ti: (bi, 0, ti)),
                   pl.BlockSpec((None, NSA_G, nc, tt), lambda bi, ti: (bi, 0, 0, ti))],
        out_shape=[jax.ShapeDtypeStruct((b, NSA_W, t), F32),
                   jax.ShapeDtypeStruct((b, NSA_G, nc, t), F32)],
        scratch_shapes=[pltpu.VMEM((NSA_R * nc, tt), F32)],
        compiler_params=_params(("parallel", "parallel")),
        name="nsa_cmp",
    )(q, cb, pos)


def _topk_kernel(n_sel, imp_ref, pos_ref, sel_o, idx_o):
    _, nc, tt = imp_ref.shape
    nsp = sel_o.shape[1]
    qpos = pos_ref[...]
    blk = lax.broadcasted_iota(I32, (nsp, 1), 0)
    cur = jnp.right_shift(qpos, NSA_BLK.bit_length() - 1)
    forced = (blk == 0) | (blk == cur) | (blk == cur - 1)
    allowed = (blk <= cur) & (blk < n_sel)
    for g in range(NSA_G):
        imp = imp_ref[g]
        if nsp > nc:
            imp = jnp.concatenate([imp, jnp.zeros((nsp - nc, tt), F32)], axis=0)
        score = jnp.where(allowed, imp + jnp.where(forced, FORCED_BONUS, 0.0), NEG)
        chosen = jnp.zeros((nsp, tt), F32)
        picks = []
        for _ in range(NSA_TOPK):
            mx = jnp.max(score, axis=0, keepdims=True)
            first = jnp.min(jnp.where(score == mx, blk, nsp), axis=0, keepdims=True)
            hit = blk == first
            chosen = jnp.where(hit, 1.0, chosen)
            score = jnp.where(hit, GONE, score)
            picks.append(first)
        sel_o[g] = chosen
        idx_o[g] = jnp.concatenate(picks, axis=0)


def _topk(imp, pos, n_sel, tt):
    b, _, nc, t = imp.shape
    nsp = -(-(n_sel + TK // NSA_BLK) // SUB) * SUB
    return pl.pallas_call(
        functools.partial(_topk_kernel, n_sel),
        grid=(b, t // tt),
        in_specs=[pl.BlockSpec((None, NSA_G, nc, tt), lambda bi, ti: (bi, 0, 0, ti)),
                  pl.BlockSpec((1, tt), lambda bi, ti: (0, ti))],
        out_specs=[pl.BlockSpec((None, NSA_G, nsp, tt), lambda bi, ti: (bi, 0, 0, ti)),
                   pl.BlockSpec((None, NSA_G, NSA_TOPK, tt), lambda bi, ti: (bi, 0, 0, ti))],
        out_shape=[jax.ShapeDtypeStruct((b, NSA_G, nsp, t), F32),
                   jax.ShapeDtypeStruct((b, NSA_G, NSA_TOPK, t), I32)],
        compiler_params=_params(("parallel", "parallel")),
        name="nsa_topk",
    )(imp, pos)


def _bf16_parts(x, n):
    parts = []
    for _ in range(n):
        bits = np.asarray(x, np.float32).view(np.uint32)
        top = ((bits + np.uint32(0x7FFF) + ((bits >> np.uint32(16)) & np.uint32(1)))
               & np.uint32(0xFFFF0000)).view(np.float32)
        parts.append(float(top))
        x = float(np.float32(x) - top)
    return parts


def _nsa_prompt_kernel(q_ref, kv4_ref, kw_ref, oc_ref, sel_ref, gate_ref, out_ref,
                       k_aug, q_aug, s_buf, m_s, acc_s):
    qi = pl.program_id(1)
    r = NSA_R * TQ
    bpt = TK // NSA_BLK
    n_parts = SUB // 2
    qt = (q_ref[...] * NSA_SCALE).T.astype(BF16)
    gt = gate_ref[...].T
    lane = lax.broadcasted_iota(I32, (1, r), 1)
    qpos = qi * TQ + (lane & (TQ - 1))
    head = lane // TQ
    krow = lax.broadcasted_iota(I32, (TK, 1), 0)
    klane = lax.broadcasted_iota(I32, (1, TK), 1)
    row8 = lax.broadcasted_iota(I32, (SUB, 1), 0)
    n_tiles = (qi * TQ + TQ + TK - 1) // TK
    onehot = jnp.where(row8 == jnp.right_shift(klane, NSA_BLK.bit_length() - 1), 1.0, 0.0)

    def key_rows(j):
        rel = j * TK - qi * TQ + klane
        coarse = (rel & -NSA_BLK).astype(F32)
        fine = (rel & (NSA_BLK - 1)).astype(F32)
        alibi = jnp.where((row8 & 1) == 0, coarse, fine)
        return jnp.concatenate([alibi, onehot], axis=0).astype(BF16)

    def causal(s, j):
        return jnp.where(j * TK + krow <= qpos, s, NEG)

    def in_window(s, j):
        return jnp.where(qpos - (j * TK + krow) <= NSA_WIN, s, NEG)

    def tile(ref, lo, j):
        return ref[lo:lo + NSA_D, pl.ds(pl.multiple_of(j * TK, TK), TK)].astype(BF16)

    slopes, sel_chains, win_chains = [], [], []
    for g in range(NSA_G):
        kg, qa = k_aug.at[g], q_aug.at[g]
        qa[:NSA_D, :] = jnp.concatenate(
            [qt[(g * NSA_R + i) * NSA_D:(g * NSA_R + i + 1) * NSA_D, :] for i in range(NSA_R)], axis=1)
        slope8 = jnp.zeros((SUB, r), F32)
        for i in range(NSA_R):
            parts = _bf16_parts(_alibi_slope(g * NSA_R + i), n_parts)
            col = jnp.zeros((SUB, 1), F32)
            for k, part in enumerate(parts):
                col = jnp.where(jnp.right_shift(row8, 1) == k, part, col)
            slope8 = jnp.where(head == i, col, slope8)
        slopes.append(slope8)
        ks_lo, vs_lo = 2 * NSA_KVW + g * NSA_D, 3 * NSA_KVW + g * NSA_D
        kw_lo, vw_lo = g * NSA_D, NSA_KVW + g * NSA_D

        def qk_sel(j, g=g, kg=kg, qa=qa, ks_lo=ks_lo, slope8=slope8):
            kg[:NSA_D, :] = tile(kv4_ref, ks_lo, j)
            kg[NSA_D:, :] = key_rows(j)
            flags = sel_ref[g, pl.ds(j * bpt, SUB), :]
            off = jnp.where(flags > 0.5, 0.0, NEG)
            qa[NSA_D:, :] = jnp.concatenate(
                [slope8, jnp.concatenate([off] * NSA_R, axis=1)], axis=0).astype(BF16)
            return _dot_tn(kg[...], qa[...])

        def qk_win(j, kg=kg, qa=qa, kw_lo=kw_lo):
            kg[:NSA_D, :] = tile(kw_ref, kw_lo, j)
            kg[NSA_D:, :] = key_rows(j)
            return _dot_tn(kg[...], qa[...])

        state = (s_buf.at[g], m_s.at[g], acc_s.at[g], None)
        sel_chains.append((qk_sel, lambda j, p, lo=vs_lo: _dot(_with_ones(tile(kv4_ref, lo, j)), p),
                           lambda s, j: s, causal) + state)
        win_chains.append((qk_win, lambda j, p, lo=vw_lo: _dot(_with_ones(tile(kw_ref, lo, j)), p),
                           in_window, lambda s, j: causal(in_window(s, j), j)) + state)

    o_sels = _flash_tiles(0, n_tiles, sel_chains, LOG2E)
    for g in range(NSA_G):
        q_aug[g, NSA_D:, :] = jnp.concatenate([slopes[g], jnp.zeros((SUB, r), F32)], axis=0).astype(BF16)
    o_wins = _flash_tiles(jnp.maximum(qi * TQ - NSA_WIN, 0) // TK, n_tiles, win_chains, LOG2E)
    for g in range(NSA_G):
        o_sel, o_win = o_sels[g], o_wins[g]
        for i in range(NSA_R):
            h = g * NSA_R + i
            rows = slice(h * NSA_D, (h + 1) * NSA_D)
            cols = slice(i * TQ, (i + 1) * TQ)
            out_ref[rows, :] = (gt[h:h + 1, :] * oc_ref[rows, :]
                                + gt[NSA_H + h:NSA_H + h + 1, :] * o_sel[:, cols]
                                + gt[2 * NSA_H + h:2 * NSA_H + h + 1, :] * o_win[:, cols])


def _nsa_prompt(q, kv4_t, kw_t, oc_t, sel, gate):
    b, t, _ = q.shape
    nsp = sel.shape[2]
    r = NSA_R * TQ
    tok = lambda n: pl.BlockSpec((None, TQ, n), lambda bi, qi: (bi, qi, 0))
    full = lambda n: pl.BlockSpec((None, n, t), lambda bi, qi: (bi, 0, 0))
    return pl.pallas_call(
        _nsa_prompt_kernel,
        grid=(b, t // TQ),
        in_specs=[tok(NSA_W), full(4 * NSA_KVW), full(2 * NSA_KVW),
                  pl.BlockSpec((None, NSA_W, TQ), lambda bi, qi: (bi, 0, qi)),
                  pl.BlockSpec((None, NSA_G, nsp, TQ), lambda bi, qi: (bi, 0, 0, qi)),
                  tok(3 * NSA_H)],
        out_specs=pl.BlockSpec((None, NSA_W, TQ), lambda bi, qi: (bi, 0, qi)),
        out_shape=jax.ShapeDtypeStruct((b, NSA_W, t), F32),
        scratch_shapes=[pltpu.VMEM((NSA_G, NSA_D + 2 * SUB, TK), BF16),
                        pltpu.VMEM((NSA_G, NSA_D + 2 * SUB, r), BF16),
                        pltpu.VMEM((NSA_G, 2, TK, r), F32),
                        pltpu.VMEM((NSA_G, 1, r), F32),
                        pltpu.VMEM((NSA_G, NSA_D + PACK, r), F32)],
        compiler_params=_params(("parallel", "arbitrary")),
        name="nsa_prompt",
    )(q, kv4_t, kw_t, oc_t, sel, gate)


def _softmax_rows(parts):
    mx = None
    for s, _, _ in parts:
        pm = jnp.max(s, axis=1, keepdims=True)
        mx = pm if mx is None else jnp.maximum(mx, pm)
    den, num = None, None
    for s, v, feature_major in parts:
        p = jnp.exp(s - mx)
        d = jnp.sum(p, axis=1, keepdims=True)
        n = _dot_nt(p.astype(BF16), v) if feature_major else _dot(p.astype(BF16), v)
        den = d if den is None else den + d
        num = n if num is None else num + n
    return num / den


def _nsa_sel_decode_kernel(ts, pos0, nbp, idx_ref, pid_ref, q_ref, new_ref, *rest):
    nblk = NSA_G * NSA_TOPK
    pages = rest[:nblk]
    out_ref = rest[nblk]
    b, t = pl.program_id(0), pl.program_id(1)
    q = q_ref[...] * NSA_SCALE
    new = new_ref[...]
    lane = lax.broadcasted_iota(I32, (1, PAGE), 1)
    jn = lax.broadcasted_iota(I32, (1, SUB), 1)
    for g in range(NSA_G):
        qg = q[g * NSA_R:(g + 1) * NSA_R, :].astype(BF16)
        slope = jnp.concatenate(
            [jnp.full((1, 1), _alibi_slope(g * NSA_R + i), F32) for i in range(NSA_R)], axis=0)
        parts = []
        has_new = jnp.zeros((), I32)
        for k in range(NSA_TOPK):
            bid = idx_ref[((b * NSA_G + g) * ts + t) * NSA_TOPK + k]
            pg = pages[g * NSA_TOPK + k][...]
            kk = pg[g * NSA_D:(g + 1) * NSA_D, :].astype(BF16)
            vv = pg[NSA_KVW + g * NSA_D:NSA_KVW + (g + 1) * NSA_D, :].astype(BF16)
            sub = bid % BPP
            rel = ((bid - sub) * NSA_BLK - pos0 + lane).astype(F32)
            s = _dot(qg, kk) + slope * rel
            mine = (lane // NSA_BLK == sub) & (bid < nbp)
            parts.append((jnp.where(mine, s, NEG), vv, True))
            has_new = has_new + (bid >= nbp).astype(I32)
        kn = new[:, 2 * NSA_KVW + g * NSA_D:2 * NSA_KVW + (g + 1) * NSA_D].astype(BF16)
        vn = new[:, 3 * NSA_KVW + g * NSA_D:3 * NSA_KVW + (g + 1) * NSA_D].astype(BF16)
        sn = _dot_nt(qg, kn) + slope * jn.astype(F32)
        ok = (jn <= t) & (jn < ts) & (has_new > 0)
        parts.append((jnp.where(ok, sn, NEG), vn, False))
        out_ref[g * NSA_R:(g + 1) * NSA_R, :] = _softmax_rows(parts)


def _nsa_sel_decode(idx, page_table, cache_t, q4, kv4_new, ts, pos0):
    db, n_pages = page_table.shape
    nbp = n_pages * BPP
    cw = 2 * NSA_KVW
    page_ids = jnp.take_along_axis(page_table, jnp.minimum(idx, nbp - 1) // BPP, axis=1).reshape(-1)
    idx_flat = idx.reshape(-1)

    def page_map(b, t, idx_r, pid, g, k):
        return (pid[((b * NSA_G + g) * ts + t) * NSA_TOPK + k], 1, 0)

    page_specs = [pl.BlockSpec((None, cw, PAGE), functools.partial(page_map, g=g, k=k))
                  for g in range(NSA_G) for k in range(NSA_TOPK)]
    grid_spec = pltpu.PrefetchScalarGridSpec(
        num_scalar_prefetch=2,
        grid=(db, ts),
        in_specs=[pl.BlockSpec((None, None, NSA_H, NSA_D), lambda b, t, idx, pt: (b, t, 0, 0)),
                  pl.BlockSpec((None, SUB, 4 * NSA_KVW), lambda b, t, idx, pt: (b, 0, 0))] + page_specs,
        out_specs=pl.BlockSpec((None, None, NSA_H, NSA_D), lambda b, t, idx, pt: (b, t, 0, 0)))
    return pl.pallas_call(
        functools.partial(_nsa_sel_decode_kernel, ts, pos0, nbp),
        grid_spec=grid_spec,
        out_shape=jax.ShapeDtypeStruct((db, ts, NSA_H, NSA_D), F32),
        compiler_params=_params(("parallel", "arbitrary")),
        name="nsa_sel_decode",
    )(idx_flat, page_ids, q4, kv4_new, *([cache_t] * (NSA_G * NSA_TOPK)))


def _nsa_win_decode_kernel(ts, q_ref, win_ref, new_ref, out_ref, win_o):
    rows = NSA_R * SUB
    win = win_ref[...]
    new = new_ref[...]
    win_o[...] = jnp.concatenate([win[:, ts:], new.T[:, :ts]], axis=1)
    tq = lax.broadcasted_iota(I32, (rows, 1), 0) & (SUB - 1)
    iw = lax.broadcasted_iota(I32, (1, NSA_WIN), 1)
    jn = lax.broadcasted_iota(I32, (1, SUB), 1)
    for g in range(NSA_G):
        qg = jnp.concatenate([q_ref[:, g * NSA_R + i, :] for i in range(NSA_R)], axis=0)
        qg = (qg * NSA_SCALE).astype(BF16)
        slope = jnp.concatenate(
            [jnp.full((SUB, 1), _alibi_slope(g * NSA_R + i), F32) for i in range(NSA_R)], axis=0)
        kw = win[g * NSA_D:(g + 1) * NSA_D, :].astype(BF16)
        vw = win[NSA_KVW + g * NSA_D:NSA_KVW + (g + 1) * NSA_D, :].astype(BF16)
        kn = new[:, g * NSA_D:(g + 1) * NSA_D].astype(BF16)
        vn = new[:, NSA_KVW + g * NSA_D:NSA_KVW + (g + 1) * NSA_D].astype(BF16)
        sw = _dot(qg, kw) + slope * (iw - NSA_WIN).astype(F32)
        sw = jnp.where(iw >= tq, sw, NEG)
        sn = _dot_nt(qg, kn) + slope * jn.astype(F32)
        sn = jnp.where((jn <= tq) & (jn < ts), sn, NEG)
        o = _softmax_rows([(sw, vw, True), (sn, vn, False)])
        for i in range(NSA_R):
            out_ref[:, g * NSA_R + i, :] = o[i * SUB:i * SUB + ts, :]


def _nsa_win_decode(q4, win_t, kw_new, ts):
    db = q4.shape[0]
    return pl.pallas_call(
        functools.partial(_nsa_win_decode_kernel, ts),
        grid=(db,),
        in_specs=[pl.BlockSpec((None, SUB, NSA_H, NSA_D), lambda b: (b, 0, 0, 0)),
                  pl.BlockSpec((None, 2 * NSA_KVW, NSA_WIN), lambda b: (b, 0, 0)),
                  pl.BlockSpec((None, SUB, 2 * NSA_KVW), lambda b: (b, 0, 0))],
        out_specs=[pl.BlockSpec((None, ts, NSA_H, NSA_D), lambda b: (b, 0, 0, 0)),
                   pl.BlockSpec((None, 2 * NSA_KVW, NSA_WIN), lambda b: (b, 0, 0))],
        out_shape=[jax.ShapeDtypeStruct((db, ts, NSA_H, NSA_D), F32),
                   jax.ShapeDtypeStruct((db, 2 * NSA_KVW, NSA_WIN), F32)],
        compiler_params=_params(("parallel",)),
        name="nsa_win_decode",
    )(q4, win_t, kw_new)


def _odd_tail(o, z_ref, x_ref, wout, gfin):
    y = x_ref[...] + _dot((o * _silu(z_ref[...])).astype(BF16), wout[...])
    return _rms(y, gfin[...])


def _odd_out_prompt_kernel(ot_ref, z_ref, x_ref, wout, gfin, out_ref):
    out_ref[...] = _odd_tail(ot_ref[...].T, z_ref, x_ref, wout, gfin)


def _odd_out_prompt(ot, z, x, wout, gfin, tm):
    b, _, t = ot.shape
    tok = pl.BlockSpec((None, tm, D_MODEL), lambda bi, ti: (bi, ti, 0))
    return pl.pallas_call(
        _odd_out_prompt_kernel,
        grid=(b, t // tm),
        in_specs=[pl.BlockSpec((None, NSA_W, tm), lambda bi, ti: (bi, 0, ti)), tok, tok,
                  _const_spec(wout.shape), _const_spec(gfin.shape)],
        out_specs=tok,
        out_shape=jax.ShapeDtypeStruct((b, t, D_MODEL), F32),
        compiler_params=_params(("parallel", "parallel")),
        name="odd_out_prompt",
    )(ot, z, x, wout, gfin)


def _odd_out_decode_kernel(oc_ref, os_ref, ow_ref, gate_ref, z_ref, x_ref, wout, gfin, out_ref):
    gate = gate_ref[...]
    oc, osel, ow = oc_ref[...], os_ref[...], ow_ref[...]
    heads = []
    for h in range(NSA_H):
        c = slice(h * NSA_D, (h + 1) * NSA_D)
        heads.append(gate[:, h:h + 1] * oc[:, c] + gate[:, NSA_H + h:NSA_H + h + 1] * osel[:, c]
                     + gate[:, 2 * NSA_H + h:2 * NSA_H + h + 1] * ow[:, c])
    out_ref[...] = _odd_tail(jnp.concatenate(heads, axis=1), z_ref, x_ref, wout, gfin)


def _odd_out_decode(oc, osel, ow, gate, z, x, wout, gfin):
    m = x.shape[0]
    args = (oc, osel, ow, gate, z, x, wout, gfin)
    return pl.pallas_call(
        _odd_out_decode_kernel,
        grid=(1,),
        in_specs=[_const_spec(a.shape) for a in args],
        out_specs=_const_spec((m, D_MODEL)),
        out_shape=jax.ShapeDtypeStruct((m, D_MODEL), F32),
        compiler_params=_params(("arbitrary",)),
        name="odd_out_decode",
    )(*args)


def _rope_tables(pos):
    half = MLA_ROPE // 2
    inv = ROPE_THETA ** (-jnp.arange(half, dtype=F32) / half)
    ang = pos.astype(F32)[:, None] * inv[None, :]
    cos, sin = jnp.cos(ang), jnp.sin(ang)
    return jnp.concatenate([cos, cos], axis=1), jnp.concatenate([-sin, sin], axis=1)


def _block_diag(x):
    t, g, r, c = x.shape
    eye = jnp.eye(g, dtype=x.dtype)
    return jnp.einsum("tgrc,gh->tgrhc", x, eye).reshape(t, g * r, g * c)


def _even_weights(norm_g, w_in, g_q, g_kv, w_uq, w_uk, w_uv):
    edges = [0, MLA_QL, MLA_QL + MLA_KVL, MLA_QL + MLA_ROW]
    edges += [edges[-1] + MLA_W, edges[-1] + MLA_W + S5_W, edges[-1] + MLA_W + 2 * S5_W]
    wb = w_in.astype(BF16)
    pieces = [wb[:, edges[i]:edges[i + 1]] for i in range(6)]
    in_w = (norm_g[None, :], *pieces, g_q[None, :], g_kv[None, :])
    uq = jnp.transpose(w_uq, (1, 0, 2)).astype(BF16)
    mla_w = (uq[:, :, :MLA_NOPE], uq[:, :, MLA_NOPE:],
             jnp.transpose(w_uk, (1, 2, 0)).astype(BF16),
             jnp.transpose(w_uv, (1, 0, 2)).astype(BF16))
    uq_t = jnp.transpose(w_uq, (1, 2, 0)).astype(BF16)
    mla_wt = (uq_t[:, :MLA_NOPE].reshape(MLA_H * MLA_NOPE, MLA_QL),
              uq_t[:, MLA_NOPE:].reshape(MLA_H * MLA_ROPE, MLA_QL),
              jnp.transpose(w_uk, (1, 0, 2)).astype(BF16),
              jnp.transpose(w_uv, (1, 2, 0)).astype(BF16))
    return in_w, mla_w, mla_wt


def _s5_weights(lam_re, lam_im, log_dt, b_re, b_im, c_re, c_im, d_skip, w_glu, b_glu):
    def bmat(b):
        return _block_diag(jnp.transpose(b.reshape(S5_NT, S5_GT, S5_N, S5_P), (0, 1, 3, 2)))

    def cmat(c):
        return _block_diag(jnp.transpose(c.reshape(S5_NT, S5_GT, S5_P, S5_N), (0, 1, 3, 2)))

    return (lam_re.reshape(1, S5_S), lam_im.reshape(1, S5_S),
            jnp.repeat(log_dt, S5_N).reshape(1, S5_S),
            bmat(b_re), bmat(b_im), cmat(c_re), cmat(c_im),
            d_skip.reshape(1, S5_W), w_glu.astype(BF16), b_glu[None, :])


def _odd_weights(norm_g, w_in, pe_k, pe_v, phi1_k, phi2_k, phi1_v, phi2_v):
    wb = w_in.astype(BF16)
    e0 = NSA_W
    e1 = e0 + 4 * NSA_KVW
    e2 = e1 + 2 * NSA_KVW
    e3 = e2 + 3 * NSA_H
    in_w = (norm_g[None, :], wb[:, :e0], wb[:, e0:e1], wb[:, e1:e2], wb[:, e2:e3], wb[:, e3:])
    pe = jnp.concatenate([pe_k, pe_k, pe_v, pe_v], axis=1)
    p1k = phi1_k.reshape(NSA_BLK, NSA_D, NSA_D)
    p1v = phi1_v.reshape(NSA_BLK, NSA_D, NSA_D)
    phi2 = _block_diag(jnp.stack([phi2_k, phi2_k, phi2_v, phi2_v], axis=0)[None])[0].astype(BF16)
    return in_w, (pe, p1k, p1v, phi2)


def _pad_tokens(x, n):
    return jnp.pad(x, ((0, 0), (0, n - x.shape[1])) + ((0, 0),) * (x.ndim - 2))


def _rows_last(x):
    nd = x.ndim
    xt = jnp.transpose(x, (0,) + tuple(range(2, nd)) + (1,))
    return xt.reshape(x.shape[0], -1, x.shape[1])


def _rows_second(x_t, feature_shape):
    b, _, rows = x_t.shape
    nf = len(feature_shape)
    xt = x_t.reshape((b,) + tuple(feature_shape) + (rows,))
    return jnp.transpose(xt, (0, nf + 1) + tuple(range(1, nf + 1)))


def kernel(x_prompt, x_sample, cache_mla, state_s5, cache_nsa_kv, state_nsa_win, page_table, norm_even, w_in_even, mla_g_q, mla_g_kv, mla_w_uq, mla_w_uk, mla_w_uv, s5_lambda_re, s5_lambda_im, s5_log_dt, s5_b_re, s5_b_im, s5_c_re, s5_c_im, s5_d, s5_w_glu, s5_b_glu, w_out_even, norm_odd, w_in_odd, nsa_pe_k, nsa_pe_v, nsa_phi1_k, nsa_phi2_k, nsa_phi1_v, nsa_phi2_v, w_out_odd, norm_final):
    b, t, _ = x_prompt.shape
    db, ts, _ = x_sample.shape
    n_pages = page_table.shape[1]
    past = n_pages * PAGE
    mp, ms = b * t, db * ts
    tm = min(1024, t)
    assert t % (PACK * NSA_BLK) == 0 and t % TK == 0 and ts <= SUB
    assert t <= NSA_BLK * 256

    pos_p = jnp.arange(t, dtype=I32)
    pos_s = past + jnp.arange(SUB, dtype=I32)
    cos_p, sin_p = _rope_tables(pos_p)
    cos_s, sin_s = _rope_tables(pos_s)
    cos_st, sin_st = jnp.tile(cos_s[:ts], (db, 1)), jnp.tile(sin_s[:ts], (db, 1))

    even_in_w, mla_w, mla_wt = _even_weights(norm_even[0], w_in_even[0], mla_g_q[0], mla_g_kv[0],
                                             mla_w_uq[0], mla_w_uk[0], mla_w_uv[0])
    s5_w = _s5_weights(s5_lambda_re[0], s5_lambda_im[0], s5_log_dt[0], s5_b_re[0], s5_b_im[0],
                       s5_c_re[0], s5_c_im[0], s5_d[0], s5_w_glu[0], s5_b_glu[0])
    wo_e = w_out_even[0].astype(BF16)
    wo_a, wo_b = wo_e[:MLA_W], wo_e[MLA_W:]

    xp2 = x_prompt.reshape(mp, D_MODEL)
    cq, rows_pt, za, u, zb = _even_in(xp2, even_in_w, cos_p, sin_p, tm, seq=t)
    mix_a = _mla_prompt(cq.reshape(b, t, MLA_QL), rows_pt, za.reshape(b, t, MLA_W), mla_wt, cos_p.T, sin_p.T)
    zeros_p = jnp.zeros((b, S5_S), F32)
    mix_b, sre_p, sim_p = _s5(u.reshape(b, t, S5_W), zb.reshape(b, t, S5_W), s5_w, zeros_p, zeros_p,
                              min(128, t))
    xp1 = _even_out(xp2, mix_a.reshape(mp, MLA_W), mix_b.reshape(mp, S5_W), wo_a, wo_b, tm)

    xs2 = x_sample.reshape(ms, D_MODEL)
    cq_s, rows_s, za_s, u_s, zb_s = _even_in(xs2, even_in_w, cos_st, sin_st, ms)
    pad3 = lambda a, n: _pad_tokens(a.reshape(db, ts, n), SUB)
    mix_a_s = _mla_decode(page_table, _rows_last(cache_mla[0]), pad3(cq_s, MLA_QL), pad3(rows_s, MLA_ROW),
                          pad3(za_s, MLA_W), mla_w, cos_s, sin_s, ts)
    st = state_s5[0]
    mix_b_s, sre_s, sim_s = _s5(u_s.reshape(db, ts, S5_W), zb_s.reshape(db, ts, S5_W), s5_w,
                                st[..., 0].reshape(db, S5_S), st[..., 1].reshape(db, S5_S), ts)
    xs1 = _even_out(xs2, mix_a_s.reshape(ms, MLA_W), mix_b_s.reshape(ms, S5_W), wo_a, wo_b, ms)

    odd_in_w, cmp_w = _odd_weights(norm_odd[0], w_in_odd[0], nsa_pe_k[0], nsa_pe_v[0], nsa_phi1_k[0],
                                   nsa_phi2_k[0], nsa_phi1_v[0], nsa_phi2_v[0])
    wo_o = w_out_odd[0].astype(BF16)
    gfin = norm_final[None, :]

    q, kv4_t, kw_t, gate, z = _odd_in(xp1, odd_in_w, tm, seq=t)
    nblk_p = t // NSA_BLK
    cb = _compress_prompt(kv4_t, *cmp_w)
    q3 = q.reshape(b, t, NSA_W)
    tt = min(512, t)
    oc_t, imp = _cmp(q3, cb.reshape(b, nblk_p, 2 * NSA_KVW), pos_p[None], tt)
    sel, _ = _topk(imp, pos_p[None], nblk_p, tt)
    o_t = _nsa_prompt(q3, kv4_t, kw_t, oc_t, sel, gate.reshape(b, t, 3 * NSA_H))
    y_prompt = _odd_out_prompt(o_t, z.reshape(b, t, NSA_W), xp1.reshape(b, t, D_MODEL), wo_o, gfin, tt)

    q_s, kv4_s, kw_s, gate_s, z_s = _odd_in(xs1, odd_in_w, ms)
    cache_nsa_t = _rows_last(cache_nsa_kv[0])
    cb_s = _compress_decode(page_table, cache_nsa_t, *cmp_w)
    n_sel_s = -(-(past + ts) // NSA_BLK)
    q_s3 = q_s.reshape(db, ts, NSA_W)
    oc_ts, imp_s = _cmp(_pad_tokens(q_s3, SUB), cb_s, pos_s[None], SUB)
    imp_l = jnp.transpose(imp_s[..., :ts], (1, 2, 0, 3)).reshape(1, NSA_G, cb_s.shape[1], ms)
    pos_l = jnp.tile(pos_s[:ts], db)[None]
    _, idx_l = _topk(imp_l, pos_l, n_sel_s, ms)
    idx_s = jnp.transpose(idx_l.reshape(NSA_G, NSA_TOPK, db, ts), (2, 0, 3, 1)).reshape(db, -1)
    q_s4 = q_s.reshape(db, ts, NSA_H, NSA_D)
    o_sel_s = _nsa_sel_decode(idx_s, page_table, cache_nsa_t, q_s4, pad3(kv4_s, 4 * NSA_KVW), ts, past)
    win_t = _rows_last(state_nsa_win[0])
    kw_s3 = kw_s.reshape(db, ts, 2 * NSA_KVW)
    o_win_s, win_st = _nsa_win_decode(_pad_tokens(q_s4, SUB), win_t, _pad_tokens(kw_s3, SUB), ts)
    oc_s = jnp.transpose(oc_ts, (0, 2, 1))[:, :ts].reshape(ms, NSA_W)
    y_sample = _odd_out_decode(oc_s, o_sel_s.reshape(ms, NSA_W), o_win_s.reshape(ms, NSA_W),
                               gate_s, z_s, xs1, wo_o, gfin)

    state = lambda re, im, n: jnp.stack([re, im], axis=-1).reshape(1, n, S5_G, S5_N, 2)
    win_shape = (2, NSA_G, NSA_D)
    if t >= NSA_WIN:
        win_pt = kw_t[:, :, t - NSA_WIN:]
    else:
        win_pt = jnp.pad(kw_t, ((0, 0), (0, 0), (NSA_WIN - t, 0)))
    return (y_prompt, y_sample.reshape(db, ts, D_MODEL),
            _rows_second(rows_pt, (MLA_ROW,))[None], rows_s.reshape(1, db, ts, MLA_ROW),
            state(sre_p, sim_p, b), state(sre_s, sim_s, db),
            _rows_second(kv4_t, (4, NSA_G, NSA_D))[None], kv4_s.reshape(1, db, ts, 4, NSA_G, NSA_D),
            _rows_second(win_pt, win_shape)[None], _rows_second(win_st, win_shape)[None])
```

```python
import functools

import jax
import jax.numpy as jnp
import numpy as np
from jax import lax
from jax.experimental import pallas as pl
from jax.experimental.pallas import tpu as pltpu

F32, BF16, I32 = jnp.float32, jnp.bfloat16, jnp.int32

D_MODEL = 1024
PAGE = 128
EPS = 1e-6
ROPE_THETA = 10000.0
MLA_H, MLA_NOPE, MLA_ROPE, MLA_V = 8, 64, 32, 64
MLA_QL, MLA_KVL = 384, 256
MLA_ROW = MLA_KVL + MLA_ROPE
MLA_W = MLA_H * MLA_V
MLA_SCALE = (MLA_NOPE + MLA_ROPE) ** -0.5
S5_G, S5_P, S5_N = 32, 16, 64
S5_W = S5_G * S5_P
S5_S = S5_G * S5_N
S5_GT = 4
S5_NT = S5_G // S5_GT
NSA_H, NSA_G, NSA_D = 16, 2, 64
NSA_R = NSA_H // NSA_G
NSA_W = NSA_H * NSA_D
NSA_KVW = NSA_G * NSA_D
NSA_BLK, NSA_TOPK, NSA_WIN = 64, 16, 512
NSA_SCALE = NSA_D ** -0.5
FORCED_BONUS = float(NSA_R + 1)
BPP = PAGE // NSA_BLK

LOG2E = 1.4426950408889634
NEG = -1e30
GONE = -3e38
TQ = 128
TK = 256
SUB = 8
PACK = 16
MLA_DECODE_GROUP = 64
VMEM_LIMIT = 56 * 1024 * 1024


def _dot(a, b):
    return jnp.dot(a, b, preferred_element_type=F32)


def _dot_nt(a, b):
    return lax.dot_general(a, b, (((1,), (1,)), ((), ())), preferred_element_type=F32)


def _dot_tn(a, b):
    return lax.dot_general(a, b, (((0,), (0,)), ((), ())), preferred_element_type=F32)


def _rms(x, g):
    return x * lax.rsqrt(jnp.mean(x * x, axis=-1, keepdims=True) + EPS) * g


def _silu(x):
    return x * jax.nn.sigmoid(x)


def _rope_nat(x, cosf, sinf):
    half = x.shape[1] // 2
    xs = jnp.concatenate([x[:, half:], x[:, :half]], axis=1)
    return x * cosf + xs * sinf


def _params(sem):
    return pltpu.CompilerParams(dimension_semantics=sem, vmem_limit_bytes=VMEM_LIMIT)


def _const_spec(shape):
    n = len(shape)
    return pl.BlockSpec(shape, lambda *a, _n=n: (0,) * _n)


def _seq_major_spec(width, tm, seq):
    per = seq // tm
    return pl.BlockSpec((None, width, tm), lambda i: (i // per, 0, i % per))


def _even_in_kernel(feature_major, x_ref, g_ref, wcq, wckv, wkr, wza, wu, wzb, gq, gkv, cos_ref, sin_ref,
                    cq_o, rows_o, za_o, u_o, zb_o):
    h = _rms(x_ref[...], g_ref[...]).astype(BF16)
    cq_o[...] = _rms(_dot(h, wcq[...]), gq[...])
    ckv = _rms(_dot(h, wckv[...]), gkv[...])
    krope = _rope_nat(_dot(h, wkr[...]), cos_ref[...], sin_ref[...])
    if feature_major:
        rows_o[:MLA_KVL, :] = ckv.T
        rows_o[MLA_KVL:, :] = krope.T
    else:
        rows_o[:, :MLA_KVL] = ckv
        rows_o[:, MLA_KVL:] = krope
    za_o[...] = _dot(h, wza[...])
    u_o[...] = _dot(h, wu[...])
    zb_o[...] = _dot(h, wzb[...])


def _even_in(x2, wts, cosf, sinf, tm, seq=None):
    m = x2.shape[0]
    tab_blocks = cosf.shape[0] // tm
    row = lambda n: pl.BlockSpec((tm, n), lambda i: (i, 0))
    tab = pl.BlockSpec((tm, MLA_ROPE), lambda i: (i % tab_blocks, 0))
    widths = (MLA_QL, MLA_ROW, MLA_W, S5_W, S5_W)
    out_specs = [row(n) for n in widths]
    out_shape = [jax.ShapeDtypeStruct((m, n), F32) for n in widths]
    if seq is not None:
        out_specs[1] = _seq_major_spec(MLA_ROW, tm, seq)
        out_shape[1] = jax.ShapeDtypeStruct((m // seq, MLA_ROW, seq), F32)
    return pl.pallas_call(
        functools.partial(_even_in_kernel, seq is not None),
        grid=(m // tm,),
        in_specs=[row(D_MODEL)] + [_const_spec(c.shape) for c in wts] + [tab, tab],
        out_specs=out_specs,
        out_shape=out_shape,
        compiler_params=_params(("parallel",)),
        name="even_in",
    )(x2, *wts, cosf, sinf)


def _mla_queries(cq, wuqn, wuqr, wuk, cosf, sinf, h):
    cqb = cq.astype(BF16)
    qn = _dot(cqb, wuqn[h])
    ql = _dot(qn.astype(BF16), wuk[h])
    qr = _rope_nat(_dot(cqb, wuqr[h]), cosf, sinf)
    return ql, qr


def _flash_tiles(lo, hi, chains, c_exp, late=(), late_lo=None):
    def start(chain, first):
        qk, _, first_mask, _, s_buf, m_s, acc_s, l_s = chain
        m_s[...] = jnp.full(m_s.shape, NEG, F32)
        acc_s[...] = jnp.zeros(acc_s.shape, F32)
        if l_s is not None:
            l_s[...] = jnp.zeros(l_s.shape, F32)
        s_buf[0] = first_mask(qk(first), first)

    for chain in chains:
        start(chain, lo)

    def consume(chain, j, s):
        _, pv, _, _, _, m_s, acc_s, l_s = chain
        m_old = m_s[...]
        m_new = jnp.maximum(m_old, jnp.max(s, axis=0, keepdims=True))
        alpha = jnp.exp2((m_old - m_new) * c_exp)
        x = (s - m_new) * c_exp
        if l_s is None:
            p = jnp.exp2(x.astype(BF16))
        else:
            pf = jnp.exp2(x)
            l_s[...] = alpha * l_s[...] + jnp.sum(pf, axis=0, keepdims=True)
            p = pf.astype(BF16)
        acc_s[...] = alpha * acc_s[...] + pv(j, p)
        m_s[...] = m_new

    def body_of(active):
        def body(j, carry):
            for chain in active:
                chain[4][1] = chain[0](j + 1)
            for chain in active:
                consume(chain, j, chain[4][0])
            for chain in active:
                chain[4][0] = chain[4][1]
            return carry
        return body

    chains = list(chains)
    if late:
        lax.fori_loop(lo, late_lo, body_of(chains), 0)
        for chain in late:
            start(chain, late_lo)
        chains = chains + list(late)
        lo = late_lo
    lax.fori_loop(lo, hi - 1, body_of(chains), 0)
    last = hi - 1
    for chain in chains:
        consume(chain, last, chain[3](chain[4][0], last))
    outs = []
    for chain in chains:
        acc, l_s = chain[6][...], chain[7]
        if l_s is None:
            dv = acc.shape[0] - PACK
            outs.append(acc[:dv, :] / acc[dv:dv + 1, :])
        else:
            outs.append(acc / l_s[...])
    return outs


def _with_ones(v):
    return jnp.concatenate([v, jnp.ones((PACK, v.shape[1]), BF16)], axis=0)


def _mla_prompt_kernel(cq_ref, rows_ref, za_ref, wuqn_t, wuqr_t, wuk_t, wuv_t, cos_ref, sin_ref,
                       out_ref, qtl, qtr, s_buf, m_s, l_s, acc_s):
    qi = pl.program_id(1)
    r = MLA_H * TQ
    half = MLA_ROPE // 2
    cq_t = cq_ref[...].T.astype(BF16)
    qn_t = _dot(wuqn_t[...], cq_t).astype(BF16)
    qr_t = _dot(wuqr_t[...], cq_t)
    cos_t, sin_t = cos_ref[...], sin_ref[...]
    for h in range(MLA_H):
        qtl[:, h * TQ:(h + 1) * TQ] = _dot(wuk_t[h], qn_t[h * MLA_NOPE:(h + 1) * MLA_NOPE, :]).astype(BF16)
        x = qr_t[h * MLA_ROPE:(h + 1) * MLA_ROPE, :]
        xs = jnp.concatenate([x[half:, :], x[:half, :]], axis=0)
        qtr[:, h * TQ:(h + 1) * TQ] = (x * cos_t + xs * sin_t).astype(BF16)
    qpos = qi * TQ + (lax.broadcasted_iota(I32, (1, r), 1) & (TQ - 1))
    krow = lax.broadcasted_iota(I32, (TK, 1), 0)

    def keys(j):
        return rows_ref[:, pl.ds(pl.multiple_of(j * TK, TK), TK)]

    def qk(j):
        kt = keys(j)
        return (_dot_tn(kt[:MLA_KVL, :].astype(BF16), qtl[...])
                + _dot_tn(kt[MLA_KVL:, :].astype(BF16), qtr[...]))

    def pv(j, p):
        return _dot(keys(j)[:MLA_KVL, :].astype(BF16), p)

    def causal(s, j):
        return jnp.where(j * TK + krow <= qpos, s, NEG)

    n_tiles = (qi * TQ + TQ + TK - 1) // TK
    chain = (qk, pv, lambda s, j: s, causal, s_buf, m_s, acc_s, l_s)
    o = _flash_tiles(0, n_tiles, [chain], MLA_SCALE * LOG2E)[0].astype(BF16)
    heads = [_dot(wuv_t[h], o[:, h * TQ:(h + 1) * TQ]) for h in range(MLA_H)]
    out_ref[...] = jnp.concatenate(heads, axis=0).T * _silu(za_ref[...])


def _mla_prompt(cq, rows_t, za, wts, cos_t, sin_t):
    b, t, _ = cq.shape
    r = MLA_H * TQ
    tok = lambda n: pl.BlockSpec((None, TQ, n), lambda bi, qi: (bi, qi, 0))
    tab = pl.BlockSpec((MLA_ROPE, TQ), lambda bi, qi: (0, qi))
    return pl.pallas_call(
        _mla_prompt_kernel,
        grid=(b, t // TQ),
        in_specs=[tok(MLA_QL), pl.BlockSpec((None, MLA_ROW, t), lambda bi, qi: (bi, 0, 0)), tok(MLA_W)]
        + [_const_spec(w.shape) for w in wts] + [tab, tab],
        out_specs=tok(MLA_W),
        out_shape=jax.ShapeDtypeStruct((b, t, MLA_W), F32),
        scratch_shapes=[pltpu.VMEM((MLA_KVL, r), BF16), pltpu.VMEM((MLA_ROPE, r), BF16),
                        pltpu.VMEM((2, TK, r), F32),
                        pltpu.VMEM((1, r), F32), pltpu.VMEM((1, r), F32), pltpu.VMEM((MLA_KVL, r), F32)],
        compiler_params=_params(("parallel", "arbitrary")),
        name="mla_prompt",
    )(cq, rows_t, za, *wts, cos_t, sin_t)


def _mla_decode_kernel(n_pages_step, ts, pt_ref, cq_ref, rows_ref, za_ref, wuqn, wuqr, wuk, wuv,
                       cos_ref, sin_ref, *rest):
    pages = rest[:n_pages_step]
    out_ref, ql_s, qr_s, m_s, l_s, acc_s = rest[n_pages_step:]
    gi = pl.program_id(1)
    rows = MLA_H * ts

    @pl.when(gi == 0)
    def _():
        cq = cq_ref[...]
        cosf, sinf = cos_ref[...], sin_ref[...]
        for h in range(MLA_H):
            ql, qr = _mla_queries(cq, wuqn, wuqr, wuk, cosf, sinf, h)
            ql_s[h * ts:(h + 1) * ts, :] = ql[:ts]
            qr_s[h * ts:(h + 1) * ts, :] = qr[:ts]
        m_s[...] = jnp.full((rows, 1), NEG, F32)
        l_s[...] = jnp.zeros((rows, 1), F32)
        acc_s[...] = jnp.zeros((rows, MLA_KVL), F32)

    def update(s, vals, feature_major):
        m_old = m_s[...]
        m_new = jnp.maximum(m_old, jnp.max(s, axis=1, keepdims=True))
        alpha = jnp.exp(m_old - m_new)
        p = jnp.exp(s - m_new)
        l_s[...] = alpha * l_s[...] + jnp.sum(p, axis=1, keepdims=True)
        pv = None
        for (lo, hi), v in vals:
            pj = p[:, lo:hi].astype(BF16)
            term = _dot_nt(pj, v) if feature_major else _dot(pj, v)
            pv = term if pv is None else pv + term
        acc_s[...] = alpha * acc_s[...] + pv
        m_s[...] = m_new

    ql, qr = ql_s[...].astype(BF16), qr_s[...].astype(BF16)
    scores, vals = [], []
    for j, pg in enumerate(pages):
        k = pg[...]
        ckv = k[:MLA_KVL, :].astype(BF16)
        scores.append(_dot(ql, ckv) + _dot(qr, k[MLA_KVL:, :].astype(BF16)))
        vals.append(((len(vals) * PAGE, (len(vals) + 1) * PAGE), ckv))
        if len(vals) == MLA_DECODE_GROUP or j == n_pages_step - 1:
            update(jnp.concatenate(scores, axis=1) * MLA_SCALE, vals, True)
            scores, vals = [], []

    @pl.when(gi == pl.num_programs(1) - 1)
    def _():
        kn = rows_ref[...]
        ckv = kn[:, :MLA_KVL].astype(BF16)
        s = (_dot_nt(ql, ckv) + _dot_nt(qr, kn[:, MLA_KVL:].astype(BF16))) * MLA_SCALE
        tq = lax.rem(lax.broadcasted_iota(I32, (rows, SUB), 0), ts)
        jk = lax.broadcasted_iota(I32, (rows, SUB), 1)
        s = jnp.where((jk <= tq) & (jk < ts), s, NEG)
        update(s, [((0, SUB), ckv)], False)
        o = (acc_s[...] / l_s[...]).astype(BF16)
        heads = [_dot(o[h * ts:(h + 1) * ts, :], wuv[h]) for h in range(MLA_H)]
        out_ref[...] = jnp.concatenate(heads, axis=1) * _silu(za_ref[...][:ts])


def _mla_decode(page_table, cache_t, cq, rows, za, wts, cosf, sinf, ts, n_pages_step=64):
    db, n_pages = page_table.shape
    rows_n = MLA_H * ts
    tok = lambda n: pl.BlockSpec((None, SUB, n), lambda b, g, pt: (b, 0, 0))
    cst = lambda shape: pl.BlockSpec(shape, lambda b, g, pt, _n=len(shape): (0,) * _n)
    page_specs = [
        pl.BlockSpec((None, MLA_ROW, PAGE), lambda b, g, pt, j=j: (pt[b, g * n_pages_step + j], 0, 0))
        for j in range(n_pages_step)]
    grid_spec = pltpu.PrefetchScalarGridSpec(
        num_scalar_prefetch=1,
        grid=(db, n_pages // n_pages_step),
        in_specs=[tok(MLA_QL), tok(MLA_ROW), tok(MLA_W)] + [cst(w.shape) for w in wts]
        + [cst(cosf.shape), cst(sinf.shape)] + page_specs,
        out_specs=pl.BlockSpec((None, ts, MLA_W), lambda b, g, pt: (b, 0, 0)),
        scratch_shapes=[pltpu.VMEM((rows_n, MLA_KVL), F32), pltpu.VMEM((rows_n, MLA_ROPE), F32),
                        pltpu.VMEM((rows_n, 1), F32), pltpu.VMEM((rows_n, 1), F32),
                        pltpu.VMEM((rows_n, MLA_KVL), F32)])
    return pl.pallas_call(
        functools.partial(_mla_decode_kernel, n_pages_step, ts),
        grid_spec=grid_spec,
        out_shape=jax.ShapeDtypeStruct((db, ts, MLA_W), F32),
        compiler_params=_params(("parallel", "arbitrary")),
        name="mla_decode",
    )(page_table, cq, rows, za, *wts, cosf, sinf, *([cache_t] * n_pages_step))


def _s5_kernel(u_ref, zb_ref, lre_ref, lim_ref, ldt_ref, wbre, wbim, wcre, wcim, d_ref, wglu, bglu,
               h0re_ref, h0im_ref, mix_o, sre_o, sim_o, bure, buim, hre, him):
    step = pl.program_id(0)
    nb, chunk, _ = u_ref.shape
    rows = nb * chunk

    @pl.when(step == 0)
    def _():
        hre[...] = h0re_ref[...]
        him[...] = h0im_ref[...]

    lre, lim = lre_ref[...], lim_ref[...]
    dt = jnp.exp(ldt_ref[...])
    mag = jnp.exp(lre * dt)
    are, aim = mag * jnp.cos(lim * dt), mag * jnp.sin(lim * dt)
    den = lre * lre + lim * lim
    cre = ((are - 1.0) * lre + aim * lim) / den
    cim = (aim * lre - (are - 1.0) * lim) / den

    ut = jnp.swapaxes(u_ref[...], 0, 1).reshape(rows, S5_W)
    kw, nw = S5_GT * S5_P, S5_GT * S5_N
    for jt in range(S5_NT):
        cr, ci = cre[:, jt * nw:(jt + 1) * nw], cim[:, jt * nw:(jt + 1) * nw]
        bre = (cr * wbre[jt] - ci * wbim[jt]).astype(BF16)
        bim = (cr * wbim[jt] + ci * wbre[jt]).astype(BF16)
        uj = ut[:, jt * kw:(jt + 1) * kw].astype(BF16)
        bure[:, jt * nw:(jt + 1) * nw] = _dot(uj, bre)
        buim[:, jt * nw:(jt + 1) * nw] = _dot(uj, bim)

    def scan(t, carry):
        hr, hi = carry
        sl = pl.ds(pl.multiple_of(t * nb, SUB), nb)
        nr = are * hr - aim * hi + bure[sl, :]
        ni = are * hi + aim * hr + buim[sl, :]
        bure[sl, :] = nr
        buim[sl, :] = ni
        return nr, ni

    hr, hi = lax.fori_loop(0, chunk, scan, (hre[...], him[...]))
    hre[...] = hr
    him[...] = hi
    sre_o[...] = hr
    sim_o[...] = hi

    ys = []
    for jt in range(S5_NT):
        sr = bure[:, jt * nw:(jt + 1) * nw].astype(BF16)
        si = buim[:, jt * nw:(jt + 1) * nw].astype(BF16)
        ys.append(_dot(sr, wcre[jt].astype(BF16)) - _dot(si, wcim[jt].astype(BF16)))
    y = jnp.concatenate(ys, axis=1) + d_ref[...] * ut
    g5 = jax.nn.gelu(y)
    ob = g5 * jax.nn.sigmoid(_dot(g5.astype(BF16), wglu[...]) + bglu[...])
    mix_o[...] = jnp.swapaxes(ob.reshape(chunk, nb, S5_W), 0, 1) * _silu(zb_ref[...])


def _s5(u, zb, wts, h0re, h0im, chunk):
    nb, t, _ = u.shape
    tok = pl.BlockSpec((nb, chunk, S5_W), lambda i: (0, i, 0))
    st = _const_spec((nb, S5_S))
    return pl.pallas_call(
        _s5_kernel,
        grid=(t // chunk,),
        in_specs=[tok, tok] + [_const_spec(w.shape) for w in wts] + [st, st],
        out_specs=[tok, st, st],
        out_shape=[jax.ShapeDtypeStruct((nb, t, S5_W), F32), jax.ShapeDtypeStruct((nb, S5_S), F32),
                   jax.ShapeDtypeStruct((nb, S5_S), F32)],
        scratch_shapes=[pltpu.VMEM((nb * chunk, S5_S), F32), pltpu.VMEM((nb * chunk, S5_S), F32),
                        pltpu.VMEM((nb, S5_S), F32), pltpu.VMEM((nb, S5_S), F32)],
        compiler_params=_params(("arbitrary",)),
        name="s5",
    )(u, zb, *wts, h0re, h0im)


def _even_out_kernel(x_ref, a_ref, b_ref, wa, wb, out_ref):
    out_ref[...] = (x_ref[...] + _dot(a_ref[...].astype(BF16), wa[...])
                    + _dot(b_ref[...].astype(BF16), wb[...]))


def _even_out(x2, mixa, mixb, wa, wb, tm):
    m = x2.shape[0]
    row = lambda n: pl.BlockSpec((tm, n), lambda i: (i, 0))
    return pl.pallas_call(
        _even_out_kernel,
        grid=(m // tm,),
        in_specs=[row(D_MODEL), row(MLA_W), row(S5_W), _const_spec(wa.shape), _const_spec(wb.shape)],
        out_specs=row(D_MODEL),
        out_shape=jax.ShapeDtypeStruct((m, D_MODEL), F32),
        compiler_params=_params(("parallel",)),
        name="even_out",
    )(x2, mixa, mixb, wa, wb)


def _odd_in_kernel(feature_major, x_ref, g_ref, wq, wkv4, wkw, wg, wz, q_o, kv4_o, kw_o, gate_o, z_o):
    h = _rms(x_ref[...], g_ref[...]).astype(BF16)
    q_o[...] = _dot(h, wq[...])
    kv4, kw = _dot(h, wkv4[...]), _dot(h, wkw[...])
    kv4_o[...] = kv4.T if feature_major else kv4
    kw_o[...] = kw.T if feature_major else kw
    gate_o[...] = jax.nn.sigmoid(_dot(h, wg[...]))
    z_o[...] = _dot(h, wz[...])


def _odd_in(x2, wts, tm, seq=None):
    m = x2.shape[0]
    row = lambda n: pl.BlockSpec((tm, n), lambda i: (i, 0))
    widths = (NSA_W, 4 * NSA_KVW, 2 * NSA_KVW, 3 * NSA_H, NSA_W)
    out_specs = [row(n) for n in widths]
    out_shape = [jax.ShapeDtypeStruct((m, n), F32) for n in widths]
    if seq is not None:
        for i in (1, 2):
            out_specs[i] = _seq_major_spec(widths[i], tm, seq)
            out_shape[i] = jax.ShapeDtypeStruct((m // seq, widths[i], seq), F32)
    return pl.pallas_call(
        functools.partial(_odd_in_kernel, seq is not None),
        grid=(m // tm,),
        in_specs=[row(D_MODEL)] + [_const_spec(w.shape) for w in wts],
        out_specs=out_specs,
        out_shape=out_shape,
        compiler_params=_params(("parallel",)),
        name="odd_in",
    )(x2, *wts)


def _compress_stage(xs, x_t, pe, base):
    n = x_t.shape[1] // NSA_BLK
    x = x_t.astype(BF16).T.reshape(n, NSA_BLK, 2 * NSA_KVW) + pe.astype(BF16)[None]
    xs[:, pl.ds(pl.multiple_of(base, PACK), n), :] = jnp.swapaxes(x, 0, 1)


def _compress_weights(w1, p1k, p1v):
    w1[...] = jnp.zeros(w1.shape, BF16)
    for s, ref in enumerate((p1k, p1k, p1v, p1v)):
        w1[:, s * NSA_D:(s + 1) * NSA_D, s * NSA_D:(s + 1) * NSA_D] = ref[...].astype(BF16)


def _compress_finish(xs, w1, phi2, out_ref):
    acc = jnp.zeros((xs.shape[1], 2 * NSA_KVW), F32)
    for r in range(NSA_BLK):
        acc = acc + _dot(xs[r], w1[r])
    out_ref[...] = _dot(_silu(acc).astype(BF16), phi2[...])


def _compress_prompt_kernel(x_ref, pe_ref, p1k, p1v, phi2, out_ref, xs, w1):
    bi = pl.program_id(0)
    nblk = x_ref.shape[1] // NSA_BLK
    _compress_stage(xs, x_ref[...], pe_ref[...], bi * nblk)

    @pl.when(bi == pl.num_programs(0) - 1)
    def _():
        _compress_weights(w1, p1k, p1v)
        _compress_finish(xs, w1, phi2, out_ref)


def _compress_prompt(kv4_t, pe, p1k, p1v, phi2):
    b, _, t = kv4_t.shape
    cw = 2 * NSA_KVW
    nblk = b * (t // NSA_BLK)
    consts = (pe, p1k, p1v, phi2)
    return pl.pallas_call(
        _compress_prompt_kernel,
        grid=(b,),
        in_specs=[pl.BlockSpec((None, cw, t), lambda i: (i, 0, 0))] + [_const_spec(c.shape) for c in consts],
        out_specs=pl.BlockSpec((nblk, cw), lambda i: (0, 0)),
        out_shape=jax.ShapeDtypeStruct((nblk, cw), F32),
        scratch_shapes=[pltpu.VMEM((NSA_BLK, nblk, cw), BF16), pltpu.VMEM((NSA_BLK, cw, cw), BF16)],
        compiler_params=_params(("arbitrary",)),
        name="compress_prompt",
    )(kv4_t, *consts)


def _compress_decode_kernel(n_pages_step, pt_ref, pe_ref, p1k, p1v, phi2, *rest):
    pages = rest[:n_pages_step]
    out_ref, xs, w1 = rest[n_pages_step:]
    bi, gi = pl.program_id(0), pl.program_id(1)
    grp = PACK // BPP
    pe = pe_ref[...]

    @pl.when((bi == 0) & (gi == 0))
    def _():
        _compress_weights(w1, p1k, p1v)

    for k in range(n_pages_step // grp):
        x_t = jnp.concatenate([pages[k * grp + j][...] for j in range(grp)], axis=1)
        _compress_stage(xs, x_t, pe, (gi * (n_pages_step // grp) + k) * PACK)

    @pl.when(gi == pl.num_programs(1) - 1)
    def _():
        _compress_finish(xs, w1, phi2, out_ref)


def _compress_decode(page_table, cache_t, pe, p1k, p1v, phi2, n_pages_step=32):
    db, n_pages = page_table.shape
    cw = 2 * NSA_KVW
    nblk = n_pages * BPP
    cst = lambda shape: pl.BlockSpec(shape, lambda b, g, pt, _n=len(shape): (0,) * _n)
    consts = (pe, p1k, p1v, phi2)
    page_specs = [
        pl.BlockSpec((None, cw, PAGE), lambda b, g, pt, j=j: (pt[b, g * n_pages_step + j], 0, 0))
        for j in range(n_pages_step)]
    grid_spec = pltpu.PrefetchScalarGridSpec(
        num_scalar_prefetch=1,
        grid=(db, n_pages // n_pages_step),
        in_specs=[cst(c.shape) for c in consts] + page_specs,
        out_specs=pl.BlockSpec((None, nblk, cw), lambda b, g, pt: (b, 0, 0)),
        scratch_shapes=[pltpu.VMEM((NSA_BLK, nblk, cw), BF16), pltpu.VMEM((NSA_BLK, cw, cw), BF16)])
    return pl.pallas_call(
        functools.partial(_compress_decode_kernel, n_pages_step),
        grid_spec=grid_spec,
        out_shape=jax.ShapeDtypeStruct((db, nblk, cw), F32),
        compiler_params=_params(("arbitrary", "arbitrary")),
        name="compress_decode",
    )(page_table, *consts, *([cache_t] * n_pages_step))


def _alibi_slope(h):
    return 2.0 ** (-8.0 * (h + 1) / NSA_H)


def _split_bf16(x):
    hi = x.astype(BF16)
    return hi, (x - hi.astype(F32)).astype(BF16)


def _cmp_kernel(q_ref, cb_ref, pos_ref, oc_o, imp_o, s_scr):
    tt = q_ref.shape[0]
    nc = cb_ref.shape[0]
    qpos = pos_ref[...]
    cpos = lax.broadcasted_iota(I32, (nc, 1), 0) * NSA_BLK + (NSA_BLK - 1)
    visible = cpos <= qpos
    dist = (qpos - cpos).astype(F32)
    q = q_ref[...] * NSA_SCALE
    cb = cb_ref[...]
    for g in range(NSA_G):
        k_hi, k_lo = _split_bf16(cb[:, g * NSA_D:(g + 1) * NSA_D])
        vc = cb[:, NSA_KVW + g * NSA_D:NSA_KVW + (g + 1) * NSA_D].astype(BF16)
        for i in range(NSA_R):
            h = g * NSA_R + i
            q_hi, q_lo = _split_bf16(q[:, h * NSA_D:(h + 1) * NSA_D])
            s = _dot_nt(k_hi, q_hi) + _dot_nt(k_hi, q_lo) + _dot_nt(k_lo, q_hi)
            s_scr[i * nc:(i + 1) * nc, :] = s - _alibi_slope(h) * dist
        s3 = jnp.where(visible[None], s_scr[...].reshape(NSA_R, nc, tt), NEG)
        mx = jnp.max(s3, axis=1, keepdims=True)
        e = jnp.where(visible[None], jnp.exp(s3 - mx), 0.0)
        den = jnp.sum(e, axis=1, keepdims=True)
        p = e / jnp.where(den > 0, den, 1.0)
        imp_o[g] = jnp.sum(p, axis=0)
        for i in range(NSA_R):
            h = g * NSA_R + i
            oc_o[h * NSA_D:(h + 1) * NSA_D, :] = _dot_tn(vc, p[i].astype(BF16))


def _cmp(q, cb, pos, tt):
    b, t, _ = q.shape
    nc = cb.shape[1]
    return pl.pallas_call(
        _cmp_kernel,
        grid=(b, t // tt),
        in_specs=[pl.BlockSpec((None, tt, NSA_W), lambda bi, ti: (bi, ti, 0)),
                  pl.BlockSpec((None, nc, 2 * NSA_KVW), lambda bi, ti: (bi, 0, 0)),
                  pl.BlockSpec((1, tt), lambda bi, ti: (0, ti))],
        out_specs=[pl.BlockSpec((None, NSA_W, tt), lambda bi, ti: (bi, 0, ti)),
                   pl.BlockSpec((None, NSA_G, nc, tt), lambda bi, ti: (bi, 0, 0, ti))],
        out_shape=[jax.ShapeDtypeStruct((b, NSA_W, t), F32),
                   jax.ShapeDtypeStruct((b, NSA_G, nc, t), F32)],
        scratch_shapes=[pltpu.VMEM((NSA_R * nc, tt), F32)],
        compiler_params=_params(("parallel", "parallel")),
        name="nsa_cmp",
    )(q, cb, pos)


def _topk_kernel(n_sel, imp_ref, pos_ref, sel_o, idx_o):
    _, nc, tt = imp_ref.shape
    nsp = sel_o.shape[1]
    qpos = pos_ref[...]
    blk = lax.broadcasted_iota(I32, (nsp, 1), 0)
    cur = jnp.right_shift(qpos, NSA_BLK.bit_length() - 1)
    forced = (blk == 0) | (blk == cur) | (blk == cur - 1)
    allowed = (blk <= cur) & (blk < n_sel)
    for g in range(NSA_G):
        imp = imp_ref[g]
        if nsp > nc:
            imp = jnp.concatenate([imp, jnp.zeros((nsp - nc, tt), F32)], axis=0)
        score = jnp.where(allowed, imp + jnp.where(forced, FORCED_BONUS, 0.0), NEG)
        chosen = jnp.zeros((nsp, tt), F32)
        picks = []
        for _ in range(NSA_TOPK):
            mx = jnp.max(score, axis=0, keepdims=True)
            first = jnp.min(jnp.where(score == mx, blk, nsp), axis=0, keepdims=True)
            hit = blk == first
            chosen = jnp.where(hit, 1.0, chosen)
            score = jnp.where(hit, GONE, score)
            picks.append(first)
        sel_o[g] = chosen
        idx_o[g] = jnp.concatenate(picks, axis=0)


def _topk(imp, pos, n_sel, tt):
    b, _, nc, t = imp.shape
    nsp = -(-(n_sel + TK // NSA_BLK) // SUB) * SUB
    return pl.pallas_call(
        functools.partial(_topk_kernel, n_sel),
        grid=(b, t // tt),
        in_specs=[pl.BlockSpec((None, NSA_G, nc, tt), lambda bi, ti: (bi, 0, 0, ti)),
                  pl.BlockSpec((1, tt), lambda bi, ti: (0, ti))],
        out_specs=[pl.BlockSpec((None, NSA_G, nsp, tt), lambda bi, ti: (bi, 0, 0, ti)),
                   pl.BlockSpec((None, NSA_G, NSA_TOPK, tt), lambda bi, ti: (bi, 0, 0, ti))],
        out_shape=[jax.ShapeDtypeStruct((b, NSA_G, nsp, t), F32),
                   jax.ShapeDtypeStruct((b, NSA_G, NSA_TOPK, t), I32)],
        compiler_params=_params(("parallel", "parallel")),
        name="nsa_topk",
    )(imp, pos)


def _bf16_parts(x, n):
    parts = []
    for _ in range(n):
        bits = np.asarray(x, np.float32).view(np.uint32)
        top = ((bits + np.uint32(0x7FFF) + ((bits >> np.uint32(16)) & np.uint32(1)))
               & np.uint32(0xFFFF0000)).view(np.float32)
        parts.append(float(top))
        x = float(np.float32(x) - top)
    return parts


def _nsa_prompt_kernel(q_ref, kv4_ref, kw_ref, oc_ref, sel_ref, gate_ref, out_ref,
                       k_aug, q_aug, s_buf, m_s, acc_s):
    qi = pl.program_id(1)
    r = NSA_R * TQ
    bpt = TK // NSA_BLK
    n_parts = SUB // 2
    qt = (q_ref[...] * NSA_SCALE).T.astype(BF16)
    gt = gate_ref[...].T
    lane = lax.broadcasted_iota(I32, (1, r), 1)
    qpos = qi * TQ + (lane & (TQ - 1))
    head = lane // TQ
    krow = lax.broadcasted_iota(I32, (TK, 1), 0)
    klane = lax.broadcasted_iota(I32, (1, TK), 1)
    row8 = lax.broadcasted_iota(I32, (SUB, 1), 0)
    n_tiles = (qi * TQ + TQ + TK - 1) // TK
    onehot = jnp.where(row8 == jnp.right_shift(klane, NSA_BLK.bit_length() - 1), 1.0, 0.0)

    def key_rows(j):
        rel = j * TK - qi * TQ + klane
        coarse = (rel & -NSA_BLK).astype(F32)
        fine = (rel & (NSA_BLK - 1)).astype(F32)
        alibi = jnp.where((row8 & 1) == 0, coarse, fine)
        return jnp.concatenate([alibi, onehot], axis=0).astype(BF16)

    def causal(s, j):
        return jnp.where(j * TK + krow <= qpos, s, NEG)

    def in_window(s, j):
        return jnp.where(qpos - (j * TK + krow) <= NSA_WIN, s, NEG)

    def tile(ref, lo, j):
        return ref[lo:lo + NSA_D, pl.ds(pl.multiple_of(j * TK, TK), TK)].astype(BF16)

    sel_chains, win_chains = [], []
    for g in range(NSA_G):
        qg = jnp.concatenate(
            [qt[(g * NSA_R + i) * NSA_D:(g * NSA_R + i + 1) * NSA_D, :] for i in range(NSA_R)], axis=1)
        slope8 = jnp.zeros((SUB, r), F32)
        for i in range(NSA_R):
            parts = _bf16_parts(_alibi_slope(g * NSA_R + i), n_parts)
            col = jnp.zeros((SUB, 1), F32)
            for k, part in enumerate(parts):
                col = jnp.where(jnp.right_shift(row8, 1) == k, part, col)
            slope8 = jnp.where(head == i, col, slope8)
        ks_lo, vs_lo = 2 * NSA_KVW + g * NSA_D, 3 * NSA_KVW + g * NSA_D
        kw_lo, vw_lo = g * NSA_D, NSA_KVW + g * NSA_D
        cs, cw = g, NSA_G + g
        ks_aug, qs_aug, kw_aug, qw_aug = k_aug.at[cs], q_aug.at[cs], k_aug.at[cw], q_aug.at[cw]
        qs_aug[:NSA_D, :] = qg
        qw_aug[:NSA_D, :] = qg
        qw_aug[NSA_D:, :] = jnp.concatenate([slope8, jnp.zeros((SUB, r), F32)], axis=0).astype(BF16)

        def qk_sel(j, g=g, kg=ks_aug, qa=qs_aug, ks_lo=ks_lo, slope8=slope8):
            kg[:NSA_D, :] = tile(kv4_ref, ks_lo, j)
            kg[NSA_D:, :] = key_rows(j)
            flags = sel_ref[g, pl.ds(j * bpt, SUB), :]
            off = jnp.where(flags > 0.5, 0.0, NEG)
            qa[NSA_D:, :] = jnp.concatenate(
                [slope8, jnp.concatenate([off] * NSA_R, axis=1)], axis=0).astype(BF16)
            return _dot_tn(kg[...], qa[...])

        def qk_win(j, kg=kw_aug, qa=qw_aug, kw_lo=kw_lo):
            kg[:NSA_D, :] = tile(kw_ref, kw_lo, j)
            kg[NSA_D:, :] = key_rows(j)
            return _dot_tn(kg[...], qa[...])

        sel_chains.append((qk_sel, lambda j, p, lo=vs_lo: _dot(_with_ones(tile(kv4_ref, lo, j)), p),
                           lambda s, j: s, causal, s_buf.at[cs], m_s.at[cs], acc_s.at[cs], None))
        win_chains.append((qk_win, lambda j, p, lo=vw_lo: _dot(_with_ones(tile(kw_ref, lo, j)), p),
                           in_window, lambda s, j: causal(in_window(s, j), j),
                           s_buf.at[cw], m_s.at[cw], acc_s.at[cw], None))

    outs = _flash_tiles(0, n_tiles, sel_chains, LOG2E, late=win_chains,
                        late_lo=jnp.maximum(qi * TQ - NSA_WIN, 0) // TK)
    for g in range(NSA_G):
        o_sel, o_win = outs[g], outs[NSA_G + g]
        for i in range(NSA_R):
            h = g * NSA_R + i
            rows = slice(h * NSA_D, (h + 1) * NSA_D)
            cols = slice(i * TQ, (i + 1) * TQ)
            out_ref[rows, :] = (gt[h:h + 1, :] * oc_ref[rows, :]
                                + gt[NSA_H + h:NSA_H + h + 1, :] * o_sel[:, cols]
                                + gt[2 * NSA_H + h:2 * NSA_H + h + 1, :] * o_win[:, cols])


def _nsa_prompt(q, kv4_t, kw_t, oc_t, sel, gate):
    b, t, _ = q.shape
    nsp = sel.shape[2]
    r = NSA_R * TQ
    tok = lambda n: pl.BlockSpec((None, TQ, n), lambda bi, qi: (bi, qi, 0))
    full = lambda n: pl.BlockSpec((None, n, t), lambda bi, qi: (bi, 0, 0))
    return pl.pallas_call(
        _nsa_prompt_kernel,
        grid=(b, t // TQ),
        in_specs=[tok(NSA_W), full(4 * NSA_KVW), full(2 * NSA_KVW),
                  pl.BlockSpec((None, NSA_W, TQ), lambda bi, qi: (bi, 0, qi)),
                  pl.BlockSpec((None, NSA_G, nsp, TQ), lambda bi, qi: (bi, 0, 0, qi)),
                  tok(3 * NSA_H)],
        out_specs=pl.BlockSpec((None, NSA_W, TQ), lambda bi, qi: (bi, 0, qi)),
        out_shape=jax.ShapeDtypeStruct((b, NSA_W, t), F32),
        scratch_shapes=[pltpu.VMEM((2 * NSA_G, NSA_D + 2 * SUB, TK), BF16),
                        pltpu.VMEM((2 * NSA_G, NSA_D + 2 * SUB, r), BF16),
                        pltpu.VMEM((2 * NSA_G, 2, TK, r), F32),
                        pltpu.VMEM((2 * NSA_G, 1, r), F32),
                        pltpu.VMEM((2 * NSA_G, NSA_D + PACK, r), F32)],
        compiler_params=_params(("parallel", "arbitrary")),
        name="nsa_prompt",
    )(q, kv4_t, kw_t, oc_t, sel, gate)


def _softmax_rows(parts):
    mx = None
    for s, _, _ in parts:
        pm = jnp.max(s, axis=1, keepdims=True)
        mx = pm if mx is None else jnp.maximum(mx, pm)
    den, num = None, None
    for s, v, feature_major in parts:
        p = jnp.exp(s - mx)
        d = jnp.sum(p, axis=1, keepdims=True)
        n = _dot_nt(p.astype(BF16), v) if feature_major else _dot(p.astype(BF16), v)
        den = d if den is None else den + d
        num = n if num is None else num + n
    return num / den


def _nsa_sel_decode_kernel(ts, pos0, nbp, idx_ref, pid_ref, q_ref, new_ref, *rest):
    nblk = NSA_G * NSA_TOPK
    pages = rest[:nblk]
    out_ref = rest[nblk]
    b, t = pl.program_id(0), pl.program_id(1)
    q = q_ref[...] * NSA_SCALE
    new = new_ref[...]
    lane = lax.broadcasted_iota(I32, (1, PAGE), 1)
    jn = lax.broadcasted_iota(I32, (1, SUB), 1)
    for g in range(NSA_G):
        qg = q[g * NSA_R:(g + 1) * NSA_R, :].astype(BF16)
        slope = jnp.concatenate(
            [jnp.full((1, 1), _alibi_slope(g * NSA_R + i), F32) for i in range(NSA_R)], axis=0)
        parts = []
        has_new = jnp.zeros((), I32)
        for k in range(NSA_TOPK):
            bid = idx_ref[((b * NSA_G + g) * ts + t) * NSA_TOPK + k]
            pg = pages[g * NSA_TOPK + k][...]
            kk = pg[g * NSA_D:(g + 1) * NSA_D, :].astype(BF16)
            vv = pg[NSA_KVW + g * NSA_D:NSA_KVW + (g + 1) * NSA_D, :].astype(BF16)
            sub = bid % BPP
            rel = ((bid - sub) * NSA_BLK - pos0 + lane).astype(F32)
            s = _dot(qg, kk) + slope * rel
            mine = (lane // NSA_BLK == sub) & (bid < nbp)
            parts.append((jnp.where(mine, s, NEG), vv, True))
            has_new = has_new + (bid >= nbp).astype(I32)
        kn = new[:, 2 * NSA_KVW + g * NSA_D:2 * NSA_KVW + (g + 1) * NSA_D].astype(BF16)
        vn = new[:, 3 * NSA_KVW + g * NSA_D:3 * NSA_KVW + (g + 1) * NSA_D].astype(BF16)
        sn = _dot_nt(qg, kn) + slope * jn.astype(F32)
        ok = (jn <= t) & (jn < ts) & (has_new > 0)
        parts.append((jnp.where(ok, sn, NEG), vn, False))
        out_ref[g * NSA_R:(g + 1) * NSA_R, :] = _softmax_rows(parts)


def _nsa_sel_decode(idx, page_table, cache_t, q4, kv4_new, ts, pos0):
    db, n_pages = page_table.shape
    nbp = n_pages * BPP
    cw = 2 * NSA_KVW
    page_ids = jnp.take_along_axis(page_table, jnp.minimum(idx, nbp - 1) // BPP, axis=1).reshape(-1)
    idx_flat = idx.reshape(-1)

    def page_map(b, t, idx_r, pid, g, k):
        return (pid[((b * NSA_G + g) * ts + t) * NSA_TOPK + k], 1, 0)

    page_specs = [pl.BlockSpec((None, cw, PAGE), functools.partial(page_map, g=g, k=k))
                  for g in range(NSA_G) for k in range(NSA_TOPK)]
    grid_spec = pltpu.PrefetchScalarGridSpec(
        num_scalar_prefetch=2,
        grid=(db, ts),
        in_specs=[pl.BlockSpec((None, None, NSA_H, NSA_D), lambda b, t, idx, pt: (b, t, 0, 0)),
                  pl.BlockSpec((None, SUB, 4 * NSA_KVW), lambda b, t, idx, pt: (b, 0, 0))] + page_specs,
        out_specs=pl.BlockSpec((None, None, NSA_H, NSA_D), lambda b, t, idx, pt: (b, t, 0, 0)))
    return pl.pallas_call(
        functools.partial(_nsa_sel_decode_kernel, ts, pos0, nbp),
        grid_spec=grid_spec,
        out_shape=jax.ShapeDtypeStruct((db, ts, NSA_H, NSA_D), F32),
        compiler_params=_params(("parallel", "arbitrary")),
        name="nsa_sel_decode",
    )(idx_flat, page_ids, q4, kv4_new, *([cache_t] * (NSA_G * NSA_TOPK)))


def _nsa_win_decode_kernel(ts, q_ref, win_ref, new_ref, out_ref, win_o):
    rows = NSA_R * SUB
    win = win_ref[...]
    new = new_ref[...]
    win_o[...] = jnp.concatenate([win[:, ts:], new.T[:, :ts]], axis=1)
    tq = lax.broadcasted_iota(I32, (rows, 1), 0) & (SUB - 1)
    iw = lax.broadcasted_iota(I32, (1, NSA_WIN), 1)
    jn = lax.broadcasted_iota(I32, (1, SUB), 1)
    for g in range(NSA_G):
        qg = jnp.concatenate([q_ref[:, g * NSA_R + i, :] for i in range(NSA_R)], axis=0)
        qg = (qg * NSA_SCALE).astype(BF16)
        slope = jnp.concatenate(
            [jnp.full((SUB, 1), _alibi_slope(g * NSA_R + i), F32) for i in range(NSA_R)], axis=0)
        kw = win[g * NSA_D:(g + 1) * NSA_D, :].astype(BF16)
        vw = win[NSA_KVW + g * NSA_D:NSA_KVW + (g + 1) * NSA_D, :].astype(BF16)
        kn = new[:, g * NSA_D:(g + 1) * NSA_D].astype(BF16)
        vn = new[:, NSA_KVW + g * NSA_D:NSA_KVW + (g + 1) * NSA_D].astype(BF16)
        sw = _dot(qg, kw) + slope * (iw - NSA_WIN).astype(F32)
        sw = jnp.where(iw >= tq, sw, NEG)
        sn = _dot_nt(qg, kn) + slope * jn.astype(F32)
        sn = jnp.where((jn <= tq) & (jn < ts), sn, NEG)
        o = _softmax_rows([(sw, vw, True), (sn, vn, False)])
        for i in range(NSA_R):
            out_ref[:, g * NSA_R + i, :] = o[i * SUB:i * SUB + ts, :]


def _nsa_win_decode(q4, win_t, kw_new, ts):
    db = q4.shape[0]
    return pl.pallas_call(
        functools.partial(_nsa_win_decode_kernel, ts),
        grid=(db,),
        in_specs=[pl.BlockSpec((None, SUB, NSA_H, NSA_D), lambda b: (b, 0, 0, 0)),
                  pl.BlockSpec((None, 2 * NSA_KVW, NSA_WIN), lambda b: (b, 0, 0)),
                  pl.BlockSpec((None, SUB, 2 * NSA_KVW), lambda b: (b, 0, 0))],
        out_specs=[pl.BlockSpec((None, ts, NSA_H, NSA_D), lambda b: (b, 0, 0, 0)),
                   pl.BlockSpec((None, 2 * NSA_KVW, NSA_WIN), lambda b: (b, 0, 0))],
        out_shape=[jax.ShapeDtypeStruct((db, ts, NSA_H, NSA_D), F32),
                   jax.ShapeDtypeStruct((db, 2 * NSA_KVW, NSA_WIN), F32)],
        compiler_params=_params(("parallel",)),
        name="nsa_win_decode",
    )(q4, win_t, kw_new)


def _odd_tail(o, z_ref, x_ref, wout, gfin):
    y = x_ref[...] + _dot((o * _silu(z_ref[...])).astype(BF16), wout[...])
    return _rms(y, gfin[...])


def _odd_out_prompt_kernel(ot_ref, z_ref, x_ref, wout, gfin, out_ref):
    out_ref[...] = _odd_tail(ot_ref[...].T, z_ref, x_ref, wout, gfin)


def _odd_out_prompt(ot, z, x, wout, gfin, tm):
    b, _, t = ot.shape
    tok = pl.BlockSpec((None, tm, D_MODEL), lambda bi, ti: (bi, ti, 0))
    return pl.pallas_call(
        _odd_out_prompt_kernel,
        grid=(b, t // tm),
        in_specs=[pl.BlockSpec((None, NSA_W, tm), lambda bi, ti: (bi, 0, ti)), tok, tok,
                  _const_spec(wout.shape), _const_spec(gfin.shape)],
        out_specs=tok,
        out_shape=jax.ShapeDtypeStruct((b, t, D_MODEL), F32),
        compiler_params=_params(("parallel", "parallel")),
        name="odd_out_prompt",
    )(ot, z, x, wout, gfin)


def _odd_out_decode_kernel(oc_ref, os_ref, ow_ref, gate_ref, z_ref, x_ref, wout, gfin, out_ref):
    gate = gate_ref[...]
    oc, osel, ow = oc_ref[...], os_ref[...], ow_ref[...]
    heads = []
    for h in range(NSA_H):
        c = slice(h * NSA_D, (h + 1) * NSA_D)
        heads.append(gate[:, h:h + 1] * oc[:, c] + gate[:, NSA_H + h:NSA_H + h + 1] * osel[:, c]
                     + gate[:, 2 * NSA_H + h:2 * NSA_H + h + 1] * ow[:, c])
    out_ref[...] = _odd_tail(jnp.concatenate(heads, axis=1), z_ref, x_ref, wout, gfin)


def _odd_out_decode(oc, osel, ow, gate, z, x, wout, gfin):
    m = x.shape[0]
    args = (oc, osel, ow, gate, z, x, wout, gfin)
    return pl.pallas_call(
        _odd_out_decode_kernel,
        grid=(1,),
        in_specs=[_const_spec(a.shape) for a in args],
        out_specs=_const_spec((m, D_MODEL)),
        out_shape=jax.ShapeDtypeStruct((m, D_MODEL), F32),
        compiler_params=_params(("arbitrary",)),
        name="odd_out_decode",
    )(*args)


def _rope_tables(pos):
    half = MLA_ROPE // 2
    inv = ROPE_THETA ** (-jnp.arange(half, dtype=F32) / half)
    ang = pos.astype(F32)[:, None] * inv[None, :]
    cos, sin = jnp.cos(ang), jnp.sin(ang)
    return jnp.concatenate([cos, cos], axis=1), jnp.concatenate([-sin, sin], axis=1)


def _block_diag(x):
    t, g, r, c = x.shape
    eye = jnp.eye(g, dtype=x.dtype)
    return jnp.einsum("tgrc,gh->tgrhc", x, eye).reshape(t, g * r, g * c)


def _even_weights(norm_g, w_in, g_q, g_kv, w_uq, w_uk, w_uv):
    edges = [0, MLA_QL, MLA_QL + MLA_KVL, MLA_QL + MLA_ROW]
    edges += [edges[-1] + MLA_W, edges[-1] + MLA_W + S5_W, edges[-1] + MLA_W + 2 * S5_W]
    wb = w_in.astype(BF16)
    pieces = [wb[:, edges[i]:edges[i + 1]] for i in range(6)]
    in_w = (norm_g[None, :], *pieces, g_q[None, :], g_kv[None, :])
    uq = jnp.transpose(w_uq, (1, 0, 2)).astype(BF16)
    mla_w = (uq[:, :, :MLA_NOPE], uq[:, :, MLA_NOPE:],
             jnp.transpose(w_uk, (1, 2, 0)).astype(BF16),
             jnp.transpose(w_uv, (1, 0, 2)).astype(BF16))
    uq_t = jnp.transpose(w_uq, (1, 2, 0)).astype(BF16)
    mla_wt = (uq_t[:, :MLA_NOPE].reshape(MLA_H * MLA_NOPE, MLA_QL),
              uq_t[:, MLA_NOPE:].reshape(MLA_H * MLA_ROPE, MLA_QL),
              jnp.transpose(w_uk, (1, 0, 2)).astype(BF16),
              jnp.transpose(w_uv, (1, 2, 0)).astype(BF16))
    return in_w, mla_w, mla_wt


def _s5_weights(lam_re, lam_im, log_dt, b_re, b_im, c_re, c_im, d_skip, w_glu, b_glu):
    def bmat(b):
        return _block_diag(jnp.transpose(b.reshape(S5_NT, S5_GT, S5_N, S5_P), (0, 1, 3, 2)))

    def cmat(c):
        return _block_diag(jnp.transpose(c.reshape(S5_NT, S5_GT, S5_P, S5_N), (0, 1, 3, 2)))

    return (lam_re.reshape(1, S5_S), lam_im.reshape(1, S5_S),
            jnp.repeat(log_dt, S5_N).reshape(1, S5_S),
            bmat(b_re), bmat(b_im), cmat(c_re), cmat(c_im),
            d_skip.reshape(1, S5_W), w_glu.astype(BF16), b_glu[None, :])


def _odd_weights(norm_g, w_in, pe_k, pe_v, phi1_k, phi2_k, phi1_v, phi2_v):
    wb = w_in.astype(BF16)
    e0 = NSA_W
    e1 = e0 + 4 * NSA_KVW
    e2 = e1 + 2 * NSA_KVW
    e3 = e2 + 3 * NSA_H
    in_w = (norm_g[None, :], wb[:, :e0], wb[:, e0:e1], wb[:, e1:e2], wb[:, e2:e3], wb[:, e3:])
    pe = jnp.concatenate([pe_k, pe_k, pe_v, pe_v], axis=1)
    p1k = phi1_k.reshape(NSA_BLK, NSA_D, NSA_D)
    p1v = phi1_v.reshape(NSA_BLK, NSA_D, NSA_D)
    phi2 = _block_diag(jnp.stack([phi2_k, phi2_k, phi2_v, phi2_v], axis=0)[None])[0].astype(BF16)
    return in_w, (pe, p1k, p1v, phi2)


def _pad_tokens(x, n):
    return jnp.pad(x, ((0, 0), (0, n - x.shape[1])) + ((0, 0),) * (x.ndim - 2))


def _rows_last(x):
    nd = x.ndim
    xt = jnp.transpose(x, (0,) + tuple(range(2, nd)) + (1,))
    return xt.reshape(x.shape[0], -1, x.shape[1])


def _rows_second(x_t, feature_shape):
    b, _, rows = x_t.shape
    nf = len(feature_shape)
    xt = x_t.reshape((b,) + tuple(feature_shape) + (rows,))
    return jnp.transpose(xt, (0, nf + 1) + tuple(range(1, nf + 1)))


def kernel(x_prompt, x_sample, cache_mla, state_s5, cache_nsa_kv, state_nsa_win, page_table, norm_even, w_in_even, mla_g_q, mla_g_kv, mla_w_uq, mla_w_uk, mla_w_uv, s5_lambda_re, s5_lambda_im, s5_log_dt, s5_b_re, s5_b_im, s5_c_re, s5_c_im, s5_d, s5_w_glu, s5_b_glu, w_out_even, norm_odd, w_in_odd, nsa_pe_k, nsa_pe_v, nsa_phi1_k, nsa_phi2_k, nsa_phi1_v, nsa_phi2_v, w_out_odd, norm_final):
    b, t, _ = x_prompt.shape
    db, ts, _ = x_sample.shape
    n_pages = page_table.shape[1]
    past = n_pages * PAGE
    mp, ms = b * t, db * ts
    tm = min(1024, t)
    assert t % (PACK * NSA_BLK) == 0 and t % TK == 0 and ts <= SUB
    assert t <= NSA_BLK * 256

    pos_p = jnp.arange(t, dtype=I32)
    pos_s = past + jnp.arange(SUB, dtype=I32)
    cos_p, sin_p = _rope_tables(pos_p)
    cos_s, sin_s = _rope_tables(pos_s)
    cos_st, sin_st = jnp.tile(cos_s[:ts], (db, 1)), jnp.tile(sin_s[:ts], (db, 1))

    even_in_w, mla_w, mla_wt = _even_weights(norm_even[0], w_in_even[0], mla_g_q[0], mla_g_kv[0],
                                             mla_w_uq[0], mla_w_uk[0], mla_w_uv[0])
    s5_w = _s5_weights(s5_lambda_re[0], s5_lambda_im[0], s5_log_dt[0], s5_b_re[0], s5_b_im[0],
                       s5_c_re[0], s5_c_im[0], s5_d[0], s5_w_glu[0], s5_b_glu[0])
    wo_e = w_out_even[0].astype(BF16)
    wo_a, wo_b = wo_e[:MLA_W], wo_e[MLA_W:]

    xp2 = x_prompt.reshape(mp, D_MODEL)
    cq, rows_pt, za, u, zb = _even_in(xp2, even_in_w, cos_p, sin_p, tm, seq=t)
    mix_a = _mla_prompt(cq.reshape(b, t, MLA_QL), rows_pt, za.reshape(b, t, MLA_W), mla_wt, cos_p.T, sin_p.T)
    zeros_p = jnp.zeros((b, S5_S), F32)
    mix_b, sre_p, sim_p = _s5(u.reshape(b, t, S5_W), zb.reshape(b, t, S5_W), s5_w, zeros_p, zeros_p,
                              min(128, t))
    xp1 = _even_out(xp2, mix_a.reshape(mp, MLA_W), mix_b.reshape(mp, S5_W), wo_a, wo_b, tm)

    xs2 = x_sample.reshape(ms, D_MODEL)
    cq_s, rows_s, za_s, u_s, zb_s = _even_in(xs2, even_in_w, cos_st, sin_st, ms)
    pad3 = lambda a, n: _pad_tokens(a.reshape(db, ts, n), SUB)
    mix_a_s = _mla_decode(page_table, _rows_last(cache_mla[0]), pad3(cq_s, MLA_QL), pad3(rows_s, MLA_ROW),
                          pad3(za_s, MLA_W), mla_w, cos_s, sin_s, ts)
    st = state_s5[0]
    mix_b_s, sre_s, sim_s = _s5(u_s.reshape(db, ts, S5_W), zb_s.reshape(db, ts, S5_W), s5_w,
                                st[..., 0].reshape(db, S5_S), st[..., 1].reshape(db, S5_S), ts)
    xs1 = _even_out(xs2, mix_a_s.reshape(ms, MLA_W), mix_b_s.reshape(ms, S5_W), wo_a, wo_b, ms)

    odd_in_w, cmp_w = _odd_weights(norm_odd[0], w_in_odd[0], nsa_pe_k[0], nsa_pe_v[0], nsa_phi1_k[0],
                                   nsa_phi2_k[0], nsa_phi1_v[0], nsa_phi2_v[0])
    wo_o = w_out_odd[0].astype(BF16)
    gfin = norm_final[None, :]

    q, kv4_t, kw_t, gate, z = _odd_in(xp1, odd_in_w, tm, seq=t)
    nblk_p = t // NSA_BLK
    cb = _compress_prompt(kv4_t, *cmp_w)
    q3 = q.reshape(b, t, NSA_W)
    tt = min(512, t)
    oc_t, imp = _cmp(q3, cb.reshape(b, nblk_p, 2 * NSA_KVW), pos_p[None], tt)
    sel, _ = _topk(imp, pos_p[None], nblk_p, tt)
    o_t = _nsa_prompt(q3, kv4_t, kw_t, oc_t, sel, gate.reshape(b, t, 3 * NSA_H))
    y_prompt = _odd_out_prompt(o_t, z.reshape(b, t, NSA_W), xp1.reshape(b, t, D_MODEL), wo_o, gfin, tt)

    q_s, kv4_s, kw_s, gate_s, z_s = _odd_in(xs1, odd_in_w, ms)
    cache_nsa_t = _rows_last(cache_nsa_kv[0])
    cb_s = _compress_decode(page_table, cache_nsa_t, *cmp_w)
    n_sel_s = -(-(past + ts) // NSA_BLK)
    q_s3 = q_s.reshape(db, ts, NSA_W)
    oc_ts, imp_s = _cmp(_pad_tokens(q_s3, SUB), cb_s, pos_s[None], SUB)
    imp_l = jnp.transpose(imp_s[..., :ts], (1, 2, 0, 3)).reshape(1, NSA_G, cb_s.shape[1], ms)
    pos_l = jnp.tile(pos_s[:ts], db)[None]
    _, idx_l = _topk(imp_l, pos_l, n_sel_s, ms)
    idx_s = jnp.transpose(idx_l.reshape(NSA_G, NSA_TOPK, db, ts), (2, 0, 3, 1)).reshape(db, -1)
    q_s4 = q_s.reshape(db, ts, NSA_H, NSA_D)
    o_sel_s = _nsa_sel_decode(idx_s, page_table, cache_nsa_t, q_s4, pad3(kv4_s, 4 * NSA_KVW), ts, past)
    win_t = _rows_last(state_nsa_win[0])
    kw_s3 = kw_s.reshape(db, ts, 2 * NSA_KVW)
    o_win_s, win_st = _nsa_win_decode(_pad_tokens(q_s4, SUB), win_t, _pad_tokens(kw_s3, SUB), ts)
    oc_s = jnp.transpose(oc_ts, (0, 2, 1))[:, :ts].reshape(ms, NSA_W)
    y_sample = _odd_out_decode(oc_s, o_sel_s.reshape(ms, NSA_W), o_win_s.reshape(ms, NSA_W),
                               gate_s, z_s, xs1, wo_o, gfin)

    state = lambda re, im, n: jnp.stack([re, im], axis=-1).reshape(1, n, S5_G, S5_N, 2)
    win_shape = (2, NSA_G, NSA_D)
    if t >= NSA_WIN:
        win_pt = kw_t[:, :, t - NSA_WIN:]
    else:
        win_pt = jnp.pad(kw_t, ((0, 0), (0, 0), (NSA_WIN - t, 0)))
    return (y_prompt, y_sample.reshape(db, ts, D_MODEL),
            _rows_second(rows_pt, (MLA_ROW,))[None], rows_s.reshape(1, db, ts, MLA_ROW),
            state(sre_p, sim_p, b), state(sre_s, sim_s, db),
            _rows_second(kv4_t, (4, NSA_G, NSA_D))[None], kv4_s.reshape(1, db, ts, 4, NSA_G, NSA_D),
            _rows_second(win_pt, win_shape)[None], _rows_second(win_st, win_shape)[None])
```

```python
import functools

import jax
import jax.numpy as jnp
import numpy as np
from jax import lax
from jax.experimental import pallas as pl
from jax.experimental.pallas import tpu as pltpu

F32, BF16, I32 = jnp.float32, jnp.bfloat16, jnp.int32

D_MODEL = 1024
PAGE = 128
EPS = 1e-6
ROPE_THETA = 10000.0
MLA_H, MLA_NOPE, MLA_ROPE, MLA_V = 8, 64, 32, 64
MLA_QL, MLA_KVL = 384, 256
MLA_ROW = MLA_KVL + MLA_ROPE
MLA_W = MLA_H * MLA_V
MLA_SCALE = (MLA_NOPE + MLA_ROPE) ** -0.5
S5_G, S5_P, S5_N = 32, 16, 64
S5_W = S5_G * S5_P
S5_S = S5_G * S5_N
S5_GT = 4
S5_NT = S5_G // S5_GT
NSA_H, NSA_G, NSA_D = 16, 2, 64
NSA_R = NSA_H // NSA_G
NSA_W = NSA_H * NSA_D
NSA_KVW = NSA_G * NSA_D
NSA_BLK, NSA_TOPK, NSA_WIN = 64, 16, 512
NSA_SCALE = NSA_D ** -0.5
FORCED_BONUS = float(NSA_R + 1)
BPP = PAGE // NSA_BLK

LOG2E = 1.4426950408889634
NEG = -1e30
GONE = -3e38
TQ = 128
TK = 256
SUB = 8
PACK = 16
VMEM_LIMIT = 56 * 1024 * 1024


def _dot(a, b):
    return jnp.dot(a, b, preferred_element_type=F32)


def _dot_nt(a, b):
    return lax.dot_general(a, b, (((1,), (1,)), ((), ())), preferred_element_type=F32)


def _dot_tn(a, b):
    return lax.dot_general(a, b, (((0,), (0,)), ((), ())), preferred_element_type=F32)


def _rms(x, g):
    return x * lax.rsqrt(jnp.mean(x * x, axis=-1, keepdims=True) + EPS) * g


def _silu(x):
    return x * jax.nn.sigmoid(x)


def _rope_nat(x, cosf, sinf):
    half = x.shape[1] // 2
    xs = jnp.concatenate([x[:, half:], x[:, :half]], axis=1)
    return x * cosf + xs * sinf


def _params(sem):
    return pltpu.CompilerParams(dimension_semantics=sem, vmem_limit_bytes=VMEM_LIMIT)


def _const_spec(shape):
    n = len(shape)
    return pl.BlockSpec(shape, lambda *a, _n=n: (0,) * _n)


def _seq_major_spec(width, tm, seq):
    per = seq // tm
    return pl.BlockSpec((None, width, tm), lambda i: (i // per, 0, i % per))


def _even_in_kernel(feature_major, x_ref, g_ref, wcq, wckv, wkr, wza, wu, wzb, gq, gkv, cos_ref, sin_ref,
                    cq_o, rows_o, za_o, u_o, zb_o):
    h = _rms(x_ref[...], g_ref[...]).astype(BF16)
    cq_o[...] = _rms(_dot(h, wcq[...]), gq[...])
    ckv = _rms(_dot(h, wckv[...]), gkv[...])
    krope = _rope_nat(_dot(h, wkr[...]), cos_ref[...], sin_ref[...])
    if feature_major:
        rows_o[:MLA_KVL, :] = ckv.T
        rows_o[MLA_KVL:, :] = krope.T
    else:
        rows_o[:, :MLA_KVL] = ckv
        rows_o[:, MLA_KVL:] = krope
    za_o[...] = _dot(h, wza[...])
    u_o[...] = _dot(h, wu[...])
    zb_o[...] = _dot(h, wzb[...])


def _even_in(x2, wts, cosf, sinf, tm, seq=None):
    m = x2.shape[0]
    tab_blocks = cosf.shape[0] // tm
    row = lambda n: pl.BlockSpec((tm, n), lambda i: (i, 0))
    tab = pl.BlockSpec((tm, MLA_ROPE), lambda i: (i % tab_blocks, 0))
    widths = (MLA_QL, MLA_ROW, MLA_W, S5_W, S5_W)
    out_specs = [row(n) for n in widths]
    out_shape = [jax.ShapeDtypeStruct((m, n), F32) for n in widths]
    if seq is not None:
        out_specs[1] = _seq_major_spec(MLA_ROW, tm, seq)
        out_shape[1] = jax.ShapeDtypeStruct((m // seq, MLA_ROW, seq), F32)
    return pl.pallas_call(
        functools.partial(_even_in_kernel, seq is not None),
        grid=(m // tm,),
        in_specs=[row(D_MODEL)] + [_const_spec(c.shape) for c in wts] + [tab, tab],
        out_specs=out_specs,
        out_shape=out_shape,
        compiler_params=_params(("parallel",)),
        name="even_in",
    )(x2, *wts, cosf, sinf)


def _mla_queries(cq, wuqn, wuqr, wuk, cosf, sinf, h):
    cqb = cq.astype(BF16)
    qn = _dot(cqb, wuqn[h])
    ql = _dot(qn.astype(BF16), wuk[h])
    qr = _rope_nat(_dot(cqb, wuqr[h]), cosf, sinf)
    return ql, qr


def _flash_tiles(lo, hi, chains, c_exp, late=(), late_lo=None):
    def start(chain, first):
        qk, _, first_mask, _, s_buf, m_s, acc_s, l_s = chain
        m_s[...] = jnp.full(m_s.shape, NEG, F32)
        acc_s[...] = jnp.zeros(acc_s.shape, F32)
        if l_s is not None:
            l_s[...] = jnp.zeros(l_s.shape, F32)
        s_buf[0] = first_mask(qk(first), first)

    for chain in chains:
        start(chain, lo)

    def consume(chain, j, s):
        _, pv, _, _, _, m_s, acc_s, l_s = chain
        m_old = m_s[...]
        m_new = jnp.maximum(m_old, jnp.max(s, axis=0, keepdims=True))
        alpha = jnp.exp2((m_old - m_new) * c_exp)
        x = (s - m_new) * c_exp
        if l_s is None:
            p = jnp.exp2(x.astype(BF16))
        else:
            pf = jnp.exp2(x)
            l_s[...] = alpha * l_s[...] + jnp.sum(pf, axis=0, keepdims=True)
            p = pf.astype(BF16)
        acc_s[...] = alpha * acc_s[...] + pv(j, p)
        m_s[...] = m_new

    def body_of(active):
        def body(j, carry):
            for chain in active:
                chain[4][1] = chain[0](j + 1)
            for chain in active:
                consume(chain, j, chain[4][0])
            for chain in active:
                chain[4][0] = chain[4][1]
            return carry
        return body

    chains = list(chains)
    if late:
        lax.fori_loop(lo, late_lo, body_of(chains), 0)
        for chain in late:
            start(chain, late_lo)
        chains = chains + list(late)
        lo = late_lo
    lax.fori_loop(lo, hi - 1, body_of(chains), 0)
    last = hi - 1
    for chain in chains:
        consume(chain, last, chain[3](chain[4][0], last))
    outs = []
    for chain in chains:
        acc, l_s = chain[6][...], chain[7]
        if l_s is None:
            dv = acc.shape[0] - PACK
            outs.append(acc[:dv, :] / acc[dv:dv + 1, :])
        else:
            outs.append(acc / l_s[...])
    return outs


def _with_ones(v):
    return jnp.concatenate([v, jnp.ones((PACK, v.shape[1]), BF16)], axis=0)


def _mla_prompt_kernel(cq_ref, rows_ref, za_ref, wuqn_t, wuqr_t, wuk_t, wuv_t, cos_ref, sin_ref,
                       out_ref, qtl, qtr, s_buf, m_s, l_s, acc_s):
    qi = pl.program_id(1)
    r = MLA_H * TQ
    half = MLA_ROPE // 2
    cq_t = cq_ref[...].T.astype(BF16)
    qn_t = _dot(wuqn_t[...], cq_t).astype(BF16)
    qr_t = _dot(wuqr_t[...], cq_t)
    cos_t, sin_t = cos_ref[...], sin_ref[...]
    for h in range(MLA_H):
        qtl[:, h * TQ:(h + 1) * TQ] = _dot(wuk_t[h], qn_t[h * MLA_NOPE:(h + 1) * MLA_NOPE, :]).astype(BF16)
        x = qr_t[h * MLA_ROPE:(h + 1) * MLA_ROPE, :]
        xs = jnp.concatenate([x[half:, :], x[:half, :]], axis=0)
        qtr[:, h * TQ:(h + 1) * TQ] = (x * cos_t + xs * sin_t).astype(BF16)
    qpos = qi * TQ + (lax.broadcasted_iota(I32, (1, r), 1) & (TQ - 1))
    krow = lax.broadcasted_iota(I32, (TK, 1), 0)

    def keys(j):
        return rows_ref[:, pl.ds(pl.multiple_of(j * TK, TK), TK)]

    def qk(j):
        kt = keys(j)
        return (_dot_tn(kt[:MLA_KVL, :].astype(BF16), qtl[...])
                + _dot_tn(kt[MLA_KVL:, :].astype(BF16), qtr[...]))

    def pv(j, p):
        return _dot(keys(j)[:MLA_KVL, :].astype(BF16), p)

    def causal(s, j):
        return jnp.where(j * TK + krow <= qpos, s, NEG)

    n_tiles = (qi * TQ + TQ + TK - 1) // TK
    chain = (qk, pv, lambda s, j: s, causal, s_buf, m_s, acc_s, l_s)
    o = _flash_tiles(0, n_tiles, [chain], MLA_SCALE * LOG2E)[0].astype(BF16)
    heads = [_dot(wuv_t[h], o[:, h * TQ:(h + 1) * TQ]) for h in range(MLA_H)]
    out_ref[...] = jnp.concatenate(heads, axis=0).T * _silu(za_ref[...])


def _mla_prompt(cq, rows_t, za, wts, cos_t, sin_t):
    b, t, _ = cq.shape
    r = MLA_H * TQ
    tok = lambda n: pl.BlockSpec((None, TQ, n), lambda bi, qi: (bi, qi, 0))
    tab = pl.BlockSpec((MLA_ROPE, TQ), lambda bi, qi: (0, qi))
    return pl.pallas_call(
        _mla_prompt_kernel,
        grid=(b, t // TQ),
        in_specs=[tok(MLA_QL), pl.BlockSpec((None, MLA_ROW, t), lambda bi, qi: (bi, 0, 0)), tok(MLA_W)]
        + [_const_spec(w.shape) for w in wts] + [tab, tab],
        out_specs=tok(MLA_W),
        out_shape=jax.ShapeDtypeStruct((b, t, MLA_W), F32),
        scratch_shapes=[pltpu.VMEM((MLA_KVL, r), BF16), pltpu.VMEM((MLA_ROPE, r), BF16),
                        pltpu.VMEM((2, TK, r), F32),
                        pltpu.VMEM((1, r), F32), pltpu.VMEM((1, r), F32), pltpu.VMEM((MLA_KVL, r), F32)],
        compiler_params=_params(("parallel", "arbitrary")),
        name="mla_prompt",
    )(cq, rows_t, za, *wts, cos_t, sin_t)


def _mla_decode_kernel(n_pages_step, ts, pt_ref, cq_ref, rows_ref, za_ref, wuqn, wuqr, wuk, wuv,
                       cos_ref, sin_ref, *rest):
    pages = rest[:n_pages_step]
    out_ref, ql_s, qr_s, m_s, l_s, acc_s = rest[n_pages_step:]
    gi = pl.program_id(1)
    rows = MLA_H * ts

    @pl.when(gi == 0)
    def _():
        cq = cq_ref[...]
        cosf, sinf = cos_ref[...], sin_ref[...]
        for h in range(MLA_H):
            ql, qr = _mla_queries(cq, wuqn, wuqr, wuk, cosf, sinf, h)
            ql_s[h * ts:(h + 1) * ts, :] = ql[:ts]
            qr_s[h * ts:(h + 1) * ts, :] = qr[:ts]
        m_s[...] = jnp.full((rows, 1), NEG, F32)
        l_s[...] = jnp.zeros((rows, 1), F32)
        acc_s[...] = jnp.zeros((rows, MLA_KVL), F32)

    def update(s, vals, feature_major):
        m_old = m_s[...]
        m_new = jnp.maximum(m_old, jnp.max(s, axis=1, keepdims=True))
        alpha = jnp.exp(m_old - m_new)
        p = jnp.exp(s - m_new)
        l_s[...] = alpha * l_s[...] + jnp.sum(p, axis=1, keepdims=True)
        pv = None
        for (lo, hi), v in vals:
            pj = p[:, lo:hi].astype(BF16)
            term = _dot_nt(pj, v) if feature_major else _dot(pj, v)
            pv = term if pv is None else pv + term
        acc_s[...] = alpha * acc_s[...] + pv
        m_s[...] = m_new

    ql, qr = ql_s[...].astype(BF16), qr_s[...].astype(BF16)
    ckv_all = jnp.concatenate([pg[:MLA_KVL, :].astype(BF16) for pg in pages], axis=1)
    kr_all = jnp.concatenate([pg[MLA_KVL:, :].astype(BF16) for pg in pages], axis=1)
    s_all = (_dot(ql, ckv_all) + _dot(qr, kr_all)) * MLA_SCALE
    update(s_all, [((0, n_pages_step * PAGE), ckv_all)], True)

    @pl.when(gi == pl.num_programs(1) - 1)
    def _():
        kn = rows_ref[...]
        ckv = kn[:, :MLA_KVL].astype(BF16)
        s = (_dot_nt(ql, ckv) + _dot_nt(qr, kn[:, MLA_KVL:].astype(BF16))) * MLA_SCALE
        tq = lax.rem(lax.broadcasted_iota(I32, (rows, SUB), 0), ts)
        jk = lax.broadcasted_iota(I32, (rows, SUB), 1)
        s = jnp.where((jk <= tq) & (jk < ts), s, NEG)
        update(s, [((0, SUB), ckv)], False)
        o = (acc_s[...] / l_s[...]).astype(BF16)
        heads = [_dot(o[h * ts:(h + 1) * ts, :], wuv[h]) for h in range(MLA_H)]
        out_ref[...] = jnp.concatenate(heads, axis=1) * _silu(za_ref[...][:ts])


def _mla_decode(page_table, cache_t, cq, rows, za, wts, cosf, sinf, ts, n_pages_step=64):
    db, n_pages = page_table.shape
    rows_n = MLA_H * ts
    tok = lambda n: pl.BlockSpec((None, SUB, n), lambda b, g, pt: (b, 0, 0))
    cst = lambda shape: pl.BlockSpec(shape, lambda b, g, pt, _n=len(shape): (0,) * _n)
    page_specs = [
        pl.BlockSpec((None, MLA_ROW, PAGE), lambda b, g, pt, j=j: (pt[b, g * n_pages_step + j], 0, 0))
        for j in range(n_pages_step)]
    grid_spec = pltpu.PrefetchScalarGridSpec(
        num_scalar_prefetch=1,
        grid=(db, n_pages // n_pages_step),
        in_specs=[tok(MLA_QL), tok(MLA_ROW), tok(MLA_W)] + [cst(w.shape) for w in wts]
        + [cst(cosf.shape), cst(sinf.shape)] + page_specs,
        out_specs=pl.BlockSpec((None, ts, MLA_W), lambda b, g, pt: (b, 0, 0)),
        scratch_shapes=[pltpu.VMEM((rows_n, MLA_KVL), F32), pltpu.VMEM((rows_n, MLA_ROPE), F32),
                        pltpu.VMEM((rows_n, 1), F32), pltpu.VMEM((rows_n, 1), F32),
                        pltpu.VMEM((rows_n, MLA_KVL), F32)])
    return pl.pallas_call(
        functools.partial(_mla_decode_kernel, n_pages_step, ts),
        grid_spec=grid_spec,
        out_shape=jax.ShapeDtypeStruct((db, ts, MLA_W), F32),
        compiler_params=_params(("parallel", "arbitrary")),
        name="mla_decode",
    )(page_table, cq, rows, za, *wts, cosf, sinf, *([cache_t] * n_pages_step))


def _s5_kernel(u_ref, zb_ref, lre_ref, lim_ref, ldt_ref, wbre, wbim, wcre, wcim, d_ref, wglu, bglu,
               h0re_ref, h0im_ref, mix_o, sre_o, sim_o, bure, buim, hre, him):
    step = pl.program_id(0)
    nb, chunk, _ = u_ref.shape
    rows = nb * chunk

    @pl.when(step == 0)
    def _():
        hre[...] = h0re_ref[...]
        him[...] = h0im_ref[...]

    lre, lim = lre_ref[...], lim_ref[...]
    dt = jnp.exp(ldt_ref[...])
    mag = jnp.exp(lre * dt)
    are, aim = mag * jnp.cos(lim * dt), mag * jnp.sin(lim * dt)
    den = lre * lre + lim * lim
    cre = ((are - 1.0) * lre + aim * lim) / den
    cim = (aim * lre - (are - 1.0) * lim) / den

    ut = jnp.swapaxes(u_ref[...], 0, 1).reshape(rows, S5_W)
    kw, nw = S5_GT * S5_P, S5_GT * S5_N
    for jt in range(S5_NT):
        cr, ci = cre[:, jt * nw:(jt + 1) * nw], cim[:, jt * nw:(jt + 1) * nw]
        bre = (cr * wbre[jt] - ci * wbim[jt]).astype(BF16)
        bim = (cr * wbim[jt] + ci * wbre[jt]).astype(BF16)
        uj = ut[:, jt * kw:(jt + 1) * kw].astype(BF16)
        bure[:, jt * nw:(jt + 1) * nw] = _dot(uj, bre)
        buim[:, jt * nw:(jt + 1) * nw] = _dot(uj, bim)

    def scan(t, carry):
        hr, hi = carry
        sl = pl.ds(pl.multiple_of(t * nb, SUB), nb)
        nr = are * hr - aim * hi + bure[sl, :]
        ni = are * hi + aim * hr + buim[sl, :]
        bure[sl, :] = nr
        buim[sl, :] = ni
        return nr, ni

    hr, hi = lax.fori_loop(0, chunk, scan, (hre[...], him[...]))
    hre[...] = hr
    him[...] = hi
    sre_o[...] = hr
    sim_o[...] = hi

    ys = []
    for jt in range(S5_NT):
        sr = bure[:, jt * nw:(jt + 1) * nw].astype(BF16)
        si = buim[:, jt * nw:(jt + 1) * nw].astype(BF16)
        ys.append(_dot(sr, wcre[jt].astype(BF16)) - _dot(si, wcim[jt].astype(BF16)))
    y = jnp.concatenate(ys, axis=1) + d_ref[...] * ut
    g5 = jax.nn.gelu(y)
    ob = g5 * jax.nn.sigmoid(_dot(g5.astype(BF16), wglu[...]) + bglu[...])
    mix_o[...] = jnp.swapaxes(ob.reshape(chunk, nb, S5_W), 0, 1) * _silu(zb_ref[...])


def _s5(u, zb, wts, h0re, h0im, chunk):
    nb, t, _ = u.shape
    tok = pl.BlockSpec((nb, chunk, S5_W), lambda i: (0, i, 0))
    st = _const_spec((nb, S5_S))
    return pl.pallas_call(
        _s5_kernel,
        grid=(t // chunk,),
        in_specs=[tok, tok] + [_const_spec(w.shape) for w in wts] + [st, st],
        out_specs=[tok, st, st],
        out_shape=[jax.ShapeDtypeStruct((nb, t, S5_W), F32), jax.ShapeDtypeStruct((nb, S5_S), F32),
                   jax.ShapeDtypeStruct((nb, S5_S), F32)],
        scratch_shapes=[pltpu.VMEM((nb * chunk, S5_S), F32), pltpu.VMEM((nb * chunk, S5_S), F32),
                        pltpu.VMEM((nb, S5_S), F32), pltpu.VMEM((nb, S5_S), F32)],
        compiler_params=_params(("arbitrary",)),
        name="s5",
    )(u, zb, *wts, h0re, h0im)


def _even_out_kernel(x_ref, a_ref, b_ref, wa, wb, out_ref):
    out_ref[...] = (x_ref[...] + _dot(a_ref[...].astype(BF16), wa[...])
                    + _dot(b_ref[...].astype(BF16), wb[...]))


def _even_out(x2, mixa, mixb, wa, wb, tm):
    m = x2.shape[0]
    row = lambda n: pl.BlockSpec((tm, n), lambda i: (i, 0))
    return pl.pallas_call(
        _even_out_kernel,
        grid=(m // tm,),
        in_specs=[row(D_MODEL), row(MLA_W), row(S5_W), _const_spec(wa.shape), _const_spec(wb.shape)],
        out_specs=row(D_MODEL),
        out_shape=jax.ShapeDtypeStruct((m, D_MODEL), F32),
        compiler_params=_params(("parallel",)),
        name="even_out",
    )(x2, mixa, mixb, wa, wb)


def _odd_in_kernel(feature_major, x_ref, g_ref, wq, wkv4, wkw, wg, wz, q_o, kv4_o, kw_o, gate_o, z_o):
    h = _rms(x_ref[...], g_ref[...]).astype(BF16)
    q_o[...] = _dot(h, wq[...])
    kv4, kw = _dot(h, wkv4[...]), _dot(h, wkw[...])
    kv4_o[...] = kv4.T if feature_major else kv4
    kw_o[...] = kw.T if feature_major else kw
    gate_o[...] = jax.nn.sigmoid(_dot(h, wg[...]))
    z_o[...] = _dot(h, wz[...])


def _odd_in(x2, wts, tm, seq=None):
    m = x2.shape[0]
    row = lambda n: pl.BlockSpec((tm, n), lambda i: (i, 0))
    widths = (NSA_W, 4 * NSA_KVW, 2 * NSA_KVW, 3 * NSA_H, NSA_W)
    out_specs = [row(n) for n in widths]
    out_shape = [jax.ShapeDtypeStruct((m, n), F32) for n in widths]
    if seq is not None:
        for i in (1, 2):
            out_specs[i] = _seq_major_spec(widths[i], tm, seq)
            out_shape[i] = jax.ShapeDtypeStruct((m // seq, widths[i], seq), F32)
    return pl.pallas_call(
        functools.partial(_odd_in_kernel, seq is not None),
        grid=(m // tm,),
        in_specs=[row(D_MODEL)] + [_const_spec(w.shape) for w in wts],
        out_specs=out_specs,
        out_shape=out_shape,
        compiler_params=_params(("parallel",)),
        name="odd_in",
    )(x2, *wts)


def _compress_stage(xs, x_t, pe, base):
    n = x_t.shape[1] // NSA_BLK
    x = x_t.astype(BF16).T.reshape(n, NSA_BLK, 2 * NSA_KVW) + pe.astype(BF16)[None]
    xs[:, pl.ds(pl.multiple_of(base, PACK), n), :] = jnp.swapaxes(x, 0, 1)


def _compress_weights(w1, p1k, p1v):
    w1[...] = jnp.zeros(w1.shape, BF16)
    for s, ref in enumerate((p1k, p1k, p1v, p1v)):
        w1[:, s * NSA_D:(s + 1) * NSA_D, s * NSA_D:(s + 1) * NSA_D] = ref[...].astype(BF16)


def _compress_finish(xs, w1, phi2, out_ref):
    acc = jnp.zeros((xs.shape[1], 2 * NSA_KVW), F32)
    for r in range(NSA_BLK):
        acc = acc + _dot(xs[r], w1[r])
    out_ref[...] = _dot(_silu(acc).astype(BF16), phi2[...])


def _compress_prompt_kernel(x_ref, pe_ref, p1k, p1v, phi2, out_ref, xs, w1):
    bi = pl.program_id(0)
    nblk = x_ref.shape[1] // NSA_BLK
    _compress_stage(xs, x_ref[...], pe_ref[...], bi * nblk)

    @pl.when(bi == pl.num_programs(0) - 1)
    def _():
        _compress_weights(w1, p1k, p1v)
        _compress_finish(xs, w1, phi2, out_ref)


def _compress_prompt(kv4_t, pe, p1k, p1v, phi2):
    b, _, t = kv4_t.shape
    cw = 2 * NSA_KVW
    nblk = b * (t // NSA_BLK)
    consts = (pe, p1k, p1v, phi2)
    return pl.pallas_call(
        _compress_prompt_kernel,
        grid=(b,),
        in_specs=[pl.BlockSpec((None, cw, t), lambda i: (i, 0, 0))] + [_const_spec(c.shape) for c in consts],
        out_specs=pl.BlockSpec((nblk, cw), lambda i: (0, 0)),
        out_shape=jax.ShapeDtypeStruct((nblk, cw), F32),
        scratch_shapes=[pltpu.VMEM((NSA_BLK, nblk, cw), BF16), pltpu.VMEM((NSA_BLK, cw, cw), BF16)],
        compiler_params=_params(("arbitrary",)),
        name="compress_prompt",
    )(kv4_t, *consts)


def _compress_decode_kernel(n_pages_step, pt_ref, pe_ref, p1k, p1v, phi2, *rest):
    pages = rest[:n_pages_step]
    out_ref, xs, w1 = rest[n_pages_step:]
    bi, gi = pl.program_id(0), pl.program_id(1)
    grp = PACK // BPP
    pe = pe_ref[...]

    @pl.when((bi == 0) & (gi == 0))
    def _():
        _compress_weights(w1, p1k, p1v)

    for k in range(n_pages_step // grp):
        x_t = jnp.concatenate([pages[k * grp + j][...] for j in range(grp)], axis=1)
        _compress_stage(xs, x_t, pe, (gi * (n_pages_step // grp) + k) * PACK)

    @pl.when(gi == pl.num_programs(1) - 1)
    def _():
        _compress_finish(xs, w1, phi2, out_ref)


def _compress_decode(page_table, cache_t, pe, p1k, p1v, phi2, n_pages_step=64):
    db, n_pages = page_table.shape
    cw = 2 * NSA_KVW
    nblk = n_pages * BPP
    cst = lambda shape: pl.BlockSpec(shape, lambda b, g, pt, _n=len(shape): (0,) * _n)
    consts = (pe, p1k, p1v, phi2)
    page_specs = [
        pl.BlockSpec((None, cw, PAGE), lambda b, g, pt, j=j: (pt[b, g * n_pages_step + j], 0, 0))
        for j in range(n_pages_step)]
    grid_spec = pltpu.PrefetchScalarGridSpec(
        num_scalar_prefetch=1,
        grid=(db, n_pages // n_pages_step),
        in_specs=[cst(c.shape) for c in consts] + page_specs,
        out_specs=pl.BlockSpec((None, nblk, cw), lambda b, g, pt: (b, 0, 0)),
        scratch_shapes=[pltpu.VMEM((NSA_BLK, nblk, cw), BF16), pltpu.VMEM((NSA_BLK, cw, cw), BF16)])
    return pl.pallas_call(
        functools.partial(_compress_decode_kernel, n_pages_step),
        grid_spec=grid_spec,
        out_shape=jax.ShapeDtypeStruct((db, nblk, cw), F32),
        compiler_params=_params(("arbitrary", "arbitrary")),
        name="compress_decode",
    )(page_table, *consts, *([cache_t] * n_pages_step))


def _alibi_slope(h):
    return 2.0 ** (-8.0 * (h + 1) / NSA_H)


def _split_bf16(x):
    hi = x.astype(BF16)
    return hi, (x - hi.astype(F32)).astype(BF16)


def _cmp_kernel(q_ref, cb_ref, pos_ref, oc_o, imp_o, s_scr):
    tt = q_ref.shape[0]
    nc = cb_ref.shape[0]
    qpos = pos_ref[...]
    cpos = lax.broadcasted_iota(I32, (nc, 1), 0) * NSA_BLK + (NSA_BLK - 1)
    visible = cpos <= qpos
    dist = (qpos - cpos).astype(F32)
    q = q_ref[...] * NSA_SCALE
    cb = cb_ref[...]
    for g in range(NSA_G):
        k_hi, k_lo = _split_bf16(cb[:, g * NSA_D:(g + 1) * NSA_D])
        vc = cb[:, NSA_KVW + g * NSA_D:NSA_KVW + (g + 1) * NSA_D].astype(BF16)
        for i in range(NSA_R):
            h = g * NSA_R + i
            q_hi, q_lo = _split_bf16(q[:, h * NSA_D:(h + 1) * NSA_D])
            s = _dot_nt(k_hi, q_hi) + _dot_nt(k_hi, q_lo) + _dot_nt(k_lo, q_hi)
            s_scr[i * nc:(i + 1) * nc, :] = s - _alibi_slope(h) * dist
        s3 = jnp.where(visible[None], s_scr[...].reshape(NSA_R, nc, tt), NEG)
        mx = jnp.max(s3, axis=1, keepdims=True)
        e = jnp.where(visible[None], jnp.exp(s3 - mx), 0.0)
        den = jnp.sum(e, axis=1, keepdims=True)
        p = e / jnp.where(den > 0, den, 1.0)
        imp_o[g] = jnp.sum(p, axis=0)
        for i in range(NSA_R):
            h = g * NSA_R + i
            oc_o[h * NSA_D:(h + 1) * NSA_D, :] = _dot_tn(vc, p[i].astype(BF16))


def _cmp(q, cb, pos, tt):
    b, t, _ = q.shape
    nc = cb.shape[1]
    return pl.pallas_call(
        _cmp_kernel,
        grid=(b, t // tt),
        in_specs=[pl.BlockSpec((None, tt, NSA_W), lambda bi, ti: (bi, ti, 0)),
                  pl.BlockSpec((None, nc, 2 * NSA_KVW), lambda bi, ti: (bi, 0, 0)),
                  pl.BlockSpec((1, tt), lambda bi, ti: (0, ti))],
        out_specs=[pl.BlockSpec((None, NSA_W, tt), lambda bi, ti: (bi, 0, ti)),
                   pl.BlockSpec((None, NSA_G, nc, tt), lambda bi, ti: (bi, 0, 0, ti))],
        out_shape=[jax.ShapeDtypeStruct((b, NSA_W, t), F32),
                   jax.ShapeDtypeStruct((b, NSA_G, nc, t), F32)],
        scratch_shapes=[pltpu.VMEM((NSA_R * nc, tt), F32)],
        compiler_params=_params(("parallel", "parallel")),
        name="nsa_cmp",
    )(q, cb, pos)


def _topk_kernel(n_sel, imp_ref, pos_ref, sel_o, idx_o):
    _, nc, tt = imp_ref.shape
    nsp = sel_o.shape[1]
    qpos = pos_ref[...]
    blk = lax.broadcasted_iota(I32, (nsp, 1), 0)
    cur = jnp.right_shift(qpos, NSA_BLK.bit_length() - 1)
    forced = (blk == 0) | (blk == cur) | (blk == cur - 1)
    allowed = (blk <= cur) & (blk < n_sel)
    for g in range(NSA_G):
        imp = imp_ref[g]
        if nsp > nc:
            imp = jnp.concatenate([imp, jnp.zeros((nsp - nc, tt), F32)], axis=0)
        score = jnp.where(allowed, imp + jnp.where(forced, FORCED_BONUS, 0.0), NEG)
        chosen = jnp.zeros((nsp, tt), F32)
        picks = []
        for _ in range(NSA_TOPK):
            mx = jnp.max(score, axis=0, keepdims=True)
            first = jnp.min(jnp.where(score == mx, blk, nsp), axis=0, keepdims=True)
            hit = blk == first
            chosen = jnp.where(hit, 1.0, chosen)
            score = jnp.where(hit, GONE, score)
            picks.append(first)
        sel_o[g] = chosen
        idx_o[g] = jnp.concatenate(picks, axis=0)


def _topk(imp, pos, n_sel, tt):
    b, _, nc, t = imp.shape
    nsp = -(-(n_sel + TK // NSA_BLK) // SUB) * SUB
    return pl.pallas_call(
        functools.partial(_topk_kernel, n_sel),
        grid=(b, t // tt),
        in_specs=[pl.BlockSpec((None, NSA_G, nc, tt), lambda bi, ti: (bi, 0, 0, ti)),
                  pl.BlockSpec((1, tt), lambda bi, ti: (0, ti))],
        out_specs=[pl.BlockSpec((None, NSA_G, nsp, tt), lambda bi, ti: (bi, 0, 0, ti)),
                   pl.BlockSpec((None, NSA_G, NSA_TOPK, tt), lambda bi, ti: (bi, 0, 0, ti))],
        out_shape=[jax.ShapeDtypeStruct((b, NSA_G, nsp, t), F32),
                   jax.ShapeDtypeStruct((b, NSA_G, NSA_TOPK, t), I32)],
        compiler_params=_params(("parallel", "parallel")),
        name="nsa_topk",
    )(imp, pos)


def _bf16_parts(x, n):
    parts = []
    for _ in range(n):
        bits = np.asarray(x, np.float32).view(np.uint32)
        top = ((bits + np.uint32(0x7FFF) + ((bits >> np.uint32(16)) & np.uint32(1)))
               & np.uint32(0xFFFF0000)).view(np.float32)
        parts.append(float(top))
        x = float(np.float32(x) - top)
    return parts


def _nsa_prompt_kernel(q_ref, kv4_ref, kw_ref, oc_ref, sel_ref, gate_ref, out_ref,
                       k_aug, q_aug, s_buf, m_s, acc_s):
    qi = pl.program_id(1)
    r = NSA_R * TQ
    bpt = TK // NSA_BLK
    n_parts = SUB // 2
    qt = (q_ref[...] * NSA_SCALE).T.astype(BF16)
    gt = gate_ref[...].T
    lane = lax.broadcasted_iota(I32, (1, r), 1)
    qpos = qi * TQ + (lane & (TQ - 1))
    head = lane // TQ
    krow = lax.broadcasted_iota(I32, (TK, 1), 0)
    klane = lax.broadcasted_iota(I32, (1, TK), 1)
    row8 = lax.broadcasted_iota(I32, (SUB, 1), 0)
    n_tiles = (qi * TQ + TQ + TK - 1) // TK
    onehot = jnp.where(row8 == jnp.right_shift(klane, NSA_BLK.bit_length() - 1), 1.0, 0.0)

    def key_rows(j):
        rel = j * TK - qi * TQ + klane
        coarse = (rel & -NSA_BLK).astype(F32)
        fine = (rel & (NSA_BLK - 1)).astype(F32)
        alibi = jnp.where((row8 & 1) == 0, coarse, fine)
        return jnp.concatenate([alibi, onehot], axis=0).astype(BF16)

    def causal(s, j):
        return jnp.where(j * TK + krow <= qpos, s, NEG)

    def in_window(s, j):
        return jnp.where(qpos - (j * TK + krow) <= NSA_WIN, s, NEG)

    def tile(ref, lo, j):
        return ref[lo:lo + NSA_D, pl.ds(pl.multiple_of(j * TK, TK), TK)].astype(BF16)

    sel_chains, win_chains = [], []
    for g in range(NSA_G):
        qg = jnp.concatenate(
            [qt[(g * NSA_R + i) * NSA_D:(g * NSA_R + i + 1) * NSA_D, :] for i in range(NSA_R)], axis=1)
        slope8 = jnp.zeros((SUB, r), F32)
        for i in range(NSA_R):
            parts = _bf16_parts(_alibi_slope(g * NSA_R + i), n_parts)
            col = jnp.zeros((SUB, 1), F32)
            for k, part in enumerate(parts):
                col = jnp.where(jnp.right_shift(row8, 1) == k, part, col)
            slope8 = jnp.where(head == i, col, slope8)
        ks_lo, vs_lo = 2 * NSA_KVW + g * NSA_D, 3 * NSA_KVW + g * NSA_D
        kw_lo, vw_lo = g * NSA_D, NSA_KVW + g * NSA_D
        cs, cw = g, NSA_G + g
        ks_aug, qs_aug, kw_aug, qw_aug = k_aug.at[cs], q_aug.at[cs], k_aug.at[cw], q_aug.at[cw]
        qs_aug[:NSA_D, :] = qg
        qw_aug[:NSA_D, :] = qg
        qw_aug[NSA_D:, :] = jnp.concatenate([slope8, jnp.zeros((SUB, r), F32)], axis=0).astype(BF16)

        def qk_sel(j, g=g, kg=ks_aug, qa=qs_aug, ks_lo=ks_lo, slope8=slope8):
            kg[:NSA_D, :] = tile(kv4_ref, ks_lo, j)
            kg[NSA_D:, :] = key_rows(j)
            flags = sel_ref[g, pl.ds(j * bpt, SUB), :]
            off = jnp.where(flags > 0.5, 0.0, NEG)
            qa[NSA_D:, :] = jnp.concatenate(
                [slope8, jnp.concatenate([off] * NSA_R, axis=1)], axis=0).astype(BF16)
            return _dot_tn(kg[...], qa[...])

        def qk_win(j, kg=kw_aug, qa=qw_aug, kw_lo=kw_lo):
            kg[:NSA_D, :] = tile(kw_ref, kw_lo, j)
            kg[NSA_D:, :] = key_rows(j)
            return _dot_tn(kg[...], qa[...])

        sel_chains.append((qk_sel, lambda j, p, lo=vs_lo: _dot(_with_ones(tile(kv4_ref, lo, j)), p),
                           lambda s, j: s, causal, s_buf.at[cs], m_s.at[cs], acc_s.at[cs], None))
        win_chains.append((qk_win, lambda j, p, lo=vw_lo: _dot(_with_ones(tile(kw_ref, lo, j)), p),
                           in_window, lambda s, j: causal(in_window(s, j), j),
                           s_buf.at[cw], m_s.at[cw], acc_s.at[cw], None))

    outs = _flash_tiles(0, n_tiles, sel_chains, LOG2E, late=win_chains,
                        late_lo=jnp.maximum(qi * TQ - NSA_WIN, 0) // TK)
    for g in range(NSA_G):
        o_sel, o_win = outs[g], outs[NSA_G + g]
        for i in range(NSA_R):
            h = g * NSA_R + i
            rows = slice(h * NSA_D, (h + 1) * NSA_D)
            cols = slice(i * TQ, (i + 1) * TQ)
            out_ref[rows, :] = (gt[h:h + 1, :] * oc_ref[rows, :]
                                + gt[NSA_H + h:NSA_H + h + 1, :] * o_sel[:, cols]
                                + gt[2 * NSA_H + h:2 * NSA_H + h + 1, :] * o_win[:, cols])


def _nsa_prompt(q, kv4_t, kw_t, oc_t, sel, gate):
    b, t, _ = q.shape
    nsp = sel.shape[2]
    r = NSA_R * TQ
    tok = lambda n: pl.BlockSpec((None, TQ, n), lambda bi, qi: (bi, qi, 0))
    full = lambda n: pl.BlockSpec((None, n, t), lambda bi, qi: (bi, 0, 0))
    return pl.pallas_call(
        _nsa_prompt_kernel,
        grid=(b, t // TQ),
        in_specs=[tok(NSA_W), full(4 * NSA_KVW), full(2 * NSA_KVW),
                  pl.BlockSpec((None, NSA_W, TQ), lambda bi, qi: (bi, 0, qi)),
                  pl.BlockSpec((None, NSA_G, nsp, TQ), lambda bi, qi: (bi, 0, 0, qi)),
                  tok(3 * NSA_H)],
        out_specs=pl.BlockSpec((None, NSA_W, TQ), lambda bi, qi: (bi, 0, qi)),
        out_shape=jax.ShapeDtypeStruct((b, NSA_W, t), F32),
        scratch_shapes=[pltpu.VMEM((2 * NSA_G, NSA_D + 2 * SUB, TK), BF16),
                        pltpu.VMEM((2 * NSA_G, NSA_D + 2 * SUB, r), BF16),
                        pltpu.VMEM((2 * NSA_G, 2, TK, r), F32),
                        pltpu.VMEM((2 * NSA_G, 1, r), F32),
                        pltpu.VMEM((2 * NSA_G, NSA_D + PACK, r), F32)],
        compiler_params=_params(("parallel", "arbitrary")),
        name="nsa_prompt",
    )(q, kv4_t, kw_t, oc_t, sel, gate)


def _softmax_rows(parts):
    mx = None
    for s, _, _ in parts:
        pm = jnp.max(s, axis=1, keepdims=True)
        mx = pm if mx is None else jnp.maximum(mx, pm)
    den, num = None, None
    for s, v, feature_major in parts:
        p = jnp.exp(s - mx)
        d = jnp.sum(p, axis=1, keepdims=True)
        n = _dot_nt(p.astype(BF16), v) if feature_major else _dot(p.astype(BF16), v)
        den = d if den is None else den + d
        num = n if num is None else num + n
    return num / den


def _nsa_sel_decode_kernel(ts, pos0, nbp, idx_ref, pid_ref, q_ref, new_ref, *rest):
    nblk = NSA_G * NSA_TOPK
    pages = rest[:nblk]
    out_ref = rest[nblk]
    b, t = pl.program_id(0), pl.program_id(1)
    q = q_ref[...] * NSA_SCALE
    new = new_ref[...]
    lane = lax.broadcasted_iota(I32, (1, PAGE), 1)
    jn = lax.broadcasted_iota(I32, (1, SUB), 1)
    for g in range(NSA_G):
        qg = q[g * NSA_R:(g + 1) * NSA_R, :].astype(BF16)
        slope = jnp.concatenate(
            [jnp.full((1, 1), _alibi_slope(g * NSA_R + i), F32) for i in range(NSA_R)], axis=0)
        parts = []
        has_new = jnp.zeros((), I32)
        for k in range(NSA_TOPK):
            bid = idx_ref[((b * NSA_G + g) * ts + t) * NSA_TOPK + k]
            pg = pages[g * NSA_TOPK + k][...]
            kk = pg[g * NSA_D:(g + 1) * NSA_D, :].astype(BF16)
            vv = pg[NSA_KVW + g * NSA_D:NSA_KVW + (g + 1) * NSA_D, :].astype(BF16)
            sub = bid % BPP
            rel = ((bid - sub) * NSA_BLK - pos0 + lane).astype(F32)
            s = _dot(qg, kk) + slope * rel
            mine = (lane // NSA_BLK == sub) & (bid < nbp)
            parts.append((jnp.where(mine, s, NEG), vv, True))
            has_new = has_new + (bid >= nbp).astype(I32)
        kn = new[:, 2 * NSA_KVW + g * NSA_D:2 * NSA_KVW + (g + 1) * NSA_D].astype(BF16)
        vn = new[:, 3 * NSA_KVW + g * NSA_D:3 * NSA_KVW + (g + 1) * NSA_D].astype(BF16)
        sn = _dot_nt(qg, kn) + slope * jn.astype(F32)
        ok = (jn <= t) & (jn < ts) & (has_new > 0)
        parts.append((jnp.where(ok, sn, NEG), vn, False))
        out_ref[g * NSA_R:(g + 1) * NSA_R, :] = _softmax_rows(parts)


def _nsa_sel_decode(idx, page_table, cache_t, q4, kv4_new, ts, pos0):
    db, n_pages = page_table.shape
    nbp = n_pages * BPP
    cw = 2 * NSA_KVW
    page_ids = jnp.take_along_axis(page_table, jnp.minimum(idx, nbp - 1) // BPP, axis=1).reshape(-1)
    idx_flat = idx.reshape(-1)

    def page_map(b, t, idx_r, pid, g, k):
        return (pid[((b * NSA_G + g) * ts + t) * NSA_TOPK + k], 1, 0)

    page_specs = [pl.BlockSpec((None, cw, PAGE), functools.partial(page_map, g=g, k=k))
                  for g in range(NSA_G) for k in range(NSA_TOPK)]
    grid_spec = pltpu.PrefetchScalarGridSpec(
        num_scalar_prefetch=2,
        grid=(db, ts),
        in_specs=[pl.BlockSpec((None, None, NSA_H, NSA_D), lambda b, t, idx, pt: (b, t, 0, 0)),
                  pl.BlockSpec((None, SUB, 4 * NSA_KVW), lambda b, t, idx, pt: (b, 0, 0))] + page_specs,
        out_specs=pl.BlockSpec((None, None, NSA_H, NSA_D), lambda b, t, idx, pt: (b, t, 0, 0)))
    return pl.pallas_call(
        functools.partial(_nsa_sel_decode_kernel, ts, pos0, nbp),
        grid_spec=grid_spec,
        out_shape=jax.ShapeDtypeStruct((db, ts, NSA_H, NSA_D), F32),
        compiler_params=_params(("parallel", "arbitrary")),
        name="nsa_sel_decode",
    )(idx_flat, page_ids, q4, kv4_new, *([cache_t] * (NSA_G * NSA_TOPK)))


def _nsa_win_decode_kernel(ts, q_ref, win_ref, new_ref, out_ref, win_o):
    rows = NSA_R * SUB
    win = win_ref[...]
    new = new_ref[...]
    win_o[...] = jnp.concatenate([win[:, ts:], new.T[:, :ts]], axis=1)
    tq = lax.broadcasted_iota(I32, (rows, 1), 0) & (SUB - 1)
    iw = lax.broadcasted_iota(I32, (1, NSA_WIN), 1)
    jn = lax.broadcasted_iota(I32, (1, SUB), 1)
    for g in range(NSA_G):
        qg = jnp.concatenate([q_ref[:, g * NSA_R + i, :] for i in range(NSA_R)], axis=0)
        qg = (qg * NSA_SCALE).astype(BF16)
        slope = jnp.concatenate(
            [jnp.full((SUB, 1), _alibi_slope(g * NSA_R + i), F32) for i in range(NSA_R)], axis=0)
        kw = win[g * NSA_D:(g + 1) * NSA_D, :].astype(BF16)
        vw = win[NSA_KVW + g * NSA_D:NSA_KVW + (g + 1) * NSA_D, :].astype(BF16)
        kn = new[:, g * NSA_D:(g + 1) * NSA_D].astype(BF16)
        vn = new[:, NSA_KVW + g * NSA_D:NSA_KVW + (g + 1) * NSA_D].astype(BF16)
        sw = _dot(qg, kw) + slope * (iw - NSA_WIN).astype(F32)
        sw = jnp.where(iw >= tq, sw, NEG)
        sn = _dot_nt(qg, kn) + slope * jn.astype(F32)
        sn = jnp.where((jn <= tq) & (jn < ts), sn, NEG)
        o = _softmax_rows([(sw, vw, True), (sn, vn, False)])
        for i in range(NSA_R):
            out_ref[:, g * NSA_R + i, :] = o[i * SUB:i * SUB + ts, :]


def _nsa_win_decode(q4, win_t, kw_new, ts):
    db = q4.shape[0]
    return pl.pallas_call(
        functools.partial(_nsa_win_decode_kernel, ts),
        grid=(db,),
        in_specs=[pl.BlockSpec((None, SUB, NSA_H, NSA_D), lambda b: (b, 0, 0, 0)),
                  pl.BlockSpec((None, 2 * NSA_KVW, NSA_WIN), lambda b: (b, 0, 0)),
                  pl.BlockSpec((None, SUB, 2 * NSA_KVW), lambda b: (b, 0, 0))],
        out_specs=[pl.BlockSpec((None, ts, NSA_H, NSA_D), lambda b: (b, 0, 0, 0)),
                   pl.BlockSpec((None, 2 * NSA_KVW, NSA_WIN), lambda b: (b, 0, 0))],
        out_shape=[jax.ShapeDtypeStruct((db, ts, NSA_H, NSA_D), F32),
                   jax.ShapeDtypeStruct((db, 2 * NSA_KVW, NSA_WIN), F32)],
        compiler_params=_params(("parallel",)),
        name="nsa_win_decode",
    )(q4, win_t, kw_new)


def _odd_tail(o, z_ref, x_ref, wout, gfin):
    y = x_ref[...] + _dot((o * _silu(z_ref[...])).astype(BF16), wout[...])
    return _rms(y, gfin[...])


def _odd_out_prompt_kernel(ot_ref, z_ref, x_ref, wout, gfin, out_ref):
    out_ref[...] = _odd_tail(ot_ref[...].T, z_ref, x_ref, wout, gfin)


def _odd_out_prompt(ot, z, x, wout, gfin, tm):
    b, _, t = ot.shape
    tok = pl.BlockSpec((None, tm, D_MODEL), lambda bi, ti: (bi, ti, 0))
    return pl.pallas_call(
        _odd_out_prompt_kernel,
        grid=(b, t // tm),
        in_specs=[pl.BlockSpec((None, NSA_W, tm), lambda bi, ti: (bi, 0, ti)), tok, tok,
                  _const_spec(wout.shape), _const_spec(gfin.shape)],
        out_specs=tok,
        out_shape=jax.ShapeDtypeStruct((b, t, D_MODEL), F32),
        compiler_params=_params(("parallel", "parallel")),
        name="odd_out_prompt",
    )(ot, z, x, wout, gfin)


def _odd_out_decode_kernel(oc_ref, os_ref, ow_ref, gate_ref, z_ref, x_ref, wout, gfin, out_ref):
    gate = gate_ref[...]
    oc, osel, ow = oc_ref[...], os_ref[...], ow_ref[...]
    heads = []
    for h in range(NSA_H):
        c = slice(h * NSA_D, (h + 1) * NSA_D)
        heads.append(gate[:, h:h + 1] * oc[:, c] + gate[:, NSA_H + h:NSA_H + h + 1] * osel[:, c]
                     + gate[:, 2 * NSA_H + h:2 * NSA_H + h + 1] * ow[:, c])
    out_ref[...] = _odd_tail(jnp.concatenate(heads, axis=1), z_ref, x_ref, wout, gfin)


def _odd_out_decode(oc, osel, ow, gate, z, x, wout, gfin):
    m = x.shape[0]
    args = (oc, osel, ow, gate, z, x, wout, gfin)
    return pl.pallas_call(
        _odd_out_decode_kernel,
        grid=(1,),
        in_specs=[_const_spec(a.shape) for a in args],
        out_specs=_const_spec((m, D_MODEL)),
        out_shape=jax.ShapeDtypeStruct((m, D_MODEL), F32),
        compiler_params=_params(("arbitrary",)),
        name="odd_out_decode",
    )(*args)


def _rope_tables(pos):
    half = MLA_ROPE // 2
    inv = ROPE_THETA ** (-jnp.arange(half, dtype=F32) / half)
    ang = pos.astype(F32)[:, None] * inv[None, :]
    cos, sin = jnp.cos(ang), jnp.sin(ang)
    return jnp.concatenate([cos, cos], axis=1), jnp.concatenate([-sin, sin], axis=1)


def _block_diag(x):
    t, g, r, c = x.shape
    eye = jnp.eye(g, dtype=x.dtype)
    return jnp.einsum("tgrc,gh->tgrhc", x, eye).reshape(t, g * r, g * c)


def _even_weights(norm_g, w_in, g_q, g_kv, w_uq, w_uk, w_uv):
    edges = [0, MLA_QL, MLA_QL + MLA_KVL, MLA_QL + MLA_ROW]
    edges += [edges[-1] + MLA_W, edges[-1] + MLA_W + S5_W, edges[-1] + MLA_W + 2 * S5_W]
    wb = w_in.astype(BF16)
    pieces = [wb[:, edges[i]:edges[i + 1]] for i in range(6)]
    in_w = (norm_g[None, :], *pieces, g_q[None, :], g_kv[None, :])
    uq = jnp.transpose(w_uq, (1, 0, 2)).astype(BF16)
    mla_w = (uq[:, :, :MLA_NOPE], uq[:, :, MLA_NOPE:],
             jnp.transpose(w_uk, (1, 2, 0)).astype(BF16),
             jnp.transpose(w_uv, (1, 0, 2)).astype(BF16))
    uq_t = jnp.transpose(w_uq, (1, 2, 0)).astype(BF16)
    mla_wt = (uq_t[:, :MLA_NOPE].reshape(MLA_H * MLA_NOPE, MLA_QL),
              uq_t[:, MLA_NOPE:].reshape(MLA_H * MLA_ROPE, MLA_QL),
              jnp.transpose(w_uk, (1, 0, 2)).astype(BF16),
              jnp.transpose(w_uv, (1, 2, 0)).astype(BF16))
    return in_w, mla_w, mla_wt


def _s5_weights(lam_re, lam_im, log_dt, b_re, b_im, c_re, c_im, d_skip, w_glu, b_glu):
    def bmat(b):
        return _block_diag(jnp.transpose(b.reshape(S5_NT, S5_GT, S5_N, S5_P), (0, 1, 3, 2)))

    def cmat(c):
        return _block_diag(jnp.transpose(c.reshape(S5_NT, S5_GT, S5_P, S5_N), (0, 1, 3, 2)))

    return (lam_re.reshape(1, S5_S), lam_im.reshape(1, S5_S),
            jnp.repeat(log_dt, S5_N).reshape(1, S5_S),
            bmat(b_re), bmat(b_im), cmat(c_re), cmat(c_im),
            d_skip.reshape(1, S5_W), w_glu.astype(BF16), b_glu[None, :])


def _odd_weights(norm_g, w_in, pe_k, pe_v, phi1_k, phi2_k, phi1_v, phi2_v):
    wb = w_in.astype(BF16)
    e0 = NSA_W
    e1 = e0 + 4 * NSA_KVW
    e2 = e1 + 2 * NSA_KVW
    e3 = e2 + 3 * NSA_H
    in_w = (norm_g[None, :], wb[:, :e0], wb[:, e0:e1], wb[:, e1:e2], wb[:, e2:e3], wb[:, e3:])
    pe = jnp.concatenate([pe_k, pe_k, pe_v, pe_v], axis=1)
    p1k = phi1_k.reshape(NSA_BLK, NSA_D, NSA_D)
    p1v = phi1_v.reshape(NSA_BLK, NSA_D, NSA_D)
    phi2 = _block_diag(jnp.stack([phi2_k, phi2_k, phi2_v, phi2_v], axis=0)[None])[0].astype(BF16)
    return in_w, (pe, p1k, p1v, phi2)


def _pad_tokens(x, n):
    return jnp.pad(x, ((0, 0), (0, n - x.shape[1])) + ((0, 0),) * (x.ndim - 2))


def _rows_last(x):
    nd = x.ndim
    xt = jnp.transpose(x, (0,) + tuple(range(2, nd)) + (1,))
    return xt.reshape(x.shape[0], -1, x.shape[1])


def _rows_second(x_t, feature_shape):
    b, _, rows = x_t.shape
    nf = len(feature_shape)
    xt = x_t.reshape((b,) + tuple(feature_shape) + (rows,))
    return jnp.transpose(xt, (0, nf + 1) + tuple(range(1, nf + 1)))


def kernel(x_prompt, x_sample, cache_mla, state_s5, cache_nsa_kv, state_nsa_win, page_table, norm_even, w_in_even, mla_g_q, mla_g_kv, mla_w_uq, mla_w_uk, mla_w_uv, s5_lambda_re, s5_lambda_im, s5_log_dt, s5_b_re, s5_b_im, s5_c_re, s5_c_im, s5_d, s5_w_glu, s5_b_glu, w_out_even, norm_odd, w_in_odd, nsa_pe_k, nsa_pe_v, nsa_phi1_k, nsa_phi2_k, nsa_phi1_v, nsa_phi2_v, w_out_odd, norm_final):
    b, t, _ = x_prompt.shape
    db, ts, _ = x_sample.shape
    n_pages = page_table.shape[1]
    past = n_pages * PAGE
    mp, ms = b * t, db * ts
    tm = min(1024, t)
    assert t % (PACK * NSA_BLK) == 0 and t % TK == 0 and ts <= SUB
    assert t <= NSA_BLK * 256

    pos_p = jnp.arange(t, dtype=I32)
    pos_s = past + jnp.arange(SUB, dtype=I32)
    cos_p, sin_p = _rope_tables(pos_p)
    cos_s, sin_s = _rope_tables(pos_s)
    cos_st, sin_st = jnp.tile(cos_s[:ts], (db, 1)), jnp.tile(sin_s[:ts], (db, 1))

    even_in_w, mla_w, mla_wt = _even_weights(norm_even[0], w_in_even[0], mla_g_q[0], mla_g_kv[0],
                                             mla_w_uq[0], mla_w_uk[0], mla_w_uv[0])
    s5_w = _s5_weights(s5_lambda_re[0], s5_lambda_im[0], s5_log_dt[0], s5_b_re[0], s5_b_im[0],
                       s5_c_re[0], s5_c_im[0], s5_d[0], s5_w_glu[0], s5_b_glu[0])
    wo_e = w_out_even[0].astype(BF16)
    wo_a, wo_b = wo_e[:MLA_W], wo_e[MLA_W:]

    xp2 = x_prompt.reshape(mp, D_MODEL)
    cq, rows_pt, za, u, zb = _even_in(xp2, even_in_w, cos_p, sin_p, tm, seq=t)
    mix_a = _mla_prompt(cq.reshape(b, t, MLA_QL), rows_pt, za.reshape(b, t, MLA_W), mla_wt, cos_p.T, sin_p.T)
    zeros_p = jnp.zeros((b, S5_S), F32)
    mix_b, sre_p, sim_p = _s5(u.reshape(b, t, S5_W), zb.reshape(b, t, S5_W), s5_w, zeros_p, zeros_p,
                              min(128, t))
    xp1 = _even_out(xp2, mix_a.reshape(mp, MLA_W), mix_b.reshape(mp, S5_W), wo_a, wo_b, tm)

    xs2 = x_sample.reshape(ms, D_MODEL)
    cq_s, rows_s, za_s, u_s, zb_s = _even_in(xs2, even_in_w, cos_st, sin_st, ms)
    pad3 = lambda a, n: _pad_tokens(a.reshape(db, ts, n), SUB)
    mix_a_s = _mla_decode(page_table, _rows_last(cache_mla[0]), pad3(cq_s, MLA_QL), pad3(rows_s, MLA_ROW),
                          pad3(za_s, MLA_W), mla_w, cos_s, sin_s, ts)
    st = state_s5[0]
    mix_b_s, sre_s, sim_s = _s5(u_s.reshape(db, ts, S5_W), zb_s.reshape(db, ts, S5_W), s5_w,
                                st[..., 0].reshape(db, S5_S), st[..., 1].reshape(db, S5_S), ts)
    xs1 = _even_out(xs2, mix_a_s.reshape(ms, MLA_W), mix_b_s.reshape(ms, S5_W), wo_a, wo_b, ms)

    odd_in_w, cmp_w = _odd_weights(norm_odd[0], w_in_odd[0], nsa_pe_k[0], nsa_pe_v[0], nsa_phi1_k[0],
                                   nsa_phi2_k[0], nsa_phi1_v[0], nsa_phi2_v[0])
    wo_o = w_out_odd[0].astype(BF16)
    gfin = norm_final[None, :]

    q, kv4_t, kw_t, gate, z = _odd_in(xp1, odd_in_w, tm, seq=t)
    nblk_p = t // NSA_BLK
    cb = _compress_prompt(kv4_t, *cmp_w)
    q3 = q.reshape(b, t, NSA_W)
    tt = min(512, t)
    oc_t, imp = _cmp(q3, cb.reshape(b, nblk_p, 2 * NSA_KVW), pos_p[None], tt)
    sel, _ = _topk(imp, pos_p[None], nblk_p, tt)
    o_t = _nsa_prompt(q3, kv4_t, kw_t, oc_t, sel, gate.reshape(b, t, 3 * NSA_H))
    y_prompt = _odd_out_prompt(o_t, z.reshape(b, t, NSA_W), xp1.reshape(b, t, D_MODEL), wo_o, gfin, tt)

    q_s, kv4_s, kw_s, gate_s, z_s = _odd_in(xs1, odd_in_w, ms)
    cache_nsa_t = _rows_last(cache_nsa_kv[0])
    cb_s = _compress_decode(page_table, cache_nsa_t, *cmp_w)
    n_sel_s = -(-(past + ts) // NSA_BLK)
    q_s3 = q_s.reshape(db, ts, NSA_W)
    oc_ts, imp_s = _cmp(_pad_tokens(q_s3, SUB), cb_s, pos_s[None], SUB)
    imp_l = jnp.transpose(imp_s[..., :ts], (1, 2, 0, 3)).reshape(1, NSA_G, cb_s.shape[1], ms)
    pos_l = jnp.tile(pos_s[:ts], db)[None]
    _, idx_l = _topk(imp_l, pos_l, n_sel_s, ms)
    idx_s = jnp.transpose(idx_l.reshape(NSA_G, NSA_TOPK, db, ts), (2, 0, 3, 1)).reshape(db, -1)
    q_s4 = q_s.reshape(db, ts, NSA_H, NSA_D)
    o_sel_s = _nsa_sel_decode(idx_s, page_table, cache_nsa_t, q_s4, pad3(kv4_s, 4 * NSA_KVW), ts, past)
    win_t = _rows_last(state_nsa_win[0])
    kw_s3 = kw_s.reshape(db, ts, 2 * NSA_KVW)
    o_win_s, win_st = _nsa_win_decode(_pad_tokens(q_s4, SUB), win_t, _pad_tokens(kw_s3, SUB), ts)
    oc_s = jnp.transpose(oc_ts, (0, 2, 1))[:, :ts].reshape(ms, NSA_W)
    y_sample = _odd_out_decode(oc_s, o_sel_s.reshape(ms, NSA_W), o_win_s.reshape(ms, NSA_W),
                               gate_s, z_s, xs1, wo_o, gfin)

    state = lambda re, im, n: jnp.stack([re, im], axis=-1).reshape(1, n, S5_G, S5_N, 2)
    win_shape = (2, NSA_G, NSA_D)
    if t >= NSA_WIN:
        win_pt = kw_t[:, :, t - NSA_WIN:]
    else:
        win_pt = jnp.pad(kw_t, ((0, 0), (0, 0), (NSA_WIN - t, 0)))
    return (y_prompt, y_sample.reshape(db, ts, D_MODEL),
            _rows_second(rows_pt, (MLA_ROW,))[None], rows_s.reshape(1, db, ts, MLA_ROW),
            state(sre_p, sim_p, b), state(sre_s, sim_s, db),
            _rows_second(kv4_t, (4, NSA_G, NSA_D))[None], kv4_s.reshape(1, db, ts, 4, NSA_G, NSA_D),
            _rows_second(win_pt, win_shape)[None], _rows_second(win_st, win_shape)[None])
```

```python
import functools

import jax
import jax.numpy as jnp
import numpy as np
from jax import lax
from jax.experimental import pallas as pl
from jax.experimental.pallas import tpu as pltpu

F32, BF16, I32 = jnp.float32, jnp.bfloat16, jnp.int32

D_MODEL = 1024
PAGE = 128
EPS = 1e-6
ROPE_THETA = 10000.0
MLA_H, MLA_NOPE, MLA_ROPE, MLA_V = 8, 64, 32, 64
MLA_QL, MLA_KVL = 384, 256
MLA_ROW = MLA_KVL + MLA_ROPE
MLA_W = MLA_H * MLA_V
MLA_SCALE = (MLA_NOPE + MLA_ROPE) ** -0.5
S5_G, S5_P, S5_N = 32, 16, 64
S5_W = S5_G * S5_P
S5_S = S5_G * S5_N
S5_GT = 4
S5_NT = S5_G // S5_GT
NSA_H, NSA_G, NSA_D = 16, 2, 64
NSA_R = NSA_H // NSA_G
NSA_W = NSA_H * NSA_D
NSA_KVW = NSA_G * NSA_D
NSA_BLK, NSA_TOPK, NSA_WIN = 64, 16, 512
NSA_SCALE = NSA_D ** -0.5
FORCED_BONUS = float(NSA_R + 1)
BPP = PAGE // NSA_BLK

LOG2E = 1.4426950408889634
NEG = -1e30
GONE = -3e38
TQ = 128
TK = 256
SUB = 8
PACK = 16
VMEM_LIMIT = 56 * 1024 * 1024


def _dot(a, b):
    return jnp.dot(a, b, preferred_element_type=F32)


def _dot_nt(a, b):
    return lax.dot_general(a, b, (((1,), (1,)), ((), ())), preferred_element_type=F32)


def _dot_tn(a, b):
    return lax.dot_general(a, b, (((0,), (0,)), ((), ())), preferred_element_type=F32)


def _rms(x, g):
    return x * lax.rsqrt(jnp.mean(x * x, axis=-1, keepdims=True) + EPS) * g


def _silu(x):
    return x * jax.nn.sigmoid(x)


def _rope_nat(x, cosf, sinf):
    half = x.shape[1] // 2
    xs = jnp.concatenate([x[:, half:], x[:, :half]], axis=1)
    return x * cosf + xs * sinf


def _params(sem):
    return pltpu.CompilerParams(dimension_semantics=sem, vmem_limit_bytes=VMEM_LIMIT)


def _const_spec(shape):
    n = len(shape)
    return pl.BlockSpec(shape, lambda *a, _n=n: (0,) * _n)


def _seq_major_spec(width, tm, seq):
    per = seq // tm
    return pl.BlockSpec((None, width, tm), lambda i: (i // per, 0, i % per))


def _even_in_kernel(feature_major, x_ref, g_ref, wcq, wckv, wkr, wza, wu, wzb, gq, gkv, cos_ref, sin_ref,
                    cq_o, rows_o, za_o, u_o, zb_o):
    h = _rms(x_ref[...], g_ref[...]).astype(BF16)
    cq_o[...] = _rms(_dot(h, wcq[...]), gq[...])
    ckv = _rms(_dot(h, wckv[...]), gkv[...])
    krope = _rope_nat(_dot(h, wkr[...]), cos_ref[...], sin_ref[...])
    if feature_major:
        rows_o[:MLA_KVL, :] = ckv.T
        rows_o[MLA_KVL:, :] = krope.T
    else:
        rows_o[:, :MLA_KVL] = ckv
        rows_o[:, MLA_KVL:] = krope
    za_o[...] = _dot(h, wza[...])
    u_o[...] = _dot(h, wu[...])
    zb_o[...] = _dot(h, wzb[...])


def _even_in(x2, wts, cosf, sinf, tm, seq=None):
    m = x2.shape[0]
    tab_blocks = cosf.shape[0] // tm
    row = lambda n: pl.BlockSpec((tm, n), lambda i: (i, 0))
    tab = pl.BlockSpec((tm, MLA_ROPE), lambda i: (i % tab_blocks, 0))
    widths = (MLA_QL, MLA_ROW, MLA_W, S5_W, S5_W)
    out_specs = [row(n) for n in widths]
    out_shape = [jax.ShapeDtypeStruct((m, n), F32) for n in widths]
    if seq is not None:
        out_specs[1] = _seq_major_spec(MLA_ROW, tm, seq)
        out_shape[1] = jax.ShapeDtypeStruct((m // seq, MLA_ROW, seq), F32)
    return pl.pallas_call(
        functools.partial(_even_in_kernel, seq is not None),
        grid=(m // tm,),
        in_specs=[row(D_MODEL)] + [_const_spec(c.shape) for c in wts] + [tab, tab],
        out_specs=out_specs,
        out_shape=out_shape,
        compiler_params=_params(("parallel",)),
        name="even_in",
    )(x2, *wts, cosf, sinf)


def _mla_queries(cq, wuqn, wuqr, wuk, cosf, sinf, h):
    cqb = cq.astype(BF16)
    qn = _dot(cqb, wuqn[h])
    ql = _dot(qn.astype(BF16), wuk[h])
    qr = _rope_nat(_dot(cqb, wuqr[h]), cosf, sinf)
    return ql, qr


def _flash_tiles(lo, hi, chains, c_exp, late=(), late_lo=None):
    def start(chain, first):
        qk, _, first_mask, _, s_buf, m_s, acc_s, l_s = chain
        m_s[...] = jnp.full(m_s.shape, NEG, F32)
        acc_s[...] = jnp.zeros(acc_s.shape, F32)
        if l_s is not None:
            l_s[...] = jnp.zeros(l_s.shape, F32)
        s_buf[0] = first_mask(qk(first), first)

    for chain in chains:
        start(chain, lo)

    def consume(chain, j, s):
        _, pv, _, _, _, m_s, acc_s, l_s = chain
        m_old = m_s[...]
        m_new = jnp.maximum(m_old, jnp.max(s, axis=0, keepdims=True))
        alpha = jnp.exp2((m_old - m_new) * c_exp)
        x = (s - m_new) * c_exp
        if l_s is None:
            p = jnp.exp2(x.astype(BF16))
        else:
            pf = jnp.exp2(x)
            l_s[...] = alpha * l_s[...] + jnp.sum(pf, axis=0, keepdims=True)
            p = pf.astype(BF16)
        acc_s[...] = alpha * acc_s[...] + pv(j, p)
        m_s[...] = m_new

    def body_of(active):
        def body(j, carry):
            for chain in active:
                chain[4][1] = chain[0](j + 1)
            for chain in active:
                consume(chain, j, chain[4][0])
            for chain in active:
                chain[4][0] = chain[4][1]
            return carry
        return body

    chains = list(chains)
    if late:
        lax.fori_loop(lo, late_lo, body_of(chains), 0)
        for chain in late:
            start(chain, late_lo)
        chains = chains + list(late)
        lo = late_lo
    lax.fori_loop(lo, hi - 1, body_of(chains), 0)
    last = hi - 1
    for chain in chains:
        consume(chain, last, chain[3](chain[4][0], last))
    outs = []
    for chain in chains:
        acc, l_s = chain[6][...], chain[7]
        if l_s is None:
            dv = acc.shape[0] - PACK
            outs.append(acc[:dv, :] / acc[dv:dv + 1, :])
        else:
            outs.append(acc / l_s[...])
    return outs


def _with_ones(v):
    return jnp.concatenate([v, jnp.ones((PACK, v.shape[1]), BF16)], axis=0)


def _mla_prompt_kernel(cq_ref, rows_ref, za_ref, wuqn_t, wuqr_t, wuk_t, wuv_t, cos_ref, sin_ref,
                       out_ref, qtl, qtr, s_buf, m_s, l_s, acc_s):
    qi = pl.program_id(1)
    r = MLA_H * TQ
    half = MLA_ROPE // 2
    cq_t = cq_ref[...].T.astype(BF16)
    qn_t = _dot(wuqn_t[...], cq_t).astype(BF16)
    qr_t = _dot(wuqr_t[...], cq_t)
    cos_t, sin_t = cos_ref[...], sin_ref[...]
    for h in range(MLA_H):
        qtl[:, h * TQ:(h + 1) * TQ] = _dot(wuk_t[h], qn_t[h * MLA_NOPE:(h + 1) * MLA_NOPE, :]).astype(BF16)
        x = qr_t[h * MLA_ROPE:(h + 1) * MLA_ROPE, :]
        xs = jnp.concatenate([x[half:, :], x[:half, :]], axis=0)
        qtr[:, h * TQ:(h + 1) * TQ] = (x * cos_t + xs * sin_t).astype(BF16)
    qpos = qi * TQ + (lax.broadcasted_iota(I32, (1, r), 1) & (TQ - 1))
    krow = lax.broadcasted_iota(I32, (TK, 1), 0)

    def keys(j):
        return rows_ref[:, pl.ds(pl.multiple_of(j * TK, TK), TK)]

    def qk(j):
        kt = keys(j)
        return (_dot_tn(kt[:MLA_KVL, :].astype(BF16), qtl[...])
                + _dot_tn(kt[MLA_KVL:, :].astype(BF16), qtr[...]))

    def pv(j, p):
        return _dot(keys(j)[:MLA_KVL, :].astype(BF16), p)

    def causal(s, j):
        return jnp.where(j * TK + krow <= qpos, s, NEG)

    n_tiles = (qi * TQ + TQ + TK - 1) // TK
    chain = (qk, pv, lambda s, j: s, causal, s_buf, m_s, acc_s, l_s)
    o = _flash_tiles(0, n_tiles, [chain], MLA_SCALE * LOG2E)[0].astype(BF16)
    heads = [_dot(wuv_t[h], o[:, h * TQ:(h + 1) * TQ]) for h in range(MLA_H)]
    out_ref[...] = jnp.concatenate(heads, axis=0).T * _silu(za_ref[...])


def _mla_prompt(cq, rows_t, za, wts, cos_t, sin_t):
    b, t, _ = cq.shape
    r = MLA_H * TQ
    tok = lambda n: pl.BlockSpec((None, TQ, n), lambda bi, qi: (bi, qi, 0))
    tab = pl.BlockSpec((MLA_ROPE, TQ), lambda bi, qi: (0, qi))
    return pl.pallas_call(
        _mla_prompt_kernel,
        grid=(b, t // TQ),
        in_specs=[tok(MLA_QL), pl.BlockSpec((None, MLA_ROW, t), lambda bi, qi: (bi, 0, 0)), tok(MLA_W)]
        + [_const_spec(w.shape) for w in wts] + [tab, tab],
        out_specs=tok(MLA_W),
        out_shape=jax.ShapeDtypeStruct((b, t, MLA_W), F32),
        scratch_shapes=[pltpu.VMEM((MLA_KVL, r), BF16), pltpu.VMEM((MLA_ROPE, r), BF16),
                        pltpu.VMEM((2, TK, r), F32),
                        pltpu.VMEM((1, r), F32), pltpu.VMEM((1, r), F32), pltpu.VMEM((MLA_KVL, r), F32)],
        compiler_params=_params(("parallel", "arbitrary")),
        name="mla_prompt",
    )(cq, rows_t, za, *wts, cos_t, sin_t)


def _mla_decode_kernel(n_pages_step, ts, pt_ref, cq_ref, rows_ref, za_ref, wuqn, wuqr, wuk, wuv,
                       cos_ref, sin_ref, *rest, phases=(0, 1, 2)):
    pages = rest[:n_pages_step]
    out_ref, ql_s, qr_s, m_s, l_s, acc_s = rest[n_pages_step:]
    gi = pl.program_id(1)
    rows = MLA_H * ts

    def only(phase, cond):
        return pl.when(cond) if phase in phases else (lambda f: None)

    @only(0, gi == 0)
    def _():
        cq = cq_ref[...]
        cosf, sinf = cos_ref[...], sin_ref[...]
        for h in range(MLA_H):
            ql, qr = _mla_queries(cq, wuqn, wuqr, wuk, cosf, sinf, h)
            ql_s[h * ts:(h + 1) * ts, :] = ql[:ts]
            qr_s[h * ts:(h + 1) * ts, :] = qr[:ts]
        m_s[...] = jnp.full((rows, 1), NEG, F32)
        l_s[...] = jnp.zeros((rows, 1), F32)
        acc_s[...] = jnp.zeros((rows, MLA_KVL), F32)

    def update(s, vals, feature_major):
        m_old = m_s[...]
        m_new = jnp.maximum(m_old, jnp.max(s, axis=1, keepdims=True))
        alpha = jnp.exp(m_old - m_new)
        p = jnp.exp(s - m_new)
        l_s[...] = alpha * l_s[...] + jnp.sum(p, axis=1, keepdims=True)
        pv = None
        for (lo, hi), v in vals:
            pj = p[:, lo:hi].astype(BF16)
            term = _dot_nt(pj, v) if feature_major else _dot(pj, v)
            pv = term if pv is None else pv + term
        acc_s[...] = alpha * acc_s[...] + pv
        m_s[...] = m_new

    ql, qr = ql_s[...].astype(BF16), qr_s[...].astype(BF16)
    if 1 in phases:
        ckv_all = jnp.concatenate([pg[:MLA_KVL, :].astype(BF16) for pg in pages], axis=1)
        kr_all = jnp.concatenate([pg[MLA_KVL:, :].astype(BF16) for pg in pages], axis=1)
        s_all = (_dot(ql, ckv_all) + _dot(qr, kr_all)) * MLA_SCALE
        update(s_all, [((0, n_pages_step * PAGE), ckv_all)], True)

    @only(2, gi == pl.num_programs(1) - 1)
    def _():
        kn = rows_ref[...]
        ckv = kn[:, :MLA_KVL].astype(BF16)
        s = (_dot_nt(ql, ckv) + _dot_nt(qr, kn[:, MLA_KVL:].astype(BF16))) * MLA_SCALE
        tq = lax.rem(lax.broadcasted_iota(I32, (rows, SUB), 0), ts)
        jk = lax.broadcasted_iota(I32, (rows, SUB), 1)
        s = jnp.where((jk <= tq) & (jk < ts), s, NEG)
        update(s, [((0, SUB), ckv)], False)
        o = (acc_s[...] / l_s[...]).astype(BF16)
        heads = [_dot(o[h * ts:(h + 1) * ts, :], wuv[h]) for h in range(MLA_H)]
        out_ref[...] = jnp.concatenate(heads, axis=1) * _silu(za_ref[...][:ts])


def _mla_decode(page_table, cache_t, cq, rows, za, wts, cosf, sinf, ts, n_pages_step=64):
    db, n_pages = page_table.shape
    rows_n = MLA_H * ts
    tok = lambda n: pl.BlockSpec((None, SUB, n), lambda b, g, pt: (b, 0, 0))
    cst = lambda shape: pl.BlockSpec(shape, lambda b, g, pt, _n=len(shape): (0,) * _n)
    page_specs = [
        pl.BlockSpec((None, MLA_ROW, PAGE), lambda b, g, pt, j=j: (pt[b, g * n_pages_step + j], 0, 0))
        for j in range(n_pages_step)]
    grid_spec = pltpu.PrefetchScalarGridSpec(
        num_scalar_prefetch=1,
        grid=(db, n_pages // n_pages_step),
        in_specs=[tok(MLA_QL), tok(MLA_ROW), tok(MLA_W)] + [cst(w.shape) for w in wts]
        + [cst(cosf.shape), cst(sinf.shape)] + page_specs,
        out_specs=pl.BlockSpec((None, ts, MLA_W), lambda b, g, pt: (b, 0, 0)),
        scratch_shapes=[pltpu.VMEM((rows_n, MLA_KVL), F32), pltpu.VMEM((rows_n, MLA_ROPE), F32),
                        pltpu.VMEM((rows_n, 1), F32), pltpu.VMEM((rows_n, 1), F32),
                        pltpu.VMEM((rows_n, MLA_KVL), F32)])
    return pl.pallas_call(
        functools.partial(_mla_decode_kernel, n_pages_step, ts),
        grid_spec=grid_spec,
        out_shape=jax.ShapeDtypeStruct((db, ts, MLA_W), F32),
        compiler_params=_params(("parallel", "arbitrary")),
        name="mla_decode",
    )(page_table, cq, rows, za, *wts, cosf, sinf, *([cache_t] * n_pages_step))


def _s5_kernel(u_ref, zb_ref, lre_ref, lim_ref, ldt_ref, wbre, wbim, wcre, wcim, d_ref, wglu, bglu,
               h0re_ref, h0im_ref, mix_o, sre_o, sim_o, bure, buim, hre, him):
    step = pl.program_id(0)
    nb, chunk, _ = u_ref.shape
    rows = nb * chunk

    @pl.when(step == 0)
    def _():
        hre[...] = h0re_ref[...]
        him[...] = h0im_ref[...]

    lre, lim = lre_ref[...], lim_ref[...]
    dt = jnp.exp(ldt_ref[...])
    mag = jnp.exp(lre * dt)
    are, aim = mag * jnp.cos(lim * dt), mag * jnp.sin(lim * dt)
    den = lre * lre + lim * lim
    cre = ((are - 1.0) * lre + aim * lim) / den
    cim = (aim * lre - (are - 1.0) * lim) / den

    ut = jnp.swapaxes(u_ref[...], 0, 1).reshape(rows, S5_W)
    kw, nw = S5_GT * S5_P, S5_GT * S5_N
    for jt in range(S5_NT):
        cr, ci = cre[:, jt * nw:(jt + 1) * nw], cim[:, jt * nw:(jt + 1) * nw]
        bre = (cr * wbre[jt] - ci * wbim[jt]).astype(BF16)
        bim = (cr * wbim[jt] + ci * wbre[jt]).astype(BF16)
        uj = ut[:, jt * kw:(jt + 1) * kw].astype(BF16)
        bure[:, jt * nw:(jt + 1) * nw] = _dot(uj, bre)
        buim[:, jt * nw:(jt + 1) * nw] = _dot(uj, bim)

    def scan(t, carry):
        hr, hi = carry
        sl = pl.ds(pl.multiple_of(t * nb, SUB), nb)
        nr = are * hr - aim * hi + bure[sl, :]
        ni = are * hi + aim * hr + buim[sl, :]
        bure[sl, :] = nr
        buim[sl, :] = ni
        return nr, ni

    hr, hi = lax.fori_loop(0, chunk, scan, (hre[...], him[...]))
    hre[...] = hr
    him[...] = hi
    sre_o[...] = hr
    sim_o[...] = hi

    ys = []
    for jt in range(S5_NT):
        sr = bure[:, jt * nw:(jt + 1) * nw].astype(BF16)
        si = buim[:, jt * nw:(jt + 1) * nw].astype(BF16)
        ys.append(_dot(sr, wcre[jt].astype(BF16)) - _dot(si, wcim[jt].astype(BF16)))
    y = jnp.concatenate(ys, axis=1) + d_ref[...] * ut
    g5 = jax.nn.gelu(y)
    ob = g5 * jax.nn.sigmoid(_dot(g5.astype(BF16), wglu[...]) + bglu[...])
    mix_o[...] = jnp.swapaxes(ob.reshape(chunk, nb, S5_W), 0, 1) * _silu(zb_ref[...])


def _s5(u, zb, wts, h0re, h0im, chunk):
    nb, t, _ = u.shape
    tok = pl.BlockSpec((nb, chunk, S5_W), lambda i: (0, i, 0))
    st = _const_spec((nb, S5_S))
    return pl.pallas_call(
        _s5_kernel,
        grid=(t // chunk,),
        in_specs=[tok, tok] + [_const_spec(w.shape) for w in wts] + [st, st],
        out_specs=[tok, st, st],
        out_shape=[jax.ShapeDtypeStruct((nb, t, S5_W), F32), jax.ShapeDtypeStruct((nb, S5_S), F32),
                   jax.ShapeDtypeStruct((nb, S5_S), F32)],
        scratch_shapes=[pltpu.VMEM((nb * chunk, S5_S), F32), pltpu.VMEM((nb * chunk, S5_S), F32),
                        pltpu.VMEM((nb, S5_S), F32), pltpu.VMEM((nb, S5_S), F32)],
        compiler_params=_params(("arbitrary",)),
        name="s5",
    )(u, zb, *wts, h0re, h0im)


def _even_out_kernel(x_ref, a_ref, b_ref, wa, wb, out_ref):
    out_ref[...] = (x_ref[...] + _dot(a_ref[...].astype(BF16), wa[...])
                    + _dot(b_ref[...].astype(BF16), wb[...]))


def _even_out(x2, mixa, mixb, wa, wb, tm):
    m = x2.shape[0]
    row = lambda n: pl.BlockSpec((tm, n), lambda i: (i, 0))
    return pl.pallas_call(
        _even_out_kernel,
        grid=(m // tm,),
        in_specs=[row(D_MODEL), row(MLA_W), row(S5_W), _const_spec(wa.shape), _const_spec(wb.shape)],
        out_specs=row(D_MODEL),
        out_shape=jax.ShapeDtypeStruct((m, D_MODEL), F32),
        compiler_params=_params(("parallel",)),
        name="even_out",
    )(x2, mixa, mixb, wa, wb)


def _odd_in_kernel(feature_major, x_ref, g_ref, wq, wkv4, wkw, wg, wz, q_o, kv4_o, kw_o, gate_o, z_o):
    h = _rms(x_ref[...], g_ref[...]).astype(BF16)
    q_o[...] = _dot(h, wq[...])
    kv4, kw = _dot(h, wkv4[...]), _dot(h, wkw[...])
    kv4_o[...] = kv4.T if feature_major else kv4
    kw_o[...] = kw.T if feature_major else kw
    gate_o[...] = jax.nn.sigmoid(_dot(h, wg[...]))
    z_o[...] = _dot(h, wz[...])


def _odd_in(x2, wts, tm, seq=None):
    m = x2.shape[0]
    row = lambda n: pl.BlockSpec((tm, n), lambda i: (i, 0))
    widths = (NSA_W, 4 * NSA_KVW, 2 * NSA_KVW, 3 * NSA_H, NSA_W)
    out_specs = [row(n) for n in widths]
    out_shape = [jax.ShapeDtypeStruct((m, n), F32) for n in widths]
    if seq is not None:
        for i in (1, 2):
            out_specs[i] = _seq_major_spec(widths[i], tm, seq)
            out_shape[i] = jax.ShapeDtypeStruct((m // seq, widths[i], seq), F32)
    return pl.pallas_call(
        functools.partial(_odd_in_kernel, seq is not None),
        grid=(m // tm,),
        in_specs=[row(D_MODEL)] + [_const_spec(w.shape) for w in wts],
        out_specs=out_specs,
        out_shape=out_shape,
        compiler_params=_params(("parallel",)),
        name="odd_in",
    )(x2, *wts)


def _compress_stage(xs, x_t, pe, base):
    n = x_t.shape[1] // NSA_BLK
    x = x_t.astype(BF16).T.reshape(n, NSA_BLK, 2 * NSA_KVW) + pe.astype(BF16)[None]
    xs[:, pl.ds(pl.multiple_of(base, PACK), n), :] = jnp.swapaxes(x, 0, 1)


def _compress_weights(w1, p1k, p1v):
    w1[...] = jnp.zeros(w1.shape, BF16)
    for s, ref in enumerate((p1k, p1k, p1v, p1v)):
        w1[:, s * NSA_D:(s + 1) * NSA_D, s * NSA_D:(s + 1) * NSA_D] = ref[...].astype(BF16)


def _compress_finish(xs, w1, phi2, out_ref):
    acc = jnp.zeros((xs.shape[1], 2 * NSA_KVW), F32)
    for r in range(NSA_BLK):
        acc = acc + _dot(xs[r], w1[r])
    out_ref[...] = _dot(_silu(acc).astype(BF16), phi2[...])


def _compress_prompt_kernel(x_ref, pe_ref, p1k, p1v, phi2, out_ref, xs, w1):
    bi = pl.program_id(0)
    nblk = x_ref.shape[1] // NSA_BLK
    _compress_stage(xs, x_ref[...], pe_ref[...], bi * nblk)

    @pl.when(bi == pl.num_programs(0) - 1)
    def _():
        _compress_weights(w1, p1k, p1v)
        _compress_finish(xs, w1, phi2, out_ref)


def _compress_prompt(kv4_t, pe, p1k, p1v, phi2):
    b, _, t = kv4_t.shape
    cw = 2 * NSA_KVW
    nblk = b * (t // NSA_BLK)
    consts = (pe, p1k, p1v, phi2)
    return pl.pallas_call(
        _compress_prompt_kernel,
        grid=(b,),
        in_specs=[pl.BlockSpec((None, cw, t), lambda i: (i, 0, 0))] + [_const_spec(c.shape) for c in consts],
        out_specs=pl.BlockSpec((nblk, cw), lambda i: (0, 0)),
        out_shape=jax.ShapeDtypeStruct((nblk, cw), F32),
        scratch_shapes=[pltpu.VMEM((NSA_BLK, nblk, cw), BF16), pltpu.VMEM((NSA_BLK, cw, cw), BF16)],
        compiler_params=_params(("arbitrary",)),
        name="compress_prompt",
    )(kv4_t, *consts)


def _compress_decode_kernel(n_pages_step, pt_ref, pe_ref, p1k, p1v, phi2, *rest, phases=(0, 1, 2)):
    pages = rest[:n_pages_step]
    out_ref, xs, w1 = rest[n_pages_step:]
    bi, gi = pl.program_id(0), pl.program_id(1)
    grp = PACK // BPP
    pe = pe_ref[...]

    if 0 in phases:
        @pl.when((bi == 0) & (gi == 0))
        def _():
            _compress_weights(w1, p1k, p1v)

    if 1 in phases:
        for k in range(n_pages_step // grp):
            x_t = jnp.concatenate([pages[k * grp + j][...] for j in range(grp)], axis=1)
            _compress_stage(xs, x_t, pe, (gi * (n_pages_step // grp) + k) * PACK)

    if 2 in phases:
        @pl.when(gi == pl.num_programs(1) - 1)
        def _():
            _compress_finish(xs, w1, phi2, out_ref)


def _compress_decode(page_table, cache_t, pe, p1k, p1v, phi2, n_pages_step=64):
    db, n_pages = page_table.shape
    cw = 2 * NSA_KVW
    nblk = n_pages * BPP
    cst = lambda shape: pl.BlockSpec(shape, lambda b, g, pt, _n=len(shape): (0,) * _n)
    consts = (pe, p1k, p1v, phi2)
    page_specs = [
        pl.BlockSpec((None, cw, PAGE), lambda b, g, pt, j=j: (pt[b, g * n_pages_step + j], 0, 0))
        for j in range(n_pages_step)]
    grid_spec = pltpu.PrefetchScalarGridSpec(
        num_scalar_prefetch=1,
        grid=(db, n_pages // n_pages_step),
        in_specs=[cst(c.shape) for c in consts] + page_specs,
        out_specs=pl.BlockSpec((None, nblk, cw), lambda b, g, pt: (b, 0, 0)),
        scratch_shapes=[pltpu.VMEM((NSA_BLK, nblk, cw), BF16), pltpu.VMEM((NSA_BLK, cw, cw), BF16)])
    return pl.pallas_call(
        functools.partial(_compress_decode_kernel, n_pages_step),
        grid_spec=grid_spec,
        out_shape=jax.ShapeDtypeStruct((db, nblk, cw), F32),
        compiler_params=_params(("arbitrary", "arbitrary")),
        name="compress_decode",
    )(page_table, *consts, *([cache_t] * n_pages_step))


def _decode_pages_kernel(n_pages_step, ts, n_mla_in, n_cmp_in, pt_ref, *refs):
    n = n_pages_step
    mla_in, cmp_in = refs[:n_mla_in], refs[n_mla_in:n_mla_in + n_cmp_in]
    first = n_mla_in + n_cmp_in
    mla_pages, nsa_pages = refs[first:first + n], refs[first + n:first + 2 * n]
    mla_out, cb_out = refs[first + 2 * n], refs[first + 2 * n + 1]
    scratch = refs[first + 2 * n + 2:]
    for phase in range(3):
        _mla_decode_kernel(n, ts, pt_ref, *mla_in, *mla_pages, mla_out, *scratch[:5], phases=(phase,))
        _compress_decode_kernel(n, pt_ref, *cmp_in, *nsa_pages, cb_out, *scratch[5:], phases=(phase,))


def _decode_pages(page_table, cache_mla_t, cq, rows, za, wts, cosf, sinf, ts,
                  cache_nsa_t, pe, p1k, p1v, phi2, n_pages_step=32):
    db, n_pages = page_table.shape
    rows_n = MLA_H * ts
    cw = 2 * NSA_KVW
    nblk = n_pages * BPP
    tok = lambda n: pl.BlockSpec((None, SUB, n), lambda b, g, pt: (b, 0, 0))
    cst = lambda shape: pl.BlockSpec(shape, lambda b, g, pt, _n=len(shape): (0,) * _n)
    page = lambda rows_: [
        pl.BlockSpec((None, rows_, PAGE), lambda b, g, pt, j=j: (pt[b, g * n_pages_step + j], 0, 0))
        for j in range(n_pages_step)]
    mla_in = (cq, rows, za, *wts, cosf, sinf)
    cmp_in = (pe, p1k, p1v, phi2)
    grid_spec = pltpu.PrefetchScalarGridSpec(
        num_scalar_prefetch=1,
        grid=(db, n_pages // n_pages_step),
        in_specs=[tok(MLA_QL), tok(MLA_ROW), tok(MLA_W)] + [cst(a.shape) for a in mla_in[3:]]
        + [cst(a.shape) for a in cmp_in] + page(MLA_ROW) + page(cw),
        out_specs=[pl.BlockSpec((None, ts, MLA_W), lambda b, g, pt: (b, 0, 0)),
                   pl.BlockSpec((None, nblk, cw), lambda b, g, pt: (b, 0, 0))],
        scratch_shapes=[pltpu.VMEM((rows_n, MLA_KVL), F32), pltpu.VMEM((rows_n, MLA_ROPE), F32),
                        pltpu.VMEM((rows_n, 1), F32), pltpu.VMEM((rows_n, 1), F32),
                        pltpu.VMEM((rows_n, MLA_KVL), F32),
                        pltpu.VMEM((NSA_BLK, nblk, cw), BF16), pltpu.VMEM((NSA_BLK, cw, cw), BF16)])
    return pl.pallas_call(
        functools.partial(_decode_pages_kernel, n_pages_step, ts, len(mla_in), len(cmp_in)),
        grid_spec=grid_spec,
        out_shape=[jax.ShapeDtypeStruct((db, ts, MLA_W), F32), jax.ShapeDtypeStruct((db, nblk, cw), F32)],
        compiler_params=_params(("arbitrary", "arbitrary")),
        name="decode_pages",
    )(page_table, *mla_in, *cmp_in, *([cache_mla_t] * n_pages_step), *([cache_nsa_t] * n_pages_step))


def _alibi_slope(h):
    return 2.0 ** (-8.0 * (h + 1) / NSA_H)


def _split_bf16(x):
    hi = x.astype(BF16)
    return hi, (x - hi.astype(F32)).astype(BF16)


def _cmp_kernel(q_ref, cb_ref, pos_ref, oc_o, imp_o, s_scr):
    tt = q_ref.shape[0]
    nc = cb_ref.shape[0]
    qpos = pos_ref[...]
    cpos = lax.broadcasted_iota(I32, (nc, 1), 0) * NSA_BLK + (NSA_BLK - 1)
    visible = cpos <= qpos
    dist = (qpos - cpos).astype(F32)
    q = q_ref[...] * NSA_SCALE
    cb = cb_ref[...]
    for g in range(NSA_G):
        k_hi, k_lo = _split_bf16(cb[:, g * NSA_D:(g + 1) * NSA_D])
        vc = cb[:, NSA_KVW + g * NSA_D:NSA_KVW + (g + 1) * NSA_D].astype(BF16)
        for i in range(NSA_R):
            h = g * NSA_R + i
            q_hi, q_lo = _split_bf16(q[:, h * NSA_D:(h + 1) * NSA_D])
            s = _dot_nt(k_hi, q_hi) + _dot_nt(k_hi, q_lo) + _dot_nt(k_lo, q_hi)
            s_scr[i * nc:(i + 1) * nc, :] = s - _alibi_slope(h) * dist
        s3 = jnp.where(visible[None], s_scr[...].reshape(NSA_R, nc, tt), NEG)
        mx = jnp.max(s3, axis=1, keepdims=True)
        e = jnp.where(visible[None], jnp.exp(s3 - mx), 0.0)
        den = jnp.sum(e, axis=1, keepdims=True)
        p = e / jnp.where(den > 0, den, 1.0)
        imp_o[g] = jnp.sum(p, axis=0)
        for i in range(NSA_R):
            h = g * NSA_R + i
            oc_o[h * NSA_D:(h + 1) * NSA_D, :] = _dot_tn(vc, p[i].astype(BF16))


def _cmp(q, cb, pos, tt):
    b, t, _ = q.shape
    nc = cb.shape[1]
    return pl.pallas_call(
        _cmp_kernel,
        grid=(b, t // tt),
        in_specs=[pl.BlockSpec((None, tt, NSA_W), lambda bi, ti: (bi, ti, 0)),
                  pl.BlockSpec((None, nc, 2 * NSA_KVW), lambda bi, ti: (bi, 0, 0)),
                  pl.BlockSpec((1, tt), lambda bi, ti: (0, ti))],
        out_specs=[pl.BlockSpec((None, NSA_W, tt), lambda bi, ti: (bi, 0, ti)),
                   pl.BlockSpec((None, NSA_G, nc, tt), lambda bi, ti: (bi, 0, 0, ti))],
        out_shape=[jax.ShapeDtypeStruct((b, NSA_W, t), F32),
                   jax.ShapeDtypeStruct((b, NSA_G, nc, t), F32)],
        scratch_shapes=[pltpu.VMEM((NSA_R * nc, tt), F32)],
        compiler_params=_params(("parallel", "parallel")),
        name="nsa_cmp",
    )(q, cb, pos)


def _topk_kernel(n_sel, imp_ref, pos_ref, sel_o, idx_o):
    _, nc, tt = imp_ref.shape
    nsp = sel_o.shape[1]
    qpos = pos_ref[...]
    blk = lax.broadcasted_iota(I32, (nsp, 1), 0)
    cur = jnp.right_shift(qpos, NSA_BLK.bit_length() - 1)
    forced = (blk == 0) | (blk == cur) | (blk == cur - 1)
    allowed = (blk <= cur) & (blk < n_sel)
    for g in range(NSA_G):
        imp = imp_ref[g]
        if nsp > nc:
            imp = jnp.concatenate([imp, jnp.zeros((nsp - nc, tt), F32)], axis=0)
        score = jnp.where(allowed, imp + jnp.where(forced, FORCED_BONUS, 0.0), NEG)
        chosen = jnp.zeros((nsp, tt), F32)
        picks = []
        for _ in range(NSA_TOPK):
            mx = jnp.max(score, axis=0, keepdims=True)
            first = jnp.min(jnp.where(score == mx, blk, nsp), axis=0, keepdims=True)
            hit = blk == first
            chosen = jnp.where(hit, 1.0, chosen)
            score = jnp.where(hit, GONE, score)
            picks.append(first)
        sel_o[g] = chosen
        idx_o[g] = jnp.concatenate(picks, axis=0)


def _topk(imp, pos, n_sel, tt):
    b, _, nc, t = imp.shape
    nsp = -(-(n_sel + TK // NSA_BLK) // SUB) * SUB
    return pl.pallas_call(
        functools.partial(_topk_kernel, n_sel),
        grid=(b, t // tt),
        in_specs=[pl.BlockSpec((None, NSA_G, nc, tt), lambda bi, ti: (bi, 0, 0, ti)),
                  pl.BlockSpec((1, tt), lambda bi, ti: (0, ti))],
        out_specs=[pl.BlockSpec((None, NSA_G, nsp, tt), lambda bi, ti: (bi, 0, 0, ti)),
                   pl.BlockSpec((None, NSA_G, NSA_TOPK, tt), lambda bi, ti: (bi, 0, 0, ti))],
        out_shape=[jax.ShapeDtypeStruct((b, NSA_G, nsp, t), F32),
                   jax.ShapeDtypeStruct((b, NSA_G, NSA_TOPK, t), I32)],
        compiler_params=_params(("parallel", "parallel")),
        name="nsa_topk",
    )(imp, pos)


def _bf16_parts(x, n):
    parts = []
    for _ in range(n):
        bits = np.asarray(x, np.float32).view(np.uint32)
        top = ((bits + np.uint32(0x7FFF) + ((bits >> np.uint32(16)) & np.uint32(1)))
               & np.uint32(0xFFFF0000)).view(np.float32)
        parts.append(float(top))
        x = float(np.float32(x) - top)
    return parts


def _nsa_prompt_kernel(q_ref, kv4_ref, kw_ref, oc_ref, sel_ref, gate_ref, out_ref,
                       k_aug, q_aug, s_buf, m_s, acc_s):
    qi = pl.program_id(1)
    r = NSA_R * TQ
    bpt = TK // NSA_BLK
    n_parts = SUB // 2
    qt = (q_ref[...] * NSA_SCALE).T.astype(BF16)
    gt = gate_ref[...].T
    lane = lax.broadcasted_iota(I32, (1, r), 1)
    qpos = qi * TQ + (lane & (TQ - 1))
    head = lane // TQ
    krow = lax.broadcasted_iota(I32, (TK, 1), 0)
    klane = lax.broadcasted_iota(I32, (1, TK), 1)
    row8 = lax.broadcasted_iota(I32, (SUB, 1), 0)
    n_tiles = (qi * TQ + TQ + TK - 1) // TK
    onehot = jnp.where(row8 == jnp.right_shift(klane, NSA_BLK.bit_length() - 1), 1.0, 0.0)

    def key_rows(j):
        rel = j * TK - qi * TQ + klane
        coarse = (rel & -NSA_BLK).astype(F32)
        fine = (rel & (NSA_BLK - 1)).astype(F32)
        alibi = jnp.where((row8 & 1) == 0, coarse, fine)
        return jnp.concatenate([alibi, onehot], axis=0).astype(BF16)

    def causal(s, j):
        return jnp.where(j * TK + krow <= qpos, s, NEG)

    def in_window(s, j):
        return jnp.where(qpos - (j * TK + krow) <= NSA_WIN, s, NEG)

    def tile(ref, lo, j):
        return ref[lo:lo + NSA_D, pl.ds(pl.multiple_of(j * TK, TK), TK)].astype(BF16)

    sel_chains, win_chains = [], []
    for g in range(NSA_G):
        qg = jnp.concatenate(
            [qt[(g * NSA_R + i) * NSA_D:(g * NSA_R + i + 1) * NSA_D, :] for i in range(NSA_R)], axis=1)
        slope8 = jnp.zeros((SUB, r), F32)
        for i in range(NSA_R):
            parts = _bf16_parts(_alibi_slope(g * NSA_R + i), n_parts)
            col = jnp.zeros((SUB, 1), F32)
            for k, part in enumerate(parts):
                col = jnp.where(jnp.right_shift(row8, 1) == k, part, col)
            slope8 = jnp.where(head == i, col, slope8)
        ks_lo, vs_lo = 2 * NSA_KVW + g * NSA_D, 3 * NSA_KVW + g * NSA_D
        kw_lo, vw_lo = g * NSA_D, NSA_KVW + g * NSA_D
        cs, cw = g, NSA_G + g
        ks_aug, qs_aug, kw_aug, qw_aug = k_aug.at[cs], q_aug.at[cs], k_aug.at[cw], q_aug.at[cw]
        qs_aug[:NSA_D, :] = qg
        qw_aug[:NSA_D, :] = qg
        qw_aug[NSA_D:, :] = jnp.concatenate([slope8, jnp.zeros((SUB, r), F32)], axis=0).astype(BF16)

        def qk_sel(j, g=g, kg=ks_aug, qa=qs_aug, ks_lo=ks_lo, slope8=slope8):
            kg[:NSA_D, :] = tile(kv4_ref, ks_lo, j)
            kg[NSA_D:, :] = key_rows(j)
            flags = sel_ref[g, pl.ds(j * bpt, SUB), :]
            off = jnp.where(flags > 0.5, 0.0, NEG)
            qa[NSA_D:, :] = jnp.concatenate(
                [slope8, jnp.concatenate([off] * NSA_R, axis=1)], axis=0).astype(BF16)
            return _dot_tn(kg[...], qa[...])

        def qk_win(j, kg=kw_aug, qa=qw_aug, kw_lo=kw_lo):
            kg[:NSA_D, :] = tile(kw_ref, kw_lo, j)
            kg[NSA_D:, :] = key_rows(j)
            return _dot_tn(kg[...], qa[...])

        sel_chains.append((qk_sel, lambda j, p, lo=vs_lo: _dot(_with_ones(tile(kv4_ref, lo, j)), p),
                           lambda s, j: s, causal, s_buf.at[cs], m_s.at[cs], acc_s.at[cs], None))
        win_chains.append((qk_win, lambda j, p, lo=vw_lo: _dot(_with_ones(tile(kw_ref, lo, j)), p),
                           in_window, lambda s, j: causal(in_window(s, j), j),
                           s_buf.at[cw], m_s.at[cw], acc_s.at[cw], None))

    outs = _flash_tiles(0, n_tiles, sel_chains, LOG2E, late=win_chains,
                        late_lo=jnp.maximum(qi * TQ - NSA_WIN, 0) // TK)
    for g in range(NSA_G):
        o_sel, o_win = outs[g], outs[NSA_G + g]
        for i in range(NSA_R):
            h = g * NSA_R + i
            rows = slice(h * NSA_D, (h + 1) * NSA_D)
            cols = slice(i * TQ, (i + 1) * TQ)
            out_ref[rows, :] = (gt[h:h + 1, :] * oc_ref[rows, :]
                                + gt[NSA_H + h:NSA_H + h + 1, :] * o_sel[:, cols]
                                + gt[2 * NSA_H + h:2 * NSA_H + h + 1, :] * o_win[:, cols])


def _nsa_prompt(q, kv4_t, kw_t, oc_t, sel, gate):
    b, t, _ = q.shape
    nsp = sel.shape[2]
    r = NSA_R * TQ
    tok = lambda n: pl.BlockSpec((None, TQ, n), lambda bi, qi: (bi, qi, 0))
    full = lambda n: pl.BlockSpec((None, n, t), lambda bi, qi: (bi, 0, 0))
    return pl.pallas_call(
        _nsa_prompt_kernel,
        grid=(b, t // TQ),
        in_specs=[tok(NSA_W), full(4 * NSA_KVW), full(2 * NSA_KVW),
                  pl.BlockSpec((None, NSA_W, TQ), lambda bi, qi: (bi, 0, qi)),
                  pl.BlockSpec((None, NSA_G, nsp, TQ), lambda bi, qi: (bi, 0, 0, qi)),
                  tok(3 * NSA_H)],
        out_specs=pl.BlockSpec((None, NSA_W, TQ), lambda bi, qi: (bi, 0, qi)),
        out_shape=jax.ShapeDtypeStruct((b, NSA_W, t), F32),
        scratch_shapes=[pltpu.VMEM((2 * NSA_G, NSA_D + 2 * SUB, TK), BF16),
                        pltpu.VMEM((2 * NSA_G, NSA_D + 2 * SUB, r), BF16),
                        pltpu.VMEM((2 * NSA_G, 2, TK, r), F32),
                        pltpu.VMEM((2 * NSA_G, 1, r), F32),
                        pltpu.VMEM((2 * NSA_G, NSA_D + PACK, r), F32)],
        compiler_params=_params(("parallel", "arbitrary")),
        name="nsa_prompt",
    )(q, kv4_t, kw_t, oc_t, sel, gate)


def _softmax_rows(parts):
    mx = None
    for s, _, _ in parts:
        pm = jnp.max(s, axis=1, keepdims=True)
        mx = pm if mx is None else jnp.maximum(mx, pm)
    den, num = None, None
    for s, v, feature_major in parts:
        p = jnp.exp(s - mx)
        d = jnp.sum(p, axis=1, keepdims=True)
        n = _dot_nt(p.astype(BF16), v) if feature_major else _dot(p.astype(BF16), v)
        den = d if den is None else den + d
        num = n if num is None else num + n
    return num / den


def _nsa_sel_decode_kernel(ts, pos0, nbp, idx_ref, pid_ref, q_ref, new_ref, *rest):
    nblk = NSA_G * NSA_TOPK
    pages = rest[:nblk]
    out_ref = rest[nblk]
    b, t = pl.program_id(0), pl.program_id(1)
    q = q_ref[...] * NSA_SCALE
    new = new_ref[...]
    lane = lax.broadcasted_iota(I32, (1, PAGE), 1)
    jn = lax.broadcasted_iota(I32, (1, SUB), 1)
    for g in range(NSA_G):
        qg = q[g * NSA_R:(g + 1) * NSA_R, :].astype(BF16)
        slope = jnp.concatenate(
            [jnp.full((1, 1), _alibi_slope(g * NSA_R + i), F32) for i in range(NSA_R)], axis=0)
        parts = []
        has_new = jnp.zeros((), I32)
        for k in range(NSA_TOPK):
            bid = idx_ref[((b * NSA_G + g) * ts + t) * NSA_TOPK + k]
            pg = pages[g * NSA_TOPK + k][...]
            kk = pg[g * NSA_D:(g + 1) * NSA_D, :].astype(BF16)
            vv = pg[NSA_KVW + g * NSA_D:NSA_KVW + (g + 1) * NSA_D, :].astype(BF16)
            sub = bid % BPP
            rel = ((bid - sub) * NSA_BLK - pos0 + lane).astype(F32)
            s = _dot(qg, kk) + slope * rel
            mine = (lane // NSA_BLK == sub) & (bid < nbp)
            parts.append((jnp.where(mine, s, NEG), vv, True))
            has_new = has_new + (bid >= nbp).astype(I32)
        kn = new[:, 2 * NSA_KVW + g * NSA_D:2 * NSA_KVW + (g + 1) * NSA_D].astype(BF16)
        vn = new[:, 3 * NSA_KVW + g * NSA_D:3 * NSA_KVW + (g + 1) * NSA_D].astype(BF16)
        sn = _dot_nt(qg, kn) + slope * jn.astype(F32)
        ok = (jn <= t) & (jn < ts) & (has_new > 0)
        parts.append((jnp.where(ok, sn, NEG), vn, False))
        out_ref[g * NSA_R:(g + 1) * NSA_R, :] = _softmax_rows(parts)


def _nsa_sel_decode(idx, page_table, cache_t, q4, kv4_new, ts, pos0):
    db, n_pages = page_table.shape
    nbp = n_pages * BPP
    cw = 2 * NSA_KVW
    page_ids = jnp.take_along_axis(page_table, jnp.minimum(idx, nbp - 1) // BPP, axis=1).reshape(-1)
    idx_flat = idx.reshape(-1)

    def page_map(b, t, idx_r, pid, g, k):
        return (pid[((b * NSA_G + g) * ts + t) * NSA_TOPK + k], 1, 0)

    page_specs = [pl.BlockSpec((None, cw, PAGE), functools.partial(page_map, g=g, k=k))
                  for g in range(NSA_G) for k in range(NSA_TOPK)]
    grid_spec = pltpu.PrefetchScalarGridSpec(
        num_scalar_prefetch=2,
        grid=(db, ts),
        in_specs=[pl.BlockSpec((None, None, NSA_H, NSA_D), lambda b, t, idx, pt: (b, t, 0, 0)),
                  pl.BlockSpec((None, SUB, 4 * NSA_KVW), lambda b, t, idx, pt: (b, 0, 0))] + page_specs,
        out_specs=pl.BlockSpec((None, None, NSA_H, NSA_D), lambda b, t, idx, pt: (b, t, 0, 0)))
    return pl.pallas_call(
        functools.partial(_nsa_sel_decode_kernel, ts, pos0, nbp),
        grid_spec=grid_spec,
        out_shape=jax.ShapeDtypeStruct((db, ts, NSA_H, NSA_D), F32),
        compiler_params=_params(("parallel", "arbitrary")),
        name="nsa_sel_decode",
    )(idx_flat, page_ids, q4, kv4_new, *([cache_t] * (NSA_G * NSA_TOPK)))


def _nsa_win_decode_kernel(ts, q_ref, win_ref, new_ref, out_ref, win_o):
    rows = NSA_R * SUB
    win = win_ref[...]
    new = new_ref[...]
    win_o[...] = jnp.concatenate([win[:, ts:], new.T[:, :ts]], axis=1)
    tq = lax.broadcasted_iota(I32, (rows, 1), 0) & (SUB - 1)
    iw = lax.broadcasted_iota(I32, (1, NSA_WIN), 1)
    jn = lax.broadcasted_iota(I32, (1, SUB), 1)
    for g in range(NSA_G):
        qg = jnp.concatenate([q_ref[:, g * NSA_R + i, :] for i in range(NSA_R)], axis=0)
        qg = (qg * NSA_SCALE).astype(BF16)
        slope = jnp.concatenate(
            [jnp.full((SUB, 1), _alibi_slope(g * NSA_R + i), F32) for i in range(NSA_R)], axis=0)
        kw = win[g * NSA_D:(g + 1) * NSA_D, :].astype(BF16)
        vw = win[NSA_KVW + g * NSA_D:NSA_KVW + (g + 1) * NSA_D, :].astype(BF16)
        kn = new[:, g * NSA_D:(g + 1) * NSA_D].astype(BF16)
        vn = new[:, NSA_KVW + g * NSA_D:NSA_KVW + (g + 1) * NSA_D].astype(BF16)
        sw = _dot(qg, kw) + slope * (iw - NSA_WIN).astype(F32)
        sw = jnp.where(iw >= tq, sw, NEG)
        sn = _dot_nt(qg, kn) + slope * jn.astype(F32)
        sn = jnp.where((jn <= tq) & (jn < ts), sn, NEG)
        o = _softmax_rows([(sw, vw, True), (sn, vn, False)])
        for i in range(NSA_R):
            out_ref[:, g * NSA_R + i, :] = o[i * SUB:i * SUB + ts, :]


def _nsa_win_decode(q4, win_t, kw_new, ts):
    db = q4.shape[0]
    return pl.pallas_call(
        functools.partial(_nsa_win_decode_kernel, ts),
        grid=(db,),
        in_specs=[pl.BlockSpec((None, SUB, NSA_H, NSA_D), lambda b: (b, 0, 0, 0)),
                  pl.BlockSpec((None, 2 * NSA_KVW, NSA_WIN), lambda b: (b, 0, 0)),
                  pl.BlockSpec((None, SUB, 2 * NSA_KVW), lambda b: (b, 0, 0))],
        out_specs=[pl.BlockSpec((None, ts, NSA_H, NSA_D), lambda b: (b, 0, 0, 0)),
                   pl.BlockSpec((None, 2 * NSA_KVW, NSA_WIN), lambda b: (b, 0, 0))],
        out_shape=[jax.ShapeDtypeStruct((db, ts, NSA_H, NSA_D), F32),
                   jax.ShapeDtypeStruct((db, 2 * NSA_KVW, NSA_WIN), F32)],
        compiler_params=_params(("parallel",)),
        name="nsa_win_decode",
    )(q4, win_t, kw_new)


def _odd_tail(o, z_ref, x_ref, wout, gfin):
    y = x_ref[...] + _dot((o * _silu(z_ref[...])).astype(BF16), wout[...])
    return _rms(y, gfin[...])


def _odd_out_prompt_kernel(ot_ref, z_ref, x_ref, wout, gfin, out_ref):
    out_ref[...] = _odd_tail(ot_ref[...].T, z_ref, x_ref, wout, gfin)


def _odd_out_prompt(ot, z, x, wout, gfin, tm):
    b, _, t = ot.shape
    tok = pl.BlockSpec((None, tm, D_MODEL), lambda bi, ti: (bi, ti, 0))
    return pl.pallas_call(
        _odd_out_prompt_kernel,
        grid=(b, t // tm),
        in_specs=[pl.BlockSpec((None, NSA_W, tm), lambda bi, ti: (bi, 0, ti)), tok, tok,
                  _const_spec(wout.shape), _const_spec(gfin.shape)],
        out_specs=tok,
        out_shape=jax.ShapeDtypeStruct((b, t, D_MODEL), F32),
        compiler_params=_params(("parallel", "parallel")),
        name="odd_out_prompt",
    )(ot, z, x, wout, gfin)


def _odd_out_decode_kernel(oc_ref, os_ref, ow_ref, gate_ref, z_ref, x_ref, wout, gfin, out_ref):
    gate = gate_ref[...]
    oc, osel, ow = oc_ref[...], os_ref[...], ow_ref[...]
    heads = []
    for h in range(NSA_H):
        c = slice(h * NSA_D, (h + 1) * NSA_D)
        heads.append(gate[:, h:h + 1] * oc[:, c] + gate[:, NSA_H + h:NSA_H + h + 1] * osel[:, c]
                     + gate[:, 2 * NSA_H + h:2 * NSA_H + h + 1] * ow[:, c])
    out_ref[...] = _odd_tail(jnp.concatenate(heads, axis=1), z_ref, x_ref, wout, gfin)


def _odd_out_decode(oc, osel, ow, gate, z, x, wout, gfin):
    m = x.shape[0]
    args = (oc, osel, ow, gate, z, x, wout, gfin)
    return pl.pallas_call(
        _odd_out_decode_kernel,
        grid=(1,),
        in_specs=[_const_spec(a.shape) for a in args],
        out_specs=_const_spec((m, D_MODEL)),
        out_shape=jax.ShapeDtypeStruct((m, D_MODEL), F32),
        compiler_params=_params(("arbitrary",)),
        name="odd_out_decode",
    )(*args)


def _rope_tables(pos):
    half = MLA_ROPE // 2
    inv = ROPE_THETA ** (-jnp.arange(half, dtype=F32) / half)
    ang = pos.astype(F32)[:, None] * inv[None, :]
    cos, sin = jnp.cos(ang), jnp.sin(ang)
    return jnp.concatenate([cos, cos], axis=1), jnp.concatenate([-sin, sin], axis=1)


def _block_diag(x):
    t, g, r, c = x.shape
    eye = jnp.eye(g, dtype=x.dtype)
    return jnp.einsum("tgrc,gh->tgrhc", x, eye).reshape(t, g * r, g * c)


def _even_weights(norm_g, w_in, g_q, g_kv, w_uq, w_uk, w_uv):
    edges = [0, MLA_QL, MLA_QL + MLA_KVL, MLA_QL + MLA_ROW]
    edges += [edges[-1] + MLA_W, edges[-1] + MLA_W + S5_W, edges[-1] + MLA_W + 2 * S5_W]
    wb = w_in.astype(BF16)
    pieces = [wb[:, edges[i]:edges[i + 1]] for i in range(6)]
    in_w = (norm_g[None, :], *pieces, g_q[None, :], g_kv[None, :])
    uq = jnp.transpose(w_uq, (1, 0, 2)).astype(BF16)
    mla_w = (uq[:, :, :MLA_NOPE], uq[:, :, MLA_NOPE:],
             jnp.transpose(w_uk, (1, 2, 0)).astype(BF16),
             jnp.transpose(w_uv, (1, 0, 2)).astype(BF16))
    uq_t = jnp.transpose(w_uq, (1, 2, 0)).astype(BF16)
    mla_wt = (uq_t[:, :MLA_NOPE].reshape(MLA_H * MLA_NOPE, MLA_QL),
              uq_t[:, MLA_NOPE:].reshape(MLA_H * MLA_ROPE, MLA_QL),
              jnp.transpose(w_uk, (1, 0, 2)).astype(BF16),
              jnp.transpose(w_uv, (1, 2, 0)).astype(BF16))
    return in_w, mla_w, mla_wt


def _s5_weights(lam_re, lam_im, log_dt, b_re, b_im, c_re, c_im, d_skip, w_glu, b_glu):
    def bmat(b):
        return _block_diag(jnp.transpose(b.reshape(S5_NT, S5_GT, S5_N, S5_P), (0, 1, 3, 2)))

    def cmat(c):
        return _block_diag(jnp.transpose(c.reshape(S5_NT, S5_GT, S5_P, S5_N), (0, 1, 3, 2)))

    return (lam_re.reshape(1, S5_S), lam_im.reshape(1, S5_S),
            jnp.repeat(log_dt, S5_N).reshape(1, S5_S),
            bmat(b_re), bmat(b_im), cmat(c_re), cmat(c_im),
            d_skip.reshape(1, S5_W), w_glu.astype(BF16), b_glu[None, :])


def _odd_weights(norm_g, w_in, pe_k, pe_v, phi1_k, phi2_k, phi1_v, phi2_v):
    wb = w_in.astype(BF16)
    e0 = NSA_W
    e1 = e0 + 4 * NSA_KVW
    e2 = e1 + 2 * NSA_KVW
    e3 = e2 + 3 * NSA_H
    in_w = (norm_g[None, :], wb[:, :e0], wb[:, e0:e1], wb[:, e1:e2], wb[:, e2:e3], wb[:, e3:])
    pe = jnp.concatenate([pe_k, pe_k, pe_v, pe_v], axis=1)
    p1k = phi1_k.reshape(NSA_BLK, NSA_D, NSA_D)
    p1v = phi1_v.reshape(NSA_BLK, NSA_D, NSA_D)
    phi2 = _block_diag(jnp.stack([phi2_k, phi2_k, phi2_v, phi2_v], axis=0)[None])[0].astype(BF16)
    return in_w, (pe, p1k, p1v, phi2)


def _pad_tokens(x, n):
    return jnp.pad(x, ((0, 0), (0, n - x.shape[1])) + ((0, 0),) * (x.ndim - 2))


def _rows_last(x):
    nd = x.ndim
    xt = jnp.transpose(x, (0,) + tuple(range(2, nd)) + (1,))
    return xt.reshape(x.shape[0], -1, x.shape[1])


def _rows_second(x_t, feature_shape):
    b, _, rows = x_t.shape
    nf = len(feature_shape)
    xt = x_t.reshape((b,) + tuple(feature_shape) + (rows,))
    return jnp.transpose(xt, (0, nf + 1) + tuple(range(1, nf + 1)))


def kernel(x_prompt, x_sample, cache_mla, state_s5, cache_nsa_kv, state_nsa_win, page_table, norm_even, w_in_even, mla_g_q, mla_g_kv, mla_w_uq, mla_w_uk, mla_w_uv, s5_lambda_re, s5_lambda_im, s5_log_dt, s5_b_re, s5_b_im, s5_c_re, s5_c_im, s5_d, s5_w_glu, s5_b_glu, w_out_even, norm_odd, w_in_odd, nsa_pe_k, nsa_pe_v, nsa_phi1_k, nsa_phi2_k, nsa_phi1_v, nsa_phi2_v, w_out_odd, norm_final):
    b, t, _ = x_prompt.shape
    db, ts, _ = x_sample.shape
    n_pages = page_table.shape[1]
    past = n_pages * PAGE
    mp, ms = b * t, db * ts
    tm = min(1024, t)
    assert t % (PACK * NSA_BLK) == 0 and t % TK == 0 and ts <= SUB
    assert t <= NSA_BLK * 256

    pos_p = jnp.arange(t, dtype=I32)
    pos_s = past + jnp.arange(SUB, dtype=I32)
    cos_p, sin_p = _rope_tables(pos_p)
    cos_s, sin_s = _rope_tables(pos_s)
    cos_st, sin_st = jnp.tile(cos_s[:ts], (db, 1)), jnp.tile(sin_s[:ts], (db, 1))

    even_in_w, mla_w, mla_wt = _even_weights(norm_even[0], w_in_even[0], mla_g_q[0], mla_g_kv[0],
                                             mla_w_uq[0], mla_w_uk[0], mla_w_uv[0])
    s5_w = _s5_weights(s5_lambda_re[0], s5_lambda_im[0], s5_log_dt[0], s5_b_re[0], s5_b_im[0],
                       s5_c_re[0], s5_c_im[0], s5_d[0], s5_w_glu[0], s5_b_glu[0])
    wo_e = w_out_even[0].astype(BF16)
    wo_a, wo_b = wo_e[:MLA_W], wo_e[MLA_W:]

    xp2 = x_prompt.reshape(mp, D_MODEL)
    cq, rows_pt, za, u, zb = _even_in(xp2, even_in_w, cos_p, sin_p, tm, seq=t)
    mix_a = _mla_prompt(cq.reshape(b, t, MLA_QL), rows_pt, za.reshape(b, t, MLA_W), mla_wt, cos_p.T, sin_p.T)
    zeros_p = jnp.zeros((b, S5_S), F32)
    mix_b, sre_p, sim_p = _s5(u.reshape(b, t, S5_W), zb.reshape(b, t, S5_W), s5_w, zeros_p, zeros_p,
                              min(128, t))
    xp1 = _even_out(xp2, mix_a.reshape(mp, MLA_W), mix_b.reshape(mp, S5_W), wo_a, wo_b, tm)

    xs2 = x_sample.reshape(ms, D_MODEL)
    cq_s, rows_s, za_s, u_s, zb_s = _even_in(xs2, even_in_w, cos_st, sin_st, ms)
    pad3 = lambda a, n: _pad_tokens(a.reshape(db, ts, n), SUB)
    odd_in_w, cmp_w = _odd_weights(norm_odd[0], w_in_odd[0], nsa_pe_k[0], nsa_pe_v[0], nsa_phi1_k[0],
                                   nsa_phi2_k[0], nsa_phi1_v[0], nsa_phi2_v[0])
    cache_nsa_t = _rows_last(cache_nsa_kv[0])
    mix_a_s, cb_s = _decode_pages(page_table, _rows_last(cache_mla[0]), pad3(cq_s, MLA_QL),
                                  pad3(rows_s, MLA_ROW), pad3(za_s, MLA_W), mla_w, cos_s, sin_s, ts,
                                  cache_nsa_t, *cmp_w)
    st = state_s5[0]
    mix_b_s, sre_s, sim_s = _s5(u_s.reshape(db, ts, S5_W), zb_s.reshape(db, ts, S5_W), s5_w,
                                st[..., 0].reshape(db, S5_S), st[..., 1].reshape(db, S5_S), ts)
    xs1 = _even_out(xs2, mix_a_s.reshape(ms, MLA_W), mix_b_s.reshape(ms, S5_W), wo_a, wo_b, ms)

    wo_o = w_out_odd[0].astype(BF16)
    gfin = norm_final[None, :]

    q, kv4_t, kw_t, gate, z = _odd_in(xp1, odd_in_w, tm, seq=t)
    nblk_p = t // NSA_BLK
    cb = _compress_prompt(kv4_t, *cmp_w)
    q3 = q.reshape(b, t, NSA_W)
    tt = min(512, t)
    oc_t, imp = _cmp(q3, cb.reshape(b, nblk_p, 2 * NSA_KVW), pos_p[None], tt)
    sel, _ = _topk(imp, pos_p[None], nblk_p, tt)
    o_t = _nsa_prompt(q3, kv4_t, kw_t, oc_t, sel, gate.reshape(b, t, 3 * NSA_H))
    y_prompt = _odd_out_prompt(o_t, z.reshape(b, t, NSA_W), xp1.reshape(b, t, D_MODEL), wo_o, gfin, tt)

    q_s, kv4_s, kw_s, gate_s, z_s = _odd_in(xs1, odd_in_w, ms)
    n_sel_s = -(-(past + ts) // NSA_BLK)
    q_s3 = q_s.reshape(db, ts, NSA_W)
    oc_ts, imp_s = _cmp(_pad_tokens(q_s3, SUB), cb_s, pos_s[None], SUB)
    imp_l = jnp.transpose(imp_s[..., :ts], (1, 2, 0, 3)).reshape(1, NSA_G, cb_s.shape[1], ms)
    pos_l = jnp.tile(pos_s[:ts], db)[None]
    _, idx_l = _topk(imp_l, pos_l, n_sel_s, ms)
    idx_s = jnp.transpose(idx_l.reshape(NSA_G, NSA_TOPK, db, ts), (2, 0, 3, 1)).reshape(db, -1)
    q_s4 = q_s.reshape(db, ts, NSA_H, NSA_D)
    o_sel_s = _nsa_sel_decode(idx_s, page_table, cache_nsa_t, q_s4, pad3(kv4_s, 4 * NSA_KVW), ts, past)
    win_t = _rows_last(state_nsa_win[0])
    kw_s3 = kw_s.reshape(db, ts, 2 * NSA_KVW)
    o_win_s, win_st = _nsa_win_decode(_pad_tokens(q_s4, SUB), win_t, _pad_tokens(kw_s3, SUB), ts)
    oc_s = jnp.transpose(oc_ts, (0, 2, 1))[:, :ts].reshape(ms, NSA_W)
    y_sample = _odd_out_decode(oc_s, o_sel_s.reshape(ms, NSA_W), o_win_s.reshape(ms, NSA_W),
                               gate_s, z_s, xs1, wo_o, gfin)

    state = lambda re, im, n: jnp.stack([re, im], axis=-1).reshape(1, n, S5_G, S5_N, 2)
    win_shape = (2, NSA_G, NSA_D)
    if t >= NSA_WIN:
        win_pt = kw_t[:, :, t - NSA_WIN:]
    else:
        win_pt = jnp.pad(kw_t, ((0, 0), (0, 0), (NSA_WIN - t, 0)))
    return (y_prompt, y_sample.reshape(db, ts, D_MODEL),
            _rows_second(rows_pt, (MLA_ROW,))[None], rows_s.reshape(1, db, ts, MLA_ROW),
            state(sre_p, sim_p, b), state(sre_s, sim_s, db),
            _rows_second(kv4_t, (4, NSA_G, NSA_D))[None], kv4_s.reshape(1, db, ts, 4, NSA_G, NSA_D),
            _rows_second(win_pt, win_shape)[None], _rows_second(win_st, win_shape)[None])
```

```python
import functools

import jax
import jax.numpy as jnp
import numpy as np
from jax import lax
from jax.experimental import pallas as pl
from jax.experimental.pallas import tpu as pltpu

F32, BF16, I32 = jnp.float32, jnp.bfloat16, jnp.int32

D_MODEL = 1024
PAGE = 128
EPS = 1e-6
ROPE_THETA = 10000.0
MLA_H, MLA_NOPE, MLA_ROPE, MLA_V = 8, 64, 32, 64
MLA_QL, MLA_KVL = 384, 256
MLA_ROW = MLA_KVL + MLA_ROPE
MLA_W = MLA_H * MLA_V
MLA_SCALE = (MLA_NOPE + MLA_ROPE) ** -0.5
S5_G, S5_P, S5_N = 32, 16, 64
S5_W = S5_G * S5_P
S5_S = S5_G * S5_N
S5_GT = 4
S5_NT = S5_G // S5_GT
NSA_H, NSA_G, NSA_D = 16, 2, 64
NSA_R = NSA_H // NSA_G
NSA_W = NSA_H * NSA_D
NSA_KVW = NSA_G * NSA_D
NSA_BLK, NSA_TOPK, NSA_WIN = 64, 16, 512
NSA_SCALE = NSA_D ** -0.5
FORCED_BONUS = float(NSA_R + 1)
BPP = PAGE // NSA_BLK

LOG2E = 1.4426950408889634
NEG = -1e30
GONE = -3e38
TQ = 128
TK = 256
SUB = 8
PACK = 16
VMEM_LIMIT = 56 * 1024 * 1024


def _dot(a, b):
    return jnp.dot(a, b, preferred_element_type=F32)


def _dot_nt(a, b):
    return lax.dot_general(a, b, (((1,), (1,)), ((), ())), preferred_element_type=F32)


def _dot_tn(a, b):
    return lax.dot_general(a, b, (((0,), (0,)), ((), ())), preferred_element_type=F32)


def _rms(x, g):
    return x * lax.rsqrt(jnp.mean(x * x, axis=-1, keepdims=True) + EPS) * g


def _silu(x):
    return x * jax.nn.sigmoid(x)


def _rope_nat(x, cosf, sinf):
    half = x.shape[1] // 2
    xs = jnp.concatenate([x[:, half:], x[:, :half]], axis=1)
    return x * cosf + xs * sinf


def _params(sem):
    return pltpu.CompilerParams(dimension_semantics=sem, vmem_limit_bytes=VMEM_LIMIT)


def _const_spec(shape):
    n = len(shape)
    return pl.BlockSpec(shape, lambda *a, _n=n: (0,) * _n)


def _seq_major_spec(width, tm, seq):
    per = seq // tm
    return pl.BlockSpec((None, width, tm), lambda i: (i // per, 0, i % per))


def _even_in_kernel(feature_major, x_ref, g_ref, wcq, wckv, wkr, wza, wu, wzb, gq, gkv, cos_ref, sin_ref,
                    cq_o, rows_o, za_o, u_o, zb_o):
    h = _rms(x_ref[...], g_ref[...]).astype(BF16)
    cq_o[...] = _rms(_dot(h, wcq[...]), gq[...])
    ckv = _rms(_dot(h, wckv[...]), gkv[...])
    krope = _rope_nat(_dot(h, wkr[...]), cos_ref[...], sin_ref[...])
    if feature_major:
        rows_o[:MLA_KVL, :] = ckv.T
        rows_o[MLA_KVL:, :] = krope.T
    else:
        rows_o[:, :MLA_KVL] = ckv
        rows_o[:, MLA_KVL:] = krope
    za_o[...] = _dot(h, wza[...])
    u_o[...] = _dot(h, wu[...])
    zb_o[...] = _dot(h, wzb[...])


def _even_in(x2, wts, cosf, sinf, tm, seq=None):
    m = x2.shape[0]
    tab_blocks = cosf.shape[0] // tm
    row = lambda n: pl.BlockSpec((tm, n), lambda i: (i, 0))
    tab = pl.BlockSpec((tm, MLA_ROPE), lambda i: (i % tab_blocks, 0))
    widths = (MLA_QL, MLA_ROW, MLA_W, S5_W, S5_W)
    out_specs = [row(n) for n in widths]
    out_shape = [jax.ShapeDtypeStruct((m, n), F32) for n in widths]
    if seq is not None:
        out_specs[1] = _seq_major_spec(MLA_ROW, tm, seq)
        out_shape[1] = jax.ShapeDtypeStruct((m // seq, MLA_ROW, seq), F32)
    return pl.pallas_call(
        functools.partial(_even_in_kernel, seq is not None),
        grid=(m // tm,),
        in_specs=[row(D_MODEL)] + [_const_spec(c.shape) for c in wts] + [tab, tab],
        out_specs=out_specs,
        out_shape=out_shape,
        compiler_params=_params(("parallel",)),
        name="even_in",
    )(x2, *wts, cosf, sinf)


def _mla_queries(cq, wuqn, wuqr, wuk, cosf, sinf, h):
    cqb = cq.astype(BF16)
    qn = _dot(cqb, wuqn[h])
    ql = _dot(qn.astype(BF16), wuk[h])
    qr = _rope_nat(_dot(cqb, wuqr[h]), cosf, sinf)
    return ql, qr


def _flash_tiles(lo, hi, chains, c_exp, late=(), late_lo=None):
    def start(chain, first):
        qk, _, first_mask, _, s_buf, m_s, acc_s, l_s = chain
        m_s[...] = jnp.full(m_s.shape, NEG, F32)
        acc_s[...] = jnp.zeros(acc_s.shape, F32)
        if l_s is not None:
            l_s[...] = jnp.zeros(l_s.shape, F32)
        s_buf[0] = first_mask(qk(first), first)

    for chain in chains:
        start(chain, lo)

    def consume(chain, j, s):
        _, pv, _, _, _, m_s, acc_s, l_s = chain
        m_old = m_s[...]
        m_new = jnp.maximum(m_old, jnp.max(s, axis=0, keepdims=True))
        alpha = jnp.exp2((m_old - m_new) * c_exp)
        x = (s - m_new) * c_exp
        if l_s is None:
            p = jnp.exp2(x.astype(BF16))
        else:
            pf = jnp.exp2(x)
            l_s[...] = alpha * l_s[...] + jnp.sum(pf, axis=0, keepdims=True)
            p = pf.astype(BF16)
        acc_s[...] = alpha * acc_s[...] + pv(j, p)
        m_s[...] = m_new

    def body_of(active):
        def body(j, carry):
            for chain in active:
                chain[4][1] = chain[0](j + 1)
            for chain in active:
                consume(chain, j, chain[4][0])
            for chain in active:
                chain[4][0] = chain[4][1]
            return carry
        return body

    chains = list(chains)
    if late:
        lax.fori_loop(lo, late_lo, body_of(chains), 0)
        for chain in late:
            start(chain, late_lo)
        chains = chains + list(late)
        lo = late_lo
    lax.fori_loop(lo, hi - 1, body_of(chains), 0)
    last = hi - 1
    for chain in chains:
        consume(chain, last, chain[3](chain[4][0], last))
    outs = []
    for chain in chains:
        acc, l_s = chain[6][...], chain[7]
        if l_s is None:
            dv = acc.shape[0] - PACK
            outs.append(acc[:dv, :] / acc[dv:dv + 1, :])
        else:
            outs.append(acc / l_s[...])
    return outs


def _with_ones(v):
    return jnp.concatenate([v, jnp.ones((PACK, v.shape[1]), BF16)], axis=0)


def _mla_prompt_kernel(cq_ref, rows_ref, za_ref, wuqn_t, wuqr_t, wuk_t, wuv_t, cos_ref, sin_ref,
                       out_ref, qtl, qtr, s_buf, m_s, l_s, acc_s):
    qi = pl.program_id(1)
    r = MLA_H * TQ
    half = MLA_ROPE // 2
    cq_t = cq_ref[...].T.astype(BF16)
    qn_t = _dot(wuqn_t[...], cq_t).astype(BF16)
    qr_t = _dot(wuqr_t[...], cq_t)
    cos_t, sin_t = cos_ref[...], sin_ref[...]
    for h in range(MLA_H):
        qtl[:, h * TQ:(h + 1) * TQ] = _dot(wuk_t[h], qn_t[h * MLA_NOPE:(h + 1) * MLA_NOPE, :]).astype(BF16)
        x = qr_t[h * MLA_ROPE:(h + 1) * MLA_ROPE, :]
        xs = jnp.concatenate([x[half:, :], x[:half, :]], axis=0)
        qtr[:, h * TQ:(h + 1) * TQ] = (x * cos_t + xs * sin_t).astype(BF16)
    qpos = qi * TQ + (lax.broadcasted_iota(I32, (1, r), 1) & (TQ - 1))
    krow = lax.broadcasted_iota(I32, (TK, 1), 0)

    def keys(j):
        return rows_ref[:, pl.ds(pl.multiple_of(j * TK, TK), TK)]

    def qk(j):
        kt = keys(j)
        return (_dot_tn(kt[:MLA_KVL, :].astype(BF16), qtl[...])
                + _dot_tn(kt[MLA_KVL:, :].astype(BF16), qtr[...]))

    def pv(j, p):
        return _dot(keys(j)[:MLA_KVL, :].astype(BF16), p)

    def causal(s, j):
        return jnp.where(j * TK + krow <= qpos, s, NEG)

    n_tiles = (qi * TQ + TQ + TK - 1) // TK
    chain = (qk, pv, lambda s, j: s, causal, s_buf, m_s, acc_s, l_s)
    o = _flash_tiles(0, n_tiles, [chain], MLA_SCALE * LOG2E)[0].astype(BF16)
    heads = [_dot(wuv_t[h], o[:, h * TQ:(h + 1) * TQ]) for h in range(MLA_H)]
    out_ref[...] = jnp.concatenate(heads, axis=0).T * _silu(za_ref[...])


def _mla_prompt(cq, rows_t, za, wts, cos_t, sin_t):
    b, t, _ = cq.shape
    r = MLA_H * TQ
    tok = lambda n: pl.BlockSpec((None, TQ, n), lambda bi, qi: (bi, qi, 0))
    tab = pl.BlockSpec((MLA_ROPE, TQ), lambda bi, qi: (0, qi))
    return pl.pallas_call(
        _mla_prompt_kernel,
        grid=(b, t // TQ),
        in_specs=[tok(MLA_QL), pl.BlockSpec((None, MLA_ROW, t), lambda bi, qi: (bi, 0, 0)), tok(MLA_W)]
        + [_const_spec(w.shape) for w in wts] + [tab, tab],
        out_specs=tok(MLA_W),
        out_shape=jax.ShapeDtypeStruct((b, t, MLA_W), F32),
        scratch_shapes=[pltpu.VMEM((MLA_KVL, r), BF16), pltpu.VMEM((MLA_ROPE, r), BF16),
                        pltpu.VMEM((2, TK, r), F32),
                        pltpu.VMEM((1, r), F32), pltpu.VMEM((1, r), F32), pltpu.VMEM((MLA_KVL, r), F32)],
        compiler_params=_params(("parallel", "arbitrary")),
        name="mla_prompt",
    )(cq, rows_t, za, *wts, cos_t, sin_t)


def _mla_decode_kernel(n_pages_step, ts, pt_ref, cq_ref, rows_ref, za_ref, wuqn, wuqr, wuk, wuv,
                       cos_ref, sin_ref, *rest, phases=(0, 1, 2)):
    pages = rest[:n_pages_step]
    out_ref, ql_s, qr_s, m_s, l_s, acc_s = rest[n_pages_step:]
    gi = pl.program_id(1)
    rows = MLA_H * ts

    def only(phase, cond):
        return pl.when(cond) if phase in phases else (lambda f: None)

    @only(0, gi == 0)
    def _():
        cq = cq_ref[...]
        cosf, sinf = cos_ref[...], sin_ref[...]
        for h in range(MLA_H):
            ql, qr = _mla_queries(cq, wuqn, wuqr, wuk, cosf, sinf, h)
            ql_s[h * ts:(h + 1) * ts, :] = ql[:ts]
            qr_s[h * ts:(h + 1) * ts, :] = qr[:ts]
        m_s[...] = jnp.full((rows, 1), NEG, F32)
        l_s[...] = jnp.zeros((rows, 1), F32)
        acc_s[...] = jnp.zeros((rows, MLA_KVL), F32)

    def update(s, vals, feature_major):
        m_old = m_s[...]
        m_new = jnp.maximum(m_old, jnp.max(s, axis=1, keepdims=True))
        alpha = jnp.exp(m_old - m_new)
        p = jnp.exp(s - m_new)
        l_s[...] = alpha * l_s[...] + jnp.sum(p, axis=1, keepdims=True)
        pv = None
        for (lo, hi), v in vals:
            pj = p[:, lo:hi].astype(BF16)
            term = _dot_nt(pj, v) if feature_major else _dot(pj, v)
            pv = term if pv is None else pv + term
        acc_s[...] = alpha * acc_s[...] + pv
        m_s[...] = m_new

    ql, qr = ql_s[...].astype(BF16), qr_s[...].astype(BF16)
    if 1 in phases:
        ckv_all = jnp.concatenate([pg[:MLA_KVL, :].astype(BF16) for pg in pages], axis=1)
        kr_all = jnp.concatenate([pg[MLA_KVL:, :].astype(BF16) for pg in pages], axis=1)
        s_all = (_dot(ql, ckv_all) + _dot(qr, kr_all)) * MLA_SCALE
        update(s_all, [((0, n_pages_step * PAGE), ckv_all)], True)

    @only(2, gi == pl.num_programs(1) - 1)
    def _():
        kn = rows_ref[...]
        ckv = kn[:, :MLA_KVL].astype(BF16)
        s = (_dot_nt(ql, ckv) + _dot_nt(qr, kn[:, MLA_KVL:].astype(BF16))) * MLA_SCALE
        tq = lax.rem(lax.broadcasted_iota(I32, (rows, SUB), 0), ts)
        jk = lax.broadcasted_iota(I32, (rows, SUB), 1)
        s = jnp.where((jk <= tq) & (jk < ts), s, NEG)
        update(s, [((0, SUB), ckv)], False)
        o = (acc_s[...] / l_s[...]).astype(BF16)
        heads = [_dot(o[h * ts:(h + 1) * ts, :], wuv[h]) for h in range(MLA_H)]
        out_ref[...] = jnp.concatenate(heads, axis=1) * _silu(za_ref[...][:ts])


def _s5_kernel(u_ref, zb_ref, lre_ref, lim_ref, ldt_ref, wbre, wbim, wcre, wcim, d_ref, wglu, bglu,
               h0re_ref, h0im_ref, mix_o, sre_o, sim_o, bure, buim, hre, him):
    step = pl.program_id(0)
    nb, chunk, _ = u_ref.shape
    rows = nb * chunk

    @pl.when(step == 0)
    def _():
        hre[...] = h0re_ref[...]
        him[...] = h0im_ref[...]

    lre, lim = lre_ref[...], lim_ref[...]
    dt = jnp.exp(ldt_ref[...])
    mag = jnp.exp(lre * dt)
    are, aim = mag * jnp.cos(lim * dt), mag * jnp.sin(lim * dt)
    den = lre * lre + lim * lim
    cre = ((are - 1.0) * lre + aim * lim) / den
    cim = (aim * lre - (are - 1.0) * lim) / den

    ut = jnp.swapaxes(u_ref[...], 0, 1).reshape(rows, S5_W)
    kw, nw = S5_GT * S5_P, S5_GT * S5_N
    for jt in range(S5_NT):
        cr, ci = cre[:, jt * nw:(jt + 1) * nw], cim[:, jt * nw:(jt + 1) * nw]
        bre = (cr * wbre[jt] - ci * wbim[jt]).astype(BF16)
        bim = (cr * wbim[jt] + ci * wbre[jt]).astype(BF16)
        uj = ut[:, jt * kw:(jt + 1) * kw].astype(BF16)
        bure[:, jt * nw:(jt + 1) * nw] = _dot(uj, bre)
        buim[:, jt * nw:(jt + 1) * nw] = _dot(uj, bim)

    def scan(t, carry):
        hr, hi = carry
        sl = pl.ds(pl.multiple_of(t * nb, SUB), nb)
        nr = are * hr - aim * hi + bure[sl, :]
        ni = are * hi + aim * hr + buim[sl, :]
        bure[sl, :] = nr
        buim[sl, :] = ni
        return nr, ni

    hr, hi = lax.fori_loop(0, chunk, scan, (hre[...], him[...]))
    hre[...] = hr
    him[...] = hi
    sre_o[...] = hr
    sim_o[...] = hi

    ys = []
    for jt in range(S5_NT):
        sr = bure[:, jt * nw:(jt + 1) * nw].astype(BF16)
        si = buim[:, jt * nw:(jt + 1) * nw].astype(BF16)
        ys.append(_dot(sr, wcre[jt].astype(BF16)) - _dot(si, wcim[jt].astype(BF16)))
    y = jnp.concatenate(ys, axis=1) + d_ref[...] * ut
    g5 = jax.nn.gelu(y)
    ob = g5 * jax.nn.sigmoid(_dot(g5.astype(BF16), wglu[...]) + bglu[...])
    mix_o[...] = jnp.swapaxes(ob.reshape(chunk, nb, S5_W), 0, 1) * _silu(zb_ref[...])


def _s5(u, zb, wts, h0re, h0im, chunk):
    nb, t, _ = u.shape
    tok = pl.BlockSpec((nb, chunk, S5_W), lambda i: (0, i, 0))
    st = _const_spec((nb, S5_S))
    return pl.pallas_call(
        _s5_kernel,
        grid=(t // chunk,),
        in_specs=[tok, tok] + [_const_spec(w.shape) for w in wts] + [st, st],
        out_specs=[tok, st, st],
        out_shape=[jax.ShapeDtypeStruct((nb, t, S5_W), F32), jax.ShapeDtypeStruct((nb, S5_S), F32),
                   jax.ShapeDtypeStruct((nb, S5_S), F32)],
        scratch_shapes=[pltpu.VMEM((nb * chunk, S5_S), F32), pltpu.VMEM((nb * chunk, S5_S), F32),
                        pltpu.VMEM((nb, S5_S), F32), pltpu.VMEM((nb, S5_S), F32)],
        compiler_params=_params(("arbitrary",)),
        name="s5",
    )(u, zb, *wts, h0re, h0im)


def _even_out_kernel(x_ref, a_ref, b_ref, wa, wb, out_ref):
    out_ref[...] = (x_ref[...] + _dot(a_ref[...].astype(BF16), wa[...])
                    + _dot(b_ref[...].astype(BF16), wb[...]))


def _even_out(x2, mixa, mixb, wa, wb, tm):
    m = x2.shape[0]
    row = lambda n: pl.BlockSpec((tm, n), lambda i: (i, 0))
    return pl.pallas_call(
        _even_out_kernel,
        grid=(m // tm,),
        in_specs=[row(D_MODEL), row(MLA_W), row(S5_W), _const_spec(wa.shape), _const_spec(wb.shape)],
        out_specs=row(D_MODEL),
        out_shape=jax.ShapeDtypeStruct((m, D_MODEL), F32),
        compiler_params=_params(("parallel",)),
        name="even_out",
    )(x2, mixa, mixb, wa, wb)


def _odd_in_kernel(feature_major, x_ref, g_ref, wq, wkv4, wkw, wg, wz, q_o, kv4_o, kw_o, gate_o, z_o):
    h = _rms(x_ref[...], g_ref[...]).astype(BF16)
    q_o[...] = _dot(h, wq[...])
    kv4, kw = _dot(h, wkv4[...]), _dot(h, wkw[...])
    kv4_o[...] = kv4.T if feature_major else kv4
    kw_o[...] = kw.T if feature_major else kw
    gate_o[...] = jax.nn.sigmoid(_dot(h, wg[...]))
    z_o[...] = _dot(h, wz[...])


def _odd_in(x2, wts, tm, seq=None):
    m = x2.shape[0]
    row = lambda n: pl.BlockSpec((tm, n), lambda i: (i, 0))
    widths = (NSA_W, 4 * NSA_KVW, 2 * NSA_KVW, 3 * NSA_H, NSA_W)
    out_specs = [row(n) for n in widths]
    out_shape = [jax.ShapeDtypeStruct((m, n), F32) for n in widths]
    if seq is not None:
        for i in (1, 2):
            out_specs[i] = _seq_major_spec(widths[i], tm, seq)
            out_shape[i] = jax.ShapeDtypeStruct((m // seq, widths[i], seq), F32)
    return pl.pallas_call(
        functools.partial(_odd_in_kernel, seq is not None),
        grid=(m // tm,),
        in_specs=[row(D_MODEL)] + [_const_spec(w.shape) for w in wts],
        out_specs=out_specs,
        out_shape=out_shape,
        compiler_params=_params(("parallel",)),
        name="odd_in",
    )(x2, *wts)


def _compress_stage(xs, x_t, pe, base):
    n = x_t.shape[1] // NSA_BLK
    x = x_t.astype(BF16).T.reshape(n, NSA_BLK, 2 * NSA_KVW) + pe.astype(BF16)[None]
    xs[:, pl.ds(pl.multiple_of(base, PACK), n), :] = jnp.swapaxes(x, 0, 1)


def _compress_weights(w1, p1k, p1v):
    w1[...] = jnp.zeros(w1.shape, BF16)
    for s, ref in enumerate((p1k, p1k, p1v, p1v)):
        w1[:, s * NSA_D:(s + 1) * NSA_D, s * NSA_D:(s + 1) * NSA_D] = ref[...].astype(BF16)


def _compress_finish(xs, w1, phi2, out_ref):
    acc = jnp.zeros((xs.shape[1], 2 * NSA_KVW), F32)
    for r in range(NSA_BLK):
        acc = acc + _dot(xs[r], w1[r])
    out_ref[...] = _dot(_silu(acc).astype(BF16), phi2[...])


def _compress_prompt_kernel(x_ref, pe_ref, p1k, p1v, phi2, out_ref, xs, w1):
    bi = pl.program_id(0)
    nblk = x_ref.shape[1] // NSA_BLK
    _compress_stage(xs, x_ref[...], pe_ref[...], bi * nblk)

    @pl.when(bi == pl.num_programs(0) - 1)
    def _():
        _compress_weights(w1, p1k, p1v)
        _compress_finish(xs, w1, phi2, out_ref)


def _compress_prompt(kv4_t, pe, p1k, p1v, phi2):
    b, _, t = kv4_t.shape
    cw = 2 * NSA_KVW
    nblk = b * (t // NSA_BLK)
    consts = (pe, p1k, p1v, phi2)
    return pl.pallas_call(
        _compress_prompt_kernel,
        grid=(b,),
        in_specs=[pl.BlockSpec((None, cw, t), lambda i: (i, 0, 0))] + [_const_spec(c.shape) for c in consts],
        out_specs=pl.BlockSpec((nblk, cw), lambda i: (0, 0)),
        out_shape=jax.ShapeDtypeStruct((nblk, cw), F32),
        scratch_shapes=[pltpu.VMEM((NSA_BLK, nblk, cw), BF16), pltpu.VMEM((NSA_BLK, cw, cw), BF16)],
        compiler_params=_params(("arbitrary",)),
        name="compress_prompt",
    )(kv4_t, *consts)


def _compress_decode_kernel(n_pages_step, pt_ref, pe_ref, p1k, p1v, phi2, *rest, phases=(0, 1, 2)):
    pages = rest[:n_pages_step]
    out_ref, xs, w1 = rest[n_pages_step:]
    bi, gi = pl.program_id(0), pl.program_id(1)
    grp = PACK // BPP
    pe = pe_ref[...]

    if 0 in phases:
        @pl.when((bi == 0) & (gi == 0))
        def _():
            _compress_weights(w1, p1k, p1v)

    if 1 in phases:
        for k in range(n_pages_step // grp):
            x_t = jnp.concatenate([pages[k * grp + j][...] for j in range(grp)], axis=1)
            _compress_stage(xs, x_t, pe, (gi * (n_pages_step // grp) + k) * PACK)

    if 2 in phases:
        @pl.when(gi == pl.num_programs(1) - 1)
        def _():
            _compress_finish(xs, w1, phi2, out_ref)


def _decode_pages_kernel(n_pages_step, ts, n_mla_in, n_cmp_in, pt_ref, *refs):
    n = n_pages_step
    mla_in, cmp_in = refs[:n_mla_in], refs[n_mla_in:n_mla_in + n_cmp_in]
    first = n_mla_in + n_cmp_in
    mla_pages, nsa_pages = refs[first:first + n], refs[first + n:first + 2 * n]
    mla_out, cb_out = refs[first + 2 * n], refs[first + 2 * n + 1]
    scratch = refs[first + 2 * n + 2:]
    for phase in range(3):
        _mla_decode_kernel(n, ts, pt_ref, *mla_in, *mla_pages, mla_out, *scratch[:5], phases=(phase,))
        _compress_decode_kernel(n, pt_ref, *cmp_in, *nsa_pages, cb_out, *scratch[5:], phases=(phase,))


def _decode_pages(page_table, cache_mla_t, cq, rows, za, wts, cosf, sinf, ts,
                  cache_nsa_t, pe, p1k, p1v, phi2, n_pages_step=32):
    db, n_pages = page_table.shape
    rows_n = MLA_H * ts
    cw = 2 * NSA_KVW
    nblk = n_pages * BPP
    tok = lambda n: pl.BlockSpec((None, SUB, n), lambda b, g, pt: (b, 0, 0))
    cst = lambda shape: pl.BlockSpec(shape, lambda b, g, pt, _n=len(shape): (0,) * _n)
    page = lambda rows_: [
        pl.BlockSpec((None, rows_, PAGE), lambda b, g, pt, j=j: (pt[b, g * n_pages_step + j], 0, 0))
        for j in range(n_pages_step)]
    mla_in = (cq, rows, za, *wts, cosf, sinf)
    cmp_in = (pe, p1k, p1v, phi2)
    grid_spec = pltpu.PrefetchScalarGridSpec(
        num_scalar_prefetch=1,
        grid=(db, n_pages // n_pages_step),
        in_specs=[tok(MLA_QL), tok(MLA_ROW), tok(MLA_W)] + [cst(a.shape) for a in mla_in[3:]]
        + [cst(a.shape) for a in cmp_in] + page(MLA_ROW) + page(cw),
        out_specs=[pl.BlockSpec((None, ts, MLA_W), lambda b, g, pt: (b, 0, 0)),
                   pl.BlockSpec((None, nblk, cw), lambda b, g, pt: (b, 0, 0))],
        scratch_shapes=[pltpu.VMEM((rows_n, MLA_KVL), F32), pltpu.VMEM((rows_n, MLA_ROPE), F32),
                        pltpu.VMEM((rows_n, 1), F32), pltpu.VMEM((rows_n, 1), F32),
                        pltpu.VMEM((rows_n, MLA_KVL), F32),
                        pltpu.VMEM((NSA_BLK, nblk, cw), BF16), pltpu.VMEM((NSA_BLK, cw, cw), BF16)])
    return pl.pallas_call(
        functools.partial(_decode_pages_kernel, n_pages_step, ts, len(mla_in), len(cmp_in)),
        grid_spec=grid_spec,
        out_shape=[jax.ShapeDtypeStruct((db, ts, MLA_W), F32), jax.ShapeDtypeStruct((db, nblk, cw), F32)],
        compiler_params=_params(("arbitrary", "arbitrary")),
        name="decode_pages",
    )(page_table, *mla_in, *cmp_in, *([cache_mla_t] * n_pages_step), *([cache_nsa_t] * n_pages_step))


def _alibi_slope(h):
    return 2.0 ** (-8.0 * (h + 1) / NSA_H)


def _split_bf16(x):
    hi = x.astype(BF16)
    return hi, (x - hi.astype(F32)).astype(BF16)


def _cmp_kernel(q_ref, cb_ref, pos_ref, oc_o, imp_o, s_scr):
    tt = q_ref.shape[0]
    nc = cb_ref.shape[0]
    qpos = pos_ref[...]
    cpos = lax.broadcasted_iota(I32, (nc, 1), 0) * NSA_BLK + (NSA_BLK - 1)
    visible = cpos <= qpos
    dist = (qpos - cpos).astype(F32)
    q = q_ref[...] * NSA_SCALE
    cb = cb_ref[...]
    for g in range(NSA_G):
        k_hi, k_lo = _split_bf16(cb[:, g * NSA_D:(g + 1) * NSA_D])
        vc = cb[:, NSA_KVW + g * NSA_D:NSA_KVW + (g + 1) * NSA_D].astype(BF16)
        for i in range(NSA_R):
            h = g * NSA_R + i
            q_hi, q_lo = _split_bf16(q[:, h * NSA_D:(h + 1) * NSA_D])
            s = _dot_nt(k_hi, q_hi) + _dot_nt(k_hi, q_lo) + _dot_nt(k_lo, q_hi)
            s_scr[i * nc:(i + 1) * nc, :] = s - _alibi_slope(h) * dist
        s3 = jnp.where(visible[None], s_scr[...].reshape(NSA_R, nc, tt), NEG)
        mx = jnp.max(s3, axis=1, keepdims=True)
        e = jnp.where(visible[None], jnp.exp(s3 - mx), 0.0)
        den = jnp.sum(e, axis=1, keepdims=True)
        p = e / jnp.where(den > 0, den, 1.0)
        imp_o[g] = jnp.sum(p, axis=0)
        for i in range(NSA_R):
            h = g * NSA_R + i
            oc_o[h * NSA_D:(h + 1) * NSA_D, :] = _dot_tn(vc, p[i].astype(BF16))


def _cmp(q, cb, pos, tt):
    b, t, _ = q.shape
    nc = cb.shape[1]
    return pl.pallas_call(
        _cmp_kernel,
        grid=(b, t // tt),
        in_specs=[pl.BlockSpec((None, tt, NSA_W), lambda bi, ti: (bi, ti, 0)),
                  pl.BlockSpec((None, nc, 2 * NSA_KVW), lambda bi, ti: (bi, 0, 0)),
                  pl.BlockSpec((1, tt), lambda bi, ti: (0, ti))],
        out_specs=[pl.BlockSpec((None, NSA_W, tt), lambda bi, ti: (bi, 0, ti)),
                   pl.BlockSpec((None, NSA_G, nc, tt), lambda bi, ti: (bi, 0, 0, ti))],
        out_shape=[jax.ShapeDtypeStruct((b, NSA_W, t), F32),
                   jax.ShapeDtypeStruct((b, NSA_G, nc, t), F32)],
        scratch_shapes=[pltpu.VMEM((NSA_R * nc, tt), F32)],
        compiler_params=_params(("parallel", "parallel")),
        name="nsa_cmp",
    )(q, cb, pos)


def _topk_kernel(n_sel, imp_ref, pos_ref, sel_o, idx_o):
    _, nc, tt = imp_ref.shape
    nsp = sel_o.shape[1]
    qpos = pos_ref[...]
    blk = lax.broadcasted_iota(I32, (nsp, 1), 0)
    cur = jnp.right_shift(qpos, NSA_BLK.bit_length() - 1)
    forced = (blk == 0) | (blk == cur) | (blk == cur - 1)
    allowed = (blk <= cur) & (blk < n_sel)
    for g in range(NSA_G):
        imp = imp_ref[g]
        if nsp > nc:
            imp = jnp.concatenate([imp, jnp.zeros((nsp - nc, tt), F32)], axis=0)
        score = jnp.where(allowed, imp + jnp.where(forced, FORCED_BONUS, 0.0), NEG)
        chosen = jnp.zeros((nsp, tt), F32)
        picks = []
        for _ in range(NSA_TOPK):
            mx = jnp.max(score, axis=0, keepdims=True)
            first = jnp.min(jnp.where(score == mx, blk, nsp), axis=0, keepdims=True)
            hit = blk == first
            chosen = jnp.where(hit, 1.0, chosen)
            score = jnp.where(hit, GONE, score)
            picks.append(first)
        sel_o[g] = chosen
        idx_o[g] = jnp.concatenate(picks, axis=0)


def _topk(imp, pos, n_sel, tt):
    b, _, nc, t = imp.shape
    nsp = -(-(n_sel + TK // NSA_BLK) // SUB) * SUB
    return pl.pallas_call(
        functools.partial(_topk_kernel, n_sel),
        grid=(b, t // tt),
        in_specs=[pl.BlockSpec((None, NSA_G, nc, tt), lambda bi, ti: (bi, 0, 0, ti)),
                  pl.BlockSpec((1, tt), lambda bi, ti: (0, ti))],
        out_specs=[pl.BlockSpec((None, NSA_G, nsp, tt), lambda bi, ti: (bi, 0, 0, ti)),
                   pl.BlockSpec((None, NSA_G, NSA_TOPK, tt), lambda bi, ti: (bi, 0, 0, ti))],
        out_shape=[jax.ShapeDtypeStruct((b, NSA_G, nsp, t), F32),
                   jax.ShapeDtypeStruct((b, NSA_G, NSA_TOPK, t), I32)],
        compiler_params=_params(("parallel", "parallel")),
        name="nsa_topk",
    )(imp, pos)


def _bf16_parts(x, n):
    parts = []
    for _ in range(n):
        bits = np.asarray(x, np.float32).view(np.uint32)
        top = ((bits + np.uint32(0x7FFF) + ((bits >> np.uint32(16)) & np.uint32(1)))
               & np.uint32(0xFFFF0000)).view(np.float32)
        parts.append(float(top))
        x = float(np.float32(x) - top)
    return parts


def _nsa_prompt_kernel(q_ref, kv4_ref, kw_ref, oc_ref, sel_ref, gate_ref, out_ref,
                       k_aug, q_aug, s_buf, m_s, acc_s):
    qi = pl.program_id(1)
    r = NSA_R * TQ
    bpt = TK // NSA_BLK
    n_parts = SUB // 2
    qt = (q_ref[...] * NSA_SCALE).T.astype(BF16)
    gt = gate_ref[...].T
    lane = lax.broadcasted_iota(I32, (1, r), 1)
    qpos = qi * TQ + (lane & (TQ - 1))
    head = lane // TQ
    krow = lax.broadcasted_iota(I32, (TK, 1), 0)
    klane = lax.broadcasted_iota(I32, (1, TK), 1)
    row8 = lax.broadcasted_iota(I32, (SUB, 1), 0)
    n_tiles = (qi * TQ + TQ + TK - 1) // TK
    onehot = jnp.where(row8 == jnp.right_shift(klane, NSA_BLK.bit_length() - 1), 1.0, 0.0)

    def key_rows(j):
        rel = j * TK - qi * TQ + klane
        coarse = (rel & -NSA_BLK).astype(F32)
        fine = (rel & (NSA_BLK - 1)).astype(F32)
        alibi = jnp.where((row8 & 1) == 0, coarse, fine)
        return jnp.concatenate([alibi, onehot], axis=0).astype(BF16)

    def causal(s, j):
        return jnp.where(j * TK + krow <= qpos, s, NEG)

    def in_window(s, j):
        return jnp.where(qpos - (j * TK + krow) <= NSA_WIN, s, NEG)

    def tile(ref, lo, j):
        return ref[lo:lo + NSA_D, pl.ds(pl.multiple_of(j * TK, TK), TK)].astype(BF16)

    sel_chains, win_chains = [], []
    for g in range(NSA_G):
        qg = jnp.concatenate(
            [qt[(g * NSA_R + i) * NSA_D:(g * NSA_R + i + 1) * NSA_D, :] for i in range(NSA_R)], axis=1)
        slope8 = jnp.zeros((SUB, r), F32)
        for i in range(NSA_R):
            parts = _bf16_parts(_alibi_slope(g * NSA_R + i), n_parts)
            col = jnp.zeros((SUB, 1), F32)
            for k, part in enumerate(parts):
                col = jnp.where(jnp.right_shift(row8, 1) == k, part, col)
            slope8 = jnp.where(head == i, col, slope8)
        ks_lo, vs_lo = 2 * NSA_KVW + g * NSA_D, 3 * NSA_KVW + g * NSA_D
        kw_lo, vw_lo = g * NSA_D, NSA_KVW + g * NSA_D
        cs, cw = g, NSA_G + g
        ks_aug, qs_aug, kw_aug, qw_aug = k_aug.at[cs], q_aug.at[cs], k_aug.at[cw], q_aug.at[cw]
        qs_aug[:NSA_D, :] = qg
        qw_aug[:NSA_D, :] = qg
        qw_aug[NSA_D:, :] = jnp.concatenate([slope8, jnp.zeros((SUB, r), F32)], axis=0).astype(BF16)

        def qk_sel(j, g=g, kg=ks_aug, qa=qs_aug, ks_lo=ks_lo, slope8=slope8):
            kg[:NSA_D, :] = tile(kv4_ref, ks_lo, j)
            kg[NSA_D:, :] = key_rows(j)
            flags = sel_ref[g, pl.ds(j * bpt, SUB), :]
            off = jnp.where(flags > 0.5, 0.0, NEG)
            qa[NSA_D:, :] = jnp.concatenate(
                [slope8, jnp.concatenate([off] * NSA_R, axis=1)], axis=0).astype(BF16)
            return _dot_tn(kg[...], qa[...])

        def qk_win(j, kg=kw_aug, qa=qw_aug, kw_lo=kw_lo):
            kg[:NSA_D, :] = tile(kw_ref, kw_lo, j)
            kg[NSA_D:, :] = key_rows(j)
            return _dot_tn(kg[...], qa[...])

        sel_chains.append((qk_sel, lambda j, p, lo=vs_lo: _dot(_with_ones(tile(kv4_ref, lo, j)), p),
                           lambda s, j: s, causal, s_buf.at[cs], m_s.at[cs], acc_s.at[cs], None))
        win_chains.append((qk_win, lambda j, p, lo=vw_lo: _dot(_with_ones(tile(kw_ref, lo, j)), p),
                           in_window, lambda s, j: causal(in_window(s, j), j),
                           s_buf.at[cw], m_s.at[cw], acc_s.at[cw], None))

    outs = _flash_tiles(0, n_tiles, sel_chains, LOG2E, late=win_chains,
                        late_lo=jnp.maximum(qi * TQ - NSA_WIN, 0) // TK)
    for g in range(NSA_G):
        o_sel, o_win = outs[g], outs[NSA_G + g]
        for i in range(NSA_R):
            h = g * NSA_R + i
            rows = slice(h * NSA_D, (h + 1) * NSA_D)
            cols = slice(i * TQ, (i + 1) * TQ)
            out_ref[rows, :] = (gt[h:h + 1, :] * oc_ref[rows, :]
                                + gt[NSA_H + h:NSA_H + h + 1, :] * o_sel[:, cols]
                                + gt[2 * NSA_H + h:2 * NSA_H + h + 1, :] * o_win[:, cols])


def _nsa_prompt(q, kv4_t, kw_t, oc_t, sel, gate):
    b, t, _ = q.shape
    nsp = sel.shape[2]
    r = NSA_R * TQ
    tok = lambda n: pl.BlockSpec((None, TQ, n), lambda bi, qi: (bi, qi, 0))
    full = lambda n: pl.BlockSpec((None, n, t), lambda bi, qi: (bi, 0, 0))
    return pl.pallas_call(
        _nsa_prompt_kernel,
        grid=(b, t // TQ),
        in_specs=[tok(NSA_W), full(4 * NSA_KVW), full(2 * NSA_KVW),
                  pl.BlockSpec((None, NSA_W, TQ), lambda bi, qi: (bi, 0, qi)),
                  pl.BlockSpec((None, NSA_G, nsp, TQ), lambda bi, qi: (bi, 0, 0, qi)),
                  tok(3 * NSA_H)],
        out_specs=pl.BlockSpec((None, NSA_W, TQ), lambda bi, qi: (bi, 0, qi)),
        out_shape=jax.ShapeDtypeStruct((b, NSA_W, t), F32),
        scratch_shapes=[pltpu.VMEM((2 * NSA_G, NSA_D + 2 * SUB, TK), BF16),
                        pltpu.VMEM((2 * NSA_G, NSA_D + 2 * SUB, r), BF16),
                        pltpu.VMEM((2 * NSA_G, 2, TK, r), F32),
                        pltpu.VMEM((2 * NSA_G, 1, r), F32),
                        pltpu.VMEM((2 * NSA_G, NSA_D + PACK, r), F32)],
        compiler_params=_params(("parallel", "arbitrary")),
        name="nsa_prompt",
    )(q, kv4_t, kw_t, oc_t, sel, gate)


def _softmax_rows(parts):
    mx = None
    for s, _, _ in parts:
        pm = jnp.max(s, axis=1, keepdims=True)
        mx = pm if mx is None else jnp.maximum(mx, pm)
    den, num = None, None
    for s, v, feature_major in parts:
        p = jnp.exp(s - mx)
        d = jnp.sum(p, axis=1, keepdims=True)
        n = _dot_nt(p.astype(BF16), v) if feature_major else _dot(p.astype(BF16), v)
        den = d if den is None else den + d
        num = n if num is None else num + n
    return num / den


def _nsa_sel_decode_kernel(ts, pos0, nbp, idx_ref, pid_ref, q_ref, new_ref, *rest):
    nblk = ts * NSA_G * NSA_TOPK
    pages = rest[:nblk]
    out_ref = rest[nblk]
    b = pl.program_id(0)
    new = new_ref[...]
    lane = lax.broadcasted_iota(I32, (1, PAGE), 1)
    jn = lax.broadcasted_iota(I32, (1, SUB), 1)
    for t, g in [(t, g) for t in range(ts) for g in range(NSA_G)]:
        qg = (q_ref[t, g * NSA_R:(g + 1) * NSA_R, :] * NSA_SCALE).astype(BF16)
        slope = jnp.concatenate(
            [jnp.full((1, 1), _alibi_slope(g * NSA_R + i), F32) for i in range(NSA_R)], axis=0)
        parts = []
        has_new = jnp.zeros((), I32)
        for k in range(NSA_TOPK):
            bid = idx_ref[((b * NSA_G + g) * ts + t) * NSA_TOPK + k]
            pg = pages[(t * NSA_G + g) * NSA_TOPK + k][...]
            kk = pg[g * NSA_D:(g + 1) * NSA_D, :].astype(BF16)
            vv = pg[NSA_KVW + g * NSA_D:NSA_KVW + (g + 1) * NSA_D, :].astype(BF16)
            sub = bid % BPP
            rel = ((bid - sub) * NSA_BLK - pos0 + lane).astype(F32)
            s = _dot(qg, kk) + slope * rel
            mine = (lane // NSA_BLK == sub) & (bid < nbp)
            parts.append((jnp.where(mine, s, NEG), vv, True))
            has_new = has_new + (bid >= nbp).astype(I32)
        kn = new[:, 2 * NSA_KVW + g * NSA_D:2 * NSA_KVW + (g + 1) * NSA_D].astype(BF16)
        vn = new[:, 3 * NSA_KVW + g * NSA_D:3 * NSA_KVW + (g + 1) * NSA_D].astype(BF16)
        sn = _dot_nt(qg, kn) + slope * jn.astype(F32)
        ok = (jn <= t) & (jn < ts) & (has_new > 0)
        parts.append((jnp.where(ok, sn, NEG), vn, False))
        out_ref[t, g * NSA_R:(g + 1) * NSA_R, :] = _softmax_rows(parts)


def _nsa_sel_decode(idx, page_table, cache_t, q4, kv4_new, ts, pos0):
    db, n_pages = page_table.shape
    nbp = n_pages * BPP
    cw = 2 * NSA_KVW
    page_ids = jnp.take_along_axis(page_table, jnp.minimum(idx, nbp - 1) // BPP, axis=1).reshape(-1)
    idx_flat = idx.reshape(-1)

    def page_map(b, idx_r, pid, t, g, k):
        return (pid[((b * NSA_G + g) * ts + t) * NSA_TOPK + k], 1, 0)

    page_specs = [pl.BlockSpec((None, cw, PAGE), functools.partial(page_map, t=t, g=g, k=k))
                  for t in range(ts) for g in range(NSA_G) for k in range(NSA_TOPK)]
    grid_spec = pltpu.PrefetchScalarGridSpec(
        num_scalar_prefetch=2,
        grid=(db,),
        in_specs=[pl.BlockSpec((None, ts, NSA_H, NSA_D), lambda b, idx, pt: (b, 0, 0, 0)),
                  pl.BlockSpec((None, SUB, 4 * NSA_KVW), lambda b, idx, pt: (b, 0, 0))] + page_specs,
        out_specs=pl.BlockSpec((None, ts, NSA_H, NSA_D), lambda b, idx, pt: (b, 0, 0, 0)))
    return pl.pallas_call(
        functools.partial(_nsa_sel_decode_kernel, ts, pos0, nbp),
        grid_spec=grid_spec,
        out_shape=jax.ShapeDtypeStruct((db, ts, NSA_H, NSA_D), F32),
        compiler_params=_params(("parallel",)),
        name="nsa_sel_decode",
    )(idx_flat, page_ids, q4, kv4_new, *([cache_t] * (ts * NSA_G * NSA_TOPK)))


def _nsa_win_decode_kernel(ts, q_ref, win_ref, new_ref, out_ref, win_o):
    rows = NSA_R * SUB
    win = win_ref[...]
    new = new_ref[...]
    win_o[...] = jnp.concatenate([win[:, ts:], new.T[:, :ts]], axis=1)
    tq = lax.broadcasted_iota(I32, (rows, 1), 0) & (SUB - 1)
    iw = lax.broadcasted_iota(I32, (1, NSA_WIN), 1)
    jn = lax.broadcasted_iota(I32, (1, SUB), 1)
    for g in range(NSA_G):
        qg = jnp.concatenate([q_ref[:, g * NSA_R + i, :] for i in range(NSA_R)], axis=0)
        qg = (qg * NSA_SCALE).astype(BF16)
        slope = jnp.concatenate(
            [jnp.full((SUB, 1), _alibi_slope(g * NSA_R + i), F32) for i in range(NSA_R)], axis=0)
        kw = win[g * NSA_D:(g + 1) * NSA_D, :].astype(BF16)
        vw = win[NSA_KVW + g * NSA_D:NSA_KVW + (g + 1) * NSA_D, :].astype(BF16)
        kn = new[:, g * NSA_D:(g + 1) * NSA_D].astype(BF16)
        vn = new[:, NSA_KVW + g * NSA_D:NSA_KVW + (g + 1) * NSA_D].astype(BF16)
        sw = _dot(qg, kw) + slope * (iw - NSA_WIN).astype(F32)
        sw = jnp.where(iw >= tq, sw, NEG)
        sn = _dot_nt(qg, kn) + slope * jn.astype(F32)
        sn = jnp.where((jn <= tq) & (jn < ts), sn, NEG)
        o = _softmax_rows([(sw, vw, True), (sn, vn, False)])
        for i in range(NSA_R):
            out_ref[:, g * NSA_R + i, :] = o[i * SUB:i * SUB + ts, :]


def _nsa_win_decode(q4, win_t, kw_new, ts):
    db = q4.shape[0]
    return pl.pallas_call(
        functools.partial(_nsa_win_decode_kernel, ts),
        grid=(db,),
        in_specs=[pl.BlockSpec((None, SUB, NSA_H, NSA_D), lambda b: (b, 0, 0, 0)),
                  pl.BlockSpec((None, 2 * NSA_KVW, NSA_WIN), lambda b: (b, 0, 0)),
                  pl.BlockSpec((None, SUB, 2 * NSA_KVW), lambda b: (b, 0, 0))],
        out_specs=[pl.BlockSpec((None, ts, NSA_H, NSA_D), lambda b: (b, 0, 0, 0)),
                   pl.BlockSpec((None, 2 * NSA_KVW, NSA_WIN), lambda b: (b, 0, 0))],
        out_shape=[jax.ShapeDtypeStruct((db, ts, NSA_H, NSA_D), F32),
                   jax.ShapeDtypeStruct((db, 2 * NSA_KVW, NSA_WIN), F32)],
        compiler_params=_params(("parallel",)),
        name="nsa_win_decode",
    )(q4, win_t, kw_new)


def _odd_tail(o, z_ref, x_ref, wout, gfin):
    y = x_ref[...] + _dot((o * _silu(z_ref[...])).astype(BF16), wout[...])
    return _rms(y, gfin[...])


def _odd_out_prompt_kernel(ot_ref, z_ref, x_ref, wout, gfin, out_ref):
    out_ref[...] = _odd_tail(ot_ref[...].T, z_ref, x_ref, wout, gfin)


def _odd_out_prompt(ot, z, x, wout, gfin, tm):
    b, _, t = ot.shape
    tok = pl.BlockSpec((None, tm, D_MODEL), lambda bi, ti: (bi, ti, 0))
    return pl.pallas_call(
        _odd_out_prompt_kernel,
        grid=(b, t // tm),
        in_specs=[pl.BlockSpec((None, NSA_W, tm), lambda bi, ti: (bi, 0, ti)), tok, tok,
                  _const_spec(wout.shape), _const_spec(gfin.shape)],
        out_specs=tok,
        out_shape=jax.ShapeDtypeStruct((b, t, D_MODEL), F32),
        compiler_params=_params(("parallel", "parallel")),
        name="odd_out_prompt",
    )(ot, z, x, wout, gfin)


def _odd_out_decode_kernel(oc_ref, os_ref, ow_ref, gate_ref, z_ref, x_ref, wout, gfin, out_ref):
    gate = gate_ref[...]
    oc, osel, ow = oc_ref[...], os_ref[...], ow_ref[...]
    heads = []
    for h in range(NSA_H):
        c = slice(h * NSA_D, (h + 1) * NSA_D)
        heads.append(gate[:, h:h + 1] * oc[:, c] + gate[:, NSA_H + h:NSA_H + h + 1] * osel[:, c]
                     + gate[:, 2 * NSA_H + h:2 * NSA_H + h + 1] * ow[:, c])
    out_ref[...] = _odd_tail(jnp.concatenate(heads, axis=1), z_ref, x_ref, wout, gfin)


def _odd_out_decode(oc, osel, ow, gate, z, x, wout, gfin):
    m = x.shape[0]
    args = (oc, osel, ow, gate, z, x, wout, gfin)
    return pl.pallas_call(
        _odd_out_decode_kernel,
        grid=(1,),
        in_specs=[_const_spec(a.shape) for a in args],
        out_specs=_const_spec((m, D_MODEL)),
        out_shape=jax.ShapeDtypeStruct((m, D_MODEL), F32),
        compiler_params=_params(("arbitrary",)),
        name="odd_out_decode",
    )(*args)


def _rope_tables(pos):
    half = MLA_ROPE // 2
    inv = ROPE_THETA ** (-jnp.arange(half, dtype=F32) / half)
    ang = pos.astype(F32)[:, None] * inv[None, :]
    cos, sin = jnp.cos(ang), jnp.sin(ang)
    return jnp.concatenate([cos, cos], axis=1), jnp.concatenate([-sin, sin], axis=1)


def _block_diag(x):
    t, g, r, c = x.shape
    eye = jnp.eye(g, dtype=x.dtype)
    return jnp.einsum("tgrc,gh->tgrhc", x, eye).reshape(t, g * r, g * c)


def _even_weights(norm_g, w_in, g_q, g_kv, w_uq, w_uk, w_uv):
    edges = [0, MLA_QL, MLA_QL + MLA_KVL, MLA_QL + MLA_ROW]
    edges += [edges[-1] + MLA_W, edges[-1] + MLA_W + S5_W, edges[-1] + MLA_W + 2 * S5_W]
    wb = w_in.astype(BF16)
    pieces = [wb[:, edges[i]:edges[i + 1]] for i in range(6)]
    in_w = (norm_g[None, :], *pieces, g_q[None, :], g_kv[None, :])
    uq = jnp.transpose(w_uq, (1, 0, 2)).astype(BF16)
    mla_w = (uq[:, :, :MLA_NOPE], uq[:, :, MLA_NOPE:],
             jnp.transpose(w_uk, (1, 2, 0)).astype(BF16),
             jnp.transpose(w_uv, (1, 0, 2)).astype(BF16))
    uq_t = jnp.transpose(w_uq, (1, 2, 0)).astype(BF16)
    mla_wt = (uq_t[:, :MLA_NOPE].reshape(MLA_H * MLA_NOPE, MLA_QL),
              uq_t[:, MLA_NOPE:].reshape(MLA_H * MLA_ROPE, MLA_QL),
              jnp.transpose(w_uk, (1, 0, 2)).astype(BF16),
              jnp.transpose(w_uv, (1, 2, 0)).astype(BF16))
    return in_w, mla_w, mla_wt


def _s5_weights(lam_re, lam_im, log_dt, b_re, b_im, c_re, c_im, d_skip, w_glu, b_glu):
    def bmat(b):
        return _block_diag(jnp.transpose(b.reshape(S5_NT, S5_GT, S5_N, S5_P), (0, 1, 3, 2)))

    def cmat(c):
        return _block_diag(jnp.transpose(c.reshape(S5_NT, S5_GT, S5_P, S5_N), (0, 1, 3, 2)))

    return (lam_re.reshape(1, S5_S), lam_im.reshape(1, S5_S),
            jnp.repeat(log_dt, S5_N).reshape(1, S5_S),
            bmat(b_re), bmat(b_im), cmat(c_re), cmat(c_im),
            d_skip.reshape(1, S5_W), w_glu.astype(BF16), b_glu[None, :])


def _odd_weights(norm_g, w_in, pe_k, pe_v, phi1_k, phi2_k, phi1_v, phi2_v):
    wb = w_in.astype(BF16)
    e0 = NSA_W
    e1 = e0 + 4 * NSA_KVW
    e2 = e1 + 2 * NSA_KVW
    e3 = e2 + 3 * NSA_H
    in_w = (norm_g[None, :], wb[:, :e0], wb[:, e0:e1], wb[:, e1:e2], wb[:, e2:e3], wb[:, e3:])
    pe = jnp.concatenate([pe_k, pe_k, pe_v, pe_v], axis=1)
    p1k = phi1_k.reshape(NSA_BLK, NSA_D, NSA_D)
    p1v = phi1_v.reshape(NSA_BLK, NSA_D, NSA_D)
    phi2 = _block_diag(jnp.stack([phi2_k, phi2_k, phi2_v, phi2_v], axis=0)[None])[0].astype(BF16)
    return in_w, (pe, p1k, p1v, phi2)


def _pad_tokens(x, n):
    return jnp.pad(x, ((0, 0), (0, n - x.shape[1])) + ((0, 0),) * (x.ndim - 2))


def _rows_last(x):
    nd = x.ndim
    xt = jnp.transpose(x, (0,) + tuple(range(2, nd)) + (1,))
    return xt.reshape(x.shape[0], -1, x.shape[1])


def _rows_second(x_t, feature_shape):
    b, _, rows = x_t.shape
    nf = len(feature_shape)
    xt = x_t.reshape((b,) + tuple(feature_shape) + (rows,))
    return jnp.transpose(xt, (0, nf + 1) + tuple(range(1, nf + 1)))


def kernel(x_prompt, x_sample, cache_mla, state_s5, cache_nsa_kv, state_nsa_win, page_table, norm_even, w_in_even, mla_g_q, mla_g_kv, mla_w_uq, mla_w_uk, mla_w_uv, s5_lambda_re, s5_lambda_im, s5_log_dt, s5_b_re, s5_b_im, s5_c_re, s5_c_im, s5_d, s5_w_glu, s5_b_glu, w_out_even, norm_odd, w_in_odd, nsa_pe_k, nsa_pe_v, nsa_phi1_k, nsa_phi2_k, nsa_phi1_v, nsa_phi2_v, w_out_odd, norm_final):
    b, t, _ = x_prompt.shape
    db, ts, _ = x_sample.shape
    n_pages = page_table.shape[1]
    past = n_pages * PAGE
    mp, ms = b * t, db * ts
    tm = min(1024, t)
    assert t % (PACK * NSA_BLK) == 0 and t % TK == 0 and ts <= SUB
    assert t <= NSA_BLK * 256

    pos_p = jnp.arange(t, dtype=I32)
    pos_s = past + jnp.arange(SUB, dtype=I32)
    cos_p, sin_p = _rope_tables(pos_p)
    cos_s, sin_s = _rope_tables(pos_s)
    cos_st, sin_st = jnp.tile(cos_s[:ts], (db, 1)), jnp.tile(sin_s[:ts], (db, 1))

    even_in_w, mla_w, mla_wt = _even_weights(norm_even[0], w_in_even[0], mla_g_q[0], mla_g_kv[0],
                                             mla_w_uq[0], mla_w_uk[0], mla_w_uv[0])
    s5_w = _s5_weights(s5_lambda_re[0], s5_lambda_im[0], s5_log_dt[0], s5_b_re[0], s5_b_im[0],
                       s5_c_re[0], s5_c_im[0], s5_d[0], s5_w_glu[0], s5_b_glu[0])
    wo_e = w_out_even[0].astype(BF16)
    wo_a, wo_b = wo_e[:MLA_W], wo_e[MLA_W:]

    xp2 = x_prompt.reshape(mp, D_MODEL)
    cq, rows_pt, za, u, zb = _even_in(xp2, even_in_w, cos_p, sin_p, tm, seq=t)
    mix_a = _mla_prompt(cq.reshape(b, t, MLA_QL), rows_pt, za.reshape(b, t, MLA_W), mla_wt, cos_p.T, sin_p.T)
    zeros_p = jnp.zeros((b, S5_S), F32)
    mix_b, sre_p, sim_p = _s5(u.reshape(b, t, S5_W), zb.reshape(b, t, S5_W), s5_w, zeros_p, zeros_p,
                              min(128, t))
    xp1 = _even_out(xp2, mix_a.reshape(mp, MLA_W), mix_b.reshape(mp, S5_W), wo_a, wo_b, tm)

    xs2 = x_sample.reshape(ms, D_MODEL)
    cq_s, rows_s, za_s, u_s, zb_s = _even_in(xs2, even_in_w, cos_st, sin_st, ms)
    pad3 = lambda a, n: _pad_tokens(a.reshape(db, ts, n), SUB)
    odd_in_w, cmp_w = _odd_weights(norm_odd[0], w_in_odd[0], nsa_pe_k[0], nsa_pe_v[0], nsa_phi1_k[0],
                                   nsa_phi2_k[0], nsa_phi1_v[0], nsa_phi2_v[0])
    cache_nsa_t = _rows_last(cache_nsa_kv[0])
    mix_a_s, cb_s = _decode_pages(page_table, _rows_last(cache_mla[0]), pad3(cq_s, MLA_QL),
                                  pad3(rows_s, MLA_ROW), pad3(za_s, MLA_W), mla_w, cos_s, sin_s, ts,
                                  cache_nsa_t, *cmp_w)
    st = state_s5[0]
    mix_b_s, sre_s, sim_s = _s5(u_s.reshape(db, ts, S5_W), zb_s.reshape(db, ts, S5_W), s5_w,
                                st[..., 0].reshape(db, S5_S), st[..., 1].reshape(db, S5_S), ts)
    xs1 = _even_out(xs2, mix_a_s.reshape(ms, MLA_W), mix_b_s.reshape(ms, S5_W), wo_a, wo_b, ms)

    wo_o = w_out_odd[0].astype(BF16)
    gfin = norm_final[None, :]

    q, kv4_t, kw_t, gate, z = _odd_in(xp1, odd_in_w, tm, seq=t)
    nblk_p = t // NSA_BLK
    cb = _compress_prompt(kv4_t, *cmp_w)
    q3 = q.reshape(b, t, NSA_W)
    tt = min(512, t)
    oc_t, imp = _cmp(q3, cb.reshape(b, nblk_p, 2 * NSA_KVW), pos_p[None], tt)
    sel, _ = _topk(imp, pos_p[None], nblk_p, tt)
    o_t = _nsa_prompt(q3, kv4_t, kw_t, oc_t, sel, gate.reshape(b, t, 3 * NSA_H))
    y_prompt = _odd_out_prompt(o_t, z.reshape(b, t, NSA_W), xp1.reshape(b, t, D_MODEL), wo_o, gfin, tt)

    q_s, kv4_s, kw_s, gate_s, z_s = _odd_in(xs1, odd_in_w, ms)
    n_sel_s = -(-(past + ts) // NSA_BLK)
    q_s3 = q_s.reshape(db, ts, NSA_W)
    oc_ts, imp_s = _cmp(_pad_tokens(q_s3, SUB), cb_s, pos_s[None], SUB)
    imp_l = jnp.transpose(imp_s[..., :ts], (1, 2, 0, 3)).reshape(1, NSA_G, cb_s.shape[1], ms)
    pos_l = jnp.tile(pos_s[:ts], db)[None]
    _, idx_l = _topk(imp_l, pos_l, n_sel_s, ms)
    idx_s = jnp.transpose(idx_l.reshape(NSA_G, NSA_TOPK, db, ts), (2, 0, 3, 1)).reshape(db, -1)
    q_s4 = q_s.reshape(db, ts, NSA_H, NSA_D)
    o_sel_s = _nsa_sel_decode(idx_s, page_table, cache_nsa_t, q_s4, pad3(kv4_s, 4 * NSA_KVW), ts, past)
    win_t = _rows_last(state_nsa_win[0])
    kw_s3 = kw_s.reshape(db, ts, 2 * NSA_KVW)
    o_win_s, win_st = _nsa_win_decode(_pad_tokens(q_s4, SUB), win_t, _pad_tokens(kw_s3, SUB), ts)
    oc_s = jnp.transpose(oc_ts, (0, 2, 1))[:, :ts].reshape(ms, NSA_W)
    y_sample = _odd_out_decode(oc_s, o_sel_s.reshape(ms, NSA_W), o_win_s.reshape(ms, NSA_W),
                               gate_s, z_s, xs1, wo_o, gfin)

    state = lambda re, im, n: jnp.stack([re, im], axis=-1).reshape(1, n, S5_G, S5_N, 2)
    win_shape = (2, NSA_G, NSA_D)
    if t >= NSA_WIN:
        win_pt = kw_t[:, :, t - NSA_WIN:]
    else:
        win_pt = jnp.pad(kw_t, ((0, 0), (0, 0), (NSA_WIN - t, 0)))
    return (y_prompt, y_sample.reshape(db, ts, D_MODEL),
            _rows_second(rows_pt, (MLA_ROW,))[None], rows_s.reshape(1, db, ts, MLA_ROW),
            state(sre_p, sim_p, b), state(sre_s, sim_s, db),
            _rows_second(kv4_t, (4, NSA_G, NSA_D))[None], kv4_s.reshape(1, db, ts, 4, NSA_G, NSA_D),
            _rows_second(win_pt, win_shape)[None], _rows_second(win_st, win_shape)[None])
```
